```python
import jax, jax.numpy as jnp
from jax import lax
import numpy as np

D_MODEL = 1024
BATCH = 8
SEQ = 8192
DEPTH = 1

CTX_LEN = 256
GRID_W = 64
LRU_WIDTH = 512
LRU_HEADS = 8
LRU_HEAD_DIM = LRU_WIDTH // LRU_HEADS
LRU_CONV_W = 4
LRU_CONV_LEFT = 2
LRU_C = 8.0
MLA_HEADS = 8
QK_NOPE_DIM = 64
QK_ROPE_DIM = 32
QK_HEAD_DIM = QK_NOPE_DIM + QK_ROPE_DIM
V_HEAD_DIM = 64
Q_LORA_RANK = 256
KV_LORA_RANK = 128
MLA_WIDTH = MLA_HEADS * V_HEAD_DIM
MLA_SCALE = QK_HEAD_DIM ** -0.5
ROPE_PAIRS_PER_AXIS = QK_ROPE_DIM // 4
ROPE_BASE = 10000.0
Q_BLOCK = 128
MIX_WIDTH = LRU_WIDTH + MLA_WIDTH
OFF_GATE = LRU_WIDTH
OFF_CQ = 2 * LRU_WIDTH
OFF_CKV = OFF_CQ + Q_LORA_RANK
OFF_KR = OFF_CKV + KV_LORA_RANK
IN_PROJ_WIDTH = OFF_KR + QK_ROPE_DIM
D_FF = 2816
FFN_CONV_W = 3
FFN_CONV_LEFT = 1
N_MOD = 6
NORM_EPS = 1e-6

kernel_name = "hybrid_rglru_mla_convffn_dit_layer"


def rms_norm(x, g):
    xf = x.astype(jnp.float32)
    y = xf * lax.rsqrt(jnp.mean(xf * xf, axis=-1, keepdims=True) + NORM_EPS)
    return (y * g.astype(jnp.float32)).astype(x.dtype)


def modulate(h, shift, scale):
    return h * (1 + scale) + shift


def dwconv(x, w, b, left):
    k_width = w.shape[0]
    n = x.shape[1]
    xp = jnp.pad(x, ((0, 0), (left, k_width - 1 - left), (0, 0)))
    out = b
    for k in range(k_width):
        out = out + xp[:, k:k + n] * w[k]
    return out


def axial_rope_tables(n_tokens, dtype):
    rows = n_tokens // GRID_W
    row = jnp.repeat(jnp.arange(rows, dtype=jnp.float32), GRID_W)
    col = jnp.tile(jnp.arange(GRID_W, dtype=jnp.float32), rows)
    inv_freq = ROPE_BASE ** (-jnp.arange(ROPE_PAIRS_PER_AXIS, dtype=jnp.float32) / ROPE_PAIRS_PER_AXIS)
    ang = jnp.concatenate([row[:, None] * inv_freq, col[:, None] * inv_freq], axis=-1)
    return jnp.cos(ang).astype(dtype), jnp.sin(ang).astype(dtype)


def axial_rope(x, cos, sin):
    p = ROPE_PAIRS_PER_AXIS
    xr1, xr2, xc1, xc2 = jnp.split(x, 4, axis=-1)
    cr, cc = cos[..., :p], cos[..., p:]
    sr, sc = sin[..., :p], sin[..., p:]
    return jnp.concatenate([xr1 * cr - xr2 * sr, xr2 * cr + xr1 * sr,
                            xc1 * cc - xc2 * sc, xc2 * cc + xc1 * sc], axis=-1)


def split_in_proj(p):
    return (p[..., :OFF_GATE], p[..., OFF_GATE:OFF_CQ], p[..., OFF_CQ:OFF_CKV],
            p[..., OFF_CKV:OFF_KR], p[..., OFF_KR:IN_PROJ_WIDTH])


def rglru_coeffs(xc, w_a, b_a, w_x, b_x, lam):
    bsz, n, _ = xc.shape
    xh = xc.reshape(bsz, n, LRU_HEADS, LRU_HEAD_DIM)
    r = jax.nn.sigmoid(jnp.einsum('blhi,hij->blhj', xh, w_a).reshape(bsz, n, LRU_WIDTH) + b_a)
    i = jax.nn.sigmoid(jnp.einsum('blhi,hij->blhj', xh, w_x).reshape(bsz, n, LRU_WIDTH) + b_x)
    log_a = -LRU_C * r.astype(jnp.float32) * jax.nn.softplus(-lam.astype(jnp.float32))
    a = jnp.exp(log_a)
    u = jnp.sqrt(-jnp.expm1(2 * log_a)) * (i * xc).astype(jnp.float32)
    return a, u


def linear_scan(a, u, h0, reverse):
    if reverse:
        a, u = jnp.flip(a, 1), jnp.flip(u, 1)
    combine = lambda l, r: (l[0] * r[0], r[0] * l[1] + r[1])
    a_cum, u_cum = lax.associative_scan(combine, (a, u), axis=1)
    h = u_cum if h0 is None else a_cum * h0[:, None, :] + u_cum
    return jnp.flip(h, 1) if reverse else h


def mla_keys_values(ckv, kr, g_kv, w_ukv, cos, sin):
    bsz, n, _ = ckv.shape
    kv = (rms_norm(ckv, g_kv) @ w_ukv).reshape(bsz, n, MLA_HEADS, QK_NOPE_DIM + V_HEAD_DIM)
    k_nope, v = kv[..., :QK_NOPE_DIM], kv[..., QK_NOPE_DIM:]
    if cos is not None:
        kr = axial_rope(kr, cos, sin)
    k_rope = jnp.broadcast_to(kr[:, :, None, :], (bsz, n, MLA_HEADS, QK_ROPE_DIM))
    return jnp.concatenate([k_nope, k_rope], axis=-1), v


def mla_queries(cq, g_q, w_uq, cos, sin):
    bsz, n, _ = cq.shape
    q = (rms_norm(cq, g_q) @ w_uq).reshape(bsz, n, MLA_HEADS, QK_HEAD_DIM)
    q_nope, q_rope = q[..., :QK_NOPE_DIM], q[..., QK_NOPE_DIM:]
    if cos is not None:
        q_rope = axial_rope(q_rope, cos, sin)
    return jnp.concatenate([q_nope, q_rope], axis=-1)


def softmax_attention(q, k, v):
    s = jnp.einsum('bqhd,bkhd->bhqk', q, k).astype(jnp.float32) * MLA_SCALE
    p = jax.nn.softmax(s, axis=-1).astype(v.dtype)
    return jnp.einsum('bhqk,bkhd->bqhd', p, v)


def token_mixers(h_lat, h_ctx, cos, sin, w_in, lru_conv_w, lru_conv_b, lru_w_a, lru_b_a,
                 lru_w_x, lru_b_x, lru_lambda, mla_g_q, mla_w_uq, mla_g_kv, mla_w_ukv, w_out,
                 with_ctx_out):
    bsz, n_lat, _ = h_lat.shape
    n_ctx = h_ctx.shape[1]
    xr_l, gr_l, cq_l, ckv_l, kr_l = split_in_proj(h_lat @ w_in)
    xr_c, gr_c, cq_c, ckv_c, kr_c = split_in_proj(h_ctx @ w_in)

    xcv_l = dwconv(xr_l, lru_conv_w, lru_conv_b, LRU_CONV_LEFT)
    xcv_c = dwconv(xr_c, lru_conv_w, lru_conv_b, LRU_CONV_LEFT)
    lat_states, ctx_states = [], []
    for d, reverse in enumerate((False, True)):
        params_d = (lru_w_a[d], lru_b_a[d], lru_w_x[d], lru_b_x[d], lru_lambda[d])
        a_c, u_c = rglru_coeffs(xcv_c, *params_d)
        h_c = linear_scan(a_c, u_c, None, reverse)
        h0 = h_c[:, 0] if reverse else h_c[:, -1]
        a_l, u_l = rglru_coeffs(xcv_l, *params_d)
        lat_states.append(linear_scan(a_l, u_l, h0, reverse))
        ctx_states.append(h_c)
    y_lru_l = (lat_states[0] + lat_states[1]).astype(h_lat.dtype) * jax.nn.gelu(gr_l)

    k_c, v_c = mla_keys_values(ckv_c, kr_c, mla_g_kv, mla_w_ukv, None, None)
    k_l, v_l = mla_keys_values(ckv_l, kr_l, mla_g_kv, mla_w_ukv, cos, sin)
    q_l = mla_queries(cq_l, mla_g_q, mla_w_uq, cos[:, None], sin[:, None])
    k_all = jnp.concatenate([k_c, k_l], axis=1)
    v_all = jnp.concatenate([v_c, v_l], axis=1)
    n_blocks = n_lat // Q_BLOCK
    q_blocks = q_l.reshape(bsz, n_blocks, Q_BLOCK, MLA_HEADS, QK_HEAD_DIM).swapaxes(0, 1)
    o_blocks = lax.map(lambda qb: softmax_attention(qb, k_all, v_all), q_blocks)
    y_mla_l = o_blocks.swapaxes(0, 1).reshape(bsz, n_lat, MLA_WIDTH)
    y_lat = jnp.concatenate([y_lru_l, y_mla_l], axis=-1) @ w_out
    if not with_ctx_out:
        return y_lat, None

    y_lru_c = (ctx_states[0] + ctx_states[1]).astype(h_ctx.dtype) * jax.nn.gelu(gr_c)
    q_c = mla_queries(cq_c, mla_g_q, mla_w_uq, None, None)
    y_mla_c = softmax_attention(q_c, k_c, v_c).reshape(bsz, n_ctx, MLA_WIDTH)
    y_ctx = jnp.concatenate([y_lru_c, y_mla_c], axis=-1) @ w_out
    return y_lat, y_ctx


def conv_ffn(h, w_up, conv_w, conv_b, w_down):
    up = dwconv(h @ w_up, conv_w, conv_b, FFN_CONV_LEFT)
    u, g = up[..., :D_FF], up[..., D_FF:]
    return (jax.nn.silu(g) * u) @ w_down


def _fwd_setup_inputs(seed: int = 0) -> dict:
    key = jax.random.key(seed)
    ks = jax.random.split(key, 32)
    L = DEPTH
    f32 = jnp.float32

    def nrm(k, shape, scale):
        return jax.random.normal(k, shape, f32) * scale

    def gain(k, shape):
        return 1.0 + 0.1 * jax.random.normal(k, shape, f32)

    a8 = jax.random.uniform(ks[17], (L, 2, LRU_WIDTH), f32, minval=0.9, maxval=0.999)
    a_base = a8 ** (1.0 / LRU_C)
    lam = jnp.log(a_base) - jnp.log1p(-a_base)
    return {
        "x": nrm(ks[0], (BATCH, SEQ, D_MODEL), 1.0),
        "c": nrm(ks[1], (BATCH, D_MODEL), 1.0),
        "ctx": nrm(ks[2], (BATCH, CTX_LEN, D_MODEL), 1.0),
        "c_ctx": nrm(ks[3], (D_MODEL,), 1.0),
        "w_mod": nrm(ks[4], (L, D_MODEL, N_MOD * D_MODEL), D_MODEL ** -0.5),
        "b_mod": nrm(ks[5], (L, N_MOD * D_MODEL), 0.02),
        "g_pre_mix": gain(ks[6], (L, D_MODEL)),
        "g_post_mix": gain(ks[7], (L, D_MODEL)),
        "g_pre_ffn": gain(ks[8], (L, D_MODEL)),
        "g_post_ffn": gain(ks[9], (L, D_MODEL)),
        "w_in": nrm(ks[10], (L, D_MODEL, IN_PROJ_WIDTH), D_MODEL ** -0.5),
        "lru_conv_w": nrm(ks[11], (L, LRU_CONV_W, LRU_WIDTH), LRU_CONV_W ** -0.5),
        "lru_conv_b": nrm(ks[12], (L, LRU_WIDTH), 0.02),
        "lru_w_a": nrm(ks[13], (L, 2, LRU_HEADS, LRU_HEAD_DIM, LRU_HEAD_DIM), LRU_HEAD_DIM ** -0.5),
        "lru_b_a": nrm(ks[14], (L, 2, LRU_WIDTH), 0.1),
        "lru_w_x": nrm(ks[15], (L, 2, LRU_HEADS, LRU_HEAD_DIM, LRU_HEAD_DIM), LRU_HEAD_DIM ** -0.5),
        "lru_b_x": nrm(ks[16], (L, 2, LRU_WIDTH), 0.1),
        "lru_lambda": lam,
        "mla_g_q": gain(ks[18], (L, Q_LORA_RANK)),
        "mla_w_uq": nrm(ks[19], (L, Q_LORA_RANK, MLA_HEADS * QK_HEAD_DIM), Q_LORA_RANK ** -0.5),
        "mla_g_kv": gain(ks[20], (L, KV_LORA_RANK)),
        "mla_w_ukv": nrm(ks[21], (L, KV_LORA_RANK, MLA_HEADS * (QK_NOPE_DIM + V_HEAD_DIM)), KV_LORA_RANK ** -0.5),
        "w_out": nrm(ks[22], (L, MIX_WIDTH, D_MODEL), MIX_WIDTH ** -0.5),
        "ffn_w_up": nrm(ks[23], (L, D_MODEL, 2 * D_FF), D_MODEL ** -0.5),
        "ffn_conv_w": nrm(ks[24], (L, FFN_CONV_W, 2 * D_FF), FFN_CONV_W ** -0.5),
        "ffn_conv_b": nrm(ks[25], (L, 2 * D_FF), 0.02),
        "ffn_w_down": nrm(ks[26], (L, D_FF, D_MODEL), D_FF ** -0.5),
    }


def _fwd_reference(x, c, ctx, c_ctx, w_mod, b_mod, g_pre_mix, g_post_mix, g_pre_ffn, g_post_ffn,
              w_in, lru_conv_w, lru_conv_b, lru_w_a, lru_b_a, lru_w_x, lru_b_x, lru_lambda,
              mla_g_q, mla_w_uq, mla_g_kv, mla_w_ukv, w_out, ffn_w_up, ffn_conv_w, ffn_conv_b,
              ffn_w_down):
    n_lat = x.shape[1]
    cos, sin = axial_rope_tables(n_lat, x.dtype)
    xc = ctx
    for l in range(DEPTH):
        last = l == DEPTH - 1
        mod_l = jax.nn.silu(c) @ w_mod[l] + b_mod[l]
        mod_c = jax.nn.silu(c_ctx) @ w_mod[l] + b_mod[l]
        sh1, sc1, gt1, sh2, sc2, gt2 = jnp.split(mod_l[:, None, :], N_MOD, axis=-1)
        csh1, csc1, cgt1, csh2, csc2, cgt2 = jnp.split(mod_c, N_MOD, axis=-1)

        h_l = modulate(rms_norm(x, g_pre_mix[l]), sh1, sc1)
        h_c = modulate(rms_norm(xc, g_pre_mix[l]), csh1, csc1)
        y_l, y_c = token_mixers(h_l, h_c, cos, sin, w_in[l], lru_conv_w[l], lru_conv_b[l],
                                lru_w_a[l], lru_b_a[l], lru_w_x[l], lru_b_x[l], lru_lambda[l],
                                mla_g_q[l], mla_w_uq[l], mla_g_kv[l], mla_w_ukv[l], w_out[l],
                                not last)
        x = x + gt1 * rms_norm(y_l, g_post_mix[l])
        f_l = conv_ffn(modulate(rms_norm(x, g_pre_ffn[l]), sh2, sc2),
                       ffn_w_up[l], ffn_conv_w[l], ffn_conv_b[l], ffn_w_down[l])
        x = x + gt2 * rms_norm(f_l, g_post_ffn[l])

        if not last:
            xc = xc + cgt1 * rms_norm(y_c, g_post_mix[l])
            f_c = conv_ffn(modulate(rms_norm(xc, g_pre_ffn[l]), csh2, csc2),
                           ffn_w_up[l], ffn_conv_w[l], ffn_conv_b[l], ffn_w_down[l])
            xc = xc + cgt2 * rms_norm(f_c, g_post_ffn[l])
    return x


import jax as _jax
import jax.numpy as _jnp

TWIN_FORMAT = 'train_step'
FWD_PARAMS = ['x', 'c', 'ctx', 'c_ctx', 'w_mod', 'b_mod', 'g_pre_mix', 'g_post_mix', 'g_pre_ffn', 'g_post_ffn', 'w_in', 'lru_conv_w', 'lru_conv_b', 'lru_w_a', 'lru_b_a', 'lru_w_x', 'lru_b_x', 'lru_lambda', 'mla_g_q', 'mla_w_uq', 'mla_g_kv', 'mla_w_ukv', 'w_out', 'ffn_w_up', 'ffn_conv_w', 'ffn_conv_b', 'ffn_w_down']
TWIN_WEIGHTS = ['c_ctx', 'w_mod', 'b_mod', 'g_pre_mix', 'g_post_mix', 'g_pre_ffn', 'g_post_ffn', 'w_in', 'lru_conv_w', 'lru_conv_b', 'lru_w_a', 'lru_b_a', 'lru_w_x', 'lru_b_x', 'lru_lambda', 'mla_g_q', 'mla_w_uq', 'mla_g_kv', 'mla_w_ukv', 'w_out', 'ffn_w_up', 'ffn_conv_w', 'ffn_conv_b', 'ffn_w_down']
TWIN_DIFF_INPUT = 'x'
TWIN_INPUTS = ['x', 'c', 'ctx', 'c_ctx', 'w_mod', 'b_mod', 'g_pre_mix', 'g_post_mix', 'g_pre_ffn', 'g_post_ffn', 'w_in', 'lru_conv_w', 'lru_conv_b', 'lru_w_a', 'lru_b_a', 'lru_w_x', 'lru_b_x', 'lru_lambda', 'mla_g_q', 'mla_w_uq', 'mla_g_kv', 'mla_w_ukv', 'w_out', 'ffn_w_up', 'ffn_conv_w', 'ffn_conv_b', 'ffn_w_down', 'loss_target', 'm_c_ctx', 'm_w_mod', 'm_b_mod', 'm_g_pre_mix', 'm_g_post_mix', 'm_g_pre_ffn', 'm_g_post_ffn', 'm_w_in', 'm_lru_conv_w', 'm_lru_conv_b', 'm_lru_w_a', 'm_lru_b_a', 'm_lru_w_x', 'm_lru_b_x', 'm_lru_lambda', 'm_mla_g_q', 'm_mla_w_uq', 'm_mla_g_kv', 'm_mla_w_ukv', 'm_w_out', 'm_ffn_w_up', 'm_ffn_conv_w', 'm_ffn_conv_b', 'm_ffn_w_down', 'v_c_ctx', 'v_w_mod', 'v_b_mod', 'v_g_pre_mix', 'v_g_post_mix', 'v_g_pre_ffn', 'v_g_post_ffn', 'v_w_in', 'v_lru_conv_w', 'v_lru_conv_b', 'v_lru_w_a', 'v_lru_b_a', 'v_lru_w_x', 'v_lru_b_x', 'v_lru_lambda', 'v_mla_g_q', 'v_mla_w_uq', 'v_mla_g_kv', 'v_mla_w_ukv', 'v_w_out', 'v_ffn_w_up', 'v_ffn_conv_w', 'v_ffn_conv_b', 'v_ffn_w_down']
TWIN_OUTPUTS = ['loss', 'grad_x', 'grad_c_ctx', 'grad_w_mod', 'grad_b_mod', 'grad_g_pre_mix', 'grad_g_post_mix', 'grad_g_pre_ffn', 'grad_g_post_ffn', 'grad_w_in', 'grad_lru_conv_w', 'grad_lru_conv_b', 'grad_lru_w_a', 'grad_lru_b_a', 'grad_lru_w_x', 'grad_lru_b_x', 'grad_lru_lambda', 'grad_mla_g_q', 'grad_mla_w_uq', 'grad_mla_g_kv', 'grad_mla_w_ukv', 'grad_w_out', 'grad_ffn_w_up', 'grad_ffn_conv_w', 'grad_ffn_conv_b', 'grad_ffn_w_down', 'delta_c_ctx', 'delta_w_mod', 'delta_b_mod', 'delta_g_pre_mix', 'delta_g_post_mix', 'delta_g_pre_ffn', 'delta_g_post_ffn', 'delta_w_in', 'delta_lru_conv_w', 'delta_lru_conv_b', 'delta_lru_w_a', 'delta_lru_b_a', 'delta_lru_w_x', 'delta_lru_b_x', 'delta_lru_lambda', 'delta_mla_g_q', 'delta_mla_w_uq', 'delta_mla_g_kv', 'delta_mla_w_ukv', 'delta_w_out', 'delta_ffn_w_up', 'delta_ffn_conv_w', 'delta_ffn_conv_b', 'delta_ffn_w_down', 'new_m_c_ctx', 'new_m_w_mod', 'new_m_b_mod', 'new_m_g_pre_mix', 'new_m_g_post_mix', 'new_m_g_pre_ffn', 'new_m_g_post_ffn', 'new_m_w_in', 'new_m_lru_conv_w', 'new_m_lru_conv_b', 'new_m_lru_w_a', 'new_m_lru_b_a', 'new_m_lru_w_x', 'new_m_lru_b_x', 'new_m_lru_lambda', 'new_m_mla_g_q', 'new_m_mla_w_uq', 'new_m_mla_g_kv', 'new_m_mla_w_ukv', 'new_m_w_out', 'new_m_ffn_w_up', 'new_m_ffn_conv_w', 'new_m_ffn_conv_b', 'new_m_ffn_w_down', 'new_v_c_ctx', 'new_v_w_mod', 'new_v_b_mod', 'new_v_g_pre_mix', 'new_v_g_post_mix', 'new_v_g_pre_ffn', 'new_v_g_post_ffn', 'new_v_w_in', 'new_v_lru_conv_w', 'new_v_lru_conv_b', 'new_v_lru_w_a', 'new_v_lru_b_a', 'new_v_lru_w_x', 'new_v_lru_b_x', 'new_v_lru_lambda', 'new_v_mla_g_q', 'new_v_mla_w_uq', 'new_v_mla_g_kv', 'new_v_mla_w_ukv', 'new_v_w_out', 'new_v_ffn_w_up', 'new_v_ffn_conv_w', 'new_v_ffn_conv_b', 'new_v_ffn_w_down']
TWIN_LEAF_KINDS = {'loss': 'loss', 'grad_x': 'grad_x', 'grad_c_ctx': 'grad_w', 'grad_w_mod': 'grad_w', 'grad_b_mod': 'grad_w', 'grad_g_pre_mix': 'grad_w', 'grad_g_post_mix': 'grad_w', 'grad_g_pre_ffn': 'grad_w', 'grad_g_post_ffn': 'grad_w', 'grad_w_in': 'grad_w', 'grad_lru_conv_w': 'grad_w', 'grad_lru_conv_b': 'grad_w', 'grad_lru_w_a': 'grad_w', 'grad_lru_b_a': 'grad_w', 'grad_lru_w_x': 'grad_w', 'grad_lru_b_x': 'grad_w', 'grad_lru_lambda': 'grad_w', 'grad_mla_g_q': 'grad_w', 'grad_mla_w_uq': 'grad_w', 'grad_mla_g_kv': 'grad_w', 'grad_mla_w_ukv': 'grad_w', 'grad_w_out': 'grad_w', 'grad_ffn_w_up': 'grad_w', 'grad_ffn_conv_w': 'grad_w', 'grad_ffn_conv_b': 'grad_w', 'grad_ffn_w_down': 'grad_w', 'delta_c_ctx': 'delta_w', 'delta_w_mod': 'delta_w', 'delta_b_mod': 'delta_w', 'delta_g_pre_mix': 'delta_w', 'delta_g_post_mix': 'delta_w', 'delta_g_pre_ffn': 'delta_w', 'delta_g_post_ffn': 'delta_w', 'delta_w_in': 'delta_w', 'delta_lru_conv_w': 'delta_w', 'delta_lru_conv_b': 'delta_w', 'delta_lru_w_a': 'delta_w', 'delta_lru_b_a': 'delta_w', 'delta_lru_w_x': 'delta_w', 'delta_lru_b_x': 'delta_w', 'delta_lru_lambda': 'delta_w', 'delta_mla_g_q': 'delta_w', 'delta_mla_w_uq': 'delta_w', 'delta_mla_g_kv': 'delta_w', 'delta_mla_w_ukv': 'delta_w', 'delta_w_out': 'delta_w', 'delta_ffn_w_up': 'delta_w', 'delta_ffn_conv_w': 'delta_w', 'delta_ffn_conv_b': 'delta_w', 'delta_ffn_w_down': 'delta_w', 'new_m_c_ctx': 'new_m', 'new_m_w_mod': 'new_m', 'new_m_b_mod': 'new_m', 'new_m_g_pre_mix': 'new_m', 'new_m_g_post_mix': 'new_m', 'new_m_g_pre_ffn': 'new_m', 'new_m_g_post_ffn': 'new_m', 'new_m_w_in': 'new_m', 'new_m_lru_conv_w': 'new_m', 'new_m_lru_conv_b': 'new_m', 'new_m_lru_w_a': 'new_m', 'new_m_lru_b_a': 'new_m', 'new_m_lru_w_x': 'new_m', 'new_m_lru_b_x': 'new_m', 'new_m_lru_lambda': 'new_m', 'new_m_mla_g_q': 'new_m', 'new_m_mla_w_uq': 'new_m', 'new_m_mla_g_kv': 'new_m', 'new_m_mla_w_ukv': 'new_m', 'new_m_w_out': 'new_m', 'new_m_ffn_w_up': 'new_m', 'new_m_ffn_conv_w': 'new_m', 'new_m_ffn_conv_b': 'new_m', 'new_m_ffn_w_down': 'new_m', 'new_v_c_ctx': 'new_v', 'new_v_w_mod': 'new_v', 'new_v_b_mod': 'new_v', 'new_v_g_pre_mix': 'new_v', 'new_v_g_post_mix': 'new_v', 'new_v_g_pre_ffn': 'new_v', 'new_v_g_post_ffn': 'new_v', 'new_v_w_in': 'new_v', 'new_v_lru_conv_w': 'new_v', 'new_v_lru_conv_b': 'new_v', 'new_v_lru_w_a': 'new_v', 'new_v_lru_b_a': 'new_v', 'new_v_lru_w_x': 'new_v', 'new_v_lru_b_x': 'new_v', 'new_v_lru_lambda': 'new_v', 'new_v_mla_g_q': 'new_v', 'new_v_mla_w_uq': 'new_v', 'new_v_mla_g_kv': 'new_v', 'new_v_mla_w_ukv': 'new_v', 'new_v_w_out': 'new_v', 'new_v_ffn_w_up': 'new_v', 'new_v_ffn_conv_w': 'new_v', 'new_v_ffn_conv_b': 'new_v', 'new_v_ffn_w_down': 'new_v'}


def _forward(args):
    return _fwd_reference(*[args[k] for k in FWD_PARAMS])


def _output_shape():
    out = _jax.eval_shape(lambda: _forward(_fwd_setup_inputs(0)))
    return out.shape, out.dtype

N_MICROBATCH = 1
ADAM_LR = 0.001
ADAM_B1 = 0.9
ADAM_B2 = 0.999
ADAM_EPS = 1e-08
ADAM_WD = 0.01
ADAM_STEP = 10
PER_EXAMPLE_BATCH_AXIS = {'x': 0, 'c': 0, 'ctx': 0, 'loss_target': 0}
SHARED_INPUTS = []
_WEIGHT_DTYPES = {'c_ctx': _jnp.float32, 'w_mod': _jnp.float32, 'b_mod': _jnp.float32, 'g_pre_mix': _jnp.float32, 'g_post_mix': _jnp.float32, 'g_pre_ffn': _jnp.float32, 'g_post_ffn': _jnp.float32, 'w_in': _jnp.float32, 'lru_conv_w': _jnp.float32, 'lru_conv_b': _jnp.float32, 'lru_w_a': _jnp.float32, 'lru_b_a': _jnp.float32, 'lru_w_x': _jnp.float32, 'lru_b_x': _jnp.float32, 'lru_lambda': _jnp.float32, 'mla_g_q': _jnp.float32, 'mla_w_uq': _jnp.float32, 'mla_g_kv': _jnp.float32, 'mla_w_ukv': _jnp.float32, 'w_out': _jnp.float32, 'ffn_w_up': _jnp.float32, 'ffn_conv_w': _jnp.float32, 'ffn_conv_b': _jnp.float32, 'ffn_w_down': _jnp.float32}
MOMENT_SCALE = {'c_ctx': 3.903511e-01, 'w_mod': 8.313624e+00, 'b_mod': 1.455685e+01, 'g_pre_mix': 5.514079e-01, 'g_post_mix': 3.013407e+01, 'g_pre_ffn': 1.206416e+00, 'g_post_ffn': 3.105782e+01, 'w_in': 5.868079e+00, 'lru_conv_w': 9.223060e+00, 'lru_conv_b': 1.313927e+01, 'lru_w_a': 4.428248e-01, 'lru_b_a': 4.856444e-01, 'lru_w_x': 1.188397e+00, 'lru_b_x': 1.592706e+00, 'lru_lambda': 1.436853e+00, 'mla_g_q': 7.246560e-02, 'mla_w_uq': 4.395984e-02, 'mla_g_kv': 1.622087e+00, 'mla_w_ukv': 6.339249e-01, 'w_out': 6.060946e+00, 'ffn_w_up': 1.268033e+00, 'ffn_conv_w': 1.445755e+00, 'ffn_conv_b': 1.889148e+00, 'ffn_w_down': 2.614150e+00}


def _to_microbatches(a, axis):
    t = _jnp.moveaxis(a, axis, 0)
    t = t.reshape((N_MICROBATCH, t.shape[0] // N_MICROBATCH) + t.shape[1:])
    return _jnp.moveaxis(t, 1, axis + 1)


def setup_inputs(seed: int = 0) -> dict:
    inp = _fwd_setup_inputs(seed)
    key = _jax.random.fold_in(_jax.random.key(seed), 7919)
    shape, _ = _output_shape()
    out = dict(inp)
    out["loss_target"] = _jax.random.normal(_jax.random.fold_in(key, 0), shape, _jnp.float32)
    for i, name in enumerate(TWIN_WEIGHTS):
        w = inp[name].astype(_jnp.float32)
        if MOMENT_SCALE is None:
            s = _jnp.sqrt(_jnp.mean(_jnp.square(w)) + 1e-30)
        else:
            s = MOMENT_SCALE[name]
        km, kv = _jax.random.split(_jax.random.fold_in(key, i + 1))
        out[name] = w
        out["m_" + name] = s * _jax.random.normal(km, w.shape, _jnp.float32)
        out["v_" + name] = (s * s) * _jax.random.uniform(kv, w.shape, _jnp.float32, 0.5, 1.5)
    if N_MICROBATCH > 1:
        for name, axis in PER_EXAMPLE_BATCH_AXIS.items():
            out[name] = _to_microbatches(out[name], axis)
    return {'x': out['x'], 'c': out['c'], 'ctx': out['ctx'], 'c_ctx': out['c_ctx'], 'w_mod': out['w_mod'], 'b_mod': out['b_mod'], 'g_pre_mix': out['g_pre_mix'], 'g_post_mix': out['g_post_mix'], 'g_pre_ffn': out['g_pre_ffn'], 'g_post_ffn': out['g_post_ffn'], 'w_in': out['w_in'], 'lru_conv_w': out['lru_conv_w'], 'lru_conv_b': out['lru_conv_b'], 'lru_w_a': out['lru_w_a'], 'lru_b_a': out['lru_b_a'], 'lru_w_x': out['lru_w_x'], 'lru_b_x': out['lru_b_x'], 'lru_lambda': out['lru_lambda'], 'mla_g_q': out['mla_g_q'], 'mla_w_uq': out['mla_w_uq'], 'mla_g_kv': out['mla_g_kv'], 'mla_w_ukv': out['mla_w_ukv'], 'w_out': out['w_out'], 'ffn_w_up': out['ffn_w_up'], 'ffn_conv_w': out['ffn_conv_w'], 'ffn_conv_b': out['ffn_conv_b'], 'ffn_w_down': out['ffn_w_down'], 'loss_target': out['loss_target'], 'm_c_ctx': out['m_c_ctx'], 'm_w_mod': out['m_w_mod'], 'm_b_mod': out['m_b_mod'], 'm_g_pre_mix': out['m_g_pre_mix'], 'm_g_post_mix': out['m_g_post_mix'], 'm_g_pre_ffn': out['m_g_pre_ffn'], 'm_g_post_ffn': out['m_g_post_ffn'], 'm_w_in': out['m_w_in'], 'm_lru_conv_w': out['m_lru_conv_w'], 'm_lru_conv_b': out['m_lru_conv_b'], 'm_lru_w_a': out['m_lru_w_a'], 'm_lru_b_a': out['m_lru_b_a'], 'm_lru_w_x': out['m_lru_w_x'], 'm_lru_b_x': out['m_lru_b_x'], 'm_lru_lambda': out['m_lru_lambda'], 'm_mla_g_q': out['m_mla_g_q'], 'm_mla_w_uq': out['m_mla_w_uq'], 'm_mla_g_kv': out['m_mla_g_kv'], 'm_mla_w_ukv': out['m_mla_w_ukv'], 'm_w_out': out['m_w_out'], 'm_ffn_w_up': out['m_ffn_w_up'], 'm_ffn_conv_w': out['m_ffn_conv_w'], 'm_ffn_conv_b': out['m_ffn_conv_b'], 'm_ffn_w_down': out['m_ffn_w_down'], 'v_c_ctx': out['v_c_ctx'], 'v_w_mod': out['v_w_mod'], 'v_b_mod': out['v_b_mod'], 'v_g_pre_mix': out['v_g_pre_mix'], 'v_g_post_mix': out['v_g_post_mix'], 'v_g_pre_ffn': out['v_g_pre_ffn'], 'v_g_post_ffn': out['v_g_post_ffn'], 'v_w_in': out['v_w_in'], 'v_lru_conv_w': out['v_lru_conv_w'], 'v_lru_conv_b': out['v_lru_conv_b'], 'v_lru_w_a': out['v_lru_w_a'], 'v_lru_b_a': out['v_lru_b_a'], 'v_lru_w_x': out['v_lru_w_x'], 'v_lru_b_x': out['v_lru_b_x'], 'v_lru_lambda': out['v_lru_lambda'], 'v_mla_g_q': out['v_mla_g_q'], 'v_mla_w_uq': out['v_mla_w_uq'], 'v_mla_g_kv': out['v_mla_g_kv'], 'v_mla_w_ukv': out['v_mla_w_ukv'], 'v_w_out': out['v_w_out'], 'v_ffn_w_up': out['v_ffn_w_up'], 'v_ffn_conv_w': out['v_ffn_conv_w'], 'v_ffn_conv_b': out['v_ffn_conv_b'], 'v_ffn_w_down': out['v_ffn_w_down']}


def _loss(weights, diff, rest, loss_target):
    with _jax.named_scope("forward"):
        args = {**rest, TWIN_DIFF_INPUT: diff, **{k: w.astype(_WEIGHT_DTYPES[k]) for k, w in weights.items()}}
        y = _forward(args)
    with _jax.named_scope("loss_head"):
        err = _jnp.square(y.astype(_jnp.float32) - loss_target)
        return 0.5 * _jnp.sum(_jnp.mean(err, axis=-1)) if err.ndim else 0.5 * err


def _adamw(w, g, m, v):
    m = ADAM_B1 * m + (1.0 - ADAM_B1) * g
    v = ADAM_B2 * v + (1.0 - ADAM_B2) * _jnp.square(g)
    m_hat = m / (1.0 - ADAM_B1 ** ADAM_STEP)
    v_hat = v / (1.0 - ADAM_B2 ** ADAM_STEP)
    delta = -ADAM_LR * (m_hat / (_jnp.sqrt(v_hat) + ADAM_EPS) + ADAM_WD * w)
    return delta, m, v


def reference(x, c, ctx, c_ctx, w_mod, b_mod, g_pre_mix, g_post_mix, g_pre_ffn, g_post_ffn, w_in, lru_conv_w, lru_conv_b, lru_w_a, lru_b_a, lru_w_x, lru_b_x, lru_lambda, mla_g_q, mla_w_uq, mla_g_kv, mla_w_ukv, w_out, ffn_w_up, ffn_conv_w, ffn_conv_b, ffn_w_down, loss_target, m_c_ctx, m_w_mod, m_b_mod, m_g_pre_mix, m_g_post_mix, m_g_pre_ffn, m_g_post_ffn, m_w_in, m_lru_conv_w, m_lru_conv_b, m_lru_w_a, m_lru_b_a, m_lru_w_x, m_lru_b_x, m_lru_lambda, m_mla_g_q, m_mla_w_uq, m_mla_g_kv, m_mla_w_ukv, m_w_out, m_ffn_w_up, m_ffn_conv_w, m_ffn_conv_b, m_ffn_w_down, v_c_ctx, v_w_mod, v_b_mod, v_g_pre_mix, v_g_post_mix, v_g_pre_ffn, v_g_post_ffn, v_w_in, v_lru_conv_w, v_lru_conv_b, v_lru_w_a, v_lru_b_a, v_lru_w_x, v_lru_b_x, v_lru_lambda, v_mla_g_q, v_mla_w_uq, v_mla_g_kv, v_mla_w_ukv, v_w_out, v_ffn_w_up, v_ffn_conv_w, v_ffn_conv_b, v_ffn_w_down):
    given = dict(x=x, c=c, ctx=ctx, c_ctx=c_ctx, w_mod=w_mod, b_mod=b_mod, g_pre_mix=g_pre_mix, g_post_mix=g_post_mix, g_pre_ffn=g_pre_ffn, g_post_ffn=g_post_ffn, w_in=w_in, lru_conv_w=lru_conv_w, lru_conv_b=lru_conv_b, lru_w_a=lru_w_a, lru_b_a=lru_b_a, lru_w_x=lru_w_x, lru_b_x=lru_b_x, lru_lambda=lru_lambda, mla_g_q=mla_g_q, mla_w_uq=mla_w_uq, mla_g_kv=mla_g_kv, mla_w_ukv=mla_w_ukv, w_out=w_out, ffn_w_up=ffn_w_up, ffn_conv_w=ffn_conv_w, ffn_conv_b=ffn_conv_b, ffn_w_down=ffn_w_down, loss_target=loss_target, m_c_ctx=m_c_ctx, m_w_mod=m_w_mod, m_b_mod=m_b_mod, m_g_pre_mix=m_g_pre_mix, m_g_post_mix=m_g_post_mix, m_g_pre_ffn=m_g_pre_ffn, m_g_post_ffn=m_g_post_ffn, m_w_in=m_w_in, m_lru_conv_w=m_lru_conv_w, m_lru_conv_b=m_lru_conv_b, m_lru_w_a=m_lru_w_a, m_lru_b_a=m_lru_b_a, m_lru_w_x=m_lru_w_x, m_lru_b_x=m_lru_b_x, m_lru_lambda=m_lru_lambda, m_mla_g_q=m_mla_g_q, m_mla_w_uq=m_mla_w_uq, m_mla_g_kv=m_mla_g_kv, m_mla_w_ukv=m_mla_w_ukv, m_w_out=m_w_out, m_ffn_w_up=m_ffn_w_up, m_ffn_conv_w=m_ffn_conv_w, m_ffn_conv_b=m_ffn_conv_b, m_ffn_w_down=m_ffn_w_down, v_c_ctx=v_c_ctx, v_w_mod=v_w_mod, v_b_mod=v_b_mod, v_g_pre_mix=v_g_pre_mix, v_g_post_mix=v_g_post_mix, v_g_pre_ffn=v_g_pre_ffn, v_g_post_ffn=v_g_post_ffn, v_w_in=v_w_in, v_lru_conv_w=v_lru_conv_w, v_lru_conv_b=v_lru_conv_b, v_lru_w_a=v_lru_w_a, v_lru_b_a=v_lru_b_a, v_lru_w_x=v_lru_w_x, v_lru_b_x=v_lru_b_x, v_lru_lambda=v_lru_lambda, v_mla_g_q=v_mla_g_q, v_mla_w_uq=v_mla_w_uq, v_mla_g_kv=v_mla_g_kv, v_mla_w_ukv=v_mla_w_ukv, v_w_out=v_w_out, v_ffn_w_up=v_ffn_w_up, v_ffn_conv_w=v_ffn_conv_w, v_ffn_conv_b=v_ffn_conv_b, v_ffn_w_down=v_ffn_w_down)
    weights = {n: given[n] for n in TWIN_WEIGHTS}
    shared = {n: given[n] for n in SHARED_INPUTS}
    per_example = {n: given[n] for n in ['x', 'c', 'ctx']}
    grad_fn = _jax.value_and_grad(_loss, argnums=(0, 1))

    def one_microbatch(ex, loss_target):
        ex = dict(ex)
        diff = ex.pop(TWIN_DIFF_INPUT)
        return grad_fn(weights, diff, {**shared, **ex}, loss_target)

    if N_MICROBATCH == 1:
        loss, (grad_w, grad_x) = one_microbatch(per_example, given["loss_target"])
    else:
        def body(carry, xs):
            loss_sum, grad_sum = carry
            l_k, (gw_k, gx_k) = one_microbatch(xs[0], xs[1])
            with _jax.named_scope("update"):
                return (loss_sum + l_k, _jax.tree.map(_jnp.add, grad_sum, gw_k)), gx_k

        init = (_jnp.zeros((), _jnp.float32), _jax.tree.map(_jnp.zeros_like, weights))
        (loss, grad_w), grad_x = _jax.lax.scan(body, init, (per_example, given["loss_target"]))
    with _jax.named_scope("update"):
        delta_w, new_m, new_v = {}, {}, {}
        for n in TWIN_WEIGHTS:
            delta_w[n], new_m[n], new_v[n] = _adamw(weights[n], grad_w[n], given["m_" + n], given["v_" + n])
    return (loss, grad_x, *[grad_w[n] for n in TWIN_WEIGHTS], *[delta_w[n] for n in TWIN_WEIGHTS],
            *[new_m[n] for n in TWIN_WEIGHTS], *[new_v[n] for n in TWIN_WEIGHTS])
```

```python
import functools
import math

import jax
import jax.numpy as jnp
from jax import lax
from jax.experimental import pallas as pl
from jax.experimental.pallas import tpu as pltpu

F32 = jnp.float32
BF16 = jnp.bfloat16
MESH = pl.DeviceIdType.MESH

N_DEV = 8
ROW_TILE = 256
SUBLANES = 8
LANES = 128
VMEM_LIMIT = 56 * 1024 * 1024

D_MODEL = 1024
LRU_W = 512
LRU_HEADS = 8
LRU_CONV_K = 4
LRU_CONV_LEFT = 2
LRU_C = 8.0
HEADS = 8
NOPE = 64
ROPE = 32
VDIM = 64
QK = NOPE + ROPE
HEAD_PAD = 128
Q_RANK = 256
KV_RANK = 128
MLA_SCALE = QK ** -0.5
ROPE_PAIRS = ROPE // 4
ROPE_BASE = 10000.0
GRID_W = 64
D_FF = 2816
FFN_CONV_K = 3
FFN_CONV_LEFT = 1
N_MOD = 6
EPS = 1e-6
IN_W = 2 * LRU_W + Q_RANK + KV_RANK + ROPE
IN_W_PAD = 2 * LRU_W + Q_RANK + KV_RANK + HEAD_PAD
OFF_GR, OFF_CQ, OFF_CKV, OFF_KR = LRU_W, 2 * LRU_W, 2 * LRU_W + Q_RANK, 2 * LRU_W + Q_RANK + KV_RANK

ADAM_LR, ADAM_B1, ADAM_B2, ADAM_EPS, ADAM_WD, ADAM_STEP = 0.001, 0.9, 0.999, 1e-08, 0.01, 10

WEIGHTS = ['c_ctx', 'w_mod', 'b_mod', 'g_pre_mix', 'g_post_mix', 'g_pre_ffn', 'g_post_ffn', 'w_in', 'lru_conv_w',
           'lru_conv_b', 'lru_w_a', 'lru_b_a', 'lru_w_x', 'lru_b_x', 'lru_lambda', 'mla_g_q', 'mla_w_uq', 'mla_g_kv',
           'mla_w_ukv', 'w_out', 'ffn_w_up', 'ffn_conv_w', 'ffn_conv_b', 'ffn_w_down']
REPLICATED = ['c_ctx', 'b_mod', 'g_pre_mix', 'g_post_mix', 'g_pre_ffn', 'g_post_ffn', 'lru_conv_b', 'lru_w_a',
              'lru_w_x', 'mla_g_q', 'mla_g_kv', 'ffn_conv_b']
SMALL_SHARDED = ['lru_conv_w', 'lru_b_a', 'lru_b_x', 'lru_lambda', 'ffn_conv_w']


def _pick(d, prefs):
    for p in prefs:
        if d % p == 0:
            return p
    return d


def _params(sem=None):
    return pltpu.CompilerParams(dimension_semantics=sem, vmem_limit_bytes=VMEM_LIMIT)


def _mm(name, a, b, mode, out_dtype=F32, tm=512, tn=512, tk=1024):
    if mode == 'nn':
        (m, k), (_, n) = a.shape, b.shape
    elif mode == 'nt':
        (m, k), (n, _) = a.shape, b.shape
    else:
        (k, m), (_, n) = a.shape, b.shape
    tm = _pick(m, (tm, 256, 128))
    tn = _pick(n, (tn, 256, 128))
    tk = _pick(k, (tk, 512, 256, 128))
    nk = k // tk
    if mode == 'nn':
        a_spec = pl.BlockSpec((tm, tk), lambda i, j, kk: (i, kk))
        b_spec = pl.BlockSpec((tk, tn), lambda i, j, kk: (kk, j))
        dn = (((1,), (0,)), ((), ()))
    elif mode == 'nt':
        a_spec = pl.BlockSpec((tm, tk), lambda i, j, kk: (i, kk))
        b_spec = pl.BlockSpec((tn, tk), lambda i, j, kk: (j, kk))
        dn = (((1,), (1,)), ((), ()))
    else:
        a_spec = pl.BlockSpec((tk, tm), lambda i, j, kk: (kk, i))
        b_spec = pl.BlockSpec((tk, tn), lambda i, j, kk: (kk, j))
        dn = (((0,), (0,)), ((), ()))

    def body(a_ref, b_ref, o_ref, acc_ref):
        kk = pl.program_id(2)

        @pl.when(kk == 0)
        def _():
            acc_ref[...] = jnp.zeros_like(acc_ref)

        acc_ref[...] += lax.dot_general(a_ref[...].astype(BF16), b_ref[...].astype(BF16), dn,
                                        preferred_element_type=F32)

        @pl.when(kk == nk - 1)
        def _():
            o_ref[...] = acc_ref[...].astype(o_ref.dtype)

    return pl.pallas_call(
        body, name=name, grid=(m // tm, n // tn, nk),
        in_specs=[a_spec, b_spec], out_specs=pl.BlockSpec((tm, tn), lambda i, j, kk: (i, j)),
        out_shape=jax.ShapeDtypeStruct((m, n), out_dtype),
        scratch_shapes=[pltpu.VMEM((tm, tn), F32)],
        compiler_params=_params(("parallel", "parallel", "arbitrary")),
    )(a, b)


def _row(a, idx=None, col=None):
    return dict(a=a, idx=idx, col=col)


def _rowwise(name, fn, n_tiles, rows, bcast, out_rows, out_acc=(), tm=ROW_TILE):
    in_specs = []
    for r in rows:
        a, idx, col = r['a'], r['idx'] or (lambda i: i), r['col']
        if a.ndim == 2:
            w, ci = col if col else (a.shape[1], 0)
            in_specs.append(pl.BlockSpec((tm, w), lambda i, idx=idx, ci=ci: (idx(i), ci)))
        else:
            in_specs.append(pl.BlockSpec((a.shape[0], tm, a.shape[2]), lambda i, idx=idx: (0, idx(i), 0)))
    for b in bcast:
        in_specs.append(pl.BlockSpec(b.shape, lambda i, nd=b.ndim: (0,) * nd))
    out_specs, out_shape = [], []
    for shape, dtype, idx in out_rows:
        idx = idx or (lambda i: i)
        if len(shape) == 2:
            out_specs.append(pl.BlockSpec((tm, shape[1]), lambda i, idx=idx: (idx(i), 0)))
        else:
            out_specs.append(pl.BlockSpec((shape[0], tm, shape[2]), lambda i, idx=idx: (0, idx(i), 0)))
        out_shape.append(jax.ShapeDtypeStruct(shape, dtype))
    for shape in out_acc:
        out_specs.append(pl.BlockSpec(shape, lambda i, nd=len(shape): (0,) * nd))
        out_shape.append(jax.ShapeDtypeStruct(shape, F32))
    nr, nb, no = len(rows), len(bcast), len(out_rows)

    def body(*refs):
        i = pl.program_id(0)
        rvals = [r[...] for r in refs[:nr]]
        bvals = list(refs[nr:nr + nb])
        o_rows, o_acc = fn(i, rvals, bvals)
        for ref, v in zip(refs[nr + nb:nr + nb + no], o_rows):
            ref[...] = v.astype(ref.dtype)
        acc_refs = refs[nr + nb + no:]
        if acc_refs:
            @pl.when(i == 0)
            def _():
                for ref in acc_refs:
                    ref[...] = jnp.zeros_like(ref)
            for ref, v in zip(acc_refs, o_acc):
                ref[...] += v

    return pl.pallas_call(
        body, name=name, grid=(n_tiles,), in_specs=in_specs, out_specs=out_specs, out_shape=out_shape,
        compiler_params=_params(("arbitrary",)),
    )(*[r['a'] for r in rows], *bcast)


def _single(name, fn, ins, out_shapes):
    def body(*refs):
        outs = fn(*refs[:len(ins)])
        for ref, v in zip(refs[len(ins):], outs):
            ref[...] = v.astype(ref.dtype)

    return pl.pallas_call(
        body, name=name,
        in_specs=[pl.BlockSpec(memory_space=pltpu.VMEM)] * len(ins),
        out_specs=[pl.BlockSpec(memory_space=pltpu.VMEM)] * len(out_shapes),
        out_shape=[jax.ShapeDtypeStruct(s, d) for s, d in out_shapes],
        compiler_params=_params(),
    )(*ins)


def _bc(p, n):
    return jnp.broadcast_to(p, (n, p.shape[-1]))


def _rs(g):
    return jnp.sum(g, axis=0, keepdims=True)


def _rms(x, g):
    return x * lax.rsqrt(jnp.mean(x * x, axis=-1, keepdims=True) + EPS) * g


def _conv_specs(r, cw, tm):
    t8 = tm // SUBLANES
    last8 = r // SUBLANES - 1
    prev = pl.BlockSpec((SUBLANES, cw), lambda c, i: (jnp.maximum(i * t8 - 1, 0), c))
    cur = pl.BlockSpec((tm, cw), lambda c, i: (i, c))
    nxt = pl.BlockSpec((SUBLANES, cw), lambda c, i: (jnp.minimum((i + 1) * t8, last8), c))
    return [prev, cur, nxt]


def _fill_ext(ext_ref, prev_ref, cur_ref, next_ref, i, n_tiles, seg_starts, tm):
    prev_ok = functools.reduce(jnp.logical_and, [i != s for s in seg_starts])
    next_ok = functools.reduce(jnp.logical_and, [i + 1 != s for s in seg_starts] + [i + 1 < n_tiles])
    ext_ref[0:SUBLANES, :] = jnp.where(prev_ok, prev_ref[...].astype(F32), 0.0)
    ext_ref[SUBLANES:SUBLANES + tm, :] = cur_ref[...].astype(F32)
    ext_ref[SUBLANES + tm:, :] = jnp.where(next_ok, next_ref[...].astype(F32), 0.0)


def _dwconv_fwd(name, x, w, b, left, seg_starts, cw=512, tm=ROW_TILE):
    r, c = x.shape
    kw = w.shape[0]
    n_tiles = r // tm

    def body(prev_ref, cur_ref, next_ref, w_ref, b_ref, o_ref, ext_ref):
        i = pl.program_id(1)
        _fill_ext(ext_ref, prev_ref, cur_ref, next_ref, i, n_tiles, seg_starts, tm)
        out = jnp.broadcast_to(b_ref[...], (tm, cw))
        for k in range(kw):
            out = out + ext_ref[pl.ds(SUBLANES + k - left, tm), :] * w_ref[k:k + 1, :]
        o_ref[...] = out

    return pl.pallas_call(
        body, name=name, grid=(c // cw, n_tiles),
        in_specs=_conv_specs(r, cw, tm) + [pl.BlockSpec((kw, cw), lambda c_, i: (0, c_)),
                                           pl.BlockSpec((1, cw), lambda c_, i: (0, c_))],
        out_specs=pl.BlockSpec((tm, cw), lambda c_, i: (i, c_)),
        out_shape=jax.ShapeDtypeStruct((r, c), F32),
        scratch_shapes=[pltpu.VMEM((tm + 2 * SUBLANES, cw), F32)],
        compiler_params=_params(("parallel", "arbitrary")),
    )(x, x, x, w, b)


def _dwconv_bwd(name, x, dy, w, left, seg_starts, out_dtype=F32, cw=512, tm=ROW_TILE):
    r, c = x.shape
    kw = w.shape[0]
    n_tiles = r // tm

    def body(xp, xc, xn, dp, dc, dn, w_ref, dx_ref, dw_ref, db_ref, xe_ref, de_ref):
        i = pl.program_id(1)
        _fill_ext(xe_ref, xp, xc, xn, i, n_tiles, seg_starts, tm)
        _fill_ext(de_ref, dp, dc, dn, i, n_tiles, seg_starts, tm)
        dyc = dc[...].astype(F32)
        dx = jnp.zeros((tm, cw), F32)
        dws = []
        for k in range(kw):
            dx = dx + de_ref[pl.ds(SUBLANES - k + left, tm), :] * w_ref[k:k + 1, :]
            dws.append(jnp.sum(dyc * xe_ref[pl.ds(SUBLANES + k - left, tm), :], axis=0, keepdims=True))
        dx_ref[...] = dx.astype(dx_ref.dtype)

        @pl.when(i == 0)
        def _():
            dw_ref[...] = jnp.zeros_like(dw_ref)
            db_ref[...] = jnp.zeros_like(db_ref)

        for k in range(kw):
            dw_ref[k:k + 1, :] += dws[k]
        db_ref[...] += jnp.sum(dyc, axis=0, keepdims=True)

    return pl.pallas_call(
        body, name=name, grid=(c // cw, n_tiles),
        in_specs=_conv_specs(r, cw, tm) + _conv_specs(r, cw, tm) + [pl.BlockSpec((kw, cw), lambda c_, i: (0, c_))],
        out_specs=[pl.BlockSpec((tm, cw), lambda c_, i: (i, c_)),
                   pl.BlockSpec((kw, cw), lambda c_, i: (0, c_)),
                   pl.BlockSpec((1, cw), lambda c_, i: (0, c_))],
        out_shape=[jax.ShapeDtypeStruct((r, c), out_dtype), jax.ShapeDtypeStruct((kw, c), F32),
                   jax.ShapeDtypeStruct((1, c), F32)],
        scratch_shapes=[pltpu.VMEM((tm + 2 * SUBLANES, cw), F32), pltpu.VMEM((tm + 2 * SUBLANES, cw), F32)],
        compiler_params=_params(("parallel", "arbitrary")),
    )(x, x, x, dy, dy, dy, w)


def _scan(name, a, u, reverse):
    t, c = a.shape
    n8 = t // SUBLANES

    def body(a_ref, u_ref, h_ref):
        row = lax.broadcasted_iota(jnp.int32, (SUBLANES, LANES), 0)
        last = 0 if reverse else SUBLANES - 1

        def step(j, carry):
            blk = (n8 - 1 - j) if reverse else j
            base = pl.multiple_of(blk * SUBLANES, SUBLANES)
            av = a_ref[pl.ds(base, SUBLANES), :]
            hv = u_ref[pl.ds(base, SUBLANES), :]
            for s in (1, 2, 4):
                shift = SUBLANES - s if reverse else s
                ok = (row < SUBLANES - s) if reverse else (row >= s)
                a_sh = jnp.where(ok, pltpu.roll(av, shift, 0), 1.0)
                h_sh = jnp.where(ok, pltpu.roll(hv, shift, 0), 0.0)
                hv = av * h_sh + hv
                av = av * a_sh
            hv = av * carry + hv
            h_ref[pl.ds(base, SUBLANES), :] = hv
            return jnp.sum(jnp.where(row == last, hv, 0.0), axis=0, keepdims=True)

        lax.fori_loop(0, n8, step, jnp.zeros((1, LANES), F32))

    return pl.pallas_call(
        body, name=name, grid=(c // LANES,),
        in_specs=[pl.BlockSpec((t, LANES), lambda j: (0, j))] * 2,
        out_specs=pl.BlockSpec((t, LANES), lambda j: (0, j)),
        out_shape=jax.ShapeDtypeStruct((t, c), F32),
        compiler_params=_params(("parallel",)),
    )(a, u)


NT_DIMS = (((1,), (1,)), ((), ()))


def _attn_fwd(q, k, v, tq=256):
    h, s, _ = q.shape
    t = k.shape[1]

    def body(q_ref, k_ref, v_ref, o_ref, lse_ref):
        sc = lax.dot_general(q_ref[0], k_ref[0], NT_DIMS, preferred_element_type=F32) * MLA_SCALE
        m = jnp.max(sc, axis=-1, keepdims=True)
        p = jnp.exp(sc - m)
        l = jnp.sum(p, axis=-1, keepdims=True)
        o = jnp.dot(p.astype(BF16), v_ref[0], preferred_element_type=F32)
        o_ref[0] = o / l
        lse_ref[0] = m + jnp.log(l)

    return pl.pallas_call(
        body, name="attn_fwd", grid=(h, s // tq),
        in_specs=[pl.BlockSpec((1, tq, HEAD_PAD), lambda hh, i: (hh, i, 0)),
                  pl.BlockSpec((1, t, HEAD_PAD), lambda hh, i: (hh, 0, 0)),
                  pl.BlockSpec((1, t, VDIM), lambda hh, i: (hh, 0, 0))],
        out_specs=[pl.BlockSpec((1, tq, VDIM), lambda hh, i: (hh, i, 0)),
                   pl.BlockSpec((1, tq, 1), lambda hh, i: (hh, i, 0))],
        out_shape=[jax.ShapeDtypeStruct((h, s, VDIM), F32), jax.ShapeDtypeStruct((h, s, 1), F32)],
        compiler_params=_params(("parallel", "arbitrary")),
    )(q, k, v)


def _attn_dq(q, k, v, o, do, lse, tq=128):
    h, s, _ = q.shape
    t = k.shape[1]

    def body(q_ref, k_ref, v_ref, o_ref, do_ref, lse_ref, dq_ref, delta_ref):
        sc = lax.dot_general(q_ref[0], k_ref[0], NT_DIMS, preferred_element_type=F32) * MLA_SCALE
        p = jnp.exp(sc - lse_ref[0])
        dof = do_ref[0]
        delta = jnp.sum(dof * o_ref[0], axis=-1, keepdims=True)
        dp = lax.dot_general(dof.astype(BF16), v_ref[0], NT_DIMS, preferred_element_type=F32)
        ds = p * (dp - delta)
        dq_ref[0] = jnp.dot(ds.astype(BF16), k_ref[0], preferred_element_type=F32) * MLA_SCALE
        delta_ref[0] = delta

    return pl.pallas_call(
        body, name="attn_dq", grid=(h, s // tq),
        in_specs=[pl.BlockSpec((1, tq, HEAD_PAD), lambda hh, i: (hh, i, 0)),
                  pl.BlockSpec((1, t, HEAD_PAD), lambda hh, i: (hh, 0, 0)),
                  pl.BlockSpec((1, t, VDIM), lambda hh, i: (hh, 0, 0)),
                  pl.BlockSpec((1, tq, VDIM), lambda hh, i: (hh, i, 0)),
                  pl.BlockSpec((1, tq, VDIM), lambda hh, i: (hh, i, 0)),
                  pl.BlockSpec((1, tq, 1), lambda hh, i: (hh, i, 0))],
        out_specs=[pl.BlockSpec((1, tq, HEAD_PAD), lambda hh, i: (hh, i, 0)),
                   pl.BlockSpec((1, tq, 1), lambda hh, i: (hh, i, 0))],
        out_shape=[jax.ShapeDtypeStruct((h, s, HEAD_PAD), F32), jax.ShapeDtypeStruct((h, s, 1), F32)],
        compiler_params=_params(("parallel", "arbitrary")),
    )(q, k, v, o, do, lse)


def _attn_dkv(q, k, v, do, lse_row, delta_row, tk=128):
    h, s, _ = q.shape
    t = k.shape[1]

    def body(q_ref, k_ref, v_ref, do_ref, lse_ref, delta_ref, dk_ref, dv_ref):
        sct = lax.dot_general(k_ref[0], q_ref[0], NT_DIMS, preferred_element_type=F32) * MLA_SCALE
        pt = jnp.exp(sct - lse_ref[0])
        dob = do_ref[0]
        dv_ref[0] = jnp.dot(pt.astype(BF16), dob, preferred_element_type=F32)
        dpt = lax.dot_general(v_ref[0], dob, NT_DIMS, preferred_element_type=F32)
        dst = pt * (dpt - delta_ref[0])
        dk_ref[0] = jnp.dot(dst.astype(BF16), q_ref[0], preferred_element_type=F32) * MLA_SCALE

    return pl.pallas_call(
        body, name="attn_dkv", grid=(h, t // tk),
        in_specs=[pl.BlockSpec((1, s, HEAD_PAD), lambda hh, i: (hh, 0, 0)),
                  pl.BlockSpec((1, tk, HEAD_PAD), lambda hh, i: (hh, i, 0)),
                  pl.BlockSpec((1, tk, VDIM), lambda hh, i: (hh, i, 0)),
                  pl.BlockSpec((1, s, VDIM), lambda hh, i: (hh, 0, 0)),
                  pl.BlockSpec((1, 1, s), lambda hh, i: (hh, 0, 0)),
                  pl.BlockSpec((1, 1, s), lambda hh, i: (hh, 0, 0))],
        out_specs=[pl.BlockSpec((1, tk, HEAD_PAD), lambda hh, i: (hh, i, 0)),
                   pl.BlockSpec((1, tk, VDIM), lambda hh, i: (hh, i, 0))],
        out_shape=[jax.ShapeDtypeStruct((h, t, HEAD_PAD), F32), jax.ShapeDtypeStruct((h, t, VDIM), F32)],
        compiler_params=_params(("parallel", "arbitrary")),
    )(q, k, v, do, lse_row, delta_row)


def _exchange(name, arrs, modes):
    n = len(arrs)
    out_shape = [jax.ShapeDtypeStruct((N_DEV,) + a.shape if md == 'ag' else a.shape, a.dtype)
                 for a, md in zip(arrs, modes)]

    def body(*refs):
        ins, outs = refs[:n], refs[n:2 * n]
        send_sems, recv_sems, local_sems = refs[2 * n:]
        x, y, c = lax.axis_index("x"), lax.axis_index("y"), lax.axis_index("c")
        me = 4 * x + 2 * y + c
        copies = []
        for a in range(n):
            ag = modes[a] == 'ag'
            mine = pltpu.make_async_copy(ins[a] if ag else ins[a].at[me], outs[a].at[me], local_sems.at[a])
            mine.start()
            copies.append(mine)
            for k in range(1, N_DEV):
                px = 1 - x if k & 4 else x
                py = 1 - y if k & 2 else y
                pc = 1 - c if k & 1 else c
                src = ins[a] if ag else ins[a].at[4 * px + 2 * py + pc]
                cp = pltpu.make_async_remote_copy(
                    src_ref=src, dst_ref=outs[a].at[me], send_sem=send_sems.at[a, k - 1],
                    recv_sem=recv_sems.at[a, k - 1], device_id=(px, py, pc), device_id_type=MESH)
                cp.start()
                copies.append(cp)
        for cp in copies:
            cp.wait()

    return pl.pallas_call(
        body, name=name,
        in_specs=[pl.BlockSpec(memory_space=pl.ANY)] * n,
        out_specs=[pl.BlockSpec(memory_space=pl.ANY)] * n,
        out_shape=out_shape,
        scratch_shapes=[pltpu.SemaphoreType.DMA((n, N_DEV - 1)), pltpu.SemaphoreType.DMA((n, N_DEV - 1)),
                        pltpu.SemaphoreType.DMA((n,))],
        compiler_params=pltpu.CompilerParams(has_side_effects=True),
    )(*arrs)


def _adamw(name, w, m, v, gparts):
    r, c = w.shape
    npart = gparts.shape[0]
    tr = _pick(r, (256, 128, 64, 32, 16, 8))
    spec = pl.BlockSpec((tr, c), lambda i: (i, 0))

    def body(w_ref, m_ref, v_ref, g_ref, go_ref, d_ref, mo_ref, vo_ref):
        g = g_ref[0]
        for p in range(1, npart):
            g = g + g_ref[p]
        m1 = ADAM_B1 * m_ref[...] + (1.0 - ADAM_B1) * g
        v1 = ADAM_B2 * v_ref[...] + (1.0 - ADAM_B2) * (g * g)
        m_hat = m1 / (1.0 - ADAM_B1 ** ADAM_STEP)
        v_hat = v1 / (1.0 - ADAM_B2 ** ADAM_STEP)
        go_ref[...] = g
        d_ref[...] = -ADAM_LR * (m_hat / (jnp.sqrt(v_hat) + ADAM_EPS) + ADAM_WD * w_ref[...])
        mo_ref[...] = m1
        vo_ref[...] = v1

    return pl.pallas_call(
        body, name=name, grid=(r // tr,),
        in_specs=[spec, spec, spec, pl.BlockSpec((npart, tr, c), lambda i: (0, i, 0))],
        out_specs=[spec] * 4, out_shape=[jax.ShapeDtypeStruct((r, c), F32)] * 4,
        compiler_params=_params(("parallel",)),
    )(w, m, v, gparts)


def _pack(arrs, rows):
    flat = jnp.concatenate([a.reshape(-1) for a in arrs])
    return jnp.pad(flat, (0, rows * LANES - flat.shape[0])).reshape(rows, LANES)


def _unpack(packed, shapes):
    flat, out, off = packed.reshape(-1), [], 0
    for s in shapes:
        n = math.prod(s)
        out.append(flat[off:off + n].reshape(s))
        off += n
    return out


def _pack_rows(n_elems):
    return -(-n_elems // (SUBLANES * LANES)) * SUBLANES


def _cols_from_shards(g):
    return g.transpose(1, 0, 2).reshape(g.shape[1], N_DEV * g.shape[2])


def _cols_to_shards(w):
    r, c = w.shape
    return w.reshape(r, N_DEV, c // N_DEV).transpose(1, 0, 2)


def _rope_tables(n_lat, n_ctx):
    rows = n_lat // GRID_W
    row = jnp.repeat(jnp.arange(rows, dtype=F32), GRID_W)
    col = jnp.tile(jnp.arange(GRID_W, dtype=F32), rows)
    inv = ROPE_BASE ** (-jnp.arange(ROPE_PAIRS, dtype=F32) / ROPE_PAIRS)
    ang_r, ang_c = row[:, None] * inv, col[:, None] * inv
    cr, sr, cc, sc = jnp.cos(ang_r), jnp.sin(ang_r), jnp.cos(ang_c), jnp.sin(ang_c)
    one, zero = jnp.ones((n_lat, 1), F32), jnp.zeros((n_lat, 1), F32)
    z8 = jnp.zeros((n_lat, ROPE_PAIRS), F32)
    cos_t = jnp.concatenate([jnp.tile(one, (1, NOPE)), cr, cr, cc, cc, jnp.tile(one, (1, HEAD_PAD - QK))], 1)
    sin_up = jnp.concatenate([jnp.tile(zero, (1, NOPE)), -sr, z8, -sc, z8, jnp.tile(zero, (1, HEAD_PAD - QK))], 1)
    sin_dn = jnp.concatenate([jnp.tile(zero, (1, NOPE)), z8, sr, z8, sc, jnp.tile(zero, (1, HEAD_PAD - QK))], 1)
    if n_ctx:
        cos_t = jnp.concatenate([jnp.ones((n_ctx, HEAD_PAD), F32), cos_t])
        sin_up = jnp.concatenate([jnp.zeros((n_ctx, HEAD_PAD), F32), sin_up])
        sin_dn = jnp.concatenate([jnp.zeros((n_ctx, HEAD_PAD), F32), sin_dn])
    return cos_t, sin_up, sin_dn


def _rope(x, cos_t, sin_up, sin_dn):
    return x * cos_t + pltpu.roll(x, HEAD_PAD - ROPE_PAIRS, 1) * sin_up + pltpu.roll(x, ROPE_PAIRS, 1) * sin_dn


def _rope_t(dy, cos_t, sin_up, sin_dn):
    return (dy * cos_t + pltpu.roll(dy * sin_up, ROPE_PAIRS, 1)
            + pltpu.roll(dy * sin_dn, HEAD_PAD - ROPE_PAIRS, 1))


def _softplus(x):
    return jnp.maximum(x, 0.0) + jnp.log(1.0 + jnp.exp(-jnp.abs(x)))


def _gates(z, xcv, lam_sp):
    outs = []
    for d in range(2):
        r = jax.nn.sigmoid(z[:, (2 * d) * LRU_W:(2 * d + 1) * LRU_W])
        ig = jax.nn.sigmoid(z[:, (2 * d + 1) * LRU_W:(2 * d + 2) * LRU_W])
        log_a = -LRU_C * r * lam_sp[:, d * LRU_W:(d + 1) * LRU_W]
        a = jnp.exp(log_a)
        u = jnp.sqrt(-jnp.tanh(log_a) * (a * a + 1.0)) * (ig * xcv)
        outs += [a, u]
    return tuple(outs)


def kernel(x, c, ctx, c_ctx, w_mod, b_mod, g_pre_mix, g_post_mix, g_pre_ffn, g_post_ffn, w_in, lru_conv_w, lru_conv_b, lru_w_a, lru_b_a, lru_w_x, lru_b_x, lru_lambda, mla_g_q, mla_w_uq, mla_g_kv, mla_w_ukv, w_out, ffn_w_up, ffn_conv_w, ffn_conv_b, ffn_w_down, loss_target, m_c_ctx, m_w_mod, m_b_mod, m_g_pre_mix, m_g_post_mix, m_g_pre_ffn, m_g_post_ffn, m_w_in, m_lru_conv_w, m_lru_conv_b, m_lru_w_a, m_lru_b_a, m_lru_w_x, m_lru_b_x, m_lru_lambda, m_mla_g_q, m_mla_w_uq, m_mla_g_kv, m_mla_w_ukv, m_w_out, m_ffn_w_up, m_ffn_conv_w, m_ffn_conv_b, m_ffn_w_down, v_c_ctx, v_w_mod, v_b_mod, v_g_pre_mix, v_g_post_mix, v_g_pre_ffn, v_g_post_ffn, v_w_in, v_lru_conv_w, v_lru_conv_b, v_lru_w_a, v_lru_b_a, v_lru_w_x, v_lru_b_x, v_lru_lambda, v_mla_g_q, v_mla_w_uq, v_mla_g_kv, v_mla_w_ukv, v_w_out, v_ffn_w_up, v_ffn_conv_w, v_ffn_conv_b, v_ffn_w_down):
    W = dict(c_ctx=c_ctx, w_mod=w_mod, b_mod=b_mod, g_pre_mix=g_pre_mix, g_post_mix=g_post_mix, g_pre_ffn=g_pre_ffn,
             g_post_ffn=g_post_ffn, w_in=w_in, lru_conv_w=lru_conv_w, lru_conv_b=lru_conv_b, lru_w_a=lru_w_a,
             lru_b_a=lru_b_a, lru_w_x=lru_w_x, lru_b_x=lru_b_x, lru_lambda=lru_lambda, mla_g_q=mla_g_q,
             mla_w_uq=mla_w_uq, mla_g_kv=mla_g_kv, mla_w_ukv=mla_w_ukv, w_out=w_out, ffn_w_up=ffn_w_up,
             ffn_conv_w=ffn_conv_w, ffn_conv_b=ffn_conv_b, ffn_w_down=ffn_w_down)
    M = dict(c_ctx=m_c_ctx, w_mod=m_w_mod, b_mod=m_b_mod, g_pre_mix=m_g_pre_mix, g_post_mix=m_g_post_mix,
             g_pre_ffn=m_g_pre_ffn, g_post_ffn=m_g_post_ffn, w_in=m_w_in, lru_conv_w=m_lru_conv_w,
             lru_conv_b=m_lru_conv_b, lru_w_a=m_lru_w_a, lru_b_a=m_lru_b_a, lru_w_x=m_lru_w_x, lru_b_x=m_lru_b_x,
             lru_lambda=m_lru_lambda, mla_g_q=m_mla_g_q, mla_w_uq=m_mla_w_uq, mla_g_kv=m_mla_g_kv,
             mla_w_ukv=m_mla_w_ukv, w_out=m_w_out, ffn_w_up=m_ffn_w_up, ffn_conv_w=m_ffn_conv_w,
             ffn_conv_b=m_ffn_conv_b, ffn_w_down=m_ffn_w_down)
    V = dict(c_ctx=v_c_ctx, w_mod=v_w_mod, b_mod=v_b_mod, g_pre_mix=v_g_pre_mix, g_post_mix=v_g_post_mix,
             g_pre_ffn=v_g_pre_ffn, g_post_ffn=v_g_post_ffn, w_in=v_w_in, lru_conv_w=v_lru_conv_w,
             lru_conv_b=v_lru_conv_b, lru_w_a=v_lru_w_a, lru_b_a=v_lru_b_a, lru_w_x=v_lru_w_x, lru_b_x=v_lru_b_x,
             lru_lambda=v_lru_lambda, mla_g_q=v_mla_g_q, mla_w_uq=v_mla_w_uq, mla_g_kv=v_mla_g_kv,
             mla_w_ukv=v_mla_w_ukv, w_out=v_w_out, ffn_w_up=v_ffn_w_up, ffn_conv_w=v_ffn_conv_w,
             ffn_conv_b=v_ffn_conv_b, ffn_w_down=v_ffn_w_down)

    D = D_MODEL
    S, CN = x.shape[1], ctx.shape[1]
    T = S + CN
    TM = ROW_TILE
    ct, ns, nt = CN // TM, S // TM, T // TM
    me = 4 * lax.axis_index("x") + 2 * lax.axis_index("y") + lax.axis_index("c")

    lat = lambda i: i + ct
    swp = lambda i: jnp.where(i < ct, i + ns, i - ct)
    lat_or_0 = lambda i: jnp.maximum(i - ct, 0)

    small_shapes = [W[n].shape[1:] for n in SMALL_SHARDED] + [(D,)]
    n_small = sum(math.prod(s) for s in small_shapes)
    small_rows = _pack_rows(n_small)
    small_loc = _pack([W[n][0] for n in SMALL_SHARDED] + [c[0]], small_rows)
    big = ['w_in', 'ffn_w_up', 'ffn_w_down', 'w_out', 'mla_w_uq', 'mla_w_ukv']
    gathered = _exchange("gather_weights", [W[n][0].astype(BF16) for n in big] + [small_loc], ['ag'] * 7)
    gw = dict(zip(big, gathered[:6]))
    small_all = [_unpack(gathered[6][d], small_shapes) for d in range(N_DEV)]
    full_small = {n: jnp.concatenate([small_all[d][j] for d in range(N_DEV)], axis=-1)
                  for j, n in enumerate(SMALL_SHARDED)}
    c_all = jnp.stack([small_all[d][-1] for d in range(N_DEV)])

    w_in_f = _cols_from_shards(gw['w_in'])
    w_in_p = jnp.concatenate([w_in_f[:, :OFF_KR], jnp.zeros((D, NOPE), BF16), w_in_f[:, OFF_KR:],
                              jnp.zeros((D, HEAD_PAD - QK), BF16)], axis=1)
    w_up_f = _cols_from_shards(gw['ffn_w_up'])
    w_down_f = gw['ffn_w_down'].reshape(D_FF, D)
    w_out_f = gw['w_out'].reshape(D, D)
    w_uq_f = _cols_from_shards(gw['mla_w_uq']).reshape(Q_RANK, HEADS, QK)
    wq_p = jnp.pad(w_uq_f, ((0, 0), (0, 0), (0, HEAD_PAD - QK))).reshape(Q_RANK, HEADS * HEAD_PAD)
    w_ukv_f = _cols_from_shards(gw['mla_w_ukv']).reshape(KV_RANK, HEADS, NOPE + VDIM)
    wk_p = jnp.pad(w_ukv_f[:, :, :NOPE], ((0, 0), (0, 0), (0, HEAD_PAD - NOPE))).reshape(KV_RANK, HEADS * HEAD_PAD)
    wv_f = w_ukv_f[:, :, NOPE:].reshape(KV_RANK, HEADS * VDIM)

    lru_cw, lru_ba, lru_bx, lru_lam, ffn_cw = [full_small[n] for n in SMALL_SHARDED]

    def block_diag(w):
        eye = jnp.eye(LRU_HEADS, dtype=w.dtype)
        return jnp.einsum('hij,hg->higj', w, eye).reshape(LRU_W, LRU_W)

    w_gate = jnp.concatenate([block_diag(lru_w_a[0, 0]), block_diag(lru_w_x[0, 0]),
                              block_diag(lru_w_a[0, 1]), block_diag(lru_w_x[0, 1])], axis=1).astype(BF16)
    b_gate = jnp.concatenate([lru_ba[0], lru_bx[0], lru_ba[1], lru_bx[1]])[None]
    lam_row = lru_lam.reshape(1, 2 * LRU_W)

    c16 = jnp.concatenate([c_all, c_ctx[None], jnp.zeros((2 * SUBLANES - N_DEV - 1, D), F32)])
    ncol = w_mod.shape[2]
    b_mod_loc = lax.dynamic_slice(b_mod, (0, me * ncol), (1, ncol))

    def mod_fwd(c16_r, w_r, b_r):
        c16_v = c16_r[...]
        sl = c16_v * jax.nn.sigmoid(c16_v)
        return (jnp.dot(sl.astype(BF16), w_r[...].astype(BF16), preferred_element_type=F32) + b_r[...],)

    (mod_part,) = _single("mod_fwd", mod_fwd, [c16, w_mod[0], b_mod_loc], [((2 * SUBLANES, ncol), F32)])
    (mod_g,) = _exchange("gather_mod", [mod_part], ['ag'])
    mod_all = _cols_from_shards(mod_g)
    mod_lat = lax.dynamic_slice(mod_all, (me, 0), (1, N_MOD * D)).reshape(N_MOD, D)
    mod_ctx = mod_all[N_DEV].reshape(N_MOD, D)

    xs, tgt = x[0], loss_target[0]
    xa = jnp.concatenate([ctx[0], xs])

    def sel_mod(i, ml, mc, r0):
        sh = jnp.where(i < ct, mc[r0:r0 + 1, :], ml[r0:r0 + 1, :])
        sc = jnp.where(i < ct, mc[r0 + 1:r0 + 2, :], ml[r0 + 1:r0 + 2, :])
        return sh, sc

    def pre_fn(xv, g, sh, sc):
        return _rms(xv, g) * (1.0 + sc) + sh

    def k_pre(i, rv, bv):
        sh, sc = sel_mod(i, bv[1], bv[2], 0)
        return (pre_fn(rv[0], bv[0][...], sh, sc),), ()

    (h_pre,) = _rowwise("pre_mix", k_pre, nt, [_row(xa)], [g_pre_mix, mod_lat, mod_ctx], [((T, D), BF16, None)])
    proj = _mm("in_proj", h_pre, w_in_p, 'nn')

    xcv = _dwconv_fwd("lru_conv", proj[:, :LRU_W], lru_cw, lru_conv_b, LRU_CONV_LEFT, (0, ct))

    def k_gates(i, rv, bv):
        xv = rv[0]
        z = jnp.dot(xv.astype(BF16), bv[0][...], preferred_element_type=F32) + bv[1][...]
        return _gates(z, xv, _bc(_softplus(-bv[2][...]), TM)), ()

    a0, u0, a1, u1 = _rowwise("lru_gates", k_gates, nt, [_row(xcv)], [w_gate, b_gate, lam_row],
                              [((T, LRU_W), F32, None), ((T, LRU_W), F32, None),
                               ((T, LRU_W), F32, swp), ((T, LRU_W), F32, swp)])
    h0 = _scan("lru_scan_f", a0, u0, False)
    h1 = _scan("lru_scan_r", a1, u1, True)

    def k_rms(i, rv, bv):
        return (_rms(rv[0], bv[0][...]),), ()

    cqn = _rowwise("q_norm", k_rms, ns, [_row(proj, lat, (Q_RANK, OFF_CQ // Q_RANK))], [mla_g_q],
                   [((S, Q_RANK), BF16, None)])[0]
    q_lin = _mm("q_proj", cqn, wq_p, 'nn')
    cos_q, sup_q, sdn_q = _rope_tables(S, 0)
    cos_k, sup_k, sdn_k = _rope_tables(S, CN)

    def k_qrope(i, rv, bv):
        ql, ctb, sub, sdb = rv
        return (jnp.stack([_rope(ql[:, hh * HEAD_PAD:(hh + 1) * HEAD_PAD], ctb, sub, sdb) for hh in range(HEADS)]),), ()

    q = _rowwise("q_rope", k_qrope, ns, [_row(q_lin), _row(cos_q), _row(sup_q), _row(sdn_q)], [],
                 [((HEADS, S, HEAD_PAD), BF16, None)])[0]

    ckvn = _rowwise("kv_norm", k_rms, nt, [_row(proj, None, (KV_RANK, OFF_CKV // KV_RANK))], [mla_g_kv],
                    [((T, KV_RANK), BF16, None)])[0]
    k_lin = _mm("k_proj", ckvn, wk_p, 'nn')
    v_lin = _mm("v_proj", ckvn, wv_f, 'nn', out_dtype=BF16)

    def k_krope(i, rv, bv):
        kl, krp, ctb, sub, sdb = rv
        kr = _rope(krp, ctb, sub, sdb)
        return (jnp.stack([kl[:, hh * HEAD_PAD:(hh + 1) * HEAD_PAD] + kr for hh in range(HEADS)]),), ()

    k = _rowwise("k_rope", k_krope, nt,
                 [_row(k_lin), _row(proj, None, (HEAD_PAD, OFF_KR // HEAD_PAD)), _row(cos_k), _row(sup_k), _row(sdn_k)],
                 [], [((HEADS, T, HEAD_PAD), BF16, None)])[0]
    v = v_lin.reshape(T, HEADS, VDIM).transpose(1, 0, 2)

    o, lse = _attn_fwd(q, k, v)
    o_t = o.transpose(1, 0, 2).reshape(S, HEADS * VDIM)

    def lru_out_fn(hf, hr, gr):
        return (hf + hr) * jax.nn.gelu(gr)

    def k_mix_in(i, rv, bv):
        return (jnp.concatenate([lru_out_fn(rv[0], rv[1], rv[2]), rv[3]], axis=1),), ()

    gr_row = _row(proj, lat, (LRU_W, OFF_GR // LRU_W))
    y_in = _rowwise("mix_in", k_mix_in, ns, [_row(h0, lat), _row(h1), gr_row, _row(o_t)], [],
                    [((S, D), BF16, None)])[0]
    y = _mm("out_proj", y_in, w_out_f, 'nn')

    def post_mix_fn(xv, yv, gt, g):
        return xv + gt * _rms(yv, g)

    def k_post_mix(i, rv, bv):
        ml = bv[0]
        x1v = post_mix_fn(rv[0], rv[1], ml[2:3, :], bv[1][...])
        return (x1v, pre_fn(x1v, bv[2][...], ml[3:4, :], ml[4:5, :])), ()

    x1, h2 = _rowwise("post_mix", k_post_mix, ns, [_row(xs), _row(y)], [mod_lat, g_post_mix, g_pre_ffn],
                      [((S, D), F32, None), ((S, D), BF16, None)])
    up = _mm("ffn_up", h2, w_up_f, 'nn')
    upc = _dwconv_fwd("ffn_conv", up, ffn_cw, ffn_conv_b, FFN_CONV_LEFT, (0,))

    def gate_fn(uv, gv):
        return gv * jax.nn.sigmoid(gv) * uv

    n_ff_tiles = D_FF // 256

    def ffn_gate(name, fn, extra_rows, outs):
        specs = [pl.BlockSpec((TM, 256), lambda i, j: (i, j)), pl.BlockSpec((TM, 256), lambda i, j: (i, j + n_ff_tiles))]
        specs += [pl.BlockSpec((TM, 256), lambda i, j: (i, j))] * len(extra_rows)

        def body(*refs):
            res = fn(*[r[...] for r in refs[:2 + len(extra_rows)]])
            for ref, val in zip(refs[2 + len(extra_rows):], res):
                ref[...] = val.astype(ref.dtype)

        return pl.pallas_call(
            body, name=name, grid=(ns, n_ff_tiles), in_specs=specs,
            out_specs=[pl.BlockSpec((TM, 256), lambda i, j, off=off: (i, j + off)) for _, _, off in outs],
            out_shape=[jax.ShapeDtypeStruct(s, d) for s, d, _ in outs],
            compiler_params=_params(("parallel", "parallel")),
        )(upc, upc, *extra_rows)

    (act,) = ffn_gate("ffn_gate", lambda uv, gv: (gate_fn(uv, gv),), [], [((S, D_FF), BF16, 0)])
    f = _mm("ffn_down", act, w_down_f, 'nn')

    def loss_fn(x1v, fv, gt, g, tg):
        x2 = x1v + gt * _rms(fv, g)
        err = x2 - tg
        return 0.5 * jnp.sum(jnp.mean(err * err, axis=-1))

    def k_loss(i, rv, bv):
        gtb, gb = _bc(bv[0][5:6, :], TM), _bc(bv[1][...], TM)
        val, (dx1v, dfv, dgt, dg) = jax.value_and_grad(loss_fn, argnums=(0, 1, 2, 3))(rv[0], rv[1], gtb, gb, rv[2])
        return (dx1v, dfv), (jnp.full((1, LANES), val, F32), _rs(dgt), _rs(dg))

    dx1_a, df, loss_acc, d_gt2, d_g_post_ffn = _rowwise(
        "loss_bwd", k_loss, ns, [_row(x1), _row(f), _row(tgt)], [mod_lat, g_post_ffn],
        [((S, D), F32, None), ((S, D), BF16, None)], [(1, LANES), (1, D), (1, D)])
    loss = lax.psum(loss_acc[0, 0], ("x", "y", "c"))

    d_act = _mm("ffn_down_dx", df, w_down_f, 'nt')
    d_w_down = _mm("ffn_down_dw", act, df, 'tn')

    def gate_bwd(uv, gv, dav):
        _, pull = jax.vjp(gate_fn, uv, gv)
        return pull(dav)

    d_upc = jnp.concatenate(ffn_gate("ffn_gate_bwd", gate_bwd, [d_act], [((S, D_FF), F32, 0), ((S, D_FF), F32, 0)]),
                            axis=1)
    d_up, d_ffn_cw, d_ffn_cb = _dwconv_bwd("ffn_conv_bwd", up, d_upc, ffn_cw, FFN_CONV_LEFT, (0,), out_dtype=BF16)
    d_h2 = _mm("ffn_up_dx", d_up, w_up_f, 'nt')
    d_w_up = _mm("ffn_up_dw", h2, d_up, 'tn')

    def k_pre_ffn_bwd(i, rv, bv):
        ml = bv[0]
        gb, shb, scb = _bc(bv[1][...], TM), _bc(ml[3:4, :], TM), _bc(ml[4:5, :], TM)
        _, pull = jax.vjp(pre_fn, rv[0], gb, shb, scb)
        dxv, dg, dsh, dsc = pull(rv[1])
        return (rv[2] + dxv,), (_rs(dg), _rs(dsh), _rs(dsc))

    dx1, d_g_pre_ffn, d_sh2, d_sc2 = _rowwise(
        "pre_ffn_bwd", k_pre_ffn_bwd, ns, [_row(x1), _row(d_h2), _row(dx1_a)], [mod_lat, g_pre_ffn],
        [((S, D), F32, None)], [(1, D), (1, D), (1, D)])

    def k_post_mix_bwd(i, rv, bv):
        gtb, gb = _bc(bv[0][2:3, :], TM), _bc(bv[1][...], TM)
        _, pull = jax.vjp(post_mix_fn, rv[0], rv[1], gtb, gb)
        _, dyv, dgt, dg = pull(rv[2])
        return (dyv,), (_rs(dgt), _rs(dg))

    dy, d_gt1, d_g_post_mix = _rowwise(
        "post_mix_bwd", k_post_mix_bwd, ns, [_row(xs), _row(y), _row(dx1)], [mod_lat, g_post_mix],
        [((S, D), BF16, None)], [(1, D), (1, D)])
    d_y_in = _mm("out_proj_dx", dy, w_out_f, 'nt')
    d_w_out = _mm("out_proj_dw", y_in, dy, 'tn')

    def k_lru_out_bwd(i, rv, bv):
        _, pull = jax.vjp(lru_out_fn, rv[0], rv[1], rv[2])
        dhf, _, dgr = pull(rv[3])
        return (dhf, dgr), ()

    d_hsum, d_gr = _rowwise("lru_out_bwd", k_lru_out_bwd, ns,
                            [_row(h0, lat), _row(h1), gr_row, _row(d_y_in, None, (LRU_W, 0))], [],
                            [((S, LRU_W), F32, None), ((S, LRU_W), F32, None)])

    do_h = d_y_in[:, LRU_W:].reshape(S, HEADS, VDIM).transpose(1, 0, 2)
    dq, delta = _attn_dq(q, k, v, o, do_h, lse)
    dk, dv = _attn_dkv(q, k, v, do_h.astype(BF16), lse.reshape(HEADS, 1, S), delta.reshape(HEADS, 1, S))

    def k_qrope_bwd(i, rv, bv):
        dqv, ctb, sub, sdb = rv
        return (jnp.concatenate([_rope_t(dqv[hh], ctb, sub, sdb) for hh in range(HEADS)], axis=1),), ()

    dq_lin = _rowwise("q_rope_bwd", k_qrope_bwd, ns, [_row(dq), _row(cos_q), _row(sup_q), _row(sdn_q)], [],
                      [((S, HEADS * HEAD_PAD), BF16, None)])[0]
    d_cqn = _mm("q_proj_dx", dq_lin, wq_p, 'nt')
    d_wq_p = _mm("q_proj_dw", cqn, dq_lin, 'tn')

    def k_rms_bwd(i, rv, bv):
        gb = _bc(bv[0][...], TM)
        _, pull = jax.vjp(_rms, rv[0], gb)
        dxv, dg = pull(rv[1])
        return (dxv,), (_rs(dg),)

    d_cq, d_g_q = _rowwise("q_norm_bwd", k_rms_bwd, ns,
                           [_row(proj, lat, (Q_RANK, OFF_CQ // Q_RANK)), _row(d_cqn)], [mla_g_q],
                           [((S, Q_RANK), F32, None)], [(1, Q_RANK)])

    def k_krope_bwd(i, rv, bv):
        dkv_, ctb, sub, sdb = rv
        tot = dkv_[0]
        for hh in range(1, HEADS):
            tot = tot + dkv_[hh]
        lane = lax.broadcasted_iota(jnp.int32, tot.shape, 1)
        tot = jnp.where((lane >= NOPE) & (lane < QK), tot, 0.0)
        return (jnp.concatenate([dkv_[hh] for hh in range(HEADS)], axis=1), _rope_t(tot, ctb, sub, sdb)), ()

    dk_lin, d_krp = _rowwise("k_rope_bwd", k_krope_bwd, nt, [_row(dk), _row(cos_k), _row(sup_k), _row(sdn_k)], [],
                             [((T, HEADS * HEAD_PAD), BF16, None), ((T, HEAD_PAD), F32, None)])
    dv_lin = dv.transpose(1, 0, 2).reshape(T, HEADS * VDIM).astype(BF16)
    d_ckvn_k = _mm("k_proj_dx", dk_lin, wk_p, 'nt')
    d_ckvn_v = _mm("v_proj_dx", dv_lin, wv_f, 'nt')
    d_wk_p = _mm("k_proj_dw", ckvn, dk_lin, 'tn')
    d_wv = _mm("v_proj_dw", ckvn, dv_lin, 'tn')

    def k_kvnorm_bwd(i, rv, bv):
        gb = _bc(bv[0][...], TM)
        _, pull = jax.vjp(_rms, rv[0], gb)
        dxv, dg = pull(rv[1] + rv[2])
        return (dxv,), (_rs(dg),)

    d_ckv, d_g_kv = _rowwise("kv_norm_bwd", k_kvnorm_bwd, nt,
                             [_row(proj, None, (KV_RANK, OFF_CKV // KV_RANK)), _row(d_ckvn_k), _row(d_ckvn_v)],
                             [mla_g_kv], [((T, KV_RANK), F32, None)], [(1, KV_RANK)])

    zc = jnp.zeros((CN, LRU_W), F32)
    one_row, zero_row = jnp.ones((1, LRU_W), F32), jnp.zeros((1, LRU_W), F32)
    lam0 = _scan("lru_scan_f_bwd", jnp.concatenate([a0[1:], one_row]), jnp.concatenate([zc, d_hsum]), True)
    lam1 = _scan("lru_scan_r_bwd", jnp.concatenate([one_row, a1[:-1]]), jnp.concatenate([d_hsum, zc]), False)
    hprev0 = jnp.concatenate([zero_row, h0[:-1]])
    hprev1 = jnp.concatenate([h1[1:], zero_row])

    def k_gates_bwd(i, rv, bv):
        xv, l0, hp0, l1, hp1 = rv
        wg, bg, lamv = [b[...] for b in bv]
        xb = xv.astype(BF16)
        z = jnp.dot(xb, wg, preferred_element_type=F32) + bg
        spb = _bc(_softplus(-lamv), TM)
        _, pull = jax.vjp(_gates, z, xv, spb)
        dz, dxv, dsp = pull((l0 * hp0, l0, l1 * hp1, l1))
        dzb = dz.astype(BF16)
        dxv = dxv + lax.dot_general(dzb, wg, NT_DIMS, preferred_element_type=F32)
        dwg = lax.dot_general(xb, dzb, (((0,), (0,)), ((), ())), preferred_element_type=F32)
        dlam = -_rs(dsp) * jax.nn.sigmoid(-lamv)
        return (dxv,), (dwg, _rs(dz), dlam)

    d_xcv, d_w_gate, d_b_gate, d_lam = _rowwise(
        "lru_gates_bwd", k_gates_bwd, nt,
        [_row(xcv), _row(lam0), _row(hprev0), _row(lam1, swp), _row(hprev1, swp)], [w_gate, b_gate, lam_row],
        [((T, LRU_W), F32, None)], [(LRU_W, 4 * LRU_W), (1, 4 * LRU_W), (1, 2 * LRU_W)])
    d_xr, d_lru_cw, d_lru_cb = _dwconv_bwd("lru_conv_bwd", proj[:, :LRU_W], d_xcv, lru_cw, LRU_CONV_LEFT, (0, ct))

    def k_dproj(i, rv, bv):
        is_lat = i >= ct
        return (jnp.concatenate([rv[0], jnp.where(is_lat, rv[1], 0.0), jnp.where(is_lat, rv[2], 0.0), rv[3], rv[4]],
                                axis=1),), ()

    d_proj = _rowwise("d_proj", k_dproj, nt,
                      [_row(d_xr), _row(d_gr, lat_or_0), _row(d_cq, lat_or_0), _row(d_ckv), _row(d_krp)], [],
                      [((T, IN_W_PAD), BF16, None)])[0]
    d_h_pre = _mm("in_proj_dx", d_proj, w_in_p, 'nt')
    d_w_in_p = _mm("in_proj_dw", h_pre, d_proj, 'tn')

    def k_pre_bwd(i, rv, bv):
        g, ml, mc = bv[0][...], bv[1], bv[2]
        sh, sc = sel_mod(i, ml, mc, 0)
        _, pull = jax.vjp(pre_fn, rv[0], _bc(g, TM), _bc(sh, TM), _bc(sc, TM))
        dxv, dg, dsh, dsc = pull(rv[1])
        is_lat = i >= ct
        dsh, dsc = _rs(dsh), _rs(dsc)
        zero = jnp.zeros_like(dsh)
        return ((dxv + jnp.where(is_lat, rv[2], 0.0),),
                (_rs(dg), jnp.where(is_lat, dsh, zero), jnp.where(is_lat, dsc, zero),
                 jnp.where(is_lat, zero, dsh), jnp.where(is_lat, zero, dsc)))

    dxa, d_g_pre_mix, d_sh1, d_sc1, d_csh1, d_csc1 = _rowwise(
        "pre_mix_bwd", k_pre_bwd, nt, [_row(xa), _row(d_h_pre), _row(dx1, lat_or_0)],
        [g_pre_mix, mod_lat, mod_ctx], [((T, D), F32, None)], [(1, D)] * 5)
    grad_x = dxa[CN:][None]

    zrow = jnp.zeros((1, D), F32)
    d_mod_lat = jnp.concatenate([d_sh1, d_sc1, d_gt1, d_sh2, d_sc2, d_gt2], axis=1)
    d_mod_ctx = jnp.concatenate([d_csh1, d_csc1, zrow, zrow, zrow, zrow], axis=1)
    d_mod_mine = jnp.concatenate([d_mod_lat, d_mod_ctx, jnp.zeros((SUBLANES - 2, N_MOD * D), F32)])
    (d_mod_all,) = _exchange("gather_dmod", [d_mod_mine], ['ag'])
    dm_lat_loc = lax.dynamic_slice(d_mod_all[:, 0], (0, me * ncol), (N_DEV, ncol))
    dm_ctx_loc = lax.dynamic_slice(d_mod_all[:, 1], (0, me * ncol), (N_DEV, ncol))

    def mod_bwd(c16_r, w_r, dml_r, dmc_r):
        c16_v = c16_r[...]
        sig = jax.nn.sigmoid(c16_v)
        sl = c16_v * sig
        dctx = dmc_r[0:1, :]
        for d in range(1, N_DEV):
            dctx = dctx + dmc_r[d:d + 1, :]
        row = lax.broadcasted_iota(jnp.int32, (2 * SUBLANES, ncol), 0)
        dm16 = dml_r[...] + jnp.where(row == N_DEV, _bc(dctx, 2 * SUBLANES), 0.0)
        dw = lax.dot_general(sl.astype(BF16), dm16.astype(BF16), (((0,), (0,)), ((), ())), preferred_element_type=F32)
        dsl = lax.dot_general(dm16.astype(BF16), w_r[...].astype(BF16), NT_DIMS, preferred_element_type=F32)
        dc = dsl * (sig * (1.0 + c16_v * (1.0 - sig)))
        return dw, dc

    dm_lat16 = jnp.concatenate([dm_lat_loc, jnp.zeros((2 * SUBLANES - N_DEV, ncol), F32)])
    g_w_mod, dc16 = _single("mod_bwd", mod_bwd, [c16, w_mod[0], dm_lat16, dm_ctx_loc],
                            [((D, ncol), F32), ((2 * SUBLANES, D), F32)])
    d_c_ctx_part = dc16[N_DEV]

    def diag_blocks(dw):
        return jnp.stack([dw[hh * 64:(hh + 1) * 64, hh * 64:(hh + 1) * 64] for hh in range(LRU_HEADS)])

    d_lru_w_a = jnp.stack([diag_blocks(d_w_gate[:, 0:LRU_W]), diag_blocks(d_w_gate[:, 2 * LRU_W:3 * LRU_W])])[None]
    d_lru_w_x = jnp.stack([diag_blocks(d_w_gate[:, LRU_W:2 * LRU_W]), diag_blocks(d_w_gate[:, 3 * LRU_W:])])[None]
    d_b_a = jnp.stack([d_b_gate[0, 0:LRU_W], d_b_gate[0, 2 * LRU_W:3 * LRU_W]])
    d_b_x = jnp.stack([d_b_gate[0, LRU_W:2 * LRU_W], d_b_gate[0, 3 * LRU_W:]])

    rep_part = dict(c_ctx=d_c_ctx_part, b_mod=d_mod_lat + d_mod_ctx, g_pre_mix=d_g_pre_mix, g_post_mix=d_g_post_mix,
                    g_pre_ffn=d_g_pre_ffn, g_post_ffn=d_g_post_ffn, lru_conv_b=d_lru_cb, lru_w_a=d_lru_w_a,
                    lru_w_x=d_lru_w_x, mla_g_q=d_g_q, mla_g_kv=d_g_kv, ffn_conv_b=d_ffn_cb)
    rep_shapes = [W[n].shape for n in REPLICATED]
    rep_rows = _pack_rows(sum(W[n].size for n in REPLICATED))
    rep_loc = _pack([rep_part[n] for n in REPLICATED], rep_rows)

    d_w_in = jnp.concatenate([d_w_in_p[:, :OFF_KR], d_w_in_p[:, OFF_KR + NOPE:OFF_KR + QK]], axis=1)
    d_w_uq = d_wq_p.reshape(Q_RANK, HEADS, HEAD_PAD)[:, :, :QK].reshape(Q_RANK, HEADS * QK)
    d_w_ukv = jnp.concatenate([d_wk_p.reshape(KV_RANK, HEADS, HEAD_PAD)[:, :, :NOPE],
                               d_wv.reshape(KV_RANK, HEADS, VDIM)], axis=2).reshape(KV_RANK, HEADS * (NOPE + VDIM))
    small_full = dict(lru_conv_w=d_lru_cw, lru_b_a=d_b_a, lru_b_x=d_b_x, lru_lambda=d_lam.reshape(2, LRU_W),
                      ffn_conv_w=d_ffn_cw)
    small_sh = jnp.concatenate([_cols_to_shards(small_full[n]).reshape(N_DEV, -1) for n in SMALL_SHARDED], axis=1)
    n_sh = small_sh.shape[1]
    sh_rows = _pack_rows(n_sh)
    small_sh = jnp.pad(small_sh, ((0, 0), (0, sh_rows * LANES - n_sh))).reshape(N_DEV, sh_rows, LANES)

    sends = [_cols_to_shards(d_w_in), _cols_to_shards(d_w_up), d_w_down.reshape(N_DEV, D_FF // N_DEV, D),
             d_w_out.reshape(N_DEV, D // N_DEV, D), _cols_to_shards(d_w_uq), _cols_to_shards(d_w_ukv), small_sh]
    recv = _exchange("exchange_grads", sends + [rep_loc], ['a2a'] * 7 + ['ag'])

    res = {}

    def adam(name, w2, m2, v2, parts):
        return _adamw("adamw_" + name, w2, m2, v2, parts)

    for n, parts in zip(big, recv[:6]):
        shp = W[n].shape
        outs = adam(n, W[n][0], M[n][0], V[n][0], parts)
        res[n] = [o_.reshape(shp) for o_ in outs]
    outs = adam('w_mod', w_mod[0], m_w_mod[0], v_w_mod[0], g_w_mod[None])
    res['w_mod'] = [o_.reshape(w_mod.shape) for o_ in outs]

    sh_shapes = [W[n].shape for n in SMALL_SHARDED]
    pk = lambda dct: _pack([dct[n] for n in SMALL_SHARDED], sh_rows)
    outs = adam('small_sharded', pk(W), pk(M), pk(V), recv[6])
    for n, vals in zip(SMALL_SHARDED, zip(*[_unpack(o_, sh_shapes) for o_ in outs])):
        res[n] = list(vals)

    pr = lambda dct: _pack([dct[n] for n in REPLICATED], rep_rows)
    outs = adam('replicated', pr(W), pr(M), pr(V), recv[7])
    for n, vals in zip(REPLICATED, zip(*[_unpack(o_, rep_shapes) for o_ in outs])):
        res[n] = list(vals)

    return (loss, grad_x, *[res[n][0] for n in WEIGHTS], *[res[n][1] for n in WEIGHTS],
            *[res[n][2] for n in WEIGHTS], *[res[n][3] for n in WEIGHTS])
```

```python
import functools
import math

import jax
import jax.numpy as jnp
from jax import lax
from jax.experimental import pallas as pl
from jax.experimental.pallas import tpu as pltpu

F32 = jnp.float32
BF16 = jnp.bfloat16
MESH = pl.DeviceIdType.MESH

N_DEV = 8
ROW_TILE = 256
SUBLANES = 8
LANES = 128
VMEM_LIMIT = 56 * 1024 * 1024

D_MODEL = 1024
LRU_W = 512
LRU_HEADS = 8
LRU_CONV_K = 4
LRU_CONV_LEFT = 2
LRU_C = 8.0
HEADS = 8
NOPE = 64
ROPE = 32
VDIM = 64
QK = NOPE + ROPE
HEAD_PAD = 128
Q_RANK = 256
KV_RANK = 128
MLA_SCALE = QK ** -0.5
ROPE_PAIRS = ROPE // 4
ROPE_BASE = 10000.0
GRID_W = 64
D_FF = 2816
FFN_CONV_K = 3
FFN_CONV_LEFT = 1
N_MOD = 6
EPS = 1e-6
IN_W = 2 * LRU_W + Q_RANK + KV_RANK + ROPE
IN_W_PAD = 2 * LRU_W + Q_RANK + KV_RANK + HEAD_PAD
OFF_GR, OFF_CQ, OFF_CKV, OFF_KR = LRU_W, 2 * LRU_W, 2 * LRU_W + Q_RANK, 2 * LRU_W + Q_RANK + KV_RANK

ADAM_LR, ADAM_B1, ADAM_B2, ADAM_EPS, ADAM_WD, ADAM_STEP = 0.001, 0.9, 0.999, 1e-08, 0.01, 10

WEIGHTS = ['c_ctx', 'w_mod', 'b_mod', 'g_pre_mix', 'g_post_mix', 'g_pre_ffn', 'g_post_ffn', 'w_in', 'lru_conv_w',
           'lru_conv_b', 'lru_w_a', 'lru_b_a', 'lru_w_x', 'lru_b_x', 'lru_lambda', 'mla_g_q', 'mla_w_uq', 'mla_g_kv',
           'mla_w_ukv', 'w_out', 'ffn_w_up', 'ffn_conv_w', 'ffn_conv_b', 'ffn_w_down']
REPLICATED = ['c_ctx', 'b_mod', 'g_pre_mix', 'g_post_mix', 'g_pre_ffn', 'g_post_ffn', 'lru_conv_b', 'lru_w_a',
              'lru_w_x', 'mla_g_q', 'mla_g_kv', 'ffn_conv_b']
SMALL_SHARDED = ['lru_conv_w', 'lru_b_a', 'lru_b_x', 'lru_lambda', 'ffn_conv_w']


def _pick(d, prefs):
    for p in prefs:
        if d % p == 0:
            return p
    return d


def _params(sem=None):
    return pltpu.CompilerParams(dimension_semantics=sem, vmem_limit_bytes=VMEM_LIMIT)


MM_TILES = (1024, 1408, 768, 512, 256, 128)


def _mm(name, a, b, mode, out_dtype=F32):
    if mode == 'nn':
        (m, k), (_, n) = a.shape, b.shape
    elif mode == 'nt':
        (m, k), (n, _) = a.shape, b.shape
    else:
        (k, m), (_, n) = a.shape, b.shape
    tm = _pick(m, MM_TILES)
    tn = _pick(n, MM_TILES)
    tk = _pick(k, MM_TILES)
    nk = k // tk
    if mode == 'nn':
        a_spec = pl.BlockSpec((tm, tk), lambda i, j, kk: (i, kk))
        b_spec = pl.BlockSpec((tk, tn), lambda i, j, kk: (kk, j))
        dn = (((1,), (0,)), ((), ()))
    elif mode == 'nt':
        a_spec = pl.BlockSpec((tm, tk), lambda i, j, kk: (i, kk))
        b_spec = pl.BlockSpec((tn, tk), lambda i, j, kk: (j, kk))
        dn = (((1,), (1,)), ((), ()))
    else:
        a_spec = pl.BlockSpec((tk, tm), lambda i, j, kk: (kk, i))
        b_spec = pl.BlockSpec((tk, tn), lambda i, j, kk: (kk, j))
        dn = (((0,), (0,)), ((), ()))

    def body(a_ref, b_ref, o_ref, acc_ref):
        kk = pl.program_id(2)

        @pl.when(kk == 0)
        def _():
            acc_ref[...] = jnp.zeros_like(acc_ref)

        acc_ref[...] += lax.dot_general(a_ref[...].astype(BF16), b_ref[...].astype(BF16), dn,
                                        preferred_element_type=F32)

        @pl.when(kk == nk - 1)
        def _():
            o_ref[...] = acc_ref[...].astype(o_ref.dtype)

    return pl.pallas_call(
        body, name=name, grid=(m // tm, n // tn, nk),
        in_specs=[a_spec, b_spec], out_specs=pl.BlockSpec((tm, tn), lambda i, j, kk: (i, j)),
        out_shape=jax.ShapeDtypeStruct((m, n), out_dtype),
        scratch_shapes=[pltpu.VMEM((tm, tn), F32)],
        compiler_params=_params(("parallel", "parallel", "arbitrary")),
    )(a, b)


def _row(a, idx=None, col=None):
    return dict(a=a, idx=idx, col=col)


def _rowwise(name, fn, n_tiles, rows, bcast, out_rows, out_acc=(), tm=ROW_TILE):
    in_specs = []
    for r in rows:
        a, idx, col = r['a'], r['idx'] or (lambda i: i), r['col']
        if a.ndim == 2:
            w, ci = col if col else (a.shape[1], 0)
            in_specs.append(pl.BlockSpec((tm, w), lambda i, idx=idx, ci=ci: (idx(i), ci)))
        else:
            in_specs.append(pl.BlockSpec((a.shape[0], tm, a.shape[2]), lambda i, idx=idx: (0, idx(i), 0)))
    for b in bcast:
        in_specs.append(pl.BlockSpec(b.shape, lambda i, nd=b.ndim: (0,) * nd))
    out_specs, out_shape = [], []
    for shape, dtype, idx in out_rows:
        idx = idx or (lambda i: i)
        if len(shape) == 2:
            out_specs.append(pl.BlockSpec((tm, shape[1]), lambda i, idx=idx: (idx(i), 0)))
        else:
            out_specs.append(pl.BlockSpec((shape[0], tm, shape[2]), lambda i, idx=idx: (0, idx(i), 0)))
        out_shape.append(jax.ShapeDtypeStruct(shape, dtype))
    for shape in out_acc:
        out_specs.append(pl.BlockSpec(shape, lambda i, nd=len(shape): (0,) * nd))
        out_shape.append(jax.ShapeDtypeStruct(shape, F32))
    nr, nb, no = len(rows), len(bcast), len(out_rows)

    def body(*refs):
        i = pl.program_id(0)
        rvals = [r[...] for r in refs[:nr]]
        bvals = list(refs[nr:nr + nb])
        o_rows, o_acc = fn(i, rvals, bvals)
        for ref, v in zip(refs[nr + nb:nr + nb + no], o_rows):
            ref[...] = v.astype(ref.dtype)
        acc_refs = refs[nr + nb + no:]
        if acc_refs:
            @pl.when(i == 0)
            def _():
                for ref in acc_refs:
                    ref[...] = jnp.zeros_like(ref)
            for ref, v in zip(acc_refs, o_acc):
                ref[...] += v

    return pl.pallas_call(
        body, name=name, grid=(n_tiles,), in_specs=in_specs, out_specs=out_specs, out_shape=out_shape,
        compiler_params=_params(("arbitrary",)),
    )(*[r['a'] for r in rows], *bcast)


def _single(name, fn, ins, out_shapes):
    def body(*refs):
        outs = fn(*refs[:len(ins)])
        for ref, v in zip(refs[len(ins):], outs):
            ref[...] = v.astype(ref.dtype)

    return pl.pallas_call(
        body, name=name,
        in_specs=[pl.BlockSpec(memory_space=pltpu.VMEM)] * len(ins),
        out_specs=[pl.BlockSpec(memory_space=pltpu.VMEM)] * len(out_shapes),
        out_shape=[jax.ShapeDtypeStruct(s, d) for s, d in out_shapes],
        compiler_params=_params(),
    )(*ins)


def _bc(p, n):
    return jnp.broadcast_to(p, (n, p.shape[-1]))


def _rs(g):
    return jnp.sum(g, axis=0, keepdims=True)


def _rms(x, g):
    return x * lax.rsqrt(jnp.mean(x * x, axis=-1, keepdims=True) + EPS) * g


def _conv_specs(r, cw, tm, halo=SUBLANES):
    th = tm // halo
    last = r // halo - 1
    prev = pl.BlockSpec((halo, cw), lambda c, i: (jnp.maximum(i * th - 1, 0), c))
    cur = pl.BlockSpec((tm, cw), lambda c, i: (i, c))
    nxt = pl.BlockSpec((halo, cw), lambda c, i: (jnp.minimum((i + 1) * th, last), c))
    return [prev, cur, nxt]


def _fill_ext(ext_ref, prev_ref, cur_ref, next_ref, i, n_tiles, seg_starts, tm):
    prev_ok = functools.reduce(jnp.logical_and, [i != s for s in seg_starts])
    next_ok = functools.reduce(jnp.logical_and, [i + 1 != s for s in seg_starts] + [i + 1 < n_tiles])
    ext_ref[0:SUBLANES, :] = jnp.where(prev_ok, prev_ref[...].astype(F32), 0.0)
    ext_ref[SUBLANES:SUBLANES + tm, :] = cur_ref[...].astype(F32)
    ext_ref[SUBLANES + tm:, :] = jnp.where(next_ok, next_ref[...].astype(F32), 0.0)


def _dwconv_fwd(name, x, w, b, left, seg_starts, cw=512, tm=ROW_TILE):
    r, c = x.shape[0], w.shape[1]
    kw = w.shape[0]
    n_tiles = r // tm

    def body(prev_ref, cur_ref, next_ref, w_ref, b_ref, o_ref, ext_ref):
        i = pl.program_id(1)
        _fill_ext(ext_ref, prev_ref, cur_ref, next_ref, i, n_tiles, seg_starts, tm)
        out = jnp.broadcast_to(b_ref[...], (tm, cw))
        for k in range(kw):
            out = out + ext_ref[pl.ds(SUBLANES + k - left, tm), :] * w_ref[k:k + 1, :]
        o_ref[...] = out

    return pl.pallas_call(
        body, name=name, grid=(c // cw, n_tiles),
        in_specs=_conv_specs(r, cw, tm) + [pl.BlockSpec((kw, cw), lambda c_, i: (0, c_)),
                                           pl.BlockSpec((1, cw), lambda c_, i: (0, c_))],
        out_specs=pl.BlockSpec((tm, cw), lambda c_, i: (i, c_)),
        out_shape=jax.ShapeDtypeStruct((r, c), F32),
        scratch_shapes=[pltpu.VMEM((tm + 2 * SUBLANES, cw), F32)],
        compiler_params=_params(("parallel", "arbitrary")),
    )(x, x, x, w, b)


def _dwconv_bwd(name, x, dy, w, left, seg_starts, out_dtype=F32, cw=512, tm=ROW_TILE):
    r, c = dy.shape
    kw = w.shape[0]
    n_tiles = r // tm

    def body(xp, xc, xn, dp, dc, dn, w_ref, dx_ref, dw_ref, db_ref, xe_ref, de_ref):
        i = pl.program_id(1)
        _fill_ext(xe_ref, xp, xc, xn, i, n_tiles, seg_starts, tm)
        _fill_ext(de_ref, dp, dc, dn, i, n_tiles, seg_starts, tm)
        dyc = dc[...].astype(F32)
        dx = jnp.zeros((tm, cw), F32)
        dws = []
        for k in range(kw):
            dx = dx + de_ref[pl.ds(SUBLANES - k + left, tm), :] * w_ref[k:k + 1, :]
            dws.append(jnp.sum(dyc * xe_ref[pl.ds(SUBLANES + k - left, tm), :], axis=0, keepdims=True))
        dx_ref[...] = dx.astype(dx_ref.dtype)

        @pl.when(i == 0)
        def _():
            dw_ref[...] = jnp.zeros_like(dw_ref)
            db_ref[...] = jnp.zeros_like(db_ref)

        for k in range(kw):
            dw_ref[k:k + 1, :] += dws[k]
        db_ref[...] += jnp.sum(dyc, axis=0, keepdims=True)

    return pl.pallas_call(
        body, name=name, grid=(c // cw, n_tiles),
        in_specs=_conv_specs(r, cw, tm) + _conv_specs(r, cw, tm) + [pl.BlockSpec((kw, cw), lambda c_, i: (0, c_))],
        out_specs=[pl.BlockSpec((tm, cw), lambda c_, i: (i, c_)),
                   pl.BlockSpec((kw, cw), lambda c_, i: (0, c_)),
                   pl.BlockSpec((1, cw), lambda c_, i: (0, c_))],
        out_shape=[jax.ShapeDtypeStruct((r, c), out_dtype), jax.ShapeDtypeStruct((kw, c), F32),
                   jax.ShapeDtypeStruct((1, c), F32)],
        scratch_shapes=[pltpu.VMEM((tm + 2 * SUBLANES, cw), F32), pltpu.VMEM((tm + 2 * SUBLANES, cw), F32)],
        compiler_params=_params(("parallel", "arbitrary")),
    )(x, x, x, dy, dy, dy, w)


FF_TILE = 256
FF_HALO = 16


def _ffn_fill(ext_ref, prev_ref, cur_ref, next_ref, i, n_tiles, tm):
    ext_ref[0:FF_HALO, :] = jnp.where(i > 0, prev_ref[...].astype(F32), 0.0)
    ext_ref[FF_HALO:FF_HALO + tm, :] = cur_ref[...].astype(F32)
    ext_ref[FF_HALO + tm:, :] = jnp.where(i + 1 < n_tiles, next_ref[...].astype(F32), 0.0)


def _ffn_conv(ext_ref, w_ref, b_ref, start, rows):
    out = jnp.broadcast_to(b_ref[...], (rows, 2 * FF_TILE))
    for k in range(FFN_CONV_K):
        out = out + ext_ref[pl.ds(start + k - FFN_CONV_LEFT, rows), :] * w_ref[k:k + 1, :]
    return out


def _ffn_mid_fwd(up, w, b, tm=ROW_TILE):
    s, c2 = up.shape
    n_tiles = s // tm
    cw = 2 * FF_TILE

    def body(prev_ref, cur_ref, next_ref, w_ref, b_ref, o_ref, ext_ref):
        i = pl.program_id(1)
        _ffn_fill(ext_ref, prev_ref, cur_ref, next_ref, i, n_tiles, tm)
        upc = _ffn_conv(ext_ref, w_ref, b_ref, FF_HALO, tm)
        uv, gv = upc[:, :FF_TILE], upc[:, FF_TILE:]
        o_ref[...] = (gv * jax.nn.sigmoid(gv) * uv).astype(o_ref.dtype)

    return pl.pallas_call(
        body, name="ffn_mid", grid=(c2 // cw, n_tiles),
        in_specs=_conv_specs(s, cw, tm, FF_HALO) + [pl.BlockSpec((FFN_CONV_K, cw), lambda c_, i: (0, c_)),
                                                   pl.BlockSpec((1, cw), lambda c_, i: (0, c_))],
        out_specs=pl.BlockSpec((tm, FF_TILE), lambda c_, i: (i, c_)),
        out_shape=jax.ShapeDtypeStruct((s, c2 // 2), BF16),
        scratch_shapes=[pltpu.VMEM((tm + 2 * FF_HALO, cw), F32)],
        compiler_params=_params(("parallel", "arbitrary")),
    )(up, up, up, w, b)


def _ffn_mid_bwd(up, d_act, w, b, tm=ROW_TILE):
    s, c2 = up.shape
    n_tiles = s // tm
    cw = 2 * FF_TILE
    h8 = SUBLANES

    def gate_bwd(upc, dact):
        uv, gv = upc[:, :FF_TILE], upc[:, FF_TILE:]
        sg = jax.nn.sigmoid(gv)
        return jnp.concatenate([dact * (gv * sg), dact * uv * (sg * (1.0 + gv * (1.0 - sg)))], axis=1)

    def body(up_p, up_c, up_n, da_p, da_c, da_n, w_ref, b_ref, dup_ref, dw_ref, db_ref, ext_ref, dext_ref):
        i = pl.program_id(1)
        _ffn_fill(ext_ref, up_p, up_c, up_n, i, n_tiles, tm)
        d_c = gate_bwd(_ffn_conv(ext_ref, w_ref, b_ref, FF_HALO, tm), da_c[...].astype(F32))
        da_prev = jnp.where(i > 0, da_p[...].astype(F32)[FF_HALO - h8:, :], 0.0)
        da_next = jnp.where(i + 1 < n_tiles, da_n[...].astype(F32)[:h8, :], 0.0)
        dext_ref[FF_HALO - h8:FF_HALO, :] = gate_bwd(_ffn_conv(ext_ref, w_ref, b_ref, FF_HALO - h8, h8), da_prev)
        dext_ref[FF_HALO:FF_HALO + tm, :] = d_c
        dext_ref[FF_HALO + tm:FF_HALO + tm + h8, :] = gate_bwd(_ffn_conv(ext_ref, w_ref, b_ref, FF_HALO + tm, h8), da_next)
        dup = jnp.zeros((tm, cw), F32)
        for k in range(FFN_CONV_K):
            dup = dup + dext_ref[pl.ds(FF_HALO - k + FFN_CONV_LEFT, tm), :] * w_ref[k:k + 1, :]
        dup_ref[...] = dup.astype(dup_ref.dtype)

        @pl.when(i == 0)
        def _():
            dw_ref[...] = jnp.zeros_like(dw_ref)
            db_ref[...] = jnp.zeros_like(db_ref)

        for k in range(FFN_CONV_K):
            dw_ref[k:k + 1, :] += jnp.sum(d_c * ext_ref[pl.ds(FF_HALO + k - FFN_CONV_LEFT, tm), :], axis=0, keepdims=True)
        db_ref[...] += jnp.sum(d_c, axis=0, keepdims=True)

    def half_specs():
        th = tm // FF_HALO
        last = s // FF_HALO - 1
        return [pl.BlockSpec((FF_HALO, FF_TILE), lambda c_, i: (jnp.maximum(i * th - 1, 0), c_)),
                pl.BlockSpec((tm, FF_TILE), lambda c_, i: (i, c_)),
                pl.BlockSpec((FF_HALO, FF_TILE), lambda c_, i: (jnp.minimum((i + 1) * th, last), c_))]

    return pl.pallas_call(
        body, name="ffn_mid_bwd", grid=(c2 // cw, n_tiles),
        in_specs=_conv_specs(s, cw, tm, FF_HALO) + half_specs() + [
            pl.BlockSpec((FFN_CONV_K, cw), lambda c_, i: (0, c_)), pl.BlockSpec((1, cw), lambda c_, i: (0, c_))],
        out_specs=[pl.BlockSpec((tm, cw), lambda c_, i: (i, c_)),
                   pl.BlockSpec((FFN_CONV_K, cw), lambda c_, i: (0, c_)),
                   pl.BlockSpec((1, cw), lambda c_, i: (0, c_))],
        out_shape=[jax.ShapeDtypeStruct((s, c2), BF16), jax.ShapeDtypeStruct((FFN_CONV_K, c2), F32),
                   jax.ShapeDtypeStruct((1, c2), F32)],
        scratch_shapes=[pltpu.VMEM((tm + 2 * FF_HALO, cw), F32), pltpu.VMEM((tm + 2 * FF_HALO, cw), F32)],
        compiler_params=_params(("parallel", "arbitrary")),
    )(up, up, up, d_act, d_act, d_act, w, b)


def _ff_to_tiles(w):
    r = w.shape[0]
    return w.reshape(r, 2, D_FF // FF_TILE, FF_TILE).transpose(0, 2, 1, 3).reshape(r, 2 * D_FF)


def _ff_from_tiles(w):
    r = w.shape[0]
    return w.reshape(r, D_FF // FF_TILE, 2, FF_TILE).transpose(0, 2, 1, 3).reshape(r, 2 * D_FF)


def _scan(name, a, u, reverse):
    t, c = a.shape
    n8 = t // SUBLANES

    def body(a_ref, u_ref, h_ref):
        row = lax.broadcasted_iota(jnp.int32, (SUBLANES, LANES), 0)
        last = 0 if reverse else SUBLANES - 1

        def step(j, carry):
            blk = (n8 - 1 - j) if reverse else j
            base = pl.multiple_of(blk * SUBLANES, SUBLANES)
            av = a_ref[pl.ds(base, SUBLANES), :]
            hv = u_ref[pl.ds(base, SUBLANES), :]
            for s in (1, 2, 4):
                shift = SUBLANES - s if reverse else s
                ok = (row < SUBLANES - s) if reverse else (row >= s)
                a_sh = jnp.where(ok, pltpu.roll(av, shift, 0), 1.0)
                h_sh = jnp.where(ok, pltpu.roll(hv, shift, 0), 0.0)
                hv = av * h_sh + hv
                av = av * a_sh
            hv = av * carry + hv
            h_ref[pl.ds(base, SUBLANES), :] = hv
            return jnp.sum(jnp.where(row == last, hv, 0.0), axis=0, keepdims=True)

        lax.fori_loop(0, n8, step, jnp.zeros((1, LANES), F32))

    return pl.pallas_call(
        body, name=name, grid=(c // LANES,),
        in_specs=[pl.BlockSpec((t, LANES), lambda j: (0, j))] * 2,
        out_specs=pl.BlockSpec((t, LANES), lambda j: (0, j)),
        out_shape=jax.ShapeDtypeStruct((t, c), F32),
        compiler_params=_params(("parallel",)),
    )(a, u)


NT_DIMS = (((1,), (1,)), ((), ()))


LOG2E = 1.4426950408889634
SCALE2 = MLA_SCALE * LOG2E
ATTN_TILES = (512, 256, 128)
KEY_CHUNKS = (768, 512, 256, 128)
QUERY_CHUNKS = (1024, 512, 256, 128)


def _attn_fwd(q, k, v):
    h, s, _ = q.shape
    t = k.shape[1]
    tq = _pick(s, ATTN_TILES)
    ck = _pick(t, KEY_CHUNKS)

    def body(q_ref, k_ref, v_ref, o_ref, lse_ref):
        qv = q_ref[0]
        m = l = acc = None
        for j in range(t // ck):
            kj, vj = k_ref[0, j * ck:(j + 1) * ck, :], v_ref[0, j * ck:(j + 1) * ck, :]
            s2 = lax.dot_general(qv, kj, NT_DIMS, preferred_element_type=F32) * SCALE2
            mj = jnp.max(s2, axis=-1, keepdims=True)
            m_new = mj if j == 0 else jnp.maximum(m, mj)
            p = jnp.exp2(s2 - m_new)
            lj = jnp.sum(p, axis=-1, keepdims=True)
            pv = jnp.dot(p.astype(BF16), vj, preferred_element_type=F32)
            if j == 0:
                l, acc = lj, pv
            else:
                alpha = jnp.exp2(m - m_new)
                l, acc = alpha * l + lj, alpha * acc + pv
            m = m_new
        o_ref[0] = acc / l
        lse_ref[0] = m + jnp.log2(l)

    return pl.pallas_call(
        body, name="attn_fwd", grid=(h, s // tq),
        in_specs=[pl.BlockSpec((1, tq, HEAD_PAD), lambda hh, i: (hh, i, 0)),
                  pl.BlockSpec((1, t, HEAD_PAD), lambda hh, i: (hh, 0, 0)),
                  pl.BlockSpec((1, t, VDIM), lambda hh, i: (hh, 0, 0))],
        out_specs=[pl.BlockSpec((1, tq, VDIM), lambda hh, i: (hh, i, 0)),
                   pl.BlockSpec((1, tq, 1), lambda hh, i: (hh, i, 0))],
        out_shape=[jax.ShapeDtypeStruct((h, s, VDIM), F32), jax.ShapeDtypeStruct((h, s, 1), F32)],
        compiler_params=_params(("parallel", "arbitrary")),
    )(q, k, v)


def _attn_dq(q, k, v, o, do, lse):
    h, s, _ = q.shape
    t = k.shape[1]
    tq = _pick(s, ATTN_TILES)
    ck = _pick(t, KEY_CHUNKS)

    def body(q_ref, k_ref, v_ref, o_ref, do_ref, lse_ref, dq_ref, delta_ref):
        qv, dof, lse2 = q_ref[0], do_ref[0], lse_ref[0]
        dob = dof.astype(BF16)
        delta = jnp.sum(dof * o_ref[0], axis=-1, keepdims=True)
        acc = None
        for j in range(t // ck):
            kj, vj = k_ref[0, j * ck:(j + 1) * ck, :], v_ref[0, j * ck:(j + 1) * ck, :]
            p = jnp.exp2(lax.dot_general(qv, kj, NT_DIMS, preferred_element_type=F32) * SCALE2 - lse2)
            dp = lax.dot_general(dob, vj, NT_DIMS, preferred_element_type=F32)
            part = jnp.dot((p * (dp - delta)).astype(BF16), kj, preferred_element_type=F32)
            acc = part if j == 0 else acc + part
        dq_ref[0] = acc * MLA_SCALE
        delta_ref[0] = delta

    return pl.pallas_call(
        body, name="attn_dq", grid=(h, s // tq),
        in_specs=[pl.BlockSpec((1, tq, HEAD_PAD), lambda hh, i: (hh, i, 0)),
                  pl.BlockSpec((1, t, HEAD_PAD), lambda hh, i: (hh, 0, 0)),
                  pl.BlockSpec((1, t, VDIM), lambda hh, i: (hh, 0, 0)),
                  pl.BlockSpec((1, tq, VDIM), lambda hh, i: (hh, i, 0)),
                  pl.BlockSpec((1, tq, VDIM), lambda hh, i: (hh, i, 0)),
                  pl.BlockSpec((1, tq, 1), lambda hh, i: (hh, i, 0))],
        out_specs=[pl.BlockSpec((1, tq, HEAD_PAD), lambda hh, i: (hh, i, 0)),
                   pl.BlockSpec((1, tq, 1), lambda hh, i: (hh, i, 0))],
        out_shape=[jax.ShapeDtypeStruct((h, s, HEAD_PAD), F32), jax.ShapeDtypeStruct((h, s, 1), F32)],
        compiler_params=_params(("parallel", "arbitrary")),
    )(q, k, v, o, do, lse)


def _attn_dkv(q, k, v, do, lse_row, delta_row):
    h, s, _ = q.shape
    t = k.shape[1]
    tk = _pick(t, (768,) + ATTN_TILES)
    cq = _pick(s, QUERY_CHUNKS)

    def body(q_ref, k_ref, v_ref, do_ref, lse_ref, delta_ref, dk_ref, dv_ref):
        kt, vt = k_ref[0], v_ref[0]
        dk = dv = None
        for j in range(s // cq):
            qj, doj = q_ref[0, j * cq:(j + 1) * cq, :], do_ref[0, j * cq:(j + 1) * cq, :]
            lse2, delta = lse_ref[0, :, j * cq:(j + 1) * cq], delta_ref[0, :, j * cq:(j + 1) * cq]
            pt = jnp.exp2(lax.dot_general(kt, qj, NT_DIMS, preferred_element_type=F32) * SCALE2 - lse2)
            dv_j = jnp.dot(pt.astype(BF16), doj, preferred_element_type=F32)
            dpt = lax.dot_general(vt, doj, NT_DIMS, preferred_element_type=F32)
            dk_j = jnp.dot((pt * (dpt - delta)).astype(BF16), qj, preferred_element_type=F32)
            dk, dv = (dk_j, dv_j) if j == 0 else (dk + dk_j, dv + dv_j)
        dk_ref[0] = dk * MLA_SCALE
        dv_ref[0] = dv

    return pl.pallas_call(
        body, name="attn_dkv", grid=(h, t // tk),
        in_specs=[pl.BlockSpec((1, s, HEAD_PAD), lambda hh, i: (hh, 0, 0)),
                  pl.BlockSpec((1, tk, HEAD_PAD), lambda hh, i: (hh, i, 0)),
                  pl.BlockSpec((1, tk, VDIM), lambda hh, i: (hh, i, 0)),
                  pl.BlockSpec((1, s, VDIM), lambda hh, i: (hh, 0, 0)),
                  pl.BlockSpec((1, 1, s), lambda hh, i: (hh, 0, 0)),
                  pl.BlockSpec((1, 1, s), lambda hh, i: (hh, 0, 0))],
        out_specs=[pl.BlockSpec((1, tk, HEAD_PAD), lambda hh, i: (hh, i, 0)),
                   pl.BlockSpec((1, tk, VDIM), lambda hh, i: (hh, i, 0))],
        out_shape=[jax.ShapeDtypeStruct((h, t, HEAD_PAD), F32), jax.ShapeDtypeStruct((h, t, VDIM), F32)],
        compiler_params=_params(("parallel", "arbitrary")),
    )(q, k, v, do, lse_row, delta_row)


def _exchange(name, arrs, modes):
    n = len(arrs)
    out_shape = [jax.ShapeDtypeStruct((N_DEV,) + a.shape if md == 'ag' else a.shape, a.dtype)
                 for a, md in zip(arrs, modes)]

    def body(*refs):
        ins, outs = refs[:n], refs[n:2 * n]
        send_sems, recv_sems, local_sems = refs[2 * n:]
        x, y, c = lax.axis_index("x"), lax.axis_index("y"), lax.axis_index("c")
        me = 4 * x + 2 * y + c
        copies = []
        for a in range(n):
            ag = modes[a] == 'ag'
            mine = pltpu.make_async_copy(ins[a] if ag else ins[a].at[me], outs[a].at[me], local_sems.at[a])
            mine.start()
            copies.append(mine)
            for k in range(1, N_DEV):
                px = 1 - x if k & 4 else x
                py = 1 - y if k & 2 else y
                pc = 1 - c if k & 1 else c
                src = ins[a] if ag else ins[a].at[4 * px + 2 * py + pc]
                cp = pltpu.make_async_remote_copy(
                    src_ref=src, dst_ref=outs[a].at[me], send_sem=send_sems.at[a, k - 1],
                    recv_sem=recv_sems.at[a, k - 1], device_id=(px, py, pc), device_id_type=MESH)
                cp.start()
                copies.append(cp)
        for cp in copies:
            cp.wait()

    return pl.pallas_call(
        body, name=name,
        in_specs=[pl.BlockSpec(memory_space=pl.ANY)] * n,
        out_specs=[pl.BlockSpec(memory_space=pl.ANY)] * n,
        out_shape=out_shape,
        scratch_shapes=[pltpu.SemaphoreType.DMA((n, N_DEV - 1)), pltpu.SemaphoreType.DMA((n, N_DEV - 1)),
                        pltpu.SemaphoreType.DMA((n,))],
        compiler_params=pltpu.CompilerParams(has_side_effects=True),
    )(*arrs)


def _adamw(name, w, m, v, gparts):
    r, c = w.shape
    npart = gparts.shape[0]
    tr = _pick(r, (256, 128, 64, 32, 16, 8))
    spec = pl.BlockSpec((tr, c), lambda i: (i, 0))

    def body(w_ref, m_ref, v_ref, g_ref, go_ref, d_ref, mo_ref, vo_ref):
        g = g_ref[0]
        for p in range(1, npart):
            g = g + g_ref[p]
        m1 = ADAM_B1 * m_ref[...] + (1.0 - ADAM_B1) * g
        v1 = ADAM_B2 * v_ref[...] + (1.0 - ADAM_B2) * (g * g)
        m_hat = m1 / (1.0 - ADAM_B1 ** ADAM_STEP)
        v_hat = v1 / (1.0 - ADAM_B2 ** ADAM_STEP)
        go_ref[...] = g
        d_ref[...] = -ADAM_LR * (m_hat / (jnp.sqrt(v_hat) + ADAM_EPS) + ADAM_WD * w_ref[...])
        mo_ref[...] = m1
        vo_ref[...] = v1

    return pl.pallas_call(
        body, name=name, grid=(r // tr,),
        in_specs=[spec, spec, spec, pl.BlockSpec((npart, tr, c), lambda i: (0, i, 0))],
        out_specs=[spec] * 4, out_shape=[jax.ShapeDtypeStruct((r, c), F32)] * 4,
        compiler_params=_params(("parallel",)),
    )(w, m, v, gparts)


def _pack(arrs, rows):
    flat = jnp.concatenate([a.reshape(-1) for a in arrs])
    return jnp.pad(flat, (0, rows * LANES - flat.shape[0])).reshape(rows, LANES)


def _unpack(packed, shapes):
    flat, out, off = packed.reshape(-1), [], 0
    for s in shapes:
        n = math.prod(s)
        out.append(flat[off:off + n].reshape(s))
        off += n
    return out


def _pack_rows(n_elems):
    return -(-n_elems // (SUBLANES * LANES)) * SUBLANES


def _cols_from_shards(g):
    return g.transpose(1, 0, 2).reshape(g.shape[1], N_DEV * g.shape[2])


def _cols_to_shards(w):
    r, c = w.shape
    return w.reshape(r, N_DEV, c // N_DEV).transpose(1, 0, 2)


def _rope_tables(n_lat, n_ctx):
    rows = n_lat // GRID_W
    inv = ROPE_BASE ** (-jnp.arange(ROPE_PAIRS, dtype=F32) / ROPE_PAIRS)
    ang_r = jnp.arange(rows, dtype=F32)[:, None] * inv
    ang_c = jnp.arange(GRID_W, dtype=F32)[:, None] * inv
    cr, sr = jnp.repeat(jnp.cos(ang_r), GRID_W, axis=0), jnp.repeat(jnp.sin(ang_r), GRID_W, axis=0)
    cc, sc = jnp.tile(jnp.cos(ang_c), (rows, 1)), jnp.tile(jnp.sin(ang_c), (rows, 1))
    one, zero = jnp.ones((n_lat, 1), F32), jnp.zeros((n_lat, 1), F32)
    z8 = jnp.zeros((n_lat, ROPE_PAIRS), F32)
    cos_t = jnp.concatenate([jnp.tile(one, (1, NOPE)), cr, cr, cc, cc, jnp.tile(one, (1, HEAD_PAD - QK))], 1)
    sin_up = jnp.concatenate([jnp.tile(zero, (1, NOPE)), -sr, z8, -sc, z8, jnp.tile(zero, (1, HEAD_PAD - QK))], 1)
    sin_dn = jnp.concatenate([jnp.tile(zero, (1, NOPE)), z8, sr, z8, sc, jnp.tile(zero, (1, HEAD_PAD - QK))], 1)
    if n_ctx:
        cos_t = jnp.concatenate([jnp.ones((n_ctx, HEAD_PAD), F32), cos_t])
        sin_up = jnp.concatenate([jnp.zeros((n_ctx, HEAD_PAD), F32), sin_up])
        sin_dn = jnp.concatenate([jnp.zeros((n_ctx, HEAD_PAD), F32), sin_dn])
    return cos_t, sin_up, sin_dn


def _rope(x, cos_t, sin_up, sin_dn):
    return x * cos_t + pltpu.roll(x, HEAD_PAD - ROPE_PAIRS, 1) * sin_up + pltpu.roll(x, ROPE_PAIRS, 1) * sin_dn


def _rope_t(dy, cos_t, sin_up, sin_dn):
    return (dy * cos_t + pltpu.roll(dy * sin_up, ROPE_PAIRS, 1)
            + pltpu.roll(dy * sin_dn, HEAD_PAD - ROPE_PAIRS, 1))


def _softplus(x):
    return jnp.maximum(x, 0.0) + jnp.log(1.0 + jnp.exp(-jnp.abs(x)))


def _gates(z, xcv, lam_sp):
    outs = []
    for d in range(2):
        r = jax.nn.sigmoid(z[:, (2 * d) * LRU_W:(2 * d + 1) * LRU_W])
        ig = jax.nn.sigmoid(z[:, (2 * d + 1) * LRU_W:(2 * d + 2) * LRU_W])
        log_a = -LRU_C * r * lam_sp[:, d * LRU_W:(d + 1) * LRU_W]
        a = jnp.exp(log_a)
        u = jnp.sqrt(-jnp.tanh(log_a) * (a * a + 1.0)) * (ig * xcv)
        outs += [a, u]
    return tuple(outs)


def kernel(x, c, ctx, c_ctx, w_mod, b_mod, g_pre_mix, g_post_mix, g_pre_ffn, g_post_ffn, w_in, lru_conv_w, lru_conv_b, lru_w_a, lru_b_a, lru_w_x, lru_b_x, lru_lambda, mla_g_q, mla_w_uq, mla_g_kv, mla_w_ukv, w_out, ffn_w_up, ffn_conv_w, ffn_conv_b, ffn_w_down, loss_target, m_c_ctx, m_w_mod, m_b_mod, m_g_pre_mix, m_g_post_mix, m_g_pre_ffn, m_g_post_ffn, m_w_in, m_lru_conv_w, m_lru_conv_b, m_lru_w_a, m_lru_b_a, m_lru_w_x, m_lru_b_x, m_lru_lambda, m_mla_g_q, m_mla_w_uq, m_mla_g_kv, m_mla_w_ukv, m_w_out, m_ffn_w_up, m_ffn_conv_w, m_ffn_conv_b, m_ffn_w_down, v_c_ctx, v_w_mod, v_b_mod, v_g_pre_mix, v_g_post_mix, v_g_pre_ffn, v_g_post_ffn, v_w_in, v_lru_conv_w, v_lru_conv_b, v_lru_w_a, v_lru_b_a, v_lru_w_x, v_lru_b_x, v_lru_lambda, v_mla_g_q, v_mla_w_uq, v_mla_g_kv, v_mla_w_ukv, v_w_out, v_ffn_w_up, v_ffn_conv_w, v_ffn_conv_b, v_ffn_w_down):
    W = dict(c_ctx=c_ctx, w_mod=w_mod, b_mod=b_mod, g_pre_mix=g_pre_mix, g_post_mix=g_post_mix, g_pre_ffn=g_pre_ffn,
             g_post_ffn=g_post_ffn, w_in=w_in, lru_conv_w=lru_conv_w, lru_conv_b=lru_conv_b, lru_w_a=lru_w_a,
             lru_b_a=lru_b_a, lru_w_x=lru_w_x, lru_b_x=lru_b_x, lru_lambda=lru_lambda, mla_g_q=mla_g_q,
             mla_w_uq=mla_w_uq, mla_g_kv=mla_g_kv, mla_w_ukv=mla_w_ukv, w_out=w_out, ffn_w_up=ffn_w_up,
             ffn_conv_w=ffn_conv_w, ffn_conv_b=ffn_conv_b, ffn_w_down=ffn_w_down)
    M = dict(c_ctx=m_c_ctx, w_mod=m_w_mod, b_mod=m_b_mod, g_pre_mix=m_g_pre_mix, g_post_mix=m_g_post_mix,
             g_pre_ffn=m_g_pre_ffn, g_post_ffn=m_g_post_ffn, w_in=m_w_in, lru_conv_w=m_lru_conv_w,
             lru_conv_b=m_lru_conv_b, lru_w_a=m_lru_w_a, lru_b_a=m_lru_b_a, lru_w_x=m_lru_w_x, lru_b_x=m_lru_b_x,
             lru_lambda=m_lru_lambda, mla_g_q=m_mla_g_q, mla_w_uq=m_mla_w_uq, mla_g_kv=m_mla_g_kv,
             mla_w_ukv=m_mla_w_ukv, w_out=m_w_out, ffn_w_up=m_ffn_w_up, ffn_conv_w=m_ffn_conv_w,
             ffn_conv_b=m_ffn_conv_b, ffn_w_down=m_ffn_w_down)
    V = dict(c_ctx=v_c_ctx, w_mod=v_w_mod, b_mod=v_b_mod, g_pre_mix=v_g_pre_mix, g_post_mix=v_g_post_mix,
             g_pre_ffn=v_g_pre_ffn, g_post_ffn=v_g_post_ffn, w_in=v_w_in, lru_conv_w=v_lru_conv_w,
             lru_conv_b=v_lru_conv_b, lru_w_a=v_lru_w_a, lru_b_a=v_lru_b_a, lru_w_x=v_lru_w_x, lru_b_x=v_lru_b_x,
             lru_lambda=v_lru_lambda, mla_g_q=v_mla_g_q, mla_w_uq=v_mla_w_uq, mla_g_kv=v_mla_g_kv,
             mla_w_ukv=v_mla_w_ukv, w_out=v_w_out, ffn_w_up=v_ffn_w_up, ffn_conv_w=v_ffn_conv_w,
             ffn_conv_b=v_ffn_conv_b, ffn_w_down=v_ffn_w_down)

    D = D_MODEL
    S, CN = x.shape[1], ctx.shape[1]
    T = S + CN
    TM = ROW_TILE
    ct, ns, nt = CN // TM, S // TM, T // TM
    me = 4 * lax.axis_index("x") + 2 * lax.axis_index("y") + lax.axis_index("c")

    lat = lambda i: i + ct
    swp = lambda i: jnp.where(i < ct, i + ns, i - ct)
    lat_or_0 = lambda i: jnp.maximum(i - ct, 0)

    small_shapes = [W[n].shape[1:] for n in SMALL_SHARDED] + [(D,)]
    n_small = sum(math.prod(s) for s in small_shapes)
    small_rows = _pack_rows(n_small)
    small_loc = _pack([W[n][0] for n in SMALL_SHARDED] + [c[0]], small_rows)
    big = ['w_in', 'ffn_w_up', 'ffn_w_down', 'w_out', 'mla_w_uq', 'mla_w_ukv']
    gathered = _exchange("gather_weights", [W[n][0].astype(BF16) for n in big] + [small_loc], ['ag'] * 7)
    gw = dict(zip(big, gathered[:6]))
    small_all = [_unpack(gathered[6][d], small_shapes) for d in range(N_DEV)]
    full_small = {n: jnp.concatenate([small_all[d][j] for d in range(N_DEV)], axis=-1)
                  for j, n in enumerate(SMALL_SHARDED)}
    c_all = jnp.stack([small_all[d][-1] for d in range(N_DEV)])

    w_in_f = _cols_from_shards(gw['w_in'])
    w_in_p = jnp.concatenate([w_in_f[:, :OFF_KR], jnp.zeros((D, NOPE), BF16), w_in_f[:, OFF_KR:],
                              jnp.zeros((D, HEAD_PAD - QK), BF16)], axis=1)
    w_up_t = _ff_to_tiles(_cols_from_shards(gw['ffn_w_up']))
    w_down_f = gw['ffn_w_down'].reshape(D_FF, D)
    w_out_f = gw['w_out'].reshape(D, D)
    w_uq_f = _cols_from_shards(gw['mla_w_uq']).reshape(Q_RANK, HEADS, QK)
    wq_p = jnp.pad(w_uq_f, ((0, 0), (0, 0), (0, HEAD_PAD - QK))).reshape(Q_RANK, HEADS * HEAD_PAD)
    w_ukv_f = _cols_from_shards(gw['mla_w_ukv']).reshape(KV_RANK, HEADS, NOPE + VDIM)
    wk_p = jnp.pad(w_ukv_f[:, :, :NOPE], ((0, 0), (0, 0), (0, HEAD_PAD - NOPE))).reshape(KV_RANK, HEADS * HEAD_PAD)
    wv_f = w_ukv_f[:, :, NOPE:].reshape(KV_RANK, HEADS * VDIM)

    lru_cw, lru_ba, lru_bx, lru_lam, ffn_cw = [full_small[n] for n in SMALL_SHARDED]
    ffn_cw_t, ffn_cb_t = _ff_to_tiles(ffn_cw), _ff_to_tiles(ffn_conv_b)

    def block_diag(w):
        eye = jnp.eye(LRU_HEADS, dtype=w.dtype)
        return jnp.einsum('hij,hg->higj', w, eye).reshape(LRU_W, LRU_W)

    w_gate = jnp.concatenate([block_diag(lru_w_a[0, 0]), block_diag(lru_w_x[0, 0]),
                              block_diag(lru_w_a[0, 1]), block_diag(lru_w_x[0, 1])], axis=1).astype(BF16)
    b_gate = jnp.concatenate([lru_ba[0], lru_bx[0], lru_ba[1], lru_bx[1]])[None]
    lam_row = lru_lam.reshape(1, 2 * LRU_W)

    c16 = jnp.concatenate([c_all, c_ctx[None], jnp.zeros((2 * SUBLANES - N_DEV - 1, D), F32)])
    ncol = w_mod.shape[2]
    b_mod_loc = lax.dynamic_slice(b_mod, (0, me * ncol), (1, ncol))

    def mod_fwd(c16_r, w_r, b_r):
        c16_v = c16_r[...]
        sl = c16_v * jax.nn.sigmoid(c16_v)
        return (jnp.dot(sl.astype(BF16), w_r[...].astype(BF16), preferred_element_type=F32) + b_r[...],)

    (mod_part,) = _single("mod_fwd", mod_fwd, [c16, w_mod[0], b_mod_loc], [((2 * SUBLANES, ncol), F32)])
    (mod_g,) = _exchange("gather_mod", [mod_part], ['ag'])
    mod_all = _cols_from_shards(mod_g)
    mod_lat = lax.dynamic_slice(mod_all, (me, 0), (1, N_MOD * D)).reshape(N_MOD, D)
    mod_ctx = mod_all[N_DEV].reshape(N_MOD, D)

    xs, tgt = x[0], loss_target[0]
    xa = jnp.concatenate([ctx[0], xs])

    def sel_mod(i, ml, mc, r0):
        sh = jnp.where(i < ct, mc[r0:r0 + 1, :], ml[r0:r0 + 1, :])
        sc = jnp.where(i < ct, mc[r0 + 1:r0 + 2, :], ml[r0 + 1:r0 + 2, :])
        return sh, sc

    def pre_fn(xv, g, sh, sc):
        return _rms(xv, g) * (1.0 + sc) + sh

    def k_pre(i, rv, bv):
        sh, sc = sel_mod(i, bv[1], bv[2], 0)
        return (pre_fn(rv[0], bv[0][...], sh, sc),), ()

    (h_pre,) = _rowwise("pre_mix", k_pre, nt, [_row(xa)], [g_pre_mix, mod_lat, mod_ctx], [((T, D), BF16, None)])
    proj = _mm("in_proj", h_pre, w_in_p, 'nn')

    xcv = _dwconv_fwd("lru_conv", proj, lru_cw, lru_conv_b, LRU_CONV_LEFT, (0, ct))

    def k_gates(i, rv, bv):
        xv = rv[0]
        z = jnp.dot(xv.astype(BF16), bv[0][...], preferred_element_type=F32) + bv[1][...]
        return _gates(z, xv, _bc(_softplus(-bv[2][...]), TM)), ()

    a0, u0, a1, u1 = _rowwise("lru_gates", k_gates, nt, [_row(xcv)], [w_gate, b_gate, lam_row],
                              [((T, LRU_W), F32, None), ((T, LRU_W), F32, None),
                               ((T, LRU_W), F32, swp), ((T, LRU_W), F32, swp)])
    h0 = _scan("lru_scan_f", a0, u0, False)
    h1 = _scan("lru_scan_r", a1, u1, True)

    def k_rms(i, rv, bv):
        return (_rms(rv[0], bv[0][...]),), ()

    cqn = _rowwise("q_norm", k_rms, ns, [_row(proj, lat, (Q_RANK, OFF_CQ // Q_RANK))], [mla_g_q],
                   [((S, Q_RANK), BF16, None)])[0]
    q_lin = _mm("q_proj", cqn, wq_p, 'nn')
    cos_q, sup_q, sdn_q = _rope_tables(S, 0)
    cos_k, sup_k, sdn_k = _rope_tables(S, CN)

    def k_qrope(i, rv, bv):
        ql, ctb, sub, sdb = rv
        return (jnp.stack([_rope(ql[:, hh * HEAD_PAD:(hh + 1) * HEAD_PAD], ctb, sub, sdb) for hh in range(HEADS)]),), ()

    q = _rowwise("q_rope", k_qrope, ns, [_row(q_lin), _row(cos_q), _row(sup_q), _row(sdn_q)], [],
                 [((HEADS, S, HEAD_PAD), BF16, None)])[0]

    ckvn = _rowwise("kv_norm", k_rms, nt, [_row(proj, None, (KV_RANK, OFF_CKV // KV_RANK))], [mla_g_kv],
                    [((T, KV_RANK), BF16, None)])[0]
    k_lin = _mm("k_proj", ckvn, wk_p, 'nn')
    v_lin = _mm("v_proj", ckvn, wv_f, 'nn', out_dtype=BF16)

    def k_krope(i, rv, bv):
        kl, krp, ctb, sub, sdb = rv
        kr = _rope(krp, ctb, sub, sdb)
        return (jnp.stack([kl[:, hh * HEAD_PAD:(hh + 1) * HEAD_PAD] + kr for hh in range(HEADS)]),), ()

    k = _rowwise("k_rope", k_krope, nt,
                 [_row(k_lin), _row(proj, None, (HEAD_PAD, OFF_KR // HEAD_PAD)), _row(cos_k), _row(sup_k), _row(sdn_k)],
                 [], [((HEADS, T, HEAD_PAD), BF16, None)])[0]
    v = v_lin.reshape(T, HEADS, VDIM).transpose(1, 0, 2)

    o, lse = _attn_fwd(q, k, v)
    o_t = o.transpose(1, 0, 2).reshape(S, HEADS * VDIM)

    def lru_out_fn(hf, hr, gr):
        return (hf + hr) * jax.nn.gelu(gr)

    def k_mix_in(i, rv, bv):
        return (jnp.concatenate([lru_out_fn(rv[0], rv[1], rv[2]), rv[3]], axis=1),), ()

    gr_row = _row(proj, lat, (LRU_W, OFF_GR // LRU_W))
    y_in = _rowwise("mix_in", k_mix_in, ns, [_row(h0, lat), _row(h1), gr_row, _row(o_t)], [],
                    [((S, D), BF16, None)])[0]
    y = _mm("out_proj", y_in, w_out_f, 'nn')

    def post_mix_fn(xv, yv, gt, g):
        return xv + gt * _rms(yv, g)

    def k_post_mix(i, rv, bv):
        ml = bv[0]
        x1v = post_mix_fn(rv[0], rv[1], ml[2:3, :], bv[1][...])
        return (x1v, pre_fn(x1v, bv[2][...], ml[3:4, :], ml[4:5, :])), ()

    x1, h2 = _rowwise("post_mix", k_post_mix, ns, [_row(xs), _row(y)], [mod_lat, g_post_mix, g_pre_ffn],
                      [((S, D), F32, None), ((S, D), BF16, None)])
    up = _mm("ffn_up", h2, w_up_t, 'nn', out_dtype=BF16)
    act = _ffn_mid_fwd(up, ffn_cw_t, ffn_cb_t)
    f = _mm("ffn_down", act, w_down_f, 'nn')

    def loss_fn(x1v, fv, gt, g, tg):
        x2 = x1v + gt * _rms(fv, g)
        err = x2 - tg
        return 0.5 * jnp.sum(jnp.mean(err * err, axis=-1))

    def k_loss(i, rv, bv):
        gtb, gb = _bc(bv[0][5:6, :], TM), _bc(bv[1][...], TM)
        val, (dx1v, dfv, dgt, dg) = jax.value_and_grad(loss_fn, argnums=(0, 1, 2, 3))(rv[0], rv[1], gtb, gb, rv[2])
        return (dx1v, dfv), (jnp.full((1, LANES), val, F32), _rs(dgt), _rs(dg))

    dx1_a, df, loss_acc, d_gt2, d_g_post_ffn = _rowwise(
        "loss_bwd", k_loss, ns, [_row(x1), _row(f), _row(tgt)], [mod_lat, g_post_ffn],
        [((S, D), F32, None), ((S, D), BF16, None)], [(1, LANES), (1, D), (1, D)])
    loss = lax.psum(loss_acc[0, 0], ("x", "y", "c"))

    d_act = _mm("ffn_down_dx", df, w_down_f, 'nt', out_dtype=BF16)
    d_w_down = _mm("ffn_down_dw", act, df, 'tn')
    d_up, d_ffn_cw_t, d_ffn_cb_t = _ffn_mid_bwd(up, d_act, ffn_cw_t, ffn_cb_t)
    d_ffn_cw, d_ffn_cb = _ff_from_tiles(d_ffn_cw_t), _ff_from_tiles(d_ffn_cb_t)
    d_h2 = _mm("ffn_up_dx", d_up, w_up_t, 'nt')
    d_w_up = _ff_from_tiles(_mm("ffn_up_dw", h2, d_up, 'tn'))

    def k_pre_ffn_bwd(i, rv, bv):
        ml = bv[0]
        gb, shb, scb = _bc(bv[1][...], TM), _bc(ml[3:4, :], TM), _bc(ml[4:5, :], TM)
        _, pull = jax.vjp(pre_fn, rv[0], gb, shb, scb)
        dxv, dg, dsh, dsc = pull(rv[1])
        return (rv[2] + dxv,), (_rs(dg), _rs(dsh), _rs(dsc))

    dx1, d_g_pre_ffn, d_sh2, d_sc2 = _rowwise(
        "pre_ffn_bwd", k_pre_ffn_bwd, ns, [_row(x1), _row(d_h2), _row(dx1_a)], [mod_lat, g_pre_ffn],
        [((S, D), F32, None)], [(1, D), (1, D), (1, D)])

    def k_post_mix_bwd(i, rv, bv):
        gtb, gb = _bc(bv[0][2:3, :], TM), _bc(bv[1][...], TM)
        _, pull = jax.vjp(post_mix_fn, rv[0], rv[1], gtb, gb)
        _, dyv, dgt, dg = pull(rv[2])
        return (dyv,), (_rs(dgt), _rs(dg))

    dy, d_gt1, d_g_post_mix = _rowwise(
        "post_mix_bwd", k_post_mix_bwd, ns, [_row(xs), _row(y), _row(dx1)], [mod_lat, g_post_mix],
        [((S, D), BF16, None)], [(1, D), (1, D)])
    d_y_in = _mm("out_proj_dx", dy, w_out_f, 'nt')
    d_w_out = _mm("out_proj_dw", y_in, dy, 'tn')

    def k_lru_out_bwd(i, rv, bv):
        _, pull = jax.vjp(lru_out_fn, rv[0], rv[1], rv[2])
        dhf, _, dgr = pull(rv[3])
        return (dhf, dgr), ()

    d_hsum, d_gr = _rowwise("lru_out_bwd", k_lru_out_bwd, ns,
                            [_row(h0, lat), _row(h1), gr_row, _row(d_y_in, None, (LRU_W, 0))], [],
                            [((S, LRU_W), F32, None), ((S, LRU_W), F32, None)])

    do_h = d_y_in[:, LRU_W:].reshape(S, HEADS, VDIM).transpose(1, 0, 2)
    dq, delta = _attn_dq(q, k, v, o, do_h, lse)
    dk, dv = _attn_dkv(q, k, v, do_h.astype(BF16), lse.reshape(HEADS, 1, S), delta.reshape(HEADS, 1, S))

    def k_qrope_bwd(i, rv, bv):
        dqv, ctb, sub, sdb = rv
        return (jnp.concatenate([_rope_t(dqv[hh], ctb, sub, sdb) for hh in range(HEADS)], axis=1),), ()

    dq_lin = _rowwise("q_rope_bwd", k_qrope_bwd, ns, [_row(dq), _row(cos_q), _row(sup_q), _row(sdn_q)], [],
                      [((S, HEADS * HEAD_PAD), BF16, None)])[0]
    d_cqn = _mm("q_proj_dx", dq_lin, wq_p, 'nt')
    d_wq_p = _mm("q_proj_dw", cqn, dq_lin, 'tn')

    def k_rms_bwd(i, rv, bv):
        gb = _bc(bv[0][...], TM)
        _, pull = jax.vjp(_rms, rv[0], gb)
        dxv, dg = pull(rv[1])
        return (dxv,), (_rs(dg),)

    d_cq, d_g_q = _rowwise("q_norm_bwd", k_rms_bwd, ns,
                           [_row(proj, lat, (Q_RANK, OFF_CQ // Q_RANK)), _row(d_cqn)], [mla_g_q],
                           [((S, Q_RANK), F32, None)], [(1, Q_RANK)])

    def k_krope_bwd(i, rv, bv):
        dkv_, ctb, sub, sdb = rv
        tot = dkv_[0]
        for hh in range(1, HEADS):
            tot = tot + dkv_[hh]
        lane = lax.broadcasted_iota(jnp.int32, tot.shape, 1)
        tot = jnp.where((lane >= NOPE) & (lane < QK), tot, 0.0)
        return (jnp.concatenate([dkv_[hh] for hh in range(HEADS)], axis=1), _rope_t(tot, ctb, sub, sdb)), ()

    dk_lin, d_krp = _rowwise("k_rope_bwd", k_krope_bwd, nt, [_row(dk), _row(cos_k), _row(sup_k), _row(sdn_k)], [],
                             [((T, HEADS * HEAD_PAD), BF16, None), ((T, HEAD_PAD), F32, None)])
    dv_lin = dv.transpose(1, 0, 2).reshape(T, HEADS * VDIM).astype(BF16)
    d_ckvn_k = _mm("k_proj_dx", dk_lin, wk_p, 'nt')
    d_ckvn_v = _mm("v_proj_dx", dv_lin, wv_f, 'nt')
    d_wk_p = _mm("k_proj_dw", ckvn, dk_lin, 'tn')
    d_wv = _mm("v_proj_dw", ckvn, dv_lin, 'tn')

    def k_kvnorm_bwd(i, rv, bv):
        gb = _bc(bv[0][...], TM)
        _, pull = jax.vjp(_rms, rv[0], gb)
        dxv, dg = pull(rv[1] + rv[2])
        return (dxv,), (_rs(dg),)

    d_ckv, d_g_kv = _rowwise("kv_norm_bwd", k_kvnorm_bwd, nt,
                             [_row(proj, None, (KV_RANK, OFF_CKV // KV_RANK)), _row(d_ckvn_k), _row(d_ckvn_v)],
                             [mla_g_kv], [((T, KV_RANK), F32, None)], [(1, KV_RANK)])

    zc = jnp.zeros((CN, LRU_W), F32)
    one_row, zero_row = jnp.ones((1, LRU_W), F32), jnp.zeros((1, LRU_W), F32)
    lam0 = _scan("lru_scan_f_bwd", jnp.concatenate([a0[1:], one_row]), jnp.concatenate([zc, d_hsum]), True)
    lam1 = _scan("lru_scan_r_bwd", jnp.concatenate([one_row, a1[:-1]]), jnp.concatenate([d_hsum, zc]), False)
    hprev0 = jnp.concatenate([zero_row, h0[:-1]])
    hprev1 = jnp.concatenate([h1[1:], zero_row])

    def k_gates_bwd(i, rv, bv):
        xv, l0, hp0, l1, hp1 = rv
        wg, bg, lamv = [b[...] for b in bv]
        xb = xv.astype(BF16)
        z = jnp.dot(xb, wg, preferred_element_type=F32) + bg
        spb = _bc(_softplus(-lamv), TM)
        _, pull = jax.vjp(_gates, z, xv, spb)
        dz, dxv, dsp = pull((l0 * hp0, l0, l1 * hp1, l1))
        dzb = dz.astype(BF16)
        dxv = dxv + lax.dot_general(dzb, wg, NT_DIMS, preferred_element_type=F32)
        dwg = lax.dot_general(xb, dzb, (((0,), (0,)), ((), ())), preferred_element_type=F32)
        dlam = -_rs(dsp) * jax.nn.sigmoid(-lamv)
        return (dxv,), (dwg, _rs(dz), dlam)

    d_xcv, d_w_gate, d_b_gate, d_lam = _rowwise(
        "lru_gates_bwd", k_gates_bwd, nt,
        [_row(xcv), _row(lam0), _row(hprev0), _row(lam1, swp), _row(hprev1, swp)], [w_gate, b_gate, lam_row],
        [((T, LRU_W), F32, None)], [(LRU_W, 4 * LRU_W), (1, 4 * LRU_W), (1, 2 * LRU_W)])
    d_xr, d_lru_cw, d_lru_cb = _dwconv_bwd("lru_conv_bwd", proj, d_xcv, lru_cw, LRU_CONV_LEFT, (0, ct))

    def k_dproj(i, rv, bv):
        is_lat = i >= ct
        return (jnp.concatenate([rv[0], jnp.where(is_lat, rv[1], 0.0), jnp.where(is_lat, rv[2], 0.0), rv[3], rv[4]],
                                axis=1),), ()

    d_proj = _rowwise("d_proj", k_dproj, nt,
                      [_row(d_xr), _row(d_gr, lat_or_0), _row(d_cq, lat_or_0), _row(d_ckv), _row(d_krp)], [],
                      [((T, IN_W_PAD), BF16, None)])[0]
    d_h_pre = _mm("in_proj_dx", d_proj, w_in_p, 'nt')
    d_w_in_p = _mm("in_proj_dw", h_pre, d_proj, 'tn')

    def k_pre_bwd(i, rv, bv):
        g, ml, mc = bv[0][...], bv[1], bv[2]
        sh, sc = sel_mod(i, ml, mc, 0)
        _, pull = jax.vjp(pre_fn, rv[0], _bc(g, TM), _bc(sh, TM), _bc(sc, TM))
        dxv, dg, dsh, dsc = pull(rv[1])
        is_lat = i >= ct
        dsh, dsc = _rs(dsh), _rs(dsc)
        zero = jnp.zeros_like(dsh)
        return ((dxv + jnp.where(is_lat, rv[2], 0.0),),
                (_rs(dg), jnp.where(is_lat, dsh, zero), jnp.where(is_lat, dsc, zero),
                 jnp.where(is_lat, zero, dsh), jnp.where(is_lat, zero, dsc)))

    dxa, d_g_pre_mix, d_sh1, d_sc1, d_csh1, d_csc1 = _rowwise(
        "pre_mix_bwd", k_pre_bwd, nt, [_row(xa), _row(d_h_pre), _row(dx1, lat_or_0)],
        [g_pre_mix, mod_lat, mod_ctx], [((T, D), F32, None)], [(1, D)] * 5)
    grad_x = dxa[CN:][None]

    zrow = jnp.zeros((1, D), F32)
    d_mod_lat = jnp.concatenate([d_sh1, d_sc1, d_gt1, d_sh2, d_sc2, d_gt2], axis=1)
    d_mod_ctx = jnp.concatenate([d_csh1, d_csc1, zrow, zrow, zrow, zrow], axis=1)
    d_mod_mine = jnp.concatenate([d_mod_lat, d_mod_ctx, jnp.zeros((SUBLANES - 2, N_MOD * D), F32)])
    (d_mod_all,) = _exchange("gather_dmod", [d_mod_mine], ['ag'])
    dm_lat_loc = lax.dynamic_slice(d_mod_all[:, 0], (0, me * ncol), (N_DEV, ncol))
    dm_ctx_loc = lax.dynamic_slice(d_mod_all[:, 1], (0, me * ncol), (N_DEV, ncol))

    def mod_bwd(c16_r, w_r, dml_r, dmc_r):
        c16_v = c16_r[...]
        sig = jax.nn.sigmoid(c16_v)
        sl = c16_v * sig
        dctx = dmc_r[0:1, :]
        for d in range(1, N_DEV):
            dctx = dctx + dmc_r[d:d + 1, :]
        row = lax.broadcasted_iota(jnp.int32, (2 * SUBLANES, ncol), 0)
        dm16 = dml_r[...] + jnp.where(row == N_DEV, _bc(dctx, 2 * SUBLANES), 0.0)
        dw = lax.dot_general(sl.astype(BF16), dm16.astype(BF16), (((0,), (0,)), ((), ())), preferred_element_type=F32)
        dsl = lax.dot_general(dm16.astype(BF16), w_r[...].astype(BF16), NT_DIMS, preferred_element_type=F32)
        dc = dsl * (sig * (1.0 + c16_v * (1.0 - sig)))
        return dw, dc

    dm_lat16 = jnp.concatenate([dm_lat_loc, jnp.zeros((2 * SUBLANES - N_DEV, ncol), F32)])
    g_w_mod, dc16 = _single("mod_bwd", mod_bwd, [c16, w_mod[0], dm_lat16, dm_ctx_loc],
                            [((D, ncol), F32), ((2 * SUBLANES, D), F32)])
    d_c_ctx_part = dc16[N_DEV]

    def diag_blocks(dw):
        return jnp.stack([dw[hh * 64:(hh + 1) * 64, hh * 64:(hh + 1) * 64] for hh in range(LRU_HEADS)])

    d_lru_w_a = jnp.stack([diag_blocks(d_w_gate[:, 0:LRU_W]), diag_blocks(d_w_gate[:, 2 * LRU_W:3 * LRU_W])])[None]
    d_lru_w_x = jnp.stack([diag_blocks(d_w_gate[:, LRU_W:2 * LRU_W]), diag_blocks(d_w_gate[:, 3 * LRU_W:])])[None]
    d_b_a = jnp.stack([d_b_gate[0, 0:LRU_W], d_b_gate[0, 2 * LRU_W:3 * LRU_W]])
    d_b_x = jnp.stack([d_b_gate[0, LRU_W:2 * LRU_W], d_b_gate[0, 3 * LRU_W:]])

    rep_part = dict(c_ctx=d_c_ctx_part, b_mod=d_mod_lat + d_mod_ctx, g_pre_mix=d_g_pre_mix, g_post_mix=d_g_post_mix,
                    g_pre_ffn=d_g_pre_ffn, g_post_ffn=d_g_post_ffn, lru_conv_b=d_lru_cb, lru_w_a=d_lru_w_a,
                    lru_w_x=d_lru_w_x, mla_g_q=d_g_q, mla_g_kv=d_g_kv, ffn_conv_b=d_ffn_cb)
    rep_shapes = [W[n].shape for n in REPLICATED]
    rep_rows = _pack_rows(sum(W[n].size for n in REPLICATED))
    rep_loc = _pack([rep_part[n] for n in REPLICATED], rep_rows)

    d_w_in = jnp.concatenate([d_w_in_p[:, :OFF_KR], d_w_in_p[:, OFF_KR + NOPE:OFF_KR + QK]], axis=1)
    d_w_uq = d_wq_p.reshape(Q_RANK, HEADS, HEAD_PAD)[:, :, :QK].reshape(Q_RANK, HEADS * QK)
    d_w_ukv = jnp.concatenate([d_wk_p.reshape(KV_RANK, HEADS, HEAD_PAD)[:, :, :NOPE],
                               d_wv.reshape(KV_RANK, HEADS, VDIM)], axis=2).reshape(KV_RANK, HEADS * (NOPE + VDIM))
    small_full = dict(lru_conv_w=d_lru_cw, lru_b_a=d_b_a, lru_b_x=d_b_x, lru_lambda=d_lam.reshape(2, LRU_W),
                      ffn_conv_w=d_ffn_cw)
    small_sh = jnp.concatenate([_cols_to_shards(small_full[n]).reshape(N_DEV, -1) for n in SMALL_SHARDED], axis=1)
    n_sh = small_sh.shape[1]
    sh_rows = _pack_rows(n_sh)
    small_sh = jnp.pad(small_sh, ((0, 0), (0, sh_rows * LANES - n_sh))).reshape(N_DEV, sh_rows, LANES)

    sends = [_cols_to_shards(d_w_in), _cols_to_shards(d_w_up), d_w_down.reshape(N_DEV, D_FF // N_DEV, D),
             d_w_out.reshape(N_DEV, D // N_DEV, D), _cols_to_shards(d_w_uq), _cols_to_shards(d_w_ukv), small_sh]
    recv = _exchange("exchange_grads", sends + [rep_loc], ['a2a'] * 7 + ['ag'])

    res = {}

    def adam(name, w2, m2, v2, parts):
        return _adamw("adamw_" + name, w2, m2, v2, parts)

    for n, parts in zip(big, recv[:6]):
        shp = W[n].shape
        outs = adam(n, W[n][0], M[n][0], V[n][0], parts)
        res[n] = [o_.reshape(shp) for o_ in outs]
    outs = adam('w_mod', w_mod[0], m_w_mod[0], v_w_mod[0], g_w_mod[None])
    res['w_mod'] = [o_.reshape(w_mod.shape) for o_ in outs]

    sh_shapes = [W[n].shape for n in SMALL_SHARDED]
    pk = lambda dct: _pack([dct[n] for n in SMALL_SHARDED], sh_rows)
    outs = adam('small_sharded', pk(W), pk(M), pk(V), recv[6])
    for n, vals in zip(SMALL_SHARDED, zip(*[_unpack(o_, sh_shapes) for o_ in outs])):
        res[n] = list(vals)

    pr = lambda dct: _pack([dct[n] for n in REPLICATED], rep_rows)
    outs = adam('replicated', pr(W), pr(M), pr(V), recv[7])
    for n, vals in zip(REPLICATED, zip(*[_unpack(o_, rep_shapes) for o_ in outs])):
        res[n] = list(vals)

    return (loss, grad_x, *[res[n][0] for n in WEIGHTS], *[res[n][1] for n in WEIGHTS],
            *[res[n][2] for n in WEIGHTS], *[res[n][3] for n in WEIGHTS])
```

```python
import functools
import math

import jax
import jax.numpy as jnp
from jax import lax
from jax.experimental import pallas as pl
from jax.experimental.pallas import tpu as pltpu

F32 = jnp.float32
BF16 = jnp.bfloat16
MESH = pl.DeviceIdType.MESH

N_DEV = 8
ROW_TILE = 256
SUBLANES = 8
LANES = 128
VMEM_LIMIT = 56 * 1024 * 1024

D_MODEL = 1024
LRU_W = 512
LRU_HEADS = 8
LRU_CONV_K = 4
LRU_CONV_LEFT = 2
LRU_C = 8.0
HEADS = 8
NOPE = 64
ROPE = 32
VDIM = 64
QK = NOPE + ROPE
HEAD_PAD = 128
Q_RANK = 256
KV_RANK = 128
MLA_SCALE = QK ** -0.5
ROPE_PAIRS = ROPE // 4
ROPE_BASE = 10000.0
GRID_W = 64
D_FF = 2816
FFN_CONV_K = 3
FFN_CONV_LEFT = 1
N_MOD = 6
EPS = 1e-6
IN_W = 2 * LRU_W + Q_RANK + KV_RANK + ROPE
IN_W_PAD = 2 * LRU_W + Q_RANK + KV_RANK + HEAD_PAD
OFF_GR, OFF_CQ, OFF_CKV, OFF_KR = LRU_W, 2 * LRU_W, 2 * LRU_W + Q_RANK, 2 * LRU_W + Q_RANK + KV_RANK

ADAM_LR, ADAM_B1, ADAM_B2, ADAM_EPS, ADAM_WD, ADAM_STEP = 0.001, 0.9, 0.999, 1e-08, 0.01, 10

WEIGHTS = ['c_ctx', 'w_mod', 'b_mod', 'g_pre_mix', 'g_post_mix', 'g_pre_ffn', 'g_post_ffn', 'w_in', 'lru_conv_w',
           'lru_conv_b', 'lru_w_a', 'lru_b_a', 'lru_w_x', 'lru_b_x', 'lru_lambda', 'mla_g_q', 'mla_w_uq', 'mla_g_kv',
           'mla_w_ukv', 'w_out', 'ffn_w_up', 'ffn_conv_w', 'ffn_conv_b', 'ffn_w_down']
REPLICATED = ['c_ctx', 'b_mod', 'g_pre_mix', 'g_post_mix', 'g_pre_ffn', 'g_post_ffn', 'lru_conv_b', 'lru_w_a',
              'lru_w_x', 'mla_g_q', 'mla_g_kv', 'ffn_conv_b']
SMALL_SHARDED = ['lru_conv_w', 'lru_b_a', 'lru_b_x', 'lru_lambda', 'ffn_conv_w']


def _pick(d, prefs):
    for p in prefs:
        if d % p == 0:
            return p
    return d


def _params(sem=None):
    return pltpu.CompilerParams(dimension_semantics=sem, vmem_limit_bytes=VMEM_LIMIT)


MM_TILES = (1024, 1408, 768, 512, 256, 128)


def _mm(name, a, b, mode, out_dtype=F32):
    if mode == 'nn':
        (m, k), (_, n) = a.shape, b.shape
    elif mode == 'nt':
        (m, k), (n, _) = a.shape, b.shape
    else:
        (k, m), (_, n) = a.shape, b.shape
    tm = _pick(m, MM_TILES)
    tn = _pick(n, MM_TILES)
    tk = _pick(k, MM_TILES)
    nk = k // tk
    if mode == 'nn':
        a_spec = pl.BlockSpec((tm, tk), lambda i, j, kk: (i, kk))
        b_spec = pl.BlockSpec((tk, tn), lambda i, j, kk: (kk, j))
        dn = (((1,), (0,)), ((), ()))
    elif mode == 'nt':
        a_spec = pl.BlockSpec((tm, tk), lambda i, j, kk: (i, kk))
        b_spec = pl.BlockSpec((tn, tk), lambda i, j, kk: (j, kk))
        dn = (((1,), (1,)), ((), ()))
    else:
        a_spec = pl.BlockSpec((tk, tm), lambda i, j, kk: (kk, i))
        b_spec = pl.BlockSpec((tk, tn), lambda i, j, kk: (kk, j))
        dn = (((0,), (0,)), ((), ()))

    def body(a_ref, b_ref, o_ref, acc_ref):
        kk = pl.program_id(2)

        @pl.when(kk == 0)
        def _():
            acc_ref[...] = jnp.zeros_like(acc_ref)

        acc_ref[...] += lax.dot_general(a_ref[...].astype(BF16), b_ref[...].astype(BF16), dn,
                                        preferred_element_type=F32)

        @pl.when(kk == nk - 1)
        def _():
            o_ref[...] = acc_ref[...].astype(o_ref.dtype)

    return pl.pallas_call(
        body, name=name, grid=(m // tm, n // tn, nk),
        in_specs=[a_spec, b_spec], out_specs=pl.BlockSpec((tm, tn), lambda i, j, kk: (i, j)),
        out_shape=jax.ShapeDtypeStruct((m, n), out_dtype),
        scratch_shapes=[pltpu.VMEM((tm, tn), F32)],
        compiler_params=_params(("parallel", "parallel", "arbitrary")),
    )(a, b)


def _row(a, idx=None, col=None):
    return dict(a=a, idx=idx, col=col)


def _rowwise(name, fn, n_tiles, rows, bcast, out_rows, out_acc=(), tm=ROW_TILE):
    in_specs = []
    for r in rows:
        a, idx, col = r['a'], r['idx'] or (lambda i: i), r['col']
        if a.ndim == 2:
            w, ci = col if col else (a.shape[1], 0)
            in_specs.append(pl.BlockSpec((tm, w), lambda i, idx=idx, ci=ci: (idx(i), ci)))
        else:
            in_specs.append(pl.BlockSpec((a.shape[0], tm, a.shape[2]), lambda i, idx=idx: (0, idx(i), 0)))
    for b in bcast:
        in_specs.append(pl.BlockSpec(b.shape, lambda i, nd=b.ndim: (0,) * nd))
    out_specs, out_shape = [], []
    for shape, dtype, idx in out_rows:
        idx = idx or (lambda i: i)
        if len(shape) == 2:
            out_specs.append(pl.BlockSpec((tm, shape[1]), lambda i, idx=idx: (idx(i), 0)))
        else:
            out_specs.append(pl.BlockSpec((shape[0], tm, shape[2]), lambda i, idx=idx: (0, idx(i), 0)))
        out_shape.append(jax.ShapeDtypeStruct(shape, dtype))
    for shape in out_acc:
        out_specs.append(pl.BlockSpec(shape, lambda i, nd=len(shape): (0,) * nd))
        out_shape.append(jax.ShapeDtypeStruct(shape, F32))
    nr, nb, no = len(rows), len(bcast), len(out_rows)

    def body(*refs):
        i = pl.program_id(0)
        rvals = [r[...] for r in refs[:nr]]
        bvals = list(refs[nr:nr + nb])
        o_rows, o_acc = fn(i, rvals, bvals)
        for ref, v in zip(refs[nr + nb:nr + nb + no], o_rows):
            ref[...] = v.astype(ref.dtype)
        acc_refs = refs[nr + nb + no:]
        if acc_refs:
            @pl.when(i == 0)
            def _():
                for ref in acc_refs:
                    ref[...] = jnp.zeros_like(ref)
            for ref, v in zip(acc_refs, o_acc):
                ref[...] += v

    return pl.pallas_call(
        body, name=name, grid=(n_tiles,), in_specs=in_specs, out_specs=out_specs, out_shape=out_shape,
        compiler_params=_params(("arbitrary",)),
    )(*[r['a'] for r in rows], *bcast)


def _single(name, fn, ins, out_shapes):
    def body(*refs):
        outs = fn(*refs[:len(ins)])
        for ref, v in zip(refs[len(ins):], outs):
            ref[...] = v.astype(ref.dtype)

    return pl.pallas_call(
        body, name=name,
        in_specs=[pl.BlockSpec(memory_space=pltpu.VMEM)] * len(ins),
        out_specs=[pl.BlockSpec(memory_space=pltpu.VMEM)] * len(out_shapes),
        out_shape=[jax.ShapeDtypeStruct(s, d) for s, d in out_shapes],
        compiler_params=_params(),
    )(*ins)


def _bc(p, n):
    return jnp.broadcast_to(p, (n, p.shape[-1]))


def _rs(g):
    return jnp.sum(g, axis=0, keepdims=True)


def _rms(x, g):
    return x * lax.rsqrt(jnp.mean(x * x, axis=-1, keepdims=True) + EPS) * g


def _conv_specs(r, cw, tm, halo=SUBLANES):
    th = tm // halo
    last = r // halo - 1
    prev = pl.BlockSpec((halo, cw), lambda c, i: (jnp.maximum(i * th - 1, 0), c))
    cur = pl.BlockSpec((tm, cw), lambda c, i: (i, c))
    nxt = pl.BlockSpec((halo, cw), lambda c, i: (jnp.minimum((i + 1) * th, last), c))
    return [prev, cur, nxt]


def _fill_ext(ext_ref, prev_ref, cur_ref, next_ref, i, n_tiles, seg_starts, tm):
    prev_ok = functools.reduce(jnp.logical_and, [i != s for s in seg_starts])
    next_ok = functools.reduce(jnp.logical_and, [i + 1 != s for s in seg_starts] + [i + 1 < n_tiles])
    ext_ref[0:SUBLANES, :] = jnp.where(prev_ok, prev_ref[...].astype(F32), 0.0)
    ext_ref[SUBLANES:SUBLANES + tm, :] = cur_ref[...].astype(F32)
    ext_ref[SUBLANES + tm:, :] = jnp.where(next_ok, next_ref[...].astype(F32), 0.0)


def _dwconv_fwd(name, x, w, b, left, seg_starts, cw=512, tm=ROW_TILE):
    r, c = x.shape[0], w.shape[1]
    kw = w.shape[0]
    n_tiles = r // tm

    def body(prev_ref, cur_ref, next_ref, w_ref, b_ref, o_ref, ext_ref):
        i = pl.program_id(1)
        _fill_ext(ext_ref, prev_ref, cur_ref, next_ref, i, n_tiles, seg_starts, tm)
        out = jnp.broadcast_to(b_ref[...], (tm, cw))
        for k in range(kw):
            out = out + ext_ref[pl.ds(SUBLANES + k - left, tm), :] * w_ref[k:k + 1, :]
        o_ref[...] = out

    return pl.pallas_call(
        body, name=name, grid=(c // cw, n_tiles),
        in_specs=_conv_specs(r, cw, tm) + [pl.BlockSpec((kw, cw), lambda c_, i: (0, c_)),
                                           pl.BlockSpec((1, cw), lambda c_, i: (0, c_))],
        out_specs=pl.BlockSpec((tm, cw), lambda c_, i: (i, c_)),
        out_shape=jax.ShapeDtypeStruct((r, c), F32),
        scratch_shapes=[pltpu.VMEM((tm + 2 * SUBLANES, cw), F32)],
        compiler_params=_params(("parallel", "arbitrary")),
    )(x, x, x, w, b)


def _dwconv_bwd(name, x, dy, w, left, seg_starts, out_dtype=F32, cw=512, tm=ROW_TILE):
    r, c = dy.shape
    kw = w.shape[0]
    n_tiles = r // tm

    def body(xp, xc, xn, dp, dc, dn, w_ref, dx_ref, dw_ref, db_ref, xe_ref, de_ref):
        i = pl.program_id(1)
        _fill_ext(xe_ref, xp, xc, xn, i, n_tiles, seg_starts, tm)
        _fill_ext(de_ref, dp, dc, dn, i, n_tiles, seg_starts, tm)
        dyc = dc[...].astype(F32)
        dx = jnp.zeros((tm, cw), F32)
        dws = []
        for k in range(kw):
            dx = dx + de_ref[pl.ds(SUBLANES - k + left, tm), :] * w_ref[k:k + 1, :]
            dws.append(jnp.sum(dyc * xe_ref[pl.ds(SUBLANES + k - left, tm), :], axis=0, keepdims=True))
        dx_ref[...] = dx.astype(dx_ref.dtype)

        @pl.when(i == 0)
        def _():
            dw_ref[...] = jnp.zeros_like(dw_ref)
            db_ref[...] = jnp.zeros_like(db_ref)

        for k in range(kw):
            dw_ref[k:k + 1, :] += dws[k]
        db_ref[...] += jnp.sum(dyc, axis=0, keepdims=True)

    return pl.pallas_call(
        body, name=name, grid=(c // cw, n_tiles),
        in_specs=_conv_specs(r, cw, tm) + _conv_specs(r, cw, tm) + [pl.BlockSpec((kw, cw), lambda c_, i: (0, c_))],
        out_specs=[pl.BlockSpec((tm, cw), lambda c_, i: (i, c_)),
                   pl.BlockSpec((kw, cw), lambda c_, i: (0, c_)),
                   pl.BlockSpec((1, cw), lambda c_, i: (0, c_))],
        out_shape=[jax.ShapeDtypeStruct((r, c), out_dtype), jax.ShapeDtypeStruct((kw, c), F32),
                   jax.ShapeDtypeStruct((1, c), F32)],
        scratch_shapes=[pltpu.VMEM((tm + 2 * SUBLANES, cw), F32), pltpu.VMEM((tm + 2 * SUBLANES, cw), F32)],
        compiler_params=_params(("parallel", "arbitrary")),
    )(x, x, x, dy, dy, dy, w)


FF_TILE = 256
FF_HALO = 16


def _ffn_fill(ext_ref, prev_ref, cur_ref, next_ref, i, n_tiles, tm):
    ext_ref[0:FF_HALO, :] = jnp.where(i > 0, prev_ref[...].astype(F32), 0.0)
    ext_ref[FF_HALO:FF_HALO + tm, :] = cur_ref[...].astype(F32)
    ext_ref[FF_HALO + tm:, :] = jnp.where(i + 1 < n_tiles, next_ref[...].astype(F32), 0.0)


def _ffn_conv(ext_ref, w_ref, b_ref, start, rows):
    out = jnp.broadcast_to(b_ref[...], (rows, 2 * FF_TILE))
    for k in range(FFN_CONV_K):
        out = out + ext_ref[pl.ds(start + k - FFN_CONV_LEFT, rows), :] * w_ref[k:k + 1, :]
    return out


def _ffn_mid_fwd(up, w, b, tm=ROW_TILE):
    s, c2 = up.shape
    n_tiles = s // tm
    cw = 2 * FF_TILE

    def body(prev_ref, cur_ref, next_ref, w_ref, b_ref, o_ref, ext_ref):
        i = pl.program_id(1)
        _ffn_fill(ext_ref, prev_ref, cur_ref, next_ref, i, n_tiles, tm)
        upc = _ffn_conv(ext_ref, w_ref, b_ref, FF_HALO, tm)
        uv, gv = upc[:, :FF_TILE], upc[:, FF_TILE:]
        o_ref[...] = (gv * jax.nn.sigmoid(gv) * uv).astype(o_ref.dtype)

    return pl.pallas_call(
        body, name="ffn_mid", grid=(c2 // cw, n_tiles),
        in_specs=_conv_specs(s, cw, tm, FF_HALO) + [pl.BlockSpec((FFN_CONV_K, cw), lambda c_, i: (0, c_)),
                                                   pl.BlockSpec((1, cw), lambda c_, i: (0, c_))],
        out_specs=pl.BlockSpec((tm, FF_TILE), lambda c_, i: (i, c_)),
        out_shape=jax.ShapeDtypeStruct((s, c2 // 2), BF16),
        scratch_shapes=[pltpu.VMEM((tm + 2 * FF_HALO, cw), F32)],
        compiler_params=_params(("parallel", "arbitrary")),
    )(up, up, up, w, b)


def _ffn_mid_bwd(up, d_act, w, b, tm=ROW_TILE):
    s, c2 = up.shape
    n_tiles = s // tm
    cw = 2 * FF_TILE
    h8 = SUBLANES

    def gate_bwd(upc, dact):
        uv, gv = upc[:, :FF_TILE], upc[:, FF_TILE:]
        sg = jax.nn.sigmoid(gv)
        return jnp.concatenate([dact * (gv * sg), dact * uv * (sg * (1.0 + gv * (1.0 - sg)))], axis=1)

    def body(up_p, up_c, up_n, da_p, da_c, da_n, w_ref, b_ref, dup_ref, dw_ref, db_ref, ext_ref, dext_ref):
        i = pl.program_id(1)
        _ffn_fill(ext_ref, up_p, up_c, up_n, i, n_tiles, tm)
        d_c = gate_bwd(_ffn_conv(ext_ref, w_ref, b_ref, FF_HALO, tm), da_c[...].astype(F32))
        da_prev = jnp.where(i > 0, da_p[...].astype(F32)[FF_HALO - h8:, :], 0.0)
        da_next = jnp.where(i + 1 < n_tiles, da_n[...].astype(F32)[:h8, :], 0.0)
        dext_ref[FF_HALO - h8:FF_HALO, :] = gate_bwd(_ffn_conv(ext_ref, w_ref, b_ref, FF_HALO - h8, h8), da_prev)
        dext_ref[FF_HALO:FF_HALO + tm, :] = d_c
        dext_ref[FF_HALO + tm:FF_HALO + tm + h8, :] = gate_bwd(_ffn_conv(ext_ref, w_ref, b_ref, FF_HALO + tm, h8), da_next)
        dup = jnp.zeros((tm, cw), F32)
        for k in range(FFN_CONV_K):
            dup = dup + dext_ref[pl.ds(FF_HALO - k + FFN_CONV_LEFT, tm), :] * w_ref[k:k + 1, :]
        dup_ref[...] = dup.astype(dup_ref.dtype)

        @pl.when(i == 0)
        def _():
            dw_ref[...] = jnp.zeros_like(dw_ref)
            db_ref[...] = jnp.zeros_like(db_ref)

        for k in range(FFN_CONV_K):
            dw_ref[k:k + 1, :] += jnp.sum(d_c * ext_ref[pl.ds(FF_HALO + k - FFN_CONV_LEFT, tm), :], axis=0, keepdims=True)
        db_ref[...] += jnp.sum(d_c, axis=0, keepdims=True)

    def half_specs():
        th = tm // FF_HALO
        last = s // FF_HALO - 1
        return [pl.BlockSpec((FF_HALO, FF_TILE), lambda c_, i: (jnp.maximum(i * th - 1, 0), c_)),
                pl.BlockSpec((tm, FF_TILE), lambda c_, i: (i, c_)),
                pl.BlockSpec((FF_HALO, FF_TILE), lambda c_, i: (jnp.minimum((i + 1) * th, last), c_))]

    return pl.pallas_call(
        body, name="ffn_mid_bwd", grid=(c2 // cw, n_tiles),
        in_specs=_conv_specs(s, cw, tm, FF_HALO) + half_specs() + [
            pl.BlockSpec((FFN_CONV_K, cw), lambda c_, i: (0, c_)), pl.BlockSpec((1, cw), lambda c_, i: (0, c_))],
        out_specs=[pl.BlockSpec((tm, cw), lambda c_, i: (i, c_)),
                   pl.BlockSpec((FFN_CONV_K, cw), lambda c_, i: (0, c_)),
                   pl.BlockSpec((1, cw), lambda c_, i: (0, c_))],
        out_shape=[jax.ShapeDtypeStruct((s, c2), BF16), jax.ShapeDtypeStruct((FFN_CONV_K, c2), F32),
                   jax.ShapeDtypeStruct((1, c2), F32)],
        scratch_shapes=[pltpu.VMEM((tm + 2 * FF_HALO, cw), F32), pltpu.VMEM((tm + 2 * FF_HALO, cw), F32)],
        compiler_params=_params(("parallel", "arbitrary")),
    )(up, up, up, d_act, d_act, d_act, w, b)


def _ff_to_tiles(w):
    r = w.shape[0]
    return w.reshape(r, 2, D_FF // FF_TILE, FF_TILE).transpose(0, 2, 1, 3).reshape(r, 2 * D_FF)


def _ff_from_tiles(w):
    r = w.shape[0]
    return w.reshape(r, D_FF // FF_TILE, 2, FF_TILE).transpose(0, 2, 1, 3).reshape(r, 2 * D_FF)


SCAN_UNROLL = 8


def _scan(name, a, u, reverse):
    t, c = a.shape
    n8 = t // SUBLANES

    def body(a_ref, u_ref, h_ref):
        row = lax.broadcasted_iota(jnp.int32, (SUBLANES, LANES), 0)
        last = 0 if reverse else SUBLANES - 1

        def step(j, carry):
            blk = (n8 - 1 - j) if reverse else j
            base = pl.multiple_of(blk * SUBLANES, SUBLANES)
            av = a_ref[pl.ds(base, SUBLANES), :]
            hv = u_ref[pl.ds(base, SUBLANES), :]
            for s in (1, 2, 4):
                shift = SUBLANES - s if reverse else s
                ok = (row < SUBLANES - s) if reverse else (row >= s)
                a_sh = jnp.where(ok, pltpu.roll(av, shift, 0), 1.0)
                h_sh = jnp.where(ok, pltpu.roll(hv, shift, 0), 0.0)
                hv = av * h_sh + hv
                av = av * a_sh
            a_last = jnp.sum(jnp.where(row == last, av, 0.0), axis=0, keepdims=True)
            h_last = jnp.sum(jnp.where(row == last, hv, 0.0), axis=0, keepdims=True)
            h_ref[pl.ds(base, SUBLANES), :] = av * carry + hv
            return a_last * carry + h_last

        lax.fori_loop(0, n8, step, jnp.zeros((1, LANES), F32), unroll=SCAN_UNROLL)

    return pl.pallas_call(
        body, name=name, grid=(c // LANES,),
        in_specs=[pl.BlockSpec((t, LANES), lambda j: (0, j))] * 2,
        out_specs=pl.BlockSpec((t, LANES), lambda j: (0, j)),
        out_shape=jax.ShapeDtypeStruct((t, c), F32),
        compiler_params=_params(("parallel",)),
    )(a, u)


NT_DIMS = (((1,), (1,)), ((), ()))


LOG2E = 1.4426950408889634
SCALE2 = MLA_SCALE * LOG2E
ATTN_TILES = (512, 256, 128)
KEY_CHUNKS = (768, 512, 256, 128)
QUERY_CHUNKS = (1024, 512, 256, 128)


def _attn_fwd(q, k, v, ride_arrs, ride_modes):
    h, s, _ = q.shape
    t = k.shape[1]
    tq = _pick(s, ATTN_TILES)
    ck = _pick(t, KEY_CHUNKS)
    n = len(ride_arrs)

    def body(*refs):
        q_ref, k_ref, v_ref = refs[:3]
        o_ref, lse_ref = refs[3 + n:5 + n]
        hh, i = pl.program_id(0), pl.program_id(1)
        finish = _riding_exchange(refs, 3, 2, n, ride_modes, (hh == 0) & (i == 0),
                                  (hh == h - 1) & (i == s // tq - 1))
        qv = q_ref[0]
        m = l = acc = None
        for j in range(t // ck):
            kj, vj = k_ref[0, j * ck:(j + 1) * ck, :], v_ref[0, j * ck:(j + 1) * ck, :]
            s2 = lax.dot_general(qv, kj, NT_DIMS, preferred_element_type=F32) * SCALE2
            mj = jnp.max(s2, axis=-1, keepdims=True)
            m_new = mj if j == 0 else jnp.maximum(m, mj)
            p = jnp.exp2(s2 - m_new)
            lj = jnp.sum(p, axis=-1, keepdims=True)
            pv = jnp.dot(p.astype(BF16), vj, preferred_element_type=F32)
            if j == 0:
                l, acc = lj, pv
            else:
                alpha = jnp.exp2(m - m_new)
                l, acc = alpha * l + lj, alpha * acc + pv
            m = m_new
        o_ref[0] = acc / l
        lse_ref[0] = m + jnp.log2(l)
        finish()

    any_spec = pl.BlockSpec(memory_space=pl.ANY)
    res = pl.pallas_call(
        body, name="attn_fwd", grid=(h, s // tq),
        in_specs=[pl.BlockSpec((1, tq, HEAD_PAD), lambda hh, i: (hh, i, 0)),
                  pl.BlockSpec((1, t, HEAD_PAD), lambda hh, i: (hh, 0, 0)),
                  pl.BlockSpec((1, t, VDIM), lambda hh, i: (hh, 0, 0))] + [any_spec] * n,
        out_specs=[pl.BlockSpec((1, tq, VDIM), lambda hh, i: (hh, i, 0)),
                   pl.BlockSpec((1, tq, 1), lambda hh, i: (hh, i, 0))] + [any_spec] * n,
        out_shape=[jax.ShapeDtypeStruct((h, s, VDIM), F32), jax.ShapeDtypeStruct((h, s, 1), F32)]
        + _exchange_shapes(ride_arrs, ride_modes),
        scratch_shapes=_exchange_sems(n),
        compiler_params=pltpu.CompilerParams(dimension_semantics=("arbitrary", "arbitrary"),
                                             vmem_limit_bytes=VMEM_LIMIT, has_side_effects=True),
    )(q, k, v, *ride_arrs)
    return res[0], res[1], res[2:]


TN_DIMS = (((0,), (0,)), ((), ()))


def _attn_bwd(q, k, v, do, lse_row, delta_row, ride_arrs, ride_modes):
    h, s, _ = q.shape
    t = k.shape[1]
    tk = _pick(t, (768,) + ATTN_TILES)
    cq = _pick(s, QUERY_CHUNKS)
    n = len(ride_arrs)

    def body(*refs):
        q_ref, k_ref, v_ref, do_ref, lse_ref, delta_ref = refs[:6]
        dq_ref, dk_ref, dv_ref = refs[6 + n:9 + n]
        hh, i = pl.program_id(0), pl.program_id(1)
        finish = _riding_exchange(refs, 6, 3, n, ride_modes, (hh == 0) & (i == 0),
                                  (hh == h - 1) & (i == t // tk - 1))

        @pl.when(i == 0)
        def _():
            dq_ref[...] = jnp.zeros_like(dq_ref)

        kt, vt = k_ref[0], v_ref[0]
        dk = dv = None
        for j in range(s // cq):
            rows = slice(j * cq, (j + 1) * cq)
            qj, doj = q_ref[0, rows, :], do_ref[0, rows, :]
            pt = jnp.exp2(lax.dot_general(kt, qj, NT_DIMS, preferred_element_type=F32) * SCALE2 - lse_ref[0, :, rows])
            dv_j = jnp.dot(pt.astype(BF16), doj, preferred_element_type=F32)
            dpt = lax.dot_general(vt, doj, NT_DIMS, preferred_element_type=F32)
            dst = (pt * (dpt - delta_ref[0, :, rows])).astype(BF16)
            dk_j = jnp.dot(dst, qj, preferred_element_type=F32)
            dq_ref[0, rows, :] += lax.dot_general(dst, kt, TN_DIMS, preferred_element_type=F32)
            dk, dv = (dk_j, dv_j) if j == 0 else (dk + dk_j, dv + dv_j)
        dk_ref[0] = dk * MLA_SCALE
        dv_ref[0] = dv
        finish()

    any_spec = pl.BlockSpec(memory_space=pl.ANY)
    res = pl.pallas_call(
        body, name="attn_bwd", grid=(h, t // tk),
        in_specs=[pl.BlockSpec((1, s, HEAD_PAD), lambda hh, i: (hh, 0, 0)),
                  pl.BlockSpec((1, tk, HEAD_PAD), lambda hh, i: (hh, i, 0)),
                  pl.BlockSpec((1, tk, VDIM), lambda hh, i: (hh, i, 0)),
                  pl.BlockSpec((1, s, VDIM), lambda hh, i: (hh, 0, 0)),
                  pl.BlockSpec((1, 1, s), lambda hh, i: (hh, 0, 0)),
                  pl.BlockSpec((1, 1, s), lambda hh, i: (hh, 0, 0))] + [any_spec] * n,
        out_specs=[pl.BlockSpec((1, s, HEAD_PAD), lambda hh, i: (hh, 0, 0)),
                   pl.BlockSpec((1, tk, HEAD_PAD), lambda hh, i: (hh, i, 0)),
                   pl.BlockSpec((1, tk, VDIM), lambda hh, i: (hh, i, 0))] + [any_spec] * n,
        out_shape=[jax.ShapeDtypeStruct((h, s, HEAD_PAD), F32), jax.ShapeDtypeStruct((h, t, HEAD_PAD), F32),
                   jax.ShapeDtypeStruct((h, t, VDIM), F32)] + _exchange_shapes(ride_arrs, ride_modes),
        scratch_shapes=_exchange_sems(n),
        compiler_params=pltpu.CompilerParams(dimension_semantics=("arbitrary", "arbitrary"),
                                             vmem_limit_bytes=VMEM_LIMIT, has_side_effects=True),
    )(q, k, v, do, lse_row, delta_row, *ride_arrs)
    return res[0], res[1], res[2], res[3:]


def _exchange_shapes(arrs, modes):
    return [jax.ShapeDtypeStruct((N_DEV,) + a.shape if md == 'ag' else a.shape, a.dtype) for a, md in zip(arrs, modes)]


def _exchange_sems(n):
    return [pltpu.SemaphoreType.DMA((n, N_DEV - 1)), pltpu.SemaphoreType.DMA((n, N_DEV - 1)),
            pltpu.SemaphoreType.DMA((n,))]


def _exchange_copies(ins, outs, modes, send_sems, recv_sems, local_sems):
    x, y, c = lax.axis_index("x"), lax.axis_index("y"), lax.axis_index("c")
    me = 4 * x + 2 * y + c
    copies = []
    for a in range(len(ins)):
        ag = modes[a] == 'ag'
        copies.append(pltpu.make_async_copy(ins[a] if ag else ins[a].at[me], outs[a].at[me], local_sems.at[a]))
        for k in range(1, N_DEV):
            px = 1 - x if k & 4 else x
            py = 1 - y if k & 2 else y
            pc = 1 - c if k & 1 else c
            src = ins[a] if ag else ins[a].at[4 * px + 2 * py + pc]
            copies.append(pltpu.make_async_remote_copy(
                src_ref=src, dst_ref=outs[a].at[me], send_sem=send_sems.at[a, k - 1],
                recv_sem=recv_sems.at[a, k - 1], device_id=(px, py, pc), device_id_type=MESH))
    return copies


def _exchange(name, arrs, modes):
    n = len(arrs)

    def body(*refs):
        copies = _exchange_copies(refs[:n], refs[n:2 * n], modes, *refs[2 * n:])
        for cp in copies:
            cp.start()
        for cp in copies:
            cp.wait()

    return pl.pallas_call(
        body, name=name,
        in_specs=[pl.BlockSpec(memory_space=pl.ANY)] * n,
        out_specs=[pl.BlockSpec(memory_space=pl.ANY)] * n,
        out_shape=_exchange_shapes(arrs, modes),
        scratch_shapes=_exchange_sems(n),
        compiler_params=pltpu.CompilerParams(has_side_effects=True),
    )(*arrs)


def _riding_exchange(refs, n_in, n_out, n, modes, first, last):
    ins = refs[n_in:n_in + n]
    outs = refs[n_in + n + n_out:n_in + 2 * n + n_out]
    sems = refs[n_in + 2 * n + n_out:n_in + 2 * n + n_out + 3]

    @pl.when(first)
    def _():
        for cp in _exchange_copies(ins, outs, modes, *sems):
            cp.start()

    def finish():
        @pl.when(last)
        def _():
            for cp in _exchange_copies(ins, outs, modes, *sems):
                cp.wait()

    return finish


def _adamw(name, w, m, v, gparts):
    r, c = w.shape
    npart = gparts.shape[0]
    tr = _pick(r, (256, 128, 64, 32, 16, 8))
    spec = pl.BlockSpec((tr, c), lambda i: (i, 0))

    def body(w_ref, m_ref, v_ref, g_ref, go_ref, d_ref, mo_ref, vo_ref):
        g = g_ref[0]
        for p in range(1, npart):
            g = g + g_ref[p]
        m1 = ADAM_B1 * m_ref[...] + (1.0 - ADAM_B1) * g
        v1 = ADAM_B2 * v_ref[...] + (1.0 - ADAM_B2) * (g * g)
        m_hat = m1 / (1.0 - ADAM_B1 ** ADAM_STEP)
        v_hat = v1 / (1.0 - ADAM_B2 ** ADAM_STEP)
        go_ref[...] = g
        d_ref[...] = -ADAM_LR * (m_hat / (jnp.sqrt(v_hat) + ADAM_EPS) + ADAM_WD * w_ref[...])
        mo_ref[...] = m1
        vo_ref[...] = v1

    return pl.pallas_call(
        body, name=name, grid=(r // tr,),
        in_specs=[spec, spec, spec, pl.BlockSpec((npart, tr, c), lambda i: (0, i, 0))],
        out_specs=[spec] * 4, out_shape=[jax.ShapeDtypeStruct((r, c), F32)] * 4,
        compiler_params=_params(("parallel",)),
    )(w, m, v, gparts)


def _pack(arrs, rows):
    flat = jnp.concatenate([a.reshape(-1) for a in arrs])
    return jnp.pad(flat, (0, rows * LANES - flat.shape[0])).reshape(rows, LANES)


def _unpack(packed, shapes):
    flat, out, off = packed.reshape(-1), [], 0
    for s in shapes:
        n = math.prod(s)
        out.append(flat[off:off + n].reshape(s))
        off += n
    return out


def _pack_rows(n_elems):
    return -(-n_elems // (SUBLANES * LANES)) * SUBLANES


def _cols_from_shards(g):
    return g.transpose(1, 0, 2).reshape(g.shape[1], N_DEV * g.shape[2])


def _cols_to_shards(w):
    r, c = w.shape
    return w.reshape(r, N_DEV, c // N_DEV).transpose(1, 0, 2)


def _rope_tables(n_lat, n_ctx):
    rows = n_lat // GRID_W
    inv = ROPE_BASE ** (-jnp.arange(ROPE_PAIRS, dtype=F32) / ROPE_PAIRS)
    ang_r = jnp.arange(rows, dtype=F32)[:, None] * inv
    ang_c = jnp.arange(GRID_W, dtype=F32)[:, None] * inv
    cr, sr = jnp.repeat(jnp.cos(ang_r), GRID_W, axis=0), jnp.repeat(jnp.sin(ang_r), GRID_W, axis=0)
    cc, sc = jnp.tile(jnp.cos(ang_c), (rows, 1)), jnp.tile(jnp.sin(ang_c), (rows, 1))
    one, zero = jnp.ones((n_lat, 1), F32), jnp.zeros((n_lat, 1), F32)
    z8 = jnp.zeros((n_lat, ROPE_PAIRS), F32)
    cos_t = jnp.concatenate([jnp.tile(one, (1, NOPE)), cr, cr, cc, cc, jnp.tile(one, (1, HEAD_PAD - QK))], 1)
    sin_up = jnp.concatenate([jnp.tile(zero, (1, NOPE)), -sr, z8, -sc, z8, jnp.tile(zero, (1, HEAD_PAD - QK))], 1)
    sin_dn = jnp.concatenate([jnp.tile(zero, (1, NOPE)), z8, sr, z8, sc, jnp.tile(zero, (1, HEAD_PAD - QK))], 1)
    if n_ctx:
        cos_t = jnp.concatenate([jnp.ones((n_ctx, HEAD_PAD), F32), cos_t])
        sin_up = jnp.concatenate([jnp.zeros((n_ctx, HEAD_PAD), F32), sin_up])
        sin_dn = jnp.concatenate([jnp.zeros((n_ctx, HEAD_PAD), F32), sin_dn])
    return cos_t, sin_up, sin_dn


def _rope(x, cos_t, sin_up, sin_dn):
    return x * cos_t + pltpu.roll(x, HEAD_PAD - ROPE_PAIRS, 1) * sin_up + pltpu.roll(x, ROPE_PAIRS, 1) * sin_dn


def _rope_t(dy, cos_t, sin_up, sin_dn):
    return (dy * cos_t + pltpu.roll(dy * sin_up, ROPE_PAIRS, 1)
            + pltpu.roll(dy * sin_dn, HEAD_PAD - ROPE_PAIRS, 1))


def _softplus(x):
    return jnp.maximum(x, 0.0) + jnp.log(1.0 + jnp.exp(-jnp.abs(x)))


def _gates(z, xcv, lam_sp):
    outs = []
    for d in range(2):
        r = jax.nn.sigmoid(z[:, (2 * d) * LRU_W:(2 * d + 1) * LRU_W])
        ig = jax.nn.sigmoid(z[:, (2 * d + 1) * LRU_W:(2 * d + 2) * LRU_W])
        log_a = -LRU_C * r * lam_sp[:, d * LRU_W:(d + 1) * LRU_W]
        a = jnp.exp(log_a)
        u = jnp.sqrt(-jnp.tanh(log_a) * (a * a + 1.0)) * (ig * xcv)
        outs += [a, u]
    return tuple(outs)


def kernel(x, c, ctx, c_ctx, w_mod, b_mod, g_pre_mix, g_post_mix, g_pre_ffn, g_post_ffn, w_in, lru_conv_w, lru_conv_b, lru_w_a, lru_b_a, lru_w_x, lru_b_x, lru_lambda, mla_g_q, mla_w_uq, mla_g_kv, mla_w_ukv, w_out, ffn_w_up, ffn_conv_w, ffn_conv_b, ffn_w_down, loss_target, m_c_ctx, m_w_mod, m_b_mod, m_g_pre_mix, m_g_post_mix, m_g_pre_ffn, m_g_post_ffn, m_w_in, m_lru_conv_w, m_lru_conv_b, m_lru_w_a, m_lru_b_a, m_lru_w_x, m_lru_b_x, m_lru_lambda, m_mla_g_q, m_mla_w_uq, m_mla_g_kv, m_mla_w_ukv, m_w_out, m_ffn_w_up, m_ffn_conv_w, m_ffn_conv_b, m_ffn_w_down, v_c_ctx, v_w_mod, v_b_mod, v_g_pre_mix, v_g_post_mix, v_g_pre_ffn, v_g_post_ffn, v_w_in, v_lru_conv_w, v_lru_conv_b, v_lru_w_a, v_lru_b_a, v_lru_w_x, v_lru_b_x, v_lru_lambda, v_mla_g_q, v_mla_w_uq, v_mla_g_kv, v_mla_w_ukv, v_w_out, v_ffn_w_up, v_ffn_conv_w, v_ffn_conv_b, v_ffn_w_down):
    W = dict(c_ctx=c_ctx, w_mod=w_mod, b_mod=b_mod, g_pre_mix=g_pre_mix, g_post_mix=g_post_mix, g_pre_ffn=g_pre_ffn,
             g_post_ffn=g_post_ffn, w_in=w_in, lru_conv_w=lru_conv_w, lru_conv_b=lru_conv_b, lru_w_a=lru_w_a,
             lru_b_a=lru_b_a, lru_w_x=lru_w_x, lru_b_x=lru_b_x, lru_lambda=lru_lambda, mla_g_q=mla_g_q,
             mla_w_uq=mla_w_uq, mla_g_kv=mla_g_kv, mla_w_ukv=mla_w_ukv, w_out=w_out, ffn_w_up=ffn_w_up,
             ffn_conv_w=ffn_conv_w, ffn_conv_b=ffn_conv_b, ffn_w_down=ffn_w_down)
    M = dict(c_ctx=m_c_ctx, w_mod=m_w_mod, b_mod=m_b_mod, g_pre_mix=m_g_pre_mix, g_post_mix=m_g_post_mix,
             g_pre_ffn=m_g_pre_ffn, g_post_ffn=m_g_post_ffn, w_in=m_w_in, lru_conv_w=m_lru_conv_w,
             lru_conv_b=m_lru_conv_b, lru_w_a=m_lru_w_a, lru_b_a=m_lru_b_a, lru_w_x=m_lru_w_x, lru_b_x=m_lru_b_x,
             lru_lambda=m_lru_lambda, mla_g_q=m_mla_g_q, mla_w_uq=m_mla_w_uq, mla_g_kv=m_mla_g_kv,
             mla_w_ukv=m_mla_w_ukv, w_out=m_w_out, ffn_w_up=m_ffn_w_up, ffn_conv_w=m_ffn_conv_w,
             ffn_conv_b=m_ffn_conv_b, ffn_w_down=m_ffn_w_down)
    V = dict(c_ctx=v_c_ctx, w_mod=v_w_mod, b_mod=v_b_mod, g_pre_mix=v_g_pre_mix, g_post_mix=v_g_post_mix,
             g_pre_ffn=v_g_pre_ffn, g_post_ffn=v_g_post_ffn, w_in=v_w_in, lru_conv_w=v_lru_conv_w,
             lru_conv_b=v_lru_conv_b, lru_w_a=v_lru_w_a, lru_b_a=v_lru_b_a, lru_w_x=v_lru_w_x, lru_b_x=v_lru_b_x,
             lru_lambda=v_lru_lambda, mla_g_q=v_mla_g_q, mla_w_uq=v_mla_w_uq, mla_g_kv=v_mla_g_kv,
             mla_w_ukv=v_mla_w_ukv, w_out=v_w_out, ffn_w_up=v_ffn_w_up, ffn_conv_w=v_ffn_conv_w,
             ffn_conv_b=v_ffn_conv_b, ffn_w_down=v_ffn_w_down)

    D = D_MODEL
    S, CN = x.shape[1], ctx.shape[1]
    T = S + CN
    TM = ROW_TILE
    ct, ns, nt = CN // TM, S // TM, T // TM
    me = 4 * lax.axis_index("x") + 2 * lax.axis_index("y") + lax.axis_index("c")

    lat = lambda i: i + ct
    swp = lambda i: jnp.where(i < ct, i + ns, i - ct)
    lat_or_0 = lambda i: jnp.maximum(i - ct, 0)

    small_shapes = [W[n].shape[1:] for n in SMALL_SHARDED] + [(D,)]
    n_small = sum(math.prod(s) for s in small_shapes)
    small_rows = _pack_rows(n_small)
    small_loc = _pack([W[n][0] for n in SMALL_SHARDED] + [c[0]], small_rows)
    early = ['w_in', 'mla_w_uq', 'mla_w_ukv']
    late = ['w_out', 'ffn_w_up', 'ffn_w_down']
    big = early + late
    gathered = _exchange("gather_weights", [W[n][0].astype(BF16) for n in early] + [small_loc], ['ag'] * 4)
    gw = dict(zip(early, gathered[:3]))
    small_all = [_unpack(gathered[3][d], small_shapes) for d in range(N_DEV)]
    full_small = {n: jnp.concatenate([small_all[d][j] for d in range(N_DEV)], axis=-1)
                  for j, n in enumerate(SMALL_SHARDED)}
    c_all = jnp.stack([small_all[d][-1] for d in range(N_DEV)])

    w_in_f = _cols_from_shards(gw['w_in'])
    w_in_p = jnp.concatenate([w_in_f[:, :OFF_KR], jnp.zeros((D, NOPE), BF16), w_in_f[:, OFF_KR:],
                              jnp.zeros((D, HEAD_PAD - QK), BF16)], axis=1)
    w_uq_f = _cols_from_shards(gw['mla_w_uq']).reshape(Q_RANK, HEADS, QK)
    wq_p = jnp.pad(w_uq_f, ((0, 0), (0, 0), (0, HEAD_PAD - QK))).reshape(Q_RANK, HEADS * HEAD_PAD)
    w_ukv_f = _cols_from_shards(gw['mla_w_ukv']).reshape(KV_RANK, HEADS, NOPE + VDIM)
    wk_p = jnp.pad(w_ukv_f[:, :, :NOPE], ((0, 0), (0, 0), (0, HEAD_PAD - NOPE))).reshape(KV_RANK, HEADS * HEAD_PAD)
    wv_f = w_ukv_f[:, :, NOPE:].reshape(KV_RANK, HEADS * VDIM)

    lru_cw, lru_ba, lru_bx, lru_lam, ffn_cw = [full_small[n] for n in SMALL_SHARDED]
    ffn_cw_t, ffn_cb_t = _ff_to_tiles(ffn_cw), _ff_to_tiles(ffn_conv_b)

    def block_diag(w):
        eye = jnp.eye(LRU_HEADS, dtype=w.dtype)
        return jnp.einsum('hij,hg->higj', w, eye).reshape(LRU_W, LRU_W)

    w_gate = jnp.concatenate([block_diag(lru_w_a[0, 0]), block_diag(lru_w_x[0, 0]),
                              block_diag(lru_w_a[0, 1]), block_diag(lru_w_x[0, 1])], axis=1).astype(BF16)
    b_gate = jnp.concatenate([lru_ba[0], lru_bx[0], lru_ba[1], lru_bx[1]])[None]
    lam_row = lru_lam.reshape(1, 2 * LRU_W)

    c16 = jnp.concatenate([c_all, c_ctx[None], jnp.zeros((2 * SUBLANES - N_DEV - 1, D), F32)])
    ncol = w_mod.shape[2]
    b_mod_loc = lax.dynamic_slice(b_mod, (0, me * ncol), (1, ncol))

    def mod_fwd(c16_r, w_r, b_r):
        c16_v = c16_r[...]
        sl = c16_v * jax.nn.sigmoid(c16_v)
        return (jnp.dot(sl.astype(BF16), w_r[...].astype(BF16), preferred_element_type=F32) + b_r[...],)

    (mod_part,) = _single("mod_fwd", mod_fwd, [c16, w_mod[0], b_mod_loc], [((2 * SUBLANES, ncol), F32)])
    (mod_g,) = _exchange("gather_mod", [mod_part], ['ag'])
    mod_all = _cols_from_shards(mod_g)
    mod_lat = lax.dynamic_slice(mod_all, (me, 0), (1, N_MOD * D)).reshape(N_MOD, D)
    mod_ctx = mod_all[N_DEV].reshape(N_MOD, D)

    xs, tgt = x[0], loss_target[0]
    xa = jnp.concatenate([ctx[0], xs])

    def sel_mod(i, ml, mc, r0):
        sh = jnp.where(i < ct, mc[r0:r0 + 1, :], ml[r0:r0 + 1, :])
        sc = jnp.where(i < ct, mc[r0 + 1:r0 + 2, :], ml[r0 + 1:r0 + 2, :])
        return sh, sc

    def pre_fn(xv, g, sh, sc):
        return _rms(xv, g) * (1.0 + sc) + sh

    def k_pre(i, rv, bv):
        sh, sc = sel_mod(i, bv[1], bv[2], 0)
        return (pre_fn(rv[0], bv[0][...], sh, sc),), ()

    (h_pre,) = _rowwise("pre_mix", k_pre, nt, [_row(xa)], [g_pre_mix, mod_lat, mod_ctx], [((T, D), BF16, None)])
    proj = _mm("in_proj", h_pre, w_in_p, 'nn')

    xcv = _dwconv_fwd("lru_conv", proj, lru_cw, lru_conv_b, LRU_CONV_LEFT, (0, ct))

    def k_gates(i, rv, bv):
        xv = rv[0]
        z = jnp.dot(xv.astype(BF16), bv[0][...], preferred_element_type=F32) + bv[1][...]
        return _gates(z, xv, _bc(_softplus(-bv[2][...]), TM)), ()

    a0, u0, a1, u1 = _rowwise("lru_gates", k_gates, nt, [_row(xcv)], [w_gate, b_gate, lam_row],
                              [((T, LRU_W), F32, None), ((T, LRU_W), F32, None),
                               ((T, LRU_W), F32, swp), ((T, LRU_W), F32, swp)])
    h0 = _scan("lru_scan_f", a0, u0, False)
    h1 = _scan("lru_scan_r", a1, u1, True)

    def k_rms(i, rv, bv):
        return (_rms(rv[0], bv[0][...]),), ()

    cqn = _rowwise("q_norm", k_rms, ns, [_row(proj, lat, (Q_RANK, OFF_CQ // Q_RANK))], [mla_g_q],
                   [((S, Q_RANK), BF16, None)])[0]
    q_lin = _mm("q_proj", cqn, wq_p, 'nn')
    cos_q, sup_q, sdn_q = _rope_tables(S, 0)
    cos_k, sup_k, sdn_k = _rope_tables(S, CN)

    def k_qrope(i, rv, bv):
        ql, ctb, sub, sdb = rv
        return (jnp.stack([_rope(ql[:, hh * HEAD_PAD:(hh + 1) * HEAD_PAD], ctb, sub, sdb) for hh in range(HEADS)]),), ()

    q = _rowwise("q_rope", k_qrope, ns, [_row(q_lin), _row(cos_q), _row(sup_q), _row(sdn_q)], [],
                 [((HEADS, S, HEAD_PAD), BF16, None)])[0]

    ckvn = _rowwise("kv_norm", k_rms, nt, [_row(proj, None, (KV_RANK, OFF_CKV // KV_RANK))], [mla_g_kv],
                    [((T, KV_RANK), BF16, None)])[0]
    k_lin = _mm("k_proj", ckvn, wk_p, 'nn')
    v_lin = _mm("v_proj", ckvn, wv_f, 'nn', out_dtype=BF16)

    def k_krope(i, rv, bv):
        kl, krp, ctb, sub, sdb = rv
        kr = _rope(krp, ctb, sub, sdb)
        return (jnp.stack([kl[:, hh * HEAD_PAD:(hh + 1) * HEAD_PAD] + kr for hh in range(HEADS)]),), ()

    k = _rowwise("k_rope", k_krope, nt,
                 [_row(k_lin), _row(proj, None, (HEAD_PAD, OFF_KR // HEAD_PAD)), _row(cos_k), _row(sup_k), _row(sdn_k)],
                 [], [((HEADS, T, HEAD_PAD), BF16, None)])[0]
    v = v_lin.reshape(T, HEADS, VDIM).transpose(1, 0, 2)

    o, lse, late_g = _attn_fwd(q, k, v, [W[n][0].astype(BF16) for n in late], ['ag'] * 3)
    w_out_f = late_g[0].reshape(D, D)
    w_up_t = _ff_to_tiles(_cols_from_shards(late_g[1]))
    w_down_f = late_g[2].reshape(D_FF, D)
    o_t = o.transpose(1, 0, 2).reshape(S, HEADS * VDIM)

    def lru_out_fn(hf, hr, gr):
        return (hf + hr) * jax.nn.gelu(gr)

    def k_mix_in(i, rv, bv):
        return (jnp.concatenate([lru_out_fn(rv[0], rv[1], rv[2]), rv[3]], axis=1),), ()

    gr_row = _row(proj, lat, (LRU_W, OFF_GR // LRU_W))
    y_in = _rowwise("mix_in", k_mix_in, ns, [_row(h0, lat), _row(h1), gr_row, _row(o_t)], [],
                    [((S, D), BF16, None)])[0]
    y = _mm("out_proj", y_in, w_out_f, 'nn')

    def post_mix_fn(xv, yv, gt, g):
        return xv + gt * _rms(yv, g)

    def k_post_mix(i, rv, bv):
        ml = bv[0]
        x1v = post_mix_fn(rv[0], rv[1], ml[2:3, :], bv[1][...])
        return (x1v, pre_fn(x1v, bv[2][...], ml[3:4, :], ml[4:5, :])), ()

    x1, h2 = _rowwise("post_mix", k_post_mix, ns, [_row(xs), _row(y)], [mod_lat, g_post_mix, g_pre_ffn],
                      [((S, D), F32, None), ((S, D), BF16, None)])
    up = _mm("ffn_up", h2, w_up_t, 'nn', out_dtype=BF16)
    act = _ffn_mid_fwd(up, ffn_cw_t, ffn_cb_t)
    f = _mm("ffn_down", act, w_down_f, 'nn')

    def loss_fn(x1v, fv, gt, g, tg):
        x2 = x1v + gt * _rms(fv, g)
        err = x2 - tg
        return 0.5 * jnp.sum(jnp.mean(err * err, axis=-1))

    def k_loss(i, rv, bv):
        gtb, gb = _bc(bv[0][5:6, :], TM), _bc(bv[1][...], TM)
        val, (dx1v, dfv, dgt, dg) = jax.value_and_grad(loss_fn, argnums=(0, 1, 2, 3))(rv[0], rv[1], gtb, gb, rv[2])
        return (dx1v, dfv), (jnp.full((1, LANES), val, F32), _rs(dgt), _rs(dg))

    dx1_a, df, loss_acc, d_gt2, d_g_post_ffn = _rowwise(
        "loss_bwd", k_loss, ns, [_row(x1), _row(f), _row(tgt)], [mod_lat, g_post_ffn],
        [((S, D), F32, None), ((S, D), BF16, None)], [(1, LANES), (1, D), (1, D)])
    loss = lax.psum(loss_acc[0, 0], ("x", "y", "c"))

    d_act = _mm("ffn_down_dx", df, w_down_f, 'nt', out_dtype=BF16)
    d_w_down = _mm("ffn_down_dw", act, df, 'tn')
    d_up, d_ffn_cw_t, d_ffn_cb_t = _ffn_mid_bwd(up, d_act, ffn_cw_t, ffn_cb_t)
    d_ffn_cw, d_ffn_cb = _ff_from_tiles(d_ffn_cw_t), _ff_from_tiles(d_ffn_cb_t)
    d_h2 = _mm("ffn_up_dx", d_up, w_up_t, 'nt')
    d_w_up = _ff_from_tiles(_mm("ffn_up_dw", h2, d_up, 'tn'))

    def k_pre_ffn_bwd(i, rv, bv):
        ml = bv[0]
        gb, shb, scb = _bc(bv[1][...], TM), _bc(ml[3:4, :], TM), _bc(ml[4:5, :], TM)
        _, pull = jax.vjp(pre_fn, rv[0], gb, shb, scb)
        dxv, dg, dsh, dsc = pull(rv[1])
        return (rv[2] + dxv,), (_rs(dg), _rs(dsh), _rs(dsc))

    dx1, d_g_pre_ffn, d_sh2, d_sc2 = _rowwise(
        "pre_ffn_bwd", k_pre_ffn_bwd, ns, [_row(x1), _row(d_h2), _row(dx1_a)], [mod_lat, g_pre_ffn],
        [((S, D), F32, None)], [(1, D), (1, D), (1, D)])

    def k_post_mix_bwd(i, rv, bv):
        gtb, gb = _bc(bv[0][2:3, :], TM), _bc(bv[1][...], TM)
        _, pull = jax.vjp(post_mix_fn, rv[0], rv[1], gtb, gb)
        _, dyv, dgt, dg = pull(rv[2])
        return (dyv,), (_rs(dgt), _rs(dg))

    dy, d_gt1, d_g_post_mix = _rowwise(
        "post_mix_bwd", k_post_mix_bwd, ns, [_row(xs), _row(y), _row(dx1)], [mod_lat, g_post_mix],
        [((S, D), BF16, None)], [(1, D), (1, D)])
    d_y_in = _mm("out_proj_dx", dy, w_out_f, 'nt')
    d_w_out = _mm("out_proj_dw", y_in, dy, 'tn')

    def k_lru_out_bwd(i, rv, bv):
        _, pull = jax.vjp(lru_out_fn, rv[0], rv[1], rv[2])
        dhf, _, dgr = pull(rv[3])
        return (dhf, dgr), ()

    d_hsum, d_gr = _rowwise("lru_out_bwd", k_lru_out_bwd, ns,
                            [_row(h0, lat), _row(h1), gr_row, _row(d_y_in, None, (LRU_W, 0))], [],
                            [((S, LRU_W), F32, None), ((S, LRU_W), F32, None)])

    do_h = d_y_in[:, LRU_W:].reshape(S, HEADS, VDIM).transpose(1, 0, 2)

    def k_delta(i, rv, bv):
        return (jnp.sum(rv[0] * rv[1], axis=-1, keepdims=True), rv[1]), ()

    delta, do_b = _rowwise("attn_delta", k_delta, ns, [_row(o), _row(do_h)], [],
                           [((HEADS, S, 1), F32, None), ((HEADS, S, VDIM), BF16, None)])
    ffn_sends = [_cols_to_shards(d_w_up), d_w_down.reshape(N_DEV, D_FF // N_DEV, D)]
    dq, dk, dv, ffn_recv = _attn_bwd(q, k, v, do_b, lse.reshape(HEADS, 1, S), delta.reshape(HEADS, 1, S),
                                     ffn_sends, ['a2a'] * 2)

    def k_qrope_bwd(i, rv, bv):
        dqv, ctb, sub, sdb = rv
        return (jnp.concatenate([_rope_t(dqv[hh] * MLA_SCALE, ctb, sub, sdb) for hh in range(HEADS)], axis=1),), ()

    dq_lin = _rowwise("q_rope_bwd", k_qrope_bwd, ns, [_row(dq), _row(cos_q), _row(sup_q), _row(sdn_q)], [],
                      [((S, HEADS * HEAD_PAD), BF16, None)])[0]
    d_cqn = _mm("q_proj_dx", dq_lin, wq_p, 'nt')
    d_wq_p = _mm("q_proj_dw", cqn, dq_lin, 'tn')

    def k_rms_bwd(i, rv, bv):
        gb = _bc(bv[0][...], TM)
        _, pull = jax.vjp(_rms, rv[0], gb)
        dxv, dg = pull(rv[1])
        return (dxv,), (_rs(dg),)

    d_cq, d_g_q = _rowwise("q_norm_bwd", k_rms_bwd, ns,
                           [_row(proj, lat, (Q_RANK, OFF_CQ // Q_RANK)), _row(d_cqn)], [mla_g_q],
                           [((S, Q_RANK), F32, None)], [(1, Q_RANK)])

    def k_krope_bwd(i, rv, bv):
        dkv_, ctb, sub, sdb = rv
        tot = dkv_[0]
        for hh in range(1, HEADS):
            tot = tot + dkv_[hh]
        lane = lax.broadcasted_iota(jnp.int32, tot.shape, 1)
        tot = jnp.where((lane >= NOPE) & (lane < QK), tot, 0.0)
        return (jnp.concatenate([dkv_[hh] for hh in range(HEADS)], axis=1), _rope_t(tot, ctb, sub, sdb)), ()

    dk_lin, d_krp = _rowwise("k_rope_bwd", k_krope_bwd, nt, [_row(dk), _row(cos_k), _row(sup_k), _row(sdn_k)], [],
                             [((T, HEADS * HEAD_PAD), BF16, None), ((T, HEAD_PAD), F32, None)])
    dv_lin = dv.transpose(1, 0, 2).reshape(T, HEADS * VDIM).astype(BF16)
    d_ckvn_k = _mm("k_proj_dx", dk_lin, wk_p, 'nt')
    d_ckvn_v = _mm("v_proj_dx", dv_lin, wv_f, 'nt')
    d_wk_p = _mm("k_proj_dw", ckvn, dk_lin, 'tn')
    d_wv = _mm("v_proj_dw", ckvn, dv_lin, 'tn')

    def k_kvnorm_bwd(i, rv, bv):
        gb = _bc(bv[0][...], TM)
        _, pull = jax.vjp(_rms, rv[0], gb)
        dxv, dg = pull(rv[1] + rv[2])
        return (dxv,), (_rs(dg),)

    d_ckv, d_g_kv = _rowwise("kv_norm_bwd", k_kvnorm_bwd, nt,
                             [_row(proj, None, (KV_RANK, OFF_CKV // KV_RANK)), _row(d_ckvn_k), _row(d_ckvn_v)],
                             [mla_g_kv], [((T, KV_RANK), F32, None)], [(1, KV_RANK)])

    zc = jnp.zeros((CN, LRU_W), F32)
    one_row, zero_row = jnp.ones((1, LRU_W), F32), jnp.zeros((1, LRU_W), F32)
    lam0 = _scan("lru_scan_f_bwd", jnp.concatenate([a0[1:], one_row]), jnp.concatenate([zc, d_hsum]), True)
    lam1 = _scan("lru_scan_r_bwd", jnp.concatenate([one_row, a1[:-1]]), jnp.concatenate([d_hsum, zc]), False)
    hprev0 = jnp.concatenate([zero_row, h0[:-1]])
    hprev1 = jnp.concatenate([h1[1:], zero_row])

    def k_gates_bwd(i, rv, bv):
        xv, l0, hp0, l1, hp1 = rv
        wg, bg, lamv = [b[...] for b in bv]
        xb = xv.astype(BF16)
        z = jnp.dot(xb, wg, preferred_element_type=F32) + bg
        spb = _bc(_softplus(-lamv), TM)
        _, pull = jax.vjp(_gates, z, xv, spb)
        dz, dxv, dsp = pull((l0 * hp0, l0, l1 * hp1, l1))
        dzb = dz.astype(BF16)
        dxv = dxv + lax.dot_general(dzb, wg, NT_DIMS, preferred_element_type=F32)
        dwg = lax.dot_general(xb, dzb, (((0,), (0,)), ((), ())), preferred_element_type=F32)
        dlam = -_rs(dsp) * jax.nn.sigmoid(-lamv)
        return (dxv,), (dwg, _rs(dz), dlam)

    d_xcv, d_w_gate, d_b_gate, d_lam = _rowwise(
        "lru_gates_bwd", k_gates_bwd, nt,
        [_row(xcv), _row(lam0), _row(hprev0), _row(lam1, swp), _row(hprev1, swp)], [w_gate, b_gate, lam_row],
        [((T, LRU_W), F32, None)], [(LRU_W, 4 * LRU_W), (1, 4 * LRU_W), (1, 2 * LRU_W)])
    d_xr, d_lru_cw, d_lru_cb = _dwconv_bwd("lru_conv_bwd", proj, d_xcv, lru_cw, LRU_CONV_LEFT, (0, ct))

    def k_dproj(i, rv, bv):
        is_lat = i >= ct
        return (jnp.concatenate([rv[0], jnp.where(is_lat, rv[1], 0.0), jnp.where(is_lat, rv[2], 0.0), rv[3], rv[4]],
                                axis=1),), ()

    d_proj = _rowwise("d_proj", k_dproj, nt,
                      [_row(d_xr), _row(d_gr, lat_or_0), _row(d_cq, lat_or_0), _row(d_ckv), _row(d_krp)], [],
                      [((T, IN_W_PAD), BF16, None)])[0]
    d_h_pre = _mm("in_proj_dx", d_proj, w_in_p, 'nt')
    d_w_in_p = _mm("in_proj_dw", h_pre, d_proj, 'tn')

    def k_pre_bwd(i, rv, bv):
        g, ml, mc = bv[0][...], bv[1], bv[2]
        sh, sc = sel_mod(i, ml, mc, 0)
        _, pull = jax.vjp(pre_fn, rv[0], _bc(g, TM), _bc(sh, TM), _bc(sc, TM))
        dxv, dg, dsh, dsc = pull(rv[1])
        is_lat = i >= ct
        dsh, dsc = _rs(dsh), _rs(dsc)
        zero = jnp.zeros_like(dsh)
        return ((dxv + jnp.where(is_lat, rv[2], 0.0),),
                (_rs(dg), jnp.where(is_lat, dsh, zero), jnp.where(is_lat, dsc, zero),
                 jnp.where(is_lat, zero, dsh), jnp.where(is_lat, zero, dsc)))

    dxa, d_g_pre_mix, d_sh1, d_sc1, d_csh1, d_csc1 = _rowwise(
        "pre_mix_bwd", k_pre_bwd, nt, [_row(xa), _row(d_h_pre), _row(dx1, lat_or_0)],
        [g_pre_mix, mod_lat, mod_ctx], [((T, D), F32, None)], [(1, D)] * 5)
    grad_x = dxa[CN:][None]

    zrow = jnp.zeros((1, D), F32)
    d_mod_lat = jnp.concatenate([d_sh1, d_sc1, d_gt1, d_sh2, d_sc2, d_gt2], axis=1)
    d_mod_ctx = jnp.concatenate([d_csh1, d_csc1, zrow, zrow, zrow, zrow], axis=1)
    d_mod_mine = jnp.concatenate([d_mod_lat, d_mod_ctx, jnp.zeros((SUBLANES - 2, N_MOD * D), F32)])
    (d_mod_all,) = _exchange("gather_dmod", [d_mod_mine], ['ag'])
    dm_lat_loc = lax.dynamic_slice(d_mod_all[:, 0], (0, me * ncol), (N_DEV, ncol))
    dm_ctx_loc = lax.dynamic_slice(d_mod_all[:, 1], (0, me * ncol), (N_DEV, ncol))

    def mod_bwd(c16_r, w_r, dml_r, dmc_r):
        c16_v = c16_r[...]
        sig = jax.nn.sigmoid(c16_v)
        sl = c16_v * sig
        dctx = dmc_r[0:1, :]
        for d in range(1, N_DEV):
            dctx = dctx + dmc_r[d:d + 1, :]
        row = lax.broadcasted_iota(jnp.int32, (2 * SUBLANES, ncol), 0)
        dm16 = dml_r[...] + jnp.where(row == N_DEV, _bc(dctx, 2 * SUBLANES), 0.0)
        dw = lax.dot_general(sl.astype(BF16), dm16.astype(BF16), (((0,), (0,)), ((), ())), preferred_element_type=F32)
        dsl = lax.dot_general(dm16.astype(BF16), w_r[...].astype(BF16), NT_DIMS, preferred_element_type=F32)
        dc = dsl * (sig * (1.0 + c16_v * (1.0 - sig)))
        return dw, dc

    dm_lat16 = jnp.concatenate([dm_lat_loc, jnp.zeros((2 * SUBLANES - N_DEV, ncol), F32)])
    g_w_mod, dc16 = _single("mod_bwd", mod_bwd, [c16, w_mod[0], dm_lat16, dm_ctx_loc],
                            [((D, ncol), F32), ((2 * SUBLANES, D), F32)])
    d_c_ctx_part = dc16[N_DEV]

    def diag_blocks(dw):
        return jnp.stack([dw[hh * 64:(hh + 1) * 64, hh * 64:(hh + 1) * 64] for hh in range(LRU_HEADS)])

    d_lru_w_a = jnp.stack([diag_blocks(d_w_gate[:, 0:LRU_W]), diag_blocks(d_w_gate[:, 2 * LRU_W:3 * LRU_W])])[None]
    d_lru_w_x = jnp.stack([diag_blocks(d_w_gate[:, LRU_W:2 * LRU_W]), diag_blocks(d_w_gate[:, 3 * LRU_W:])])[None]
    d_b_a = jnp.stack([d_b_gate[0, 0:LRU_W], d_b_gate[0, 2 * LRU_W:3 * LRU_W]])
    d_b_x = jnp.stack([d_b_gate[0, LRU_W:2 * LRU_W], d_b_gate[0, 3 * LRU_W:]])

    rep_part = dict(c_ctx=d_c_ctx_part, b_mod=d_mod_lat + d_mod_ctx, g_pre_mix=d_g_pre_mix, g_post_mix=d_g_post_mix,
                    g_pre_ffn=d_g_pre_ffn, g_post_ffn=d_g_post_ffn, lru_conv_b=d_lru_cb, lru_w_a=d_lru_w_a,
                    lru_w_x=d_lru_w_x, mla_g_q=d_g_q, mla_g_kv=d_g_kv, ffn_conv_b=d_ffn_cb)
    rep_shapes = [W[n].shape for n in REPLICATED]
    rep_rows = _pack_rows(sum(W[n].size for n in REPLICATED))
    rep_loc = _pack([rep_part[n] for n in REPLICATED], rep_rows)

    d_w_in = jnp.concatenate([d_w_in_p[:, :OFF_KR], d_w_in_p[:, OFF_KR + NOPE:OFF_KR + QK]], axis=1)
    d_w_uq = d_wq_p.reshape(Q_RANK, HEADS, HEAD_PAD)[:, :, :QK].reshape(Q_RANK, HEADS * QK)
    d_w_ukv = jnp.concatenate([d_wk_p.reshape(KV_RANK, HEADS, HEAD_PAD)[:, :, :NOPE],
                               d_wv.reshape(KV_RANK, HEADS, VDIM)], axis=2).reshape(KV_RANK, HEADS * (NOPE + VDIM))
    small_full = dict(lru_conv_w=d_lru_cw, lru_b_a=d_b_a, lru_b_x=d_b_x, lru_lambda=d_lam.reshape(2, LRU_W),
                      ffn_conv_w=d_ffn_cw)
    small_sh = jnp.concatenate([_cols_to_shards(small_full[n]).reshape(N_DEV, -1) for n in SMALL_SHARDED], axis=1)
    n_sh = small_sh.shape[1]
    sh_rows = _pack_rows(n_sh)
    small_sh = jnp.pad(small_sh, ((0, 0), (0, sh_rows * LANES - n_sh))).reshape(N_DEV, sh_rows, LANES)

    sends = [_cols_to_shards(d_w_in), _cols_to_shards(d_w_uq), _cols_to_shards(d_w_ukv),
             d_w_out.reshape(N_DEV, D // N_DEV, D), small_sh]
    recv = _exchange("exchange_grads", sends + [rep_loc], ['a2a'] * 5 + ['ag'])
    big_parts = list(recv[:4]) + list(ffn_recv)

    res = {}

    def adam(name, w2, m2, v2, parts):
        return _adamw("adamw_" + name, w2, m2, v2, parts)

    for n, parts in zip(big, big_parts):
        shp = W[n].shape
        outs = adam(n, W[n][0], M[n][0], V[n][0], parts)
        res[n] = [o_.reshape(shp) for o_ in outs]
    outs = adam('w_mod', w_mod[0], m_w_mod[0], v_w_mod[0], g_w_mod[None])
    res['w_mod'] = [o_.reshape(w_mod.shape) for o_ in outs]

    sh_shapes = [W[n].shape for n in SMALL_SHARDED]
    pk = lambda dct: _pack([dct[n] for n in SMALL_SHARDED], sh_rows)
    outs = adam('small_sharded', pk(W), pk(M), pk(V), recv[4])
    for n, vals in zip(SMALL_SHARDED, zip(*[_unpack(o_, sh_shapes) for o_ in outs])):
        res[n] = list(vals)

    pr = lambda dct: _pack([dct[n] for n in REPLICATED], rep_rows)
    outs = adam('replicated', pr(W), pr(M), pr(V), recv[5])
    for n, vals in zip(REPLICATED, zip(*[_unpack(o_, rep_shapes) for o_ in outs])):
        res[n] = list(vals)

    return (loss, grad_x, *[res[n][0] for n in WEIGHTS], *[res[n][1] for n in WEIGHTS],
            *[res[n][2] for n in WEIGHTS], *[res[n][3] for n in WEIGHTS])
```

```python
import functools
import math

import jax
import jax.numpy as jnp
from jax import lax
from jax.experimental import pallas as pl
from jax.experimental.pallas import tpu as pltpu

F32 = jnp.float32
BF16 = jnp.bfloat16
MESH = pl.DeviceIdType.MESH

N_DEV = 8
ROW_TILE = 256
SUBLANES = 8
LANES = 128
VMEM_LIMIT = 56 * 1024 * 1024

D_MODEL = 1024
LRU_W = 512
LRU_HEADS = 8
LRU_CONV_K = 4
LRU_CONV_LEFT = 2
LRU_C = 8.0
HEADS = 8
NOPE = 64
ROPE = 32
VDIM = 64
QK = NOPE + ROPE
HEAD_PAD = 128
Q_RANK = 256
KV_RANK = 128
MLA_SCALE = QK ** -0.5
ROPE_PAIRS = ROPE // 4
ROPE_BASE = 10000.0
GRID_W = 64
D_FF = 2816
FFN_CONV_K = 3
FFN_CONV_LEFT = 1
N_MOD = 6
EPS = 1e-6
IN_W = 2 * LRU_W + Q_RANK + KV_RANK + ROPE
IN_W_PAD = 2 * LRU_W + Q_RANK + KV_RANK + HEAD_PAD
OFF_GR, OFF_CQ, OFF_CKV, OFF_KR = LRU_W, 2 * LRU_W, 2 * LRU_W + Q_RANK, 2 * LRU_W + Q_RANK + KV_RANK

ADAM_LR, ADAM_B1, ADAM_B2, ADAM_EPS, ADAM_WD, ADAM_STEP = 0.001, 0.9, 0.999, 1e-08, 0.01, 10

WEIGHTS = ['c_ctx', 'w_mod', 'b_mod', 'g_pre_mix', 'g_post_mix', 'g_pre_ffn', 'g_post_ffn', 'w_in', 'lru_conv_w',
           'lru_conv_b', 'lru_w_a', 'lru_b_a', 'lru_w_x', 'lru_b_x', 'lru_lambda', 'mla_g_q', 'mla_w_uq', 'mla_g_kv',
           'mla_w_ukv', 'w_out', 'ffn_w_up', 'ffn_conv_w', 'ffn_conv_b', 'ffn_w_down']
REPLICATED = ['c_ctx', 'b_mod', 'g_pre_mix', 'g_post_mix', 'g_pre_ffn', 'g_post_ffn', 'lru_conv_b', 'lru_w_a',
              'lru_w_x', 'mla_g_q', 'mla_g_kv', 'ffn_conv_b']
SMALL_SHARDED = ['lru_conv_w', 'lru_b_a', 'lru_b_x', 'lru_lambda', 'ffn_conv_w']


def _pick(d, prefs):
    for p in prefs:
        if d % p == 0:
            return p
    return d


def _params(sem=None):
    return pltpu.CompilerParams(dimension_semantics=sem, vmem_limit_bytes=VMEM_LIMIT)


MM_TILES = (1024, 1408, 768, 512, 256, 128)


def _mm(name, a, b, mode, out_dtype=F32):
    if mode == 'nn':
        (m, k), (_, n) = a.shape, b.shape
    elif mode == 'nt':
        (m, k), (n, _) = a.shape, b.shape
    else:
        (k, m), (_, n) = a.shape, b.shape
    tm = _pick(m, MM_TILES)
    tn = _pick(n, MM_TILES)
    tk = _pick(k, MM_TILES)
    nk = k // tk
    if mode == 'nn':
        a_spec = pl.BlockSpec((tm, tk), lambda i, j, kk: (i, kk))
        b_spec = pl.BlockSpec((tk, tn), lambda i, j, kk: (kk, j))
        dn = (((1,), (0,)), ((), ()))
    elif mode == 'nt':
        a_spec = pl.BlockSpec((tm, tk), lambda i, j, kk: (i, kk))
        b_spec = pl.BlockSpec((tn, tk), lambda i, j, kk: (j, kk))
        dn = (((1,), (1,)), ((), ()))
    else:
        a_spec = pl.BlockSpec((tk, tm), lambda i, j, kk: (kk, i))
        b_spec = pl.BlockSpec((tk, tn), lambda i, j, kk: (kk, j))
        dn = (((0,), (0,)), ((), ()))

    def body(a_ref, b_ref, o_ref, acc_ref):
        kk = pl.program_id(2)

        @pl.when(kk == 0)
        def _():
            acc_ref[...] = jnp.zeros_like(acc_ref)

        acc_ref[...] += lax.dot_general(a_ref[...].astype(BF16), b_ref[...].astype(BF16), dn,
                                        preferred_element_type=F32)

        @pl.when(kk == nk - 1)
        def _():
            o_ref[...] = acc_ref[...].astype(o_ref.dtype)

    return pl.pallas_call(
        body, name=name, grid=(m // tm, n // tn, nk),
        in_specs=[a_spec, b_spec], out_specs=pl.BlockSpec((tm, tn), lambda i, j, kk: (i, j)),
        out_shape=jax.ShapeDtypeStruct((m, n), out_dtype),
        scratch_shapes=[pltpu.VMEM((tm, tn), F32)],
        compiler_params=_params(("parallel", "parallel", "arbitrary")),
    )(a, b)


def _row(a, idx=None, col=None):
    return dict(a=a, idx=idx, col=col)


def _rowwise(name, fn, n_tiles, rows, bcast, out_rows, out_acc=(), tm=ROW_TILE):
    in_specs = []
    for r in rows:
        a, idx, col = r['a'], r['idx'] or (lambda i: i), r['col']
        if a.ndim == 2:
            w, ci = col if col else (a.shape[1], 0)
            in_specs.append(pl.BlockSpec((tm, w), lambda i, idx=idx, ci=ci: (idx(i), ci)))
        else:
            in_specs.append(pl.BlockSpec((a.shape[0], tm, a.shape[2]), lambda i, idx=idx: (0, idx(i), 0)))
    for b in bcast:
        in_specs.append(pl.BlockSpec(b.shape, lambda i, nd=b.ndim: (0,) * nd))
    out_specs, out_shape = [], []
    for shape, dtype, idx in out_rows:
        idx = idx or (lambda i: i)
        if len(shape) == 2:
            out_specs.append(pl.BlockSpec((tm, shape[1]), lambda i, idx=idx: (idx(i), 0)))
        else:
            out_specs.append(pl.BlockSpec((shape[0], tm, shape[2]), lambda i, idx=idx: (0, idx(i), 0)))
        out_shape.append(jax.ShapeDtypeStruct(shape, dtype))
    for shape in out_acc:
        out_specs.append(pl.BlockSpec(shape, lambda i, nd=len(shape): (0,) * nd))
        out_shape.append(jax.ShapeDtypeStruct(shape, F32))
    nr, nb, no = len(rows), len(bcast), len(out_rows)

    def body(*refs):
        i = pl.program_id(0)
        rvals = [r[...] for r in refs[:nr]]
        bvals = list(refs[nr:nr + nb])
        o_rows, o_acc = fn(i, rvals, bvals)
        for ref, v in zip(refs[nr + nb:nr + nb + no], o_rows):
            ref[...] = v.astype(ref.dtype)
        acc_refs = refs[nr + nb + no:]
        if acc_refs:
            @pl.when(i == 0)
            def _():
                for ref in acc_refs:
                    ref[...] = jnp.zeros_like(ref)
            for ref, v in zip(acc_refs, o_acc):
                ref[...] += v

    return pl.pallas_call(
        body, name=name, grid=(n_tiles,), in_specs=in_specs, out_specs=out_specs, out_shape=out_shape,
        compiler_params=_params(("arbitrary",)),
    )(*[r['a'] for r in rows], *bcast)


def _single(name, fn, ins, out_shapes):
    def body(*refs):
        outs = fn(*refs[:len(ins)])
        for ref, v in zip(refs[len(ins):], outs):
            ref[...] = v.astype(ref.dtype)

    return pl.pallas_call(
        body, name=name,
        in_specs=[pl.BlockSpec(memory_space=pltpu.VMEM)] * len(ins),
        out_specs=[pl.BlockSpec(memory_space=pltpu.VMEM)] * len(out_shapes),
        out_shape=[jax.ShapeDtypeStruct(s, d) for s, d in out_shapes],
        compiler_params=_params(),
    )(*ins)


def _bc(p, n):
    return jnp.broadcast_to(p, (n, p.shape[-1]))


def _rs(g):
    return jnp.sum(g, axis=0, keepdims=True)


def _rms(x, g):
    return x * lax.rsqrt(jnp.mean(x * x, axis=-1, keepdims=True) + EPS) * g


def _conv_specs(r, cw, tm, halo=SUBLANES):
    th = tm // halo
    last = r // halo - 1
    prev = pl.BlockSpec((halo, cw), lambda c, i: (jnp.maximum(i * th - 1, 0), c))
    cur = pl.BlockSpec((tm, cw), lambda c, i: (i, c))
    nxt = pl.BlockSpec((halo, cw), lambda c, i: (jnp.minimum((i + 1) * th, last), c))
    return [prev, cur, nxt]


def _fill_ext(ext_ref, prev_ref, cur_ref, next_ref, i, n_tiles, seg_starts, tm):
    prev_ok = functools.reduce(jnp.logical_and, [i != s for s in seg_starts])
    next_ok = functools.reduce(jnp.logical_and, [i + 1 != s for s in seg_starts] + [i + 1 < n_tiles])
    ext_ref[0:SUBLANES, :] = jnp.where(prev_ok, prev_ref[...].astype(F32), 0.0)
    ext_ref[SUBLANES:SUBLANES + tm, :] = cur_ref[...].astype(F32)
    ext_ref[SUBLANES + tm:, :] = jnp.where(next_ok, next_ref[...].astype(F32), 0.0)


def _dwconv_fwd(name, x, w, b, left, seg_starts, cw=512, tm=ROW_TILE):
    r, c = x.shape[0], w.shape[1]
    kw = w.shape[0]
    n_tiles = r // tm

    def body(prev_ref, cur_ref, next_ref, w_ref, b_ref, o_ref, ext_ref):
        i = pl.program_id(1)
        _fill_ext(ext_ref, prev_ref, cur_ref, next_ref, i, n_tiles, seg_starts, tm)
        out = jnp.broadcast_to(b_ref[...], (tm, cw))
        for k in range(kw):
            out = out + ext_ref[pl.ds(SUBLANES + k - left, tm), :] * w_ref[k:k + 1, :]
        o_ref[...] = out

    return pl.pallas_call(
        body, name=name, grid=(c // cw, n_tiles),
        in_specs=_conv_specs(r, cw, tm) + [pl.BlockSpec((kw, cw), lambda c_, i: (0, c_)),
                                           pl.BlockSpec((1, cw), lambda c_, i: (0, c_))],
        out_specs=pl.BlockSpec((tm, cw), lambda c_, i: (i, c_)),
        out_shape=jax.ShapeDtypeStruct((r, c), F32),
        scratch_shapes=[pltpu.VMEM((tm + 2 * SUBLANES, cw), F32)],
        compiler_params=_params(("parallel", "arbitrary")),
    )(x, x, x, w, b)


def _dwconv_bwd(name, x, dy, w, left, seg_starts, out_dtype=F32, cw=512, tm=ROW_TILE):
    r, c = dy.shape
    kw = w.shape[0]
    n_tiles = r // tm

    def body(xp, xc, xn, dp, dc, dn, w_ref, dx_ref, dw_ref, db_ref, xe_ref, de_ref):
        i = pl.program_id(1)
        _fill_ext(xe_ref, xp, xc, xn, i, n_tiles, seg_starts, tm)
        _fill_ext(de_ref, dp, dc, dn, i, n_tiles, seg_starts, tm)
        dyc = dc[...].astype(F32)
        dx = jnp.zeros((tm, cw), F32)
        dws = []
        for k in range(kw):
            dx = dx + de_ref[pl.ds(SUBLANES - k + left, tm), :] * w_ref[k:k + 1, :]
            dws.append(jnp.sum(dyc * xe_ref[pl.ds(SUBLANES + k - left, tm), :], axis=0, keepdims=True))
        dx_ref[...] = dx.astype(dx_ref.dtype)

        @pl.when(i == 0)
        def _():
            dw_ref[...] = jnp.zeros_like(dw_ref)
            db_ref[...] = jnp.zeros_like(db_ref)

        for k in range(kw):
            dw_ref[k:k + 1, :] += dws[k]
        db_ref[...] += jnp.sum(dyc, axis=0, keepdims=True)

    return pl.pallas_call(
        body, name=name, grid=(c // cw, n_tiles),
        in_specs=_conv_specs(r, cw, tm) + _conv_specs(r, cw, tm) + [pl.BlockSpec((kw, cw), lambda c_, i: (0, c_))],
        out_specs=[pl.BlockSpec((tm, cw), lambda c_, i: (i, c_)),
                   pl.BlockSpec((kw, cw), lambda c_, i: (0, c_)),
                   pl.BlockSpec((1, cw), lambda c_, i: (0, c_))],
        out_shape=[jax.ShapeDtypeStruct((r, c), out_dtype), jax.ShapeDtypeStruct((kw, c), F32),
                   jax.ShapeDtypeStruct((1, c), F32)],
        scratch_shapes=[pltpu.VMEM((tm + 2 * SUBLANES, cw), F32), pltpu.VMEM((tm + 2 * SUBLANES, cw), F32)],
        compiler_params=_params(("parallel", "arbitrary")),
    )(x, x, x, dy, dy, dy, w)


FF_TILE = 256
FF_HALO = 16
FF_STRIP = 32


def _ffn_fill(ext_ref, prev_ref, cur_ref, next_ref, i, n_tiles, tm):
    ext_ref[0:FF_HALO, :] = jnp.where(i > 0, prev_ref[...].astype(F32), 0.0)
    ext_ref[FF_HALO:FF_HALO + tm, :] = cur_ref[...].astype(F32)
    ext_ref[FF_HALO + tm:, :] = jnp.where(i + 1 < n_tiles, next_ref[...].astype(F32), 0.0)


def _ffn_conv(ext_ref, w_ref, b_ref, start, rows):
    out = jnp.broadcast_to(b_ref[...], (rows, 2 * FF_TILE))
    for k in range(FFN_CONV_K):
        out = out + ext_ref[pl.ds(start + k - FFN_CONV_LEFT, rows), :] * w_ref[k:k + 1, :]
    return out


def _ffn_mid_fwd(up, w, b, tm=ROW_TILE):
    s, c2 = up.shape
    n_tiles = s // tm
    cw = 2 * FF_TILE

    def body(prev_ref, cur_ref, next_ref, w_ref, b_ref, o_ref, ext_ref):
        i = pl.program_id(1)
        _ffn_fill(ext_ref, prev_ref, cur_ref, next_ref, i, n_tiles, tm)
        for r0 in range(0, tm, FF_STRIP):
            upc = _ffn_conv(ext_ref, w_ref, b_ref, FF_HALO + r0, FF_STRIP)
            uv, gv = upc[:, :FF_TILE], upc[:, FF_TILE:]
            o_ref[r0:r0 + FF_STRIP, :] = (gv * jax.nn.sigmoid(gv) * uv).astype(o_ref.dtype)

    return pl.pallas_call(
        body, name="ffn_mid", grid=(c2 // cw, n_tiles),
        in_specs=_conv_specs(s, cw, tm, FF_HALO) + [pl.BlockSpec((FFN_CONV_K, cw), lambda c_, i: (0, c_)),
                                                   pl.BlockSpec((1, cw), lambda c_, i: (0, c_))],
        out_specs=pl.BlockSpec((tm, FF_TILE), lambda c_, i: (i, c_)),
        out_shape=jax.ShapeDtypeStruct((s, c2 // 2), BF16),
        scratch_shapes=[pltpu.VMEM((tm + 2 * FF_HALO, cw), F32)],
        compiler_params=_params(("parallel", "arbitrary")),
    )(up, up, up, w, b)


def _ffn_mid_bwd(up, d_act, w, b, tm=ROW_TILE):
    s, c2 = up.shape
    n_tiles = s // tm
    cw = 2 * FF_TILE
    h8 = SUBLANES

    def gate_bwd(upc, dact):
        uv, gv = upc[:, :FF_TILE], upc[:, FF_TILE:]
        sg = jax.nn.sigmoid(gv)
        return jnp.concatenate([dact * (gv * sg), dact * uv * (sg * (1.0 + gv * (1.0 - sg)))], axis=1)

    def body(up_p, up_c, up_n, da_p, da_c, da_n, w_ref, b_ref, dup_ref, dw_ref, db_ref, ext_ref, dext_ref):
        i = pl.program_id(1)
        _ffn_fill(ext_ref, up_p, up_c, up_n, i, n_tiles, tm)
        dws = [jnp.zeros((1, cw), F32) for _ in range(FFN_CONV_K)]
        dbs = jnp.zeros((1, cw), F32)
        for r0 in range(0, tm, FF_STRIP):
            d_c = gate_bwd(_ffn_conv(ext_ref, w_ref, b_ref, FF_HALO + r0, FF_STRIP),
                           da_c[r0:r0 + FF_STRIP, :].astype(F32))
            dext_ref[FF_HALO + r0:FF_HALO + r0 + FF_STRIP, :] = d_c
            for k in range(FFN_CONV_K):
                xk = ext_ref[pl.ds(FF_HALO + r0 + k - FFN_CONV_LEFT, FF_STRIP), :]
                dws[k] = dws[k] + jnp.sum(d_c * xk, axis=0, keepdims=True)
            dbs = dbs + jnp.sum(d_c, axis=0, keepdims=True)
        da_prev = jnp.where(i > 0, da_p[...].astype(F32)[FF_HALO - h8:, :], 0.0)
        da_next = jnp.where(i + 1 < n_tiles, da_n[...].astype(F32)[:h8, :], 0.0)
        dext_ref[FF_HALO - h8:FF_HALO, :] = gate_bwd(_ffn_conv(ext_ref, w_ref, b_ref, FF_HALO - h8, h8), da_prev)
        dext_ref[FF_HALO + tm:FF_HALO + tm + h8, :] = gate_bwd(_ffn_conv(ext_ref, w_ref, b_ref, FF_HALO + tm, h8), da_next)
        for r0 in range(0, tm, FF_STRIP):
            dup = jnp.zeros((FF_STRIP, cw), F32)
            for k in range(FFN_CONV_K):
                dup = dup + dext_ref[pl.ds(FF_HALO + r0 - k + FFN_CONV_LEFT, FF_STRIP), :] * w_ref[k:k + 1, :]
            dup_ref[r0:r0 + FF_STRIP, :] = dup.astype(dup_ref.dtype)

        @pl.when(i == 0)
        def _():
            dw_ref[...] = jnp.zeros_like(dw_ref)
            db_ref[...] = jnp.zeros_like(db_ref)

        for k in range(FFN_CONV_K):
            dw_ref[k:k + 1, :] += dws[k]
        db_ref[...] += dbs

    def half_specs():
        th = tm // FF_HALO
        last = s // FF_HALO - 1
        return [pl.BlockSpec((FF_HALO, FF_TILE), lambda c_, i: (jnp.maximum(i * th - 1, 0), c_)),
                pl.BlockSpec((tm, FF_TILE), lambda c_, i: (i, c_)),
                pl.BlockSpec((FF_HALO, FF_TILE), lambda c_, i: (jnp.minimum((i + 1) * th, last), c_))]

    return pl.pallas_call(
        body, name="ffn_mid_bwd", grid=(c2 // cw, n_tiles),
        in_specs=_conv_specs(s, cw, tm, FF_HALO) + half_specs() + [
            pl.BlockSpec((FFN_CONV_K, cw), lambda c_, i: (0, c_)), pl.BlockSpec((1, cw), lambda c_, i: (0, c_))],
        out_specs=[pl.BlockSpec((tm, cw), lambda c_, i: (i, c_)),
                   pl.BlockSpec((FFN_CONV_K, cw), lambda c_, i: (0, c_)),
                   pl.BlockSpec((1, cw), lambda c_, i: (0, c_))],
        out_shape=[jax.ShapeDtypeStruct((s, c2), BF16), jax.ShapeDtypeStruct((FFN_CONV_K, c2), F32),
                   jax.ShapeDtypeStruct((1, c2), F32)],
        scratch_shapes=[pltpu.VMEM((tm + 2 * FF_HALO, cw), F32), pltpu.VMEM((tm + 2 * FF_HALO, cw), F32)],
        compiler_params=_params(("parallel", "arbitrary")),
    )(up, up, up, d_act, d_act, d_act, w, b)


def _ff_permute(name, w, to_tiles):
    r = w.shape[0]
    nb = D_FF // FF_TILE
    natural = pl.BlockSpec((r, FF_TILE), lambda j, half: (0, half * nb + j))
    tiled = pl.BlockSpec((r, FF_TILE), lambda j, half: (0, 2 * j + half))

    def body(x_ref, o_ref):
        o_ref[...] = x_ref[...]

    return pl.pallas_call(
        body, name=name, grid=(nb, 2),
        in_specs=[natural if to_tiles else tiled], out_specs=tiled if to_tiles else natural,
        out_shape=jax.ShapeDtypeStruct(w.shape, w.dtype), compiler_params=_params(("parallel", "parallel")),
    )(w)


def _ff_to_tiles(w):
    r = w.shape[0]
    return w.reshape(r, 2, D_FF // FF_TILE, FF_TILE).transpose(0, 2, 1, 3).reshape(r, 2 * D_FF)


def _ff_from_tiles(w):
    r = w.shape[0]
    return w.reshape(r, D_FF // FF_TILE, 2, FF_TILE).transpose(0, 2, 1, 3).reshape(r, 2 * D_FF)


SCAN_UNROLL = 8


def _scan(name, a, u, reverse, shifted=False, u_off=0):
    t, c = a.shape
    us = u.shape[0]
    n8 = t // SUBLANES
    lo, hi = 0, SUBLANES - 1

    def body(a_ref, u_ref, h_ref):
        row = lax.broadcasted_iota(jnp.int32, (SUBLANES, LANES), 0)
        last = lo if reverse else hi

        def tile(ref, base):
            return ref[pl.ds(pl.multiple_of(base, SUBLANES), SUBLANES), :]

        def local(blk):
            base = blk * SUBLANES
            av = tile(a_ref, base)
            if shifted and reverse:
                nb = tile(a_ref, jnp.minimum(base + SUBLANES, t - SUBLANES))
                edge = jnp.where(base + SUBLANES >= t, 1.0, pltpu.roll(nb, hi, 0))
                av = jnp.where(row < hi, pltpu.roll(av, hi, 0), edge)
            elif shifted:
                pb = tile(a_ref, jnp.maximum(base - SUBLANES, 0))
                edge = jnp.where(base == 0, 1.0, pltpu.roll(pb, 1, 0))
                av = jnp.where(row >= 1, pltpu.roll(av, 1, 0), edge)
            ub = base - u_off
            hv = jnp.where((ub >= 0) & (ub < us), tile(u_ref, jnp.clip(ub, 0, us - SUBLANES)), 0.0)
            for s in (1, 2, 4):
                shift = SUBLANES - s if reverse else s
                ok = (row < SUBLANES - s) if reverse else (row >= s)
                a_sh = jnp.where(ok, pltpu.roll(av, shift, 0), 1.0)
                h_sh = jnp.where(ok, pltpu.roll(hv, shift, 0), 0.0)
                hv = av * h_sh + hv
                av = av * a_sh
            a_last = jnp.sum(jnp.where(row == last, av, 0.0), axis=0, keepdims=True)
            h_last = jnp.sum(jnp.where(row == last, hv, 0.0), axis=0, keepdims=True)
            return base, av, hv, a_last, h_last

        def step(j, carry):
            parts = []
            for k in range(SCAN_UNROLL):
                idx = j * SCAN_UNROLL + k
                parts.append(local((n8 - 1 - idx) if reverse else idx))
            for base, av, hv, a_last, h_last in parts:
                h_ref[pl.ds(pl.multiple_of(base, SUBLANES), SUBLANES), :] = av * carry + hv
                carry = a_last * carry + h_last
            return carry

        lax.fori_loop(0, n8 // SCAN_UNROLL, step, jnp.zeros((1, LANES), F32))

    return pl.pallas_call(
        body, name=name, grid=(c // LANES,),
        in_specs=[pl.BlockSpec((t, LANES), lambda j: (0, j)), pl.BlockSpec((us, LANES), lambda j: (0, j))],
        out_specs=pl.BlockSpec((t, LANES), lambda j: (0, j)),
        out_shape=jax.ShapeDtypeStruct((t, c), F32),
        compiler_params=_params(("parallel",)),
    )(a, u)


NT_DIMS = (((1,), (1,)), ((), ()))


LOG2E = 1.4426950408889634
SCALE2 = MLA_SCALE * LOG2E
ATTN_TILES = (512, 256, 128)
KEY_CHUNKS = (768, 512, 256, 128)
QUERY_CHUNKS = (1024, 512, 256, 128)


def _attn_fwd(q, k, v, ride_arrs, ride_modes):
    h, s, _ = q.shape
    t = k.shape[1]
    tq = _pick(s, ATTN_TILES)
    ck = _pick(t, KEY_CHUNKS)
    n = len(ride_arrs)

    def body(*refs):
        q_ref, k_ref, v_ref = refs[:3]
        o_ref, lse_ref = refs[3 + n:5 + n]
        hh, i = pl.program_id(0), pl.program_id(1)
        finish = _riding_exchange(refs, 3, 2, n, ride_modes, (hh == 0) & (i == 0),
                                  (hh == h - 1) & (i == s // tq - 1))
        qv = q_ref[0]
        m = l = acc = None
        for j in range(t // ck):
            kj, vj = k_ref[0, j * ck:(j + 1) * ck, :], v_ref[0, j * ck:(j + 1) * ck, :]
            s2 = lax.dot_general(qv, kj, NT_DIMS, preferred_element_type=F32) * SCALE2
            mj = jnp.max(s2, axis=-1, keepdims=True)
            m_new = mj if j == 0 else jnp.maximum(m, mj)
            p = jnp.exp2(s2 - m_new)
            lj = jnp.sum(p, axis=-1, keepdims=True)
            pv = jnp.dot(p.astype(BF16), vj, preferred_element_type=F32)
            if j == 0:
                l, acc = lj, pv
            else:
                alpha = jnp.exp2(m - m_new)
                l, acc = alpha * l + lj, alpha * acc + pv
            m = m_new
        o_ref[0] = acc / l
        lse_ref[0] = m + jnp.log2(l)
        finish()

    any_spec = pl.BlockSpec(memory_space=pl.ANY)
    res = pl.pallas_call(
        body, name="attn_fwd", grid=(h, s // tq),
        in_specs=[pl.BlockSpec((1, tq, HEAD_PAD), lambda hh, i: (hh, i, 0)),
                  pl.BlockSpec((1, t, HEAD_PAD), lambda hh, i: (hh, 0, 0)),
                  pl.BlockSpec((1, t, VDIM), lambda hh, i: (hh, 0, 0))] + [any_spec] * n,
        out_specs=[pl.BlockSpec((1, tq, VDIM), lambda hh, i: (hh, i, 0)),
                   pl.BlockSpec((1, tq, 1), lambda hh, i: (hh, i, 0))] + [any_spec] * n,
        out_shape=[jax.ShapeDtypeStruct((h, s, VDIM), F32), jax.ShapeDtypeStruct((h, s, 1), F32)]
        + _exchange_shapes(ride_arrs, ride_modes),
        scratch_shapes=_exchange_sems(n),
        compiler_params=pltpu.CompilerParams(dimension_semantics=("arbitrary", "arbitrary"),
                                             vmem_limit_bytes=VMEM_LIMIT, has_side_effects=True),
    )(q, k, v, *ride_arrs)
    return res[0], res[1], res[2:]


TN_DIMS = (((0,), (0,)), ((), ()))


def _attn_bwd(q, k, v, do, lse_row, delta_row, ride_arrs, ride_modes):
    h, s, _ = q.shape
    t = k.shape[1]
    tk = _pick(t, (768,) + ATTN_TILES)
    cq = _pick(s, QUERY_CHUNKS)
    n = len(ride_arrs)

    def body(*refs):
        q_ref, k_ref, v_ref, do_ref, lse_ref, delta_ref = refs[:6]
        dq_ref, dk_ref, dv_ref = refs[6 + n:9 + n]
        hh, i = pl.program_id(0), pl.program_id(1)
        finish = _riding_exchange(refs, 6, 3, n, ride_modes, (hh == 0) & (i == 0),
                                  (hh == h - 1) & (i == t // tk - 1))

        @pl.when(i == 0)
        def _():
            dq_ref[...] = jnp.zeros_like(dq_ref)

        kt, vt = k_ref[0], v_ref[0]
        dk = dv = None
        for j in range(s // cq):
            rows = slice(j * cq, (j + 1) * cq)
            qj, doj = q_ref[0, rows, :], do_ref[0, rows, :]
            pt = jnp.exp2(lax.dot_general(kt, qj, NT_DIMS, preferred_element_type=F32) * SCALE2 - lse_ref[0, :, rows])
            dv_j = jnp.dot(pt.astype(BF16), doj, preferred_element_type=F32)
            dpt = lax.dot_general(vt, doj, NT_DIMS, preferred_element_type=F32)
            dst = (pt * (dpt - delta_ref[0, :, rows])).astype(BF16)
            dk_j = jnp.dot(dst, qj, preferred_element_type=F32)
            dq_ref[0, rows, :] += lax.dot_general(dst, kt, TN_DIMS, preferred_element_type=F32)
            dk, dv = (dk_j, dv_j) if j == 0 else (dk + dk_j, dv + dv_j)
        dk_ref[0] = dk * MLA_SCALE
        dv_ref[0] = dv
        finish()

    any_spec = pl.BlockSpec(memory_space=pl.ANY)
    res = pl.pallas_call(
        body, name="attn_bwd", grid=(h, t // tk),
        in_specs=[pl.BlockSpec((1, s, HEAD_PAD), lambda hh, i: (hh, 0, 0)),
                  pl.BlockSpec((1, tk, HEAD_PAD), lambda hh, i: (hh, i, 0)),
                  pl.BlockSpec((1, tk, VDIM), lambda hh, i: (hh, i, 0)),
                  pl.BlockSpec((1, s, VDIM), lambda hh, i: (hh, 0, 0)),
                  pl.BlockSpec((1, 1, s), lambda hh, i: (hh, 0, 0)),
                  pl.BlockSpec((1, 1, s), lambda hh, i: (hh, 0, 0))] + [any_spec] * n,
        out_specs=[pl.BlockSpec((1, s, HEAD_PAD), lambda hh, i: (hh, 0, 0)),
                   pl.BlockSpec((1, tk, HEAD_PAD), lambda hh, i: (hh, i, 0)),
                   pl.BlockSpec((1, tk, VDIM), lambda hh, i: (hh, i, 0))] + [any_spec] * n,
        out_shape=[jax.ShapeDtypeStruct((h, s, HEAD_PAD), F32), jax.ShapeDtypeStruct((h, t, HEAD_PAD), F32),
                   jax.ShapeDtypeStruct((h, t, VDIM), F32)] + _exchange_shapes(ride_arrs, ride_modes),
        scratch_shapes=_exchange_sems(n),
        compiler_params=pltpu.CompilerParams(dimension_semantics=("arbitrary", "arbitrary"),
                                             vmem_limit_bytes=VMEM_LIMIT, has_side_effects=True),
    )(q, k, v, do, lse_row, delta_row, *ride_arrs)
    return res[0], res[1], res[2], res[3:]


def _exchange_shapes(arrs, modes):
    return [jax.ShapeDtypeStruct((N_DEV,) + a.shape if md == 'ag' else a.shape, a.dtype) for a, md in zip(arrs, modes)]


def _exchange_sems(n):
    return [pltpu.SemaphoreType.DMA((n, N_DEV - 1)), pltpu.SemaphoreType.DMA((n, N_DEV - 1)),
            pltpu.SemaphoreType.DMA((n,))]


def _exchange_copies(ins, outs, modes, send_sems, recv_sems, local_sems):
    x, y, c = lax.axis_index("x"), lax.axis_index("y"), lax.axis_index("c")
    me = 4 * x + 2 * y + c
    copies = []
    for a in range(len(ins)):
        ag = modes[a] == 'ag'
        copies.append(pltpu.make_async_copy(ins[a] if ag else ins[a].at[me], outs[a].at[me], local_sems.at[a]))
        for k in range(1, N_DEV):
            px = 1 - x if k & 4 else x
            py = 1 - y if k & 2 else y
            pc = 1 - c if k & 1 else c
            src = ins[a] if ag else ins[a].at[4 * px + 2 * py + pc]
            copies.append(pltpu.make_async_remote_copy(
                src_ref=src, dst_ref=outs[a].at[me], send_sem=send_sems.at[a, k - 1],
                recv_sem=recv_sems.at[a, k - 1], device_id=(px, py, pc), device_id_type=MESH))
    return copies


def _exchange(name, arrs, modes):
    n = len(arrs)

    def body(*refs):
        copies = _exchange_copies(refs[:n], refs[n:2 * n], modes, *refs[2 * n:])
        for cp in copies:
            cp.start()
        for cp in copies:
            cp.wait()

    return pl.pallas_call(
        body, name=name,
        in_specs=[pl.BlockSpec(memory_space=pl.ANY)] * n,
        out_specs=[pl.BlockSpec(memory_space=pl.ANY)] * n,
        out_shape=_exchange_shapes(arrs, modes),
        scratch_shapes=_exchange_sems(n),
        compiler_params=pltpu.CompilerParams(has_side_effects=True),
    )(*arrs)


def _riding_exchange(refs, n_in, n_out, n, modes, first, last):
    ins = refs[n_in:n_in + n]
    outs = refs[n_in + n + n_out:n_in + 2 * n + n_out]
    sems = refs[n_in + 2 * n + n_out:n_in + 2 * n + n_out + 3]

    @pl.when(first)
    def _():
        for cp in _exchange_copies(ins, outs, modes, *sems):
            cp.start()

    def finish():
        @pl.when(last)
        def _():
            for cp in _exchange_copies(ins, outs, modes, *sems):
                cp.wait()

    return finish


def _adamw(name, w, m, v, gparts):
    r, c = w.shape
    npart = gparts.shape[0]
    tr = _pick(r, (256, 128, 64, 32, 16, 8))
    spec = pl.BlockSpec((tr, c), lambda i: (i, 0))

    def body(w_ref, m_ref, v_ref, g_ref, go_ref, d_ref, mo_ref, vo_ref):
        g = g_ref[0]
        for p in range(1, npart):
            g = g + g_ref[p]
        m1 = ADAM_B1 * m_ref[...] + (1.0 - ADAM_B1) * g
        v1 = ADAM_B2 * v_ref[...] + (1.0 - ADAM_B2) * (g * g)
        m_hat = m1 / (1.0 - ADAM_B1 ** ADAM_STEP)
        v_hat = v1 / (1.0 - ADAM_B2 ** ADAM_STEP)
        go_ref[...] = g
        d_ref[...] = -ADAM_LR * (m_hat / (jnp.sqrt(v_hat) + ADAM_EPS) + ADAM_WD * w_ref[...])
        mo_ref[...] = m1
        vo_ref[...] = v1

    return pl.pallas_call(
        body, name=name, grid=(r // tr,),
        in_specs=[spec, spec, spec, pl.BlockSpec((npart, tr, c), lambda i: (0, i, 0))],
        out_specs=[spec] * 4, out_shape=[jax.ShapeDtypeStruct((r, c), F32)] * 4,
        compiler_params=_params(("parallel",)),
    )(w, m, v, gparts)


def _pack(arrs, rows):
    flat = jnp.concatenate([a.reshape(-1) for a in arrs])
    return jnp.pad(flat, (0, rows * LANES - flat.shape[0])).reshape(rows, LANES)


def _unpack(packed, shapes):
    flat, out, off = packed.reshape(-1), [], 0
    for s in shapes:
        n = math.prod(s)
        out.append(flat[off:off + n].reshape(s))
        off += n
    return out


def _pack_rows(n_elems):
    return -(-n_elems // (SUBLANES * LANES)) * SUBLANES


def _cols_from_shards(g):
    return g.transpose(1, 0, 2).reshape(g.shape[1], N_DEV * g.shape[2])


def _cols_to_shards(w):
    r, c = w.shape
    return w.reshape(r, N_DEV, c // N_DEV).transpose(1, 0, 2)


def _rope_tables(n_lat, n_ctx):
    rows = n_lat // GRID_W
    inv = ROPE_BASE ** (-jnp.arange(ROPE_PAIRS, dtype=F32) / ROPE_PAIRS)
    ang_r = jnp.arange(rows, dtype=F32)[:, None] * inv
    ang_c = jnp.arange(GRID_W, dtype=F32)[:, None] * inv
    cr, sr = jnp.repeat(jnp.cos(ang_r), GRID_W, axis=0), jnp.repeat(jnp.sin(ang_r), GRID_W, axis=0)
    cc, sc = jnp.tile(jnp.cos(ang_c), (rows, 1)), jnp.tile(jnp.sin(ang_c), (rows, 1))
    one, zero = jnp.ones((n_lat, 1), F32), jnp.zeros((n_lat, 1), F32)
    z8 = jnp.zeros((n_lat, ROPE_PAIRS), F32)
    cos_t = jnp.concatenate([jnp.tile(one, (1, NOPE)), cr, cr, cc, cc, jnp.tile(one, (1, HEAD_PAD - QK))], 1)
    sin_up = jnp.concatenate([jnp.tile(zero, (1, NOPE)), -sr, z8, -sc, z8, jnp.tile(zero, (1, HEAD_PAD - QK))], 1)
    sin_dn = jnp.concatenate([jnp.tile(zero, (1, NOPE)), z8, sr, z8, sc, jnp.tile(zero, (1, HEAD_PAD - QK))], 1)
    if n_ctx:
        cos_t = jnp.concatenate([jnp.ones((n_ctx, HEAD_PAD), F32), cos_t])
        sin_up = jnp.concatenate([jnp.zeros((n_ctx, HEAD_PAD), F32), sin_up])
        sin_dn = jnp.concatenate([jnp.zeros((n_ctx, HEAD_PAD), F32), sin_dn])
    return cos_t, sin_up, sin_dn


def _rope(x, cos_t, sin_up, sin_dn):
    return x * cos_t + pltpu.roll(x, HEAD_PAD - ROPE_PAIRS, 1) * sin_up + pltpu.roll(x, ROPE_PAIRS, 1) * sin_dn


def _rope_t(dy, cos_t, sin_up, sin_dn):
    return (dy * cos_t + pltpu.roll(dy * sin_up, ROPE_PAIRS, 1)
            + pltpu.roll(dy * sin_dn, HEAD_PAD - ROPE_PAIRS, 1))


def _softplus(x):
    return jnp.maximum(x, 0.0) + jnp.log(1.0 + jnp.exp(-jnp.abs(x)))


def _gates(z, xcv, lam_sp):
    outs = []
    for d in range(2):
        r = jax.nn.sigmoid(z[:, (2 * d) * LRU_W:(2 * d + 1) * LRU_W])
        ig = jax.nn.sigmoid(z[:, (2 * d + 1) * LRU_W:(2 * d + 2) * LRU_W])
        log_a = -LRU_C * r * lam_sp[:, d * LRU_W:(d + 1) * LRU_W]
        a = jnp.exp(log_a)
        u = jnp.sqrt(-jnp.tanh(log_a) * (a * a + 1.0)) * (ig * xcv)
        outs += [a, u]
    return tuple(outs)


def kernel(x, c, ctx, c_ctx, w_mod, b_mod, g_pre_mix, g_post_mix, g_pre_ffn, g_post_ffn, w_in, lru_conv_w, lru_conv_b, lru_w_a, lru_b_a, lru_w_x, lru_b_x, lru_lambda, mla_g_q, mla_w_uq, mla_g_kv, mla_w_ukv, w_out, ffn_w_up, ffn_conv_w, ffn_conv_b, ffn_w_down, loss_target, m_c_ctx, m_w_mod, m_b_mod, m_g_pre_mix, m_g_post_mix, m_g_pre_ffn, m_g_post_ffn, m_w_in, m_lru_conv_w, m_lru_conv_b, m_lru_w_a, m_lru_b_a, m_lru_w_x, m_lru_b_x, m_lru_lambda, m_mla_g_q, m_mla_w_uq, m_mla_g_kv, m_mla_w_ukv, m_w_out, m_ffn_w_up, m_ffn_conv_w, m_ffn_conv_b, m_ffn_w_down, v_c_ctx, v_w_mod, v_b_mod, v_g_pre_mix, v_g_post_mix, v_g_pre_ffn, v_g_post_ffn, v_w_in, v_lru_conv_w, v_lru_conv_b, v_lru_w_a, v_lru_b_a, v_lru_w_x, v_lru_b_x, v_lru_lambda, v_mla_g_q, v_mla_w_uq, v_mla_g_kv, v_mla_w_ukv, v_w_out, v_ffn_w_up, v_ffn_conv_w, v_ffn_conv_b, v_ffn_w_down):
    W = dict(c_ctx=c_ctx, w_mod=w_mod, b_mod=b_mod, g_pre_mix=g_pre_mix, g_post_mix=g_post_mix, g_pre_ffn=g_pre_ffn,
             g_post_ffn=g_post_ffn, w_in=w_in, lru_conv_w=lru_conv_w, lru_conv_b=lru_conv_b, lru_w_a=lru_w_a,
             lru_b_a=lru_b_a, lru_w_x=lru_w_x, lru_b_x=lru_b_x, lru_lambda=lru_lambda, mla_g_q=mla_g_q,
             mla_w_uq=mla_w_uq, mla_g_kv=mla_g_kv, mla_w_ukv=mla_w_ukv, w_out=w_out, ffn_w_up=ffn_w_up,
             ffn_conv_w=ffn_conv_w, ffn_conv_b=ffn_conv_b, ffn_w_down=ffn_w_down)
    M = dict(c_ctx=m_c_ctx, w_mod=m_w_mod, b_mod=m_b_mod, g_pre_mix=m_g_pre_mix, g_post_mix=m_g_post_mix,
             g_pre_ffn=m_g_pre_ffn, g_post_ffn=m_g_post_ffn, w_in=m_w_in, lru_conv_w=m_lru_conv_w,
             lru_conv_b=m_lru_conv_b, lru_w_a=m_lru_w_a, lru_b_a=m_lru_b_a, lru_w_x=m_lru_w_x, lru_b_x=m_lru_b_x,
             lru_lambda=m_lru_lambda, mla_g_q=m_mla_g_q, mla_w_uq=m_mla_w_uq, mla_g_kv=m_mla_g_kv,
             mla_w_ukv=m_mla_w_ukv, w_out=m_w_out, ffn_w_up=m_ffn_w_up, ffn_conv_w=m_ffn_conv_w,
             ffn_conv_b=m_ffn_conv_b, ffn_w_down=m_ffn_w_down)
    V = dict(c_ctx=v_c_ctx, w_mod=v_w_mod, b_mod=v_b_mod, g_pre_mix=v_g_pre_mix, g_post_mix=v_g_post_mix,
             g_pre_ffn=v_g_pre_ffn, g_post_ffn=v_g_post_ffn, w_in=v_w_in, lru_conv_w=v_lru_conv_w,
             lru_conv_b=v_lru_conv_b, lru_w_a=v_lru_w_a, lru_b_a=v_lru_b_a, lru_w_x=v_lru_w_x, lru_b_x=v_lru_b_x,
             lru_lambda=v_lru_lambda, mla_g_q=v_mla_g_q, mla_w_uq=v_mla_w_uq, mla_g_kv=v_mla_g_kv,
             mla_w_ukv=v_mla_w_ukv, w_out=v_w_out, ffn_w_up=v_ffn_w_up, ffn_conv_w=v_ffn_conv_w,
             ffn_conv_b=v_ffn_conv_b, ffn_w_down=v_ffn_w_down)

    D = D_MODEL
    S, CN = x.shape[1], ctx.shape[1]
    T = S + CN
    TM = ROW_TILE
    ct, ns, nt = CN // TM, S // TM, T // TM
    me = 4 * lax.axis_index("x") + 2 * lax.axis_index("y") + lax.axis_index("c")

    lat = lambda i: i + ct
    swp = lambda i: jnp.where(i < ct, i + ns, i - ct)
    lat_or_0 = lambda i: jnp.maximum(i - ct, 0)

    small_shapes = [W[n].shape[1:] for n in SMALL_SHARDED] + [(D,)]
    n_small = sum(math.prod(s) for s in small_shapes)
    small_rows = _pack_rows(n_small)
    small_loc = _pack([W[n][0] for n in SMALL_SHARDED] + [c[0]], small_rows)
    early = ['w_in', 'mla_w_uq', 'mla_w_ukv']
    late = ['w_out', 'ffn_w_up', 'ffn_w_down']
    big = early + late
    gathered = _exchange("gather_weights", [W[n][0].astype(BF16) for n in early] + [small_loc], ['ag'] * 4)
    gw = dict(zip(early, gathered[:3]))
    small_all = [_unpack(gathered[3][d], small_shapes) for d in range(N_DEV)]
    full_small = {n: jnp.concatenate([small_all[d][j] for d in range(N_DEV)], axis=-1)
                  for j, n in enumerate(SMALL_SHARDED)}
    c_all = jnp.stack([small_all[d][-1] for d in range(N_DEV)])

    w_in_f = _cols_from_shards(gw['w_in'])
    w_in_p = jnp.concatenate([w_in_f[:, :OFF_KR], jnp.zeros((D, NOPE), BF16), w_in_f[:, OFF_KR:],
                              jnp.zeros((D, HEAD_PAD - QK), BF16)], axis=1)
    w_uq_f = _cols_from_shards(gw['mla_w_uq']).reshape(Q_RANK, HEADS, QK)
    wq_p = jnp.pad(w_uq_f, ((0, 0), (0, 0), (0, HEAD_PAD - QK))).reshape(Q_RANK, HEADS * HEAD_PAD)
    w_ukv_f = _cols_from_shards(gw['mla_w_ukv']).reshape(KV_RANK, HEADS, NOPE + VDIM)
    wk_p = jnp.pad(w_ukv_f[:, :, :NOPE], ((0, 0), (0, 0), (0, HEAD_PAD - NOPE))).reshape(KV_RANK, HEADS * HEAD_PAD)
    wv_f = w_ukv_f[:, :, NOPE:].reshape(KV_RANK, HEADS * VDIM)

    lru_cw, lru_ba, lru_bx, lru_lam, ffn_cw = [full_small[n] for n in SMALL_SHARDED]
    ffn_cw_t, ffn_cb_t = _ff_to_tiles(ffn_cw), _ff_to_tiles(ffn_conv_b)

    def block_diag(w):
        eye = jnp.eye(LRU_HEADS, dtype=w.dtype)
        return jnp.einsum('hij,hg->higj', w, eye).reshape(LRU_W, LRU_W)

    w_gate = jnp.concatenate([block_diag(lru_w_a[0, 0]), block_diag(lru_w_x[0, 0]),
                              block_diag(lru_w_a[0, 1]), block_diag(lru_w_x[0, 1])], axis=1).astype(BF16)
    b_gate = jnp.concatenate([lru_ba[0], lru_bx[0], lru_ba[1], lru_bx[1]])[None]
    lam_row = lru_lam.reshape(1, 2 * LRU_W)

    c16 = jnp.concatenate([c_all, c_ctx[None], jnp.zeros((2 * SUBLANES - N_DEV - 1, D), F32)])
    ncol = w_mod.shape[2]
    b_mod_loc = lax.dynamic_slice(b_mod, (0, me * ncol), (1, ncol))

    def mod_fwd(c16_r, w_r, b_r):
        c16_v = c16_r[...]
        sl = c16_v * jax.nn.sigmoid(c16_v)
        return (jnp.dot(sl.astype(BF16), w_r[...].astype(BF16), preferred_element_type=F32) + b_r[...],)

    (mod_part,) = _single("mod_fwd", mod_fwd, [c16, w_mod[0], b_mod_loc], [((2 * SUBLANES, ncol), F32)])
    (mod_g,) = _exchange("gather_mod", [mod_part], ['ag'])
    mod_all = _cols_from_shards(mod_g)
    mod_lat = lax.dynamic_slice(mod_all, (me, 0), (1, N_MOD * D)).reshape(N_MOD, D)
    mod_ctx = mod_all[N_DEV].reshape(N_MOD, D)

    xs, tgt = x[0], loss_target[0]
    xa_rows = [_row(ctx[0], lambda i: jnp.minimum(i, ct - 1)), _row(xs, lat_or_0)]

    def sel_mod(i, ml, mc, r0):
        sh = jnp.where(i < ct, mc[r0:r0 + 1, :], ml[r0:r0 + 1, :])
        sc = jnp.where(i < ct, mc[r0 + 1:r0 + 2, :], ml[r0 + 1:r0 + 2, :])
        return sh, sc

    def pre_fn(xv, g, sh, sc):
        return _rms(xv, g) * (1.0 + sc) + sh

    def k_pre(i, rv, bv):
        sh, sc = sel_mod(i, bv[1], bv[2], 0)
        return (pre_fn(jnp.where(i < ct, rv[0], rv[1]), bv[0][...], sh, sc),), ()

    (h_pre,) = _rowwise("pre_mix", k_pre, nt, xa_rows, [g_pre_mix, mod_lat, mod_ctx], [((T, D), BF16, None)])
    proj = _mm("in_proj", h_pre, w_in_p, 'nn')

    xcv = _dwconv_fwd("lru_conv", proj, lru_cw, lru_conv_b, LRU_CONV_LEFT, (0, ct))

    def k_gates(i, rv, bv):
        xv = rv[0]
        z = jnp.dot(xv.astype(BF16), bv[0][...], preferred_element_type=F32) + bv[1][...]
        return _gates(z, xv, _bc(_softplus(-bv[2][...]), TM)), ()

    a0, u0, a1, u1 = _rowwise("lru_gates", k_gates, nt, [_row(xcv)], [w_gate, b_gate, lam_row],
                              [((T, LRU_W), F32, None), ((T, LRU_W), F32, None),
                               ((T, LRU_W), F32, swp), ((T, LRU_W), F32, swp)])
    h0 = _scan("lru_scan_f", a0, u0, False)
    h1 = _scan("lru_scan_r", a1, u1, True)

    def k_rms(i, rv, bv):
        return (_rms(rv[0], bv[0][...]),), ()

    cqn = _rowwise("q_norm", k_rms, ns, [_row(proj, lat, (Q_RANK, OFF_CQ // Q_RANK))], [mla_g_q],
                   [((S, Q_RANK), BF16, None)])[0]
    q_lin = _mm("q_proj", cqn, wq_p, 'nn')
    cos_q, sup_q, sdn_q = _rope_tables(S, 0)
    cos_k, sup_k, sdn_k = _rope_tables(S, CN)

    def k_qrope(i, rv, bv):
        ql, ctb, sub, sdb = rv
        return (jnp.stack([_rope(ql[:, hh * HEAD_PAD:(hh + 1) * HEAD_PAD], ctb, sub, sdb) for hh in range(HEADS)]),), ()

    q = _rowwise("q_rope", k_qrope, ns, [_row(q_lin), _row(cos_q), _row(sup_q), _row(sdn_q)], [],
                 [((HEADS, S, HEAD_PAD), BF16, None)])[0]

    ckvn = _rowwise("kv_norm", k_rms, nt, [_row(proj, None, (KV_RANK, OFF_CKV // KV_RANK))], [mla_g_kv],
                    [((T, KV_RANK), BF16, None)])[0]
    k_lin = _mm("k_proj", ckvn, wk_p, 'nn')
    v_lin = _mm("v_proj", ckvn, wv_f, 'nn', out_dtype=BF16)

    def k_krope(i, rv, bv):
        kl, krp, ctb, sub, sdb = rv
        kr = _rope(krp, ctb, sub, sdb)
        return (jnp.stack([kl[:, hh * HEAD_PAD:(hh + 1) * HEAD_PAD] + kr for hh in range(HEADS)]),), ()

    k = _rowwise("k_rope", k_krope, nt,
                 [_row(k_lin), _row(proj, None, (HEAD_PAD, OFF_KR // HEAD_PAD)), _row(cos_k), _row(sup_k), _row(sdn_k)],
                 [], [((HEADS, T, HEAD_PAD), BF16, None)])[0]
    v = v_lin.reshape(T, HEADS, VDIM).transpose(1, 0, 2)

    o, lse, late_g = _attn_fwd(q, k, v, [W[n][0].astype(BF16) for n in late], ['ag'] * 3)
    w_out_f = late_g[0].reshape(D, D)
    w_up_t = _ff_permute("w_up_to_tiles", _cols_from_shards(late_g[1]), True)
    w_down_f = late_g[2].reshape(D_FF, D)
    o_t = o.transpose(1, 0, 2).reshape(S, HEADS * VDIM)

    def lru_out_fn(hf, hr, gr):
        return (hf + hr) * jax.nn.gelu(gr)

    def k_mix_in(i, rv, bv):
        return (jnp.concatenate([lru_out_fn(rv[0], rv[1], rv[2]), rv[3]], axis=1),), ()

    gr_row = _row(proj, lat, (LRU_W, OFF_GR // LRU_W))
    y_in = _rowwise("mix_in", k_mix_in, ns, [_row(h0, lat), _row(h1), gr_row, _row(o_t)], [],
                    [((S, D), BF16, None)])[0]
    y = _mm("out_proj", y_in, w_out_f, 'nn')

    def post_mix_fn(xv, yv, gt, g):
        return xv + gt * _rms(yv, g)

    def k_post_mix(i, rv, bv):
        ml = bv[0]
        x1v = post_mix_fn(rv[0], rv[1], ml[2:3, :], bv[1][...])
        return (x1v, pre_fn(x1v, bv[2][...], ml[3:4, :], ml[4:5, :])), ()

    x1, h2 = _rowwise("post_mix", k_post_mix, ns, [_row(xs), _row(y)], [mod_lat, g_post_mix, g_pre_ffn],
                      [((S, D), F32, None), ((S, D), BF16, None)])
    up = _mm("ffn_up", h2, w_up_t, 'nn', out_dtype=BF16)
    act = _ffn_mid_fwd(up, ffn_cw_t, ffn_cb_t)
    f = _mm("ffn_down", act, w_down_f, 'nn')

    def loss_fn(x1v, fv, gt, g, tg):
        x2 = x1v + gt * _rms(fv, g)
        err = x2 - tg
        return 0.5 * jnp.sum(jnp.mean(err * err, axis=-1))

    def k_loss(i, rv, bv):
        gtb, gb = _bc(bv[0][5:6, :], TM), _bc(bv[1][...], TM)
        val, (dx1v, dfv, dgt, dg) = jax.value_and_grad(loss_fn, argnums=(0, 1, 2, 3))(rv[0], rv[1], gtb, gb, rv[2])
        return (dx1v, dfv), (jnp.full((1, LANES), val, F32), _rs(dgt), _rs(dg))

    dx1_a, df, loss_acc, d_gt2, d_g_post_ffn = _rowwise(
        "loss_bwd", k_loss, ns, [_row(x1), _row(f), _row(tgt)], [mod_lat, g_post_ffn],
        [((S, D), F32, None), ((S, D), BF16, None)], [(1, LANES), (1, D), (1, D)])
    loss = lax.psum(loss_acc[0, 0], ("x", "y", "c"))

    d_act = _mm("ffn_down_dx", df, w_down_f, 'nt', out_dtype=BF16)
    d_w_down = _mm("ffn_down_dw", act, df, 'tn')
    d_up, d_ffn_cw_t, d_ffn_cb_t = _ffn_mid_bwd(up, d_act, ffn_cw_t, ffn_cb_t)
    d_ffn_cw, d_ffn_cb = _ff_from_tiles(d_ffn_cw_t), _ff_from_tiles(d_ffn_cb_t)
    d_h2 = _mm("ffn_up_dx", d_up, w_up_t, 'nt')
    d_w_up = _ff_permute("d_w_up_from_tiles", _mm("ffn_up_dw", h2, d_up, 'tn'), False)

    def k_pre_ffn_bwd(i, rv, bv):
        ml = bv[0]
        gb, shb, scb = _bc(bv[1][...], TM), _bc(ml[3:4, :], TM), _bc(ml[4:5, :], TM)
        _, pull = jax.vjp(pre_fn, rv[0], gb, shb, scb)
        dxv, dg, dsh, dsc = pull(rv[1])
        return (rv[2] + dxv,), (_rs(dg), _rs(dsh), _rs(dsc))

    dx1, d_g_pre_ffn, d_sh2, d_sc2 = _rowwise(
        "pre_ffn_bwd", k_pre_ffn_bwd, ns, [_row(x1), _row(d_h2), _row(dx1_a)], [mod_lat, g_pre_ffn],
        [((S, D), F32, None)], [(1, D), (1, D), (1, D)])

    def k_post_mix_bwd(i, rv, bv):
        gtb, gb = _bc(bv[0][2:3, :], TM), _bc(bv[1][...], TM)
        _, pull = jax.vjp(post_mix_fn, rv[0], rv[1], gtb, gb)
        _, dyv, dgt, dg = pull(rv[2])
        return (dyv,), (_rs(dgt), _rs(dg))

    dy, d_gt1, d_g_post_mix = _rowwise(
        "post_mix_bwd", k_post_mix_bwd, ns, [_row(xs), _row(y), _row(dx1)], [mod_lat, g_post_mix],
        [((S, D), BF16, None)], [(1, D), (1, D)])
    d_y_in = _mm("out_proj_dx", dy, w_out_f, 'nt')
    d_w_out = _mm("out_proj_dw", y_in, dy, 'tn')

    def k_lru_out_bwd(i, rv, bv):
        _, pull = jax.vjp(lru_out_fn, rv[0], rv[1], rv[2])
        dhf, _, dgr = pull(rv[3])
        return (dhf, dgr), ()

    d_hsum, d_gr = _rowwise("lru_out_bwd", k_lru_out_bwd, ns,
                            [_row(h0, lat), _row(h1), gr_row, _row(d_y_in, None, (LRU_W, 0))], [],
                            [((S, LRU_W), F32, None), ((S, LRU_W), F32, None)])

    do_h = d_y_in[:, LRU_W:].reshape(S, HEADS, VDIM).transpose(1, 0, 2)

    def k_delta(i, rv, bv):
        return (jnp.sum(rv[0] * rv[1], axis=-1, keepdims=True), rv[1]), ()

    delta, do_b = _rowwise("attn_delta", k_delta, ns, [_row(o), _row(do_h)], [],
                           [((HEADS, S, 1), F32, None), ((HEADS, S, VDIM), BF16, None)])
    late_sends = [d_w_out.reshape(N_DEV, D // N_DEV, D), _cols_to_shards(d_w_up),
                  d_w_down.reshape(N_DEV, D_FF // N_DEV, D)]
    dq, dk, dv, late_recv = _attn_bwd(q, k, v, do_b, lse.reshape(HEADS, 1, S), delta.reshape(HEADS, 1, S),
                                      late_sends, ['a2a'] * 3)

    def k_qrope_bwd(i, rv, bv):
        dqv, ctb, sub, sdb = rv
        return (jnp.concatenate([_rope_t(dqv[hh] * MLA_SCALE, ctb, sub, sdb) for hh in range(HEADS)], axis=1),), ()

    dq_lin = _rowwise("q_rope_bwd", k_qrope_bwd, ns, [_row(dq), _row(cos_q), _row(sup_q), _row(sdn_q)], [],
                      [((S, HEADS * HEAD_PAD), BF16, None)])[0]
    d_cqn = _mm("q_proj_dx", dq_lin, wq_p, 'nt')
    d_wq_p = _mm("q_proj_dw", cqn, dq_lin, 'tn')

    def k_rms_bwd(i, rv, bv):
        gb = _bc(bv[0][...], TM)
        _, pull = jax.vjp(_rms, rv[0], gb)
        dxv, dg = pull(rv[1])
        return (dxv,), (_rs(dg),)

    d_cq, d_g_q = _rowwise("q_norm_bwd", k_rms_bwd, ns,
                           [_row(proj, lat, (Q_RANK, OFF_CQ // Q_RANK)), _row(d_cqn)], [mla_g_q],
                           [((S, Q_RANK), F32, None)], [(1, Q_RANK)])

    def k_krope_bwd(i, rv, bv):
        dkv_, ctb, sub, sdb = rv
        tot = dkv_[0]
        for hh in range(1, HEADS):
            tot = tot + dkv_[hh]
        lane = lax.broadcasted_iota(jnp.int32, tot.shape, 1)
        tot = jnp.where((lane >= NOPE) & (lane < QK), tot, 0.0)
        return (jnp.concatenate([dkv_[hh] for hh in range(HEADS)], axis=1), _rope_t(tot, ctb, sub, sdb)), ()

    dk_lin, d_krp = _rowwise("k_rope_bwd", k_krope_bwd, nt, [_row(dk), _row(cos_k), _row(sup_k), _row(sdn_k)], [],
                             [((T, HEADS * HEAD_PAD), BF16, None), ((T, HEAD_PAD), F32, None)])
    dv_lin = dv.transpose(1, 0, 2).reshape(T, HEADS * VDIM).astype(BF16)
    d_ckvn_k = _mm("k_proj_dx", dk_lin, wk_p, 'nt')
    d_ckvn_v = _mm("v_proj_dx", dv_lin, wv_f, 'nt')
    d_wk_p = _mm("k_proj_dw", ckvn, dk_lin, 'tn')
    d_wv = _mm("v_proj_dw", ckvn, dv_lin, 'tn')

    def k_kvnorm_bwd(i, rv, bv):
        gb = _bc(bv[0][...], TM)
        _, pull = jax.vjp(_rms, rv[0], gb)
        dxv, dg = pull(rv[1] + rv[2])
        return (dxv,), (_rs(dg),)

    d_ckv, d_g_kv = _rowwise("kv_norm_bwd", k_kvnorm_bwd, nt,
                             [_row(proj, None, (KV_RANK, OFF_CKV // KV_RANK)), _row(d_ckvn_k), _row(d_ckvn_v)],
                             [mla_g_kv], [((T, KV_RANK), F32, None)], [(1, KV_RANK)])

    zero_row = jnp.zeros((1, LRU_W), F32)
    lam0 = _scan("lru_scan_f_bwd", a0, d_hsum, True, shifted=True, u_off=CN)
    lam1 = _scan("lru_scan_r_bwd", a1, d_hsum, False, shifted=True, u_off=0)
    hprev0 = jnp.concatenate([zero_row, h0[:-1]])
    hprev1 = jnp.concatenate([h1[1:], zero_row])

    def k_gates_bwd(i, rv, bv):
        xv, l0, hp0, l1, hp1 = rv
        wg, bg, lamv = [b[...] for b in bv]
        xb = xv.astype(BF16)
        z = jnp.dot(xb, wg, preferred_element_type=F32) + bg
        spb = _bc(_softplus(-lamv), TM)
        _, pull = jax.vjp(_gates, z, xv, spb)
        dz, dxv, dsp = pull((l0 * hp0, l0, l1 * hp1, l1))
        dzb = dz.astype(BF16)
        dxv = dxv + lax.dot_general(dzb, wg, NT_DIMS, preferred_element_type=F32)
        dwg = lax.dot_general(xb, dzb, (((0,), (0,)), ((), ())), preferred_element_type=F32)
        dlam = -_rs(dsp) * jax.nn.sigmoid(-lamv)
        return (dxv,), (dwg, _rs(dz), dlam)

    d_xcv, d_w_gate, d_b_gate, d_lam = _rowwise(
        "lru_gates_bwd", k_gates_bwd, nt,
        [_row(xcv), _row(lam0), _row(hprev0), _row(lam1, swp), _row(hprev1, swp)], [w_gate, b_gate, lam_row],
        [((T, LRU_W), F32, None)], [(LRU_W, 4 * LRU_W), (1, 4 * LRU_W), (1, 2 * LRU_W)])
    d_xr, d_lru_cw, d_lru_cb = _dwconv_bwd("lru_conv_bwd", proj, d_xcv, lru_cw, LRU_CONV_LEFT, (0, ct))

    def k_dproj(i, rv, bv):
        is_lat = i >= ct
        return (jnp.concatenate([rv[0], jnp.where(is_lat, rv[1], 0.0), jnp.where(is_lat, rv[2], 0.0), rv[3], rv[4]],
                                axis=1),), ()

    d_proj = _rowwise("d_proj", k_dproj, nt,
                      [_row(d_xr), _row(d_gr, lat_or_0), _row(d_cq, lat_or_0), _row(d_ckv), _row(d_krp)], [],
                      [((T, IN_W_PAD), BF16, None)])[0]
    d_h_pre = _mm("in_proj_dx", d_proj, w_in_p, 'nt')
    d_w_in_p = _mm("in_proj_dw", h_pre, d_proj, 'tn')

    def k_pre_bwd(i, rv, bv):
        g, ml, mc = bv[0][...], bv[1], bv[2]
        sh, sc = sel_mod(i, ml, mc, 0)
        _, pull = jax.vjp(pre_fn, jnp.where(i < ct, rv[0], rv[1]), _bc(g, TM), _bc(sh, TM), _bc(sc, TM))
        dxv, dg, dsh, dsc = pull(rv[2])
        is_lat = i >= ct
        dsh, dsc = _rs(dsh), _rs(dsc)
        zero = jnp.zeros_like(dsh)
        return ((dxv + rv[3],),
                (_rs(dg), jnp.where(is_lat, dsh, zero), jnp.where(is_lat, dsc, zero),
                 jnp.where(is_lat, zero, dsh), jnp.where(is_lat, zero, dsc)))

    dxl, d_g_pre_mix, d_sh1, d_sc1, d_csh1, d_csc1 = _rowwise(
        "pre_mix_bwd", k_pre_bwd, nt, xa_rows + [_row(d_h_pre), _row(dx1, lat_or_0)],
        [g_pre_mix, mod_lat, mod_ctx], [((S, D), F32, lat_or_0)], [(1, D)] * 5)
    grad_x = dxl[None]

    zrow = jnp.zeros((1, D), F32)
    d_mod_lat = jnp.concatenate([d_sh1, d_sc1, d_gt1, d_sh2, d_sc2, d_gt2], axis=1)
    d_mod_ctx = jnp.concatenate([d_csh1, d_csc1, zrow, zrow, zrow, zrow], axis=1)
    d_mod_mine = jnp.concatenate([d_mod_lat, d_mod_ctx, jnp.zeros((SUBLANES - 2, N_MOD * D), F32)])
    (d_mod_all,) = _exchange("gather_dmod", [d_mod_mine], ['ag'])
    dm_lat_loc = lax.dynamic_slice(d_mod_all[:, 0], (0, me * ncol), (N_DEV, ncol))
    dm_ctx_loc = lax.dynamic_slice(d_mod_all[:, 1], (0, me * ncol), (N_DEV, ncol))

    def mod_bwd(c16_r, w_r, dml_r, dmc_r):
        c16_v = c16_r[...]
        sig = jax.nn.sigmoid(c16_v)
        sl = c16_v * sig
        dctx = dmc_r[0:1, :]
        for d in range(1, N_DEV):
            dctx = dctx + dmc_r[d:d + 1, :]
        row = lax.broadcasted_iota(jnp.int32, (2 * SUBLANES, ncol), 0)
        dm16 = dml_r[...] + jnp.where(row == N_DEV, _bc(dctx, 2 * SUBLANES), 0.0)
        dw = lax.dot_general(sl.astype(BF16), dm16.astype(BF16), (((0,), (0,)), ((), ())), preferred_element_type=F32)
        dsl = lax.dot_general(dm16.astype(BF16), w_r[...].astype(BF16), NT_DIMS, preferred_element_type=F32)
        dc = dsl * (sig * (1.0 + c16_v * (1.0 - sig)))
        return dw, dc

    dm_lat16 = jnp.concatenate([dm_lat_loc, jnp.zeros((2 * SUBLANES - N_DEV, ncol), F32)])
    g_w_mod, dc16 = _single("mod_bwd", mod_bwd, [c16, w_mod[0], dm_lat16, dm_ctx_loc],
                            [((D, ncol), F32), ((2 * SUBLANES, D), F32)])
    d_c_ctx_part = dc16[N_DEV]

    def diag_blocks(dw):
        return jnp.stack([dw[hh * 64:(hh + 1) * 64, hh * 64:(hh + 1) * 64] for hh in range(LRU_HEADS)])

    d_lru_w_a = jnp.stack([diag_blocks(d_w_gate[:, 0:LRU_W]), diag_blocks(d_w_gate[:, 2 * LRU_W:3 * LRU_W])])[None]
    d_lru_w_x = jnp.stack([diag_blocks(d_w_gate[:, LRU_W:2 * LRU_W]), diag_blocks(d_w_gate[:, 3 * LRU_W:])])[None]
    d_b_a = jnp.stack([d_b_gate[0, 0:LRU_W], d_b_gate[0, 2 * LRU_W:3 * LRU_W]])
    d_b_x = jnp.stack([d_b_gate[0, LRU_W:2 * LRU_W], d_b_gate[0, 3 * LRU_W:]])

    rep_part = dict(c_ctx=d_c_ctx_part, b_mod=d_mod_lat + d_mod_ctx, g_pre_mix=d_g_pre_mix, g_post_mix=d_g_post_mix,
                    g_pre_ffn=d_g_pre_ffn, g_post_ffn=d_g_post_ffn, lru_conv_b=d_lru_cb, lru_w_a=d_lru_w_a,
                    lru_w_x=d_lru_w_x, mla_g_q=d_g_q, mla_g_kv=d_g_kv, ffn_conv_b=d_ffn_cb)
    rep_shapes = [W[n].shape for n in REPLICATED]
    rep_rows = -(-_pack_rows(sum(W[n].size for n in REPLICATED)) // ROW_TILE) * ROW_TILE
    rep_loc = _pack([rep_part[n] for n in REPLICATED], rep_rows)

    d_w_in = jnp.concatenate([d_w_in_p[:, :OFF_KR], d_w_in_p[:, OFF_KR + NOPE:OFF_KR + QK]], axis=1)
    d_w_uq = d_wq_p.reshape(Q_RANK, HEADS, HEAD_PAD)[:, :, :QK].reshape(Q_RANK, HEADS * QK)
    d_w_ukv = jnp.concatenate([d_wk_p.reshape(KV_RANK, HEADS, HEAD_PAD)[:, :, :NOPE],
                               d_wv.reshape(KV_RANK, HEADS, VDIM)], axis=2).reshape(KV_RANK, HEADS * (NOPE + VDIM))
    small_full = dict(lru_conv_w=d_lru_cw, lru_b_a=d_b_a, lru_b_x=d_b_x, lru_lambda=d_lam.reshape(2, LRU_W),
                      ffn_conv_w=d_ffn_cw)
    small_sh = jnp.concatenate([_cols_to_shards(small_full[n]).reshape(N_DEV, -1) for n in SMALL_SHARDED], axis=1)
    n_sh = small_sh.shape[1]
    sh_rows = _pack_rows(n_sh)
    small_sh = jnp.pad(small_sh, ((0, 0), (0, sh_rows * LANES - n_sh))).reshape(N_DEV, sh_rows, LANES)

    sends = [_cols_to_shards(d_w_in), _cols_to_shards(d_w_uq), _cols_to_shards(d_w_ukv), small_sh]
    recv = _exchange("exchange_grads", sends + [rep_loc], ['a2a'] * 4 + ['ag'])
    big_parts = list(recv[:3]) + list(late_recv)

    res = {}

    def adam(name, w2, m2, v2, parts):
        return _adamw("adamw_" + name, w2, m2, v2, parts)

    for n, parts in zip(big, big_parts):
        shp = W[n].shape
        outs = adam(n, W[n][0], M[n][0], V[n][0], parts)
        res[n] = [o_.reshape(shp) for o_ in outs]
    outs = adam('w_mod', w_mod[0], m_w_mod[0], v_w_mod[0], g_w_mod[None])
    res['w_mod'] = [o_.reshape(w_mod.shape) for o_ in outs]

    sh_shapes = [W[n].shape for n in SMALL_SHARDED]
    pk = lambda dct: _pack([dct[n] for n in SMALL_SHARDED], sh_rows)
    outs = adam('small_sharded', pk(W), pk(M), pk(V), recv[3])
    for n, vals in zip(SMALL_SHARDED, zip(*[_unpack(o_, sh_shapes) for o_ in outs])):
        res[n] = list(vals)

    pr = lambda dct: _pack([dct[n] for n in REPLICATED], rep_rows)
    outs = adam('replicated', pr(W), pr(M), pr(V), recv[4])
    for n, vals in zip(REPLICATED, zip(*[_unpack(o_, rep_shapes) for o_ in outs])):
        res[n] = list(vals)

    return (loss, grad_x, *[res[n][0] for n in WEIGHTS], *[res[n][1] for n in WEIGHTS],
            *[res[n][2] for n in WEIGHTS], *[res[n][3] for n in WEIGHTS])
```

```python
import functools
import math

import jax
import jax.numpy as jnp
from jax import lax
from jax.experimental import pallas as pl
from jax.experimental.pallas import tpu as pltpu

F32 = jnp.float32
BF16 = jnp.bfloat16
MESH = pl.DeviceIdType.MESH

N_DEV = 8
ROW_TILE = 256
SUBLANES = 8
LANES = 128
VMEM_LIMIT = 56 * 1024 * 1024

D_MODEL = 1024
LRU_W = 512
LRU_HEADS = 8
LRU_CONV_K = 4
LRU_CONV_LEFT = 2
LRU_C = 8.0
HEADS = 8
NOPE = 64
ROPE = 32
VDIM = 64
QK = NOPE + ROPE
HEAD_PAD = 128
Q_RANK = 256
KV_RANK = 128
MLA_SCALE = QK ** -0.5
ROPE_PAIRS = ROPE // 4
ROPE_BASE = 10000.0
GRID_W = 64
D_FF = 2816
FFN_CONV_K = 3
FFN_CONV_LEFT = 1
N_MOD = 6
EPS = 1e-6
IN_W = 2 * LRU_W + Q_RANK + KV_RANK + ROPE
IN_W_PAD = 2 * LRU_W + Q_RANK + KV_RANK + HEAD_PAD
OFF_GR, OFF_CQ, OFF_CKV, OFF_KR = LRU_W, 2 * LRU_W, 2 * LRU_W + Q_RANK, 2 * LRU_W + Q_RANK + KV_RANK

ADAM_LR, ADAM_B1, ADAM_B2, ADAM_EPS, ADAM_WD, ADAM_STEP = 0.001, 0.9, 0.999, 1e-08, 0.01, 10

WEIGHTS = ['c_ctx', 'w_mod', 'b_mod', 'g_pre_mix', 'g_post_mix', 'g_pre_ffn', 'g_post_ffn', 'w_in', 'lru_conv_w',
           'lru_conv_b', 'lru_w_a', 'lru_b_a', 'lru_w_x', 'lru_b_x', 'lru_lambda', 'mla_g_q', 'mla_w_uq', 'mla_g_kv',
           'mla_w_ukv', 'w_out', 'ffn_w_up', 'ffn_conv_w', 'ffn_conv_b', 'ffn_w_down']
REPLICATED = ['c_ctx', 'b_mod', 'g_pre_mix', 'g_post_mix', 'g_pre_ffn', 'g_post_ffn', 'lru_conv_b', 'lru_w_a',
              'lru_w_x', 'mla_g_q', 'mla_g_kv', 'ffn_conv_b']
SMALL_SHARDED = ['lru_conv_w', 'lru_b_a', 'lru_b_x', 'lru_lambda', 'ffn_conv_w']


def _pick(d, prefs):
    for p in prefs:
        if d % p == 0:
            return p
    return d


def _params(sem=None):
    return pltpu.CompilerParams(dimension_semantics=sem, vmem_limit_bytes=VMEM_LIMIT)


MM_TILES = (1024, 1408, 768, 512, 256, 128)


def _mm(name, a, b, mode, out_dtype=F32):
    if mode == 'nn':
        (m, k), (_, n) = a.shape, b.shape
    elif mode == 'nt':
        (m, k), (n, _) = a.shape, b.shape
    else:
        (k, m), (_, n) = a.shape, b.shape
    tm = _pick(m, MM_TILES)
    tn = _pick(n, MM_TILES)
    tk = _pick(k, MM_TILES)
    nk = k // tk
    if mode == 'nn':
        a_spec = pl.BlockSpec((tm, tk), lambda i, j, kk: (i, kk))
        b_spec = pl.BlockSpec((tk, tn), lambda i, j, kk: (kk, j))
        dn = (((1,), (0,)), ((), ()))
    elif mode == 'nt':
        a_spec = pl.BlockSpec((tm, tk), lambda i, j, kk: (i, kk))
        b_spec = pl.BlockSpec((tn, tk), lambda i, j, kk: (j, kk))
        dn = (((1,), (1,)), ((), ()))
    else:
        a_spec = pl.BlockSpec((tk, tm), lambda i, j, kk: (kk, i))
        b_spec = pl.BlockSpec((tk, tn), lambda i, j, kk: (kk, j))
        dn = (((0,), (0,)), ((), ()))

    def body(a_ref, b_ref, o_ref, acc_ref):
        kk = pl.program_id(2)

        @pl.when(kk == 0)
        def _():
            acc_ref[...] = jnp.zeros_like(acc_ref)

        acc_ref[...] += lax.dot_general(a_ref[...].astype(BF16), b_ref[...].astype(BF16), dn,
                                        preferred_element_type=F32)

        @pl.when(kk == nk - 1)
        def _():
            o_ref[...] = acc_ref[...].astype(o_ref.dtype)

    return pl.pallas_call(
        body, name=name, grid=(m // tm, n // tn, nk),
        in_specs=[a_spec, b_spec], out_specs=pl.BlockSpec((tm, tn), lambda i, j, kk: (i, j)),
        out_shape=jax.ShapeDtypeStruct((m, n), out_dtype),
        scratch_shapes=[pltpu.VMEM((tm, tn), F32)],
        compiler_params=_params(("parallel", "parallel", "arbitrary")),
    )(a, b)


def _row(a, idx=None, col=None):
    return dict(a=a, idx=idx, col=col)


def _rowwise(name, fn, n_tiles, rows, bcast, out_rows, out_acc=(), tm=ROW_TILE):
    in_specs = []
    for r in rows:
        a, idx, col = r['a'], r['idx'] or (lambda i: i), r['col']
        if a.ndim == 2:
            w, ci = col if col else (a.shape[1], 0)
            in_specs.append(pl.BlockSpec((tm, w), lambda i, idx=idx, ci=ci: (idx(i), ci)))
        else:
            in_specs.append(pl.BlockSpec((a.shape[0], tm, a.shape[2]), lambda i, idx=idx: (0, idx(i), 0)))
    for b in bcast:
        in_specs.append(pl.BlockSpec(b.shape, lambda i, nd=b.ndim: (0,) * nd))
    out_specs, out_shape = [], []
    for shape, dtype, idx in out_rows:
        idx = idx or (lambda i: i)
        if len(shape) == 2:
            out_specs.append(pl.BlockSpec((tm, shape[1]), lambda i, idx=idx: (idx(i), 0)))
        else:
            out_specs.append(pl.BlockSpec((shape[0], tm, shape[2]), lambda i, idx=idx: (0, idx(i), 0)))
        out_shape.append(jax.ShapeDtypeStruct(shape, dtype))
    for shape in out_acc:
        out_specs.append(pl.BlockSpec(shape, lambda i, nd=len(shape): (0,) * nd))
        out_shape.append(jax.ShapeDtypeStruct(shape, F32))
    nr, nb, no = len(rows), len(bcast), len(out_rows)

    def body(*refs):
        i = pl.program_id(0)
        rvals = [r[...] for r in refs[:nr]]
        bvals = list(refs[nr:nr + nb])
        o_rows, o_acc = fn(i, rvals, bvals)
        for ref, v in zip(refs[nr + nb:nr + nb + no], o_rows):
            ref[...] = v.astype(ref.dtype)
        acc_refs = refs[nr + nb + no:]
        if acc_refs:
            @pl.when(i == 0)
            def _():
                for ref in acc_refs:
                    ref[...] = jnp.zeros_like(ref)
            for ref, v in zip(acc_refs, o_acc):
                ref[...] += v

    return pl.pallas_call(
        body, name=name, grid=(n_tiles,), in_specs=in_specs, out_specs=out_specs, out_shape=out_shape,
        compiler_params=_params(("arbitrary",)),
    )(*[r['a'] for r in rows], *bcast)


def _single(name, fn, ins, out_shapes):
    def body(*refs):
        outs = fn(*refs[:len(ins)])
        for ref, v in zip(refs[len(ins):], outs):
            ref[...] = v.astype(ref.dtype)

    return pl.pallas_call(
        body, name=name,
        in_specs=[pl.BlockSpec(memory_space=pltpu.VMEM)] * len(ins),
        out_specs=[pl.BlockSpec(memory_space=pltpu.VMEM)] * len(out_shapes),
        out_shape=[jax.ShapeDtypeStruct(s, d) for s, d in out_shapes],
        compiler_params=_params(),
    )(*ins)


def _bc(p, n):
    return jnp.broadcast_to(p, (n, p.shape[-1]))


def _rs(g):
    return jnp.sum(g, axis=0, keepdims=True)


def _rms(x, g):
    return x * lax.rsqrt(jnp.mean(x * x, axis=-1, keepdims=True) + EPS) * g


def _conv_specs(r, cw, tm, halo=SUBLANES):
    th = tm // halo
    last = r // halo - 1
    prev = pl.BlockSpec((halo, cw), lambda c, i: (jnp.maximum(i * th - 1, 0), c))
    cur = pl.BlockSpec((tm, cw), lambda c, i: (i, c))
    nxt = pl.BlockSpec((halo, cw), lambda c, i: (jnp.minimum((i + 1) * th, last), c))
    return [prev, cur, nxt]


def _fill_ext(ext_ref, prev_ref, cur_ref, next_ref, i, n_tiles, seg_starts, tm):
    prev_ok = functools.reduce(jnp.logical_and, [i != s for s in seg_starts])
    next_ok = functools.reduce(jnp.logical_and, [i + 1 != s for s in seg_starts] + [i + 1 < n_tiles])
    ext_ref[0:SUBLANES, :] = jnp.where(prev_ok, prev_ref[...].astype(F32), 0.0)
    ext_ref[SUBLANES:SUBLANES + tm, :] = cur_ref[...].astype(F32)
    ext_ref[SUBLANES + tm:, :] = jnp.where(next_ok, next_ref[...].astype(F32), 0.0)


def _dwconv_fwd(name, x, w, b, left, seg_starts, cw=512, tm=ROW_TILE):
    r, c = x.shape[0], w.shape[1]
    kw = w.shape[0]
    n_tiles = r // tm

    def body(prev_ref, cur_ref, next_ref, w_ref, b_ref, o_ref, ext_ref):
        i = pl.program_id(1)
        _fill_ext(ext_ref, prev_ref, cur_ref, next_ref, i, n_tiles, seg_starts, tm)
        out = jnp.broadcast_to(b_ref[...], (tm, cw))
        for k in range(kw):
            out = out + ext_ref[pl.ds(SUBLANES + k - left, tm), :] * w_ref[k:k + 1, :]
        o_ref[...] = out

    return pl.pallas_call(
        body, name=name, grid=(c // cw, n_tiles),
        in_specs=_conv_specs(r, cw, tm) + [pl.BlockSpec((kw, cw), lambda c_, i: (0, c_)),
                                           pl.BlockSpec((1, cw), lambda c_, i: (0, c_))],
        out_specs=pl.BlockSpec((tm, cw), lambda c_, i: (i, c_)),
        out_shape=jax.ShapeDtypeStruct((r, c), F32),
        scratch_shapes=[pltpu.VMEM((tm + 2 * SUBLANES, cw), F32)],
        compiler_params=_params(("parallel", "arbitrary")),
    )(x, x, x, w, b)


def _dwconv_bwd(name, x, dy, w, left, seg_starts, out_dtype=F32, cw=512, tm=ROW_TILE):
    r, c = dy.shape
    kw = w.shape[0]
    n_tiles = r // tm

    def body(xp, xc, xn, dp, dc, dn, w_ref, dx_ref, dw_ref, db_ref, xe_ref, de_ref):
        i = pl.program_id(1)
        _fill_ext(xe_ref, xp, xc, xn, i, n_tiles, seg_starts, tm)
        _fill_ext(de_ref, dp, dc, dn, i, n_tiles, seg_starts, tm)
        dyc = dc[...].astype(F32)
        dx = jnp.zeros((tm, cw), F32)
        dws = []
        for k in range(kw):
            dx = dx + de_ref[pl.ds(SUBLANES - k + left, tm), :] * w_ref[k:k + 1, :]
            dws.append(jnp.sum(dyc * xe_ref[pl.ds(SUBLANES + k - left, tm), :], axis=0, keepdims=True))
        dx_ref[...] = dx.astype(dx_ref.dtype)

        @pl.when(i == 0)
        def _():
            dw_ref[...] = jnp.zeros_like(dw_ref)
            db_ref[...] = jnp.zeros_like(db_ref)

        for k in range(kw):
            dw_ref[k:k + 1, :] += dws[k]
        db_ref[...] += jnp.sum(dyc, axis=0, keepdims=True)

    return pl.pallas_call(
        body, name=name, grid=(c // cw, n_tiles),
        in_specs=_conv_specs(r, cw, tm) + _conv_specs(r, cw, tm) + [pl.BlockSpec((kw, cw), lambda c_, i: (0, c_))],
        out_specs=[pl.BlockSpec((tm, cw), lambda c_, i: (i, c_)),
                   pl.BlockSpec((kw, cw), lambda c_, i: (0, c_)),
                   pl.BlockSpec((1, cw), lambda c_, i: (0, c_))],
        out_shape=[jax.ShapeDtypeStruct((r, c), out_dtype), jax.ShapeDtypeStruct((kw, c), F32),
                   jax.ShapeDtypeStruct((1, c), F32)],
        scratch_shapes=[pltpu.VMEM((tm + 2 * SUBLANES, cw), F32), pltpu.VMEM((tm + 2 * SUBLANES, cw), F32)],
        compiler_params=_params(("parallel", "arbitrary")),
    )(x, x, x, dy, dy, dy, w)


FF_TILE = 256
FF_HALO = 16
FF_STRIP = 32


def _ffn_fill(ext_ref, prev_ref, cur_ref, next_ref, i, n_tiles, tm):
    ext_ref[0:FF_HALO, :] = jnp.where(i > 0, prev_ref[...].astype(F32), 0.0)
    ext_ref[FF_HALO:FF_HALO + tm, :] = cur_ref[...].astype(F32)
    ext_ref[FF_HALO + tm:, :] = jnp.where(i + 1 < n_tiles, next_ref[...].astype(F32), 0.0)


def _ffn_conv(ext_ref, w_ref, b_ref, start, rows):
    out = jnp.broadcast_to(b_ref[...], (rows, 2 * FF_TILE))
    for k in range(FFN_CONV_K):
        out = out + ext_ref[pl.ds(start + k - FFN_CONV_LEFT, rows), :] * w_ref[k:k + 1, :]
    return out


def _ffn_mid_fwd(up, w, b):
    s, c2 = up.shape
    tm = _pick(s, (2 * ROW_TILE, ROW_TILE))
    n_tiles = s // tm
    cw = 2 * FF_TILE

    def body(prev_ref, cur_ref, next_ref, w_ref, b_ref, o_ref, ext_ref):
        i = pl.program_id(1)
        _ffn_fill(ext_ref, prev_ref, cur_ref, next_ref, i, n_tiles, tm)
        for r0 in range(0, tm, FF_STRIP):
            upc = _ffn_conv(ext_ref, w_ref, b_ref, FF_HALO + r0, FF_STRIP)
            uv, gv = upc[:, :FF_TILE], upc[:, FF_TILE:]
            o_ref[r0:r0 + FF_STRIP, :] = (gv * jax.nn.sigmoid(gv) * uv).astype(o_ref.dtype)

    return pl.pallas_call(
        body, name="ffn_mid", grid=(c2 // cw, n_tiles),
        in_specs=_conv_specs(s, cw, tm, FF_HALO) + [pl.BlockSpec((FFN_CONV_K, cw), lambda c_, i: (0, c_)),
                                                   pl.BlockSpec((1, cw), lambda c_, i: (0, c_))],
        out_specs=pl.BlockSpec((tm, FF_TILE), lambda c_, i: (i, c_)),
        out_shape=jax.ShapeDtypeStruct((s, c2 // 2), BF16),
        scratch_shapes=[pltpu.VMEM((tm + 2 * FF_HALO, cw), F32)],
        compiler_params=_params(("parallel", "arbitrary")),
    )(up, up, up, w, b)


def _ffn_mid_bwd(up, d_act, w, b, tm=ROW_TILE):
    s, c2 = up.shape
    n_tiles = s // tm
    cw = 2 * FF_TILE
    h8 = SUBLANES

    def gate_bwd(upc, dact):
        uv, gv = upc[:, :FF_TILE], upc[:, FF_TILE:]
        sg = jax.nn.sigmoid(gv)
        return jnp.concatenate([dact * (gv * sg), dact * uv * (sg * (1.0 + gv * (1.0 - sg)))], axis=1)

    def body(up_p, up_c, up_n, da_p, da_c, da_n, w_ref, b_ref, dup_ref, dw_ref, db_ref, ext_ref, dext_ref):
        i = pl.program_id(1)
        _ffn_fill(ext_ref, up_p, up_c, up_n, i, n_tiles, tm)
        dws = [jnp.zeros((1, cw), F32) for _ in range(FFN_CONV_K)]
        dbs = jnp.zeros((1, cw), F32)
        for r0 in range(0, tm, FF_STRIP):
            d_c = gate_bwd(_ffn_conv(ext_ref, w_ref, b_ref, FF_HALO + r0, FF_STRIP),
                           da_c[r0:r0 + FF_STRIP, :].astype(F32))
            dext_ref[FF_HALO + r0:FF_HALO + r0 + FF_STRIP, :] = d_c
            for k in range(FFN_CONV_K):
                xk = ext_ref[pl.ds(FF_HALO + r0 + k - FFN_CONV_LEFT, FF_STRIP), :]
                dws[k] = dws[k] + jnp.sum(d_c * xk, axis=0, keepdims=True)
            dbs = dbs + jnp.sum(d_c, axis=0, keepdims=True)
        da_prev = jnp.where(i > 0, da_p[...].astype(F32)[FF_HALO - h8:, :], 0.0)
        da_next = jnp.where(i + 1 < n_tiles, da_n[...].astype(F32)[:h8, :], 0.0)
        dext_ref[FF_HALO - h8:FF_HALO, :] = gate_bwd(_ffn_conv(ext_ref, w_ref, b_ref, FF_HALO - h8, h8), da_prev)
        dext_ref[FF_HALO + tm:FF_HALO + tm + h8, :] = gate_bwd(_ffn_conv(ext_ref, w_ref, b_ref, FF_HALO + tm, h8), da_next)
        for r0 in range(0, tm, FF_STRIP):
            dup = jnp.zeros((FF_STRIP, cw), F32)
            for k in range(FFN_CONV_K):
                dup = dup + dext_ref[pl.ds(FF_HALO + r0 - k + FFN_CONV_LEFT, FF_STRIP), :] * w_ref[k:k + 1, :]
            dup_ref[r0:r0 + FF_STRIP, :] = dup.astype(dup_ref.dtype)

        @pl.when(i == 0)
        def _():
            dw_ref[...] = jnp.zeros_like(dw_ref)
            db_ref[...] = jnp.zeros_like(db_ref)

        for k in range(FFN_CONV_K):
            dw_ref[k:k + 1, :] += dws[k]
        db_ref[...] += dbs

    def half_specs():
        th = tm // FF_HALO
        last = s // FF_HALO - 1
        return [pl.BlockSpec((FF_HALO, FF_TILE), lambda c_, i: (jnp.maximum(i * th - 1, 0), c_)),
                pl.BlockSpec((tm, FF_TILE), lambda c_, i: (i, c_)),
                pl.BlockSpec((FF_HALO, FF_TILE), lambda c_, i: (jnp.minimum((i + 1) * th, last), c_))]

    return pl.pallas_call(
        body, name="ffn_mid_bwd", grid=(c2 // cw, n_tiles),
        in_specs=_conv_specs(s, cw, tm, FF_HALO) + half_specs() + [
            pl.BlockSpec((FFN_CONV_K, cw), lambda c_, i: (0, c_)), pl.BlockSpec((1, cw), lambda c_, i: (0, c_))],
        out_specs=[pl.BlockSpec((tm, cw), lambda c_, i: (i, c_)),
                   pl.BlockSpec((FFN_CONV_K, cw), lambda c_, i: (0, c_)),
                   pl.BlockSpec((1, cw), lambda c_, i: (0, c_))],
        out_shape=[jax.ShapeDtypeStruct((s, c2), BF16), jax.ShapeDtypeStruct((FFN_CONV_K, c2), F32),
                   jax.ShapeDtypeStruct((1, c2), F32)],
        scratch_shapes=[pltpu.VMEM((tm + 2 * FF_HALO, cw), F32), pltpu.VMEM((tm + 2 * FF_HALO, cw), F32)],
        compiler_params=_params(("parallel", "arbitrary")),
    )(up, up, up, d_act, d_act, d_act, w, b)


def _ff_permute(name, w, to_tiles):
    r = w.shape[0]
    nb = D_FF // FF_TILE
    natural = pl.BlockSpec((r, FF_TILE), lambda j, half: (0, half * nb + j))
    tiled = pl.BlockSpec((r, FF_TILE), lambda j, half: (0, 2 * j + half))

    def body(x_ref, o_ref):
        o_ref[...] = x_ref[...]

    return pl.pallas_call(
        body, name=name, grid=(nb, 2),
        in_specs=[natural if to_tiles else tiled], out_specs=tiled if to_tiles else natural,
        out_shape=jax.ShapeDtypeStruct(w.shape, w.dtype), compiler_params=_params(("parallel", "parallel")),
    )(w)


def _ff_to_tiles(w):
    r = w.shape[0]
    return w.reshape(r, 2, D_FF // FF_TILE, FF_TILE).transpose(0, 2, 1, 3).reshape(r, 2 * D_FF)


def _ff_from_tiles(w):
    r = w.shape[0]
    return w.reshape(r, D_FF // FF_TILE, 2, FF_TILE).transpose(0, 2, 1, 3).reshape(r, 2 * D_FF)


SCAN_UNROLL = 8


def _scan(name, a, u, reverse, shifted=False, u_off=0):
    t, c = a.shape
    us = u.shape[0]
    n8 = t // SUBLANES
    lo, hi = 0, SUBLANES - 1

    def body(a_ref, u_ref, h_ref):
        row = lax.broadcasted_iota(jnp.int32, (SUBLANES, LANES), 0)
        last = lo if reverse else hi

        def tile(ref, base):
            return ref[pl.ds(pl.multiple_of(base, SUBLANES), SUBLANES), :]

        def local(blk):
            base = blk * SUBLANES
            av = tile(a_ref, base)
            if shifted and reverse:
                nb = tile(a_ref, jnp.minimum(base + SUBLANES, t - SUBLANES))
                edge = jnp.where(base + SUBLANES >= t, 1.0, pltpu.roll(nb, hi, 0))
                av = jnp.where(row < hi, pltpu.roll(av, hi, 0), edge)
            elif shifted:
                pb = tile(a_ref, jnp.maximum(base - SUBLANES, 0))
                edge = jnp.where(base == 0, 1.0, pltpu.roll(pb, 1, 0))
                av = jnp.where(row >= 1, pltpu.roll(av, 1, 0), edge)
            ub = base - u_off
            hv = jnp.where((ub >= 0) & (ub < us), tile(u_ref, jnp.clip(ub, 0, us - SUBLANES)), 0.0)
            for s in (1, 2, 4):
                shift = SUBLANES - s if reverse else s
                ok = (row < SUBLANES - s) if reverse else (row >= s)
                a_sh = jnp.where(ok, pltpu.roll(av, shift, 0), 1.0)
                h_sh = jnp.where(ok, pltpu.roll(hv, shift, 0), 0.0)
                hv = av * h_sh + hv
                av = av * a_sh
            a_last = jnp.sum(jnp.where(row == last, av, 0.0), axis=0, keepdims=True)
            h_last = jnp.sum(jnp.where(row == last, hv, 0.0), axis=0, keepdims=True)
            return base, av, hv, a_last, h_last

        def step(j, carry):
            parts = []
            for k in range(SCAN_UNROLL):
                idx = j * SCAN_UNROLL + k
                parts.append(local((n8 - 1 - idx) if reverse else idx))
            for base, av, hv, a_last, h_last in parts:
                h_ref[pl.ds(pl.multiple_of(base, SUBLANES), SUBLANES), :] = av * carry + hv
                carry = a_last * carry + h_last
            return carry

        lax.fori_loop(0, n8 // SCAN_UNROLL, step, jnp.zeros((1, LANES), F32))

    return pl.pallas_call(
        body, name=name, grid=(c // LANES,),
        in_specs=[pl.BlockSpec((t, LANES), lambda j: (0, j)), pl.BlockSpec((us, LANES), lambda j: (0, j))],
        out_specs=pl.BlockSpec((t, LANES), lambda j: (0, j)),
        out_shape=jax.ShapeDtypeStruct((t, c), F32),
        compiler_params=_params(("parallel",)),
    )(a, u)


NT_DIMS = (((1,), (1,)), ((), ()))


LOG2E = 1.4426950408889634
SCALE2 = MLA_SCALE * LOG2E
ATTN_TILES = (512, 256, 128)
KEY_CHUNKS = (768, 512, 256, 128)
QUERY_CHUNKS = (1024, 512, 256, 128)


def _attn_fwd(q, k, v, ride_arrs, ride_modes):
    h, s, _ = q.shape
    t = k.shape[1]
    tq = _pick(s, ATTN_TILES)
    ck = _pick(t, KEY_CHUNKS)
    n = len(ride_arrs)

    def body(*refs):
        q_ref, k_ref, v_ref = refs[:3]
        o_ref, lse_ref = refs[3 + n:5 + n]
        hh, i = pl.program_id(0), pl.program_id(1)
        finish = _riding_exchange(refs, 3, 2, n, ride_modes, (hh == 0) & (i == 0),
                                  (hh == h - 1) & (i == s // tq - 1))
        qv = q_ref[0]
        m = l = acc = None
        for j in range(t // ck):
            kj, vj = k_ref[0, j * ck:(j + 1) * ck, :], v_ref[0, j * ck:(j + 1) * ck, :]
            s2 = lax.dot_general(qv, kj, NT_DIMS, preferred_element_type=F32) * SCALE2
            mj = jnp.max(s2, axis=-1, keepdims=True)
            m_new = mj if j == 0 else jnp.maximum(m, mj)
            p = jnp.exp2(s2 - m_new)
            lj = jnp.sum(p, axis=-1, keepdims=True)
            pv = jnp.dot(p.astype(BF16), vj, preferred_element_type=F32)
            if j == 0:
                l, acc = lj, pv
            else:
                alpha = jnp.exp2(m - m_new)
                l, acc = alpha * l + lj, alpha * acc + pv
            m = m_new
        o_ref[0] = acc / l
        lse_ref[0] = m + jnp.log2(l)
        finish()

    any_spec = pl.BlockSpec(memory_space=pl.ANY)
    res = pl.pallas_call(
        body, name="attn_fwd", grid=(h, s // tq),
        in_specs=[pl.BlockSpec((1, tq, HEAD_PAD), lambda hh, i: (hh, i, 0)),
                  pl.BlockSpec((1, t, HEAD_PAD), lambda hh, i: (hh, 0, 0)),
                  pl.BlockSpec((1, t, VDIM), lambda hh, i: (hh, 0, 0))] + [any_spec] * n,
        out_specs=[pl.BlockSpec((1, tq, VDIM), lambda hh, i: (hh, i, 0)),
                   pl.BlockSpec((1, tq, 1), lambda hh, i: (hh, i, 0))] + [any_spec] * n,
        out_shape=[jax.ShapeDtypeStruct((h, s, VDIM), F32), jax.ShapeDtypeStruct((h, s, 1), F32)]
        + _exchange_shapes(ride_arrs, ride_modes),
        scratch_shapes=_exchange_sems(n),
        compiler_params=pltpu.CompilerParams(dimension_semantics=("arbitrary", "arbitrary"),
                                             vmem_limit_bytes=VMEM_LIMIT, has_side_effects=True),
    )(q, k, v, *ride_arrs)
    return res[0], res[1], res[2:]


TN_DIMS = (((0,), (0,)), ((), ()))


def _attn_bwd(q, k, v, do, lse_row, delta_row, ride_arrs, ride_modes):
    h, s, _ = q.shape
    t = k.shape[1]
    tk = _pick(t, (768,) + ATTN_TILES)
    cq = _pick(s, QUERY_CHUNKS)
    n = len(ride_arrs)

    def body(*refs):
        q_ref, k_ref, v_ref, do_ref, lse_ref, delta_ref = refs[:6]
        dq_ref, dk_ref, dv_ref = refs[6 + n:9 + n]
        hh, i = pl.program_id(0), pl.program_id(1)
        finish = _riding_exchange(refs, 6, 3, n, ride_modes, (hh == 0) & (i == 0),
                                  (hh == h - 1) & (i == t // tk - 1))

        @pl.when(i == 0)
        def _():
            dq_ref[...] = jnp.zeros_like(dq_ref)

        kt, vt = k_ref[0], v_ref[0]
        dk = dv = None
        for j in range(s // cq):
            rows = slice(j * cq, (j + 1) * cq)
            qj, doj = q_ref[0, rows, :], do_ref[0, rows, :]
            pt = jnp.exp2(lax.dot_general(kt, qj, NT_DIMS, preferred_element_type=F32) * SCALE2 - lse_ref[0, :, rows])
            dv_j = jnp.dot(pt.astype(BF16), doj, preferred_element_type=F32)
            dpt = lax.dot_general(vt, doj, NT_DIMS, preferred_element_type=F32)
            dst = (pt * (dpt - delta_ref[0, :, rows])).astype(BF16)
            dk_j = jnp.dot(dst, qj, preferred_element_type=F32)
            dq_ref[0, rows, :] += lax.dot_general(dst, kt, TN_DIMS, preferred_element_type=F32)
            dk, dv = (dk_j, dv_j) if j == 0 else (dk + dk_j, dv + dv_j)
        dk_ref[0] = dk * MLA_SCALE
        dv_ref[0] = dv
        finish()

    any_spec = pl.BlockSpec(memory_space=pl.ANY)
    res = pl.pallas_call(
        body, name="attn_bwd", grid=(h, t // tk),
        in_specs=[pl.BlockSpec((1, s, HEAD_PAD), lambda hh, i: (hh, 0, 0)),
                  pl.BlockSpec((1, tk, HEAD_PAD), lambda hh, i: (hh, i, 0)),
                  pl.BlockSpec((1, tk, VDIM), lambda hh, i: (hh, i, 0)),
                  pl.BlockSpec((1, s, VDIM), lambda hh, i: (hh, 0, 0)),
                  pl.BlockSpec((1, 1, s), lambda hh, i: (hh, 0, 0)),
                  pl.BlockSpec((1, 1, s), lambda hh, i: (hh, 0, 0))] + [any_spec] * n,
        out_specs=[pl.BlockSpec((1, s, HEAD_PAD), lambda hh, i: (hh, 0, 0)),
                   pl.BlockSpec((1, tk, HEAD_PAD), lambda hh, i: (hh, i, 0)),
                   pl.BlockSpec((1, tk, VDIM), lambda hh, i: (hh, i, 0))] + [any_spec] * n,
        out_shape=[jax.ShapeDtypeStruct((h, s, HEAD_PAD), F32), jax.ShapeDtypeStruct((h, t, HEAD_PAD), F32),
                   jax.ShapeDtypeStruct((h, t, VDIM), F32)] + _exchange_shapes(ride_arrs, ride_modes),
        scratch_shapes=_exchange_sems(n),
        compiler_params=pltpu.CompilerParams(dimension_semantics=("arbitrary", "arbitrary"),
                                             vmem_limit_bytes=VMEM_LIMIT, has_side_effects=True),
    )(q, k, v, do, lse_row, delta_row, *ride_arrs)
    return res[0], res[1], res[2], res[3:]


def _exchange_shapes(arrs, modes):
    return [jax.ShapeDtypeStruct((N_DEV,) + a.shape if md == 'ag' else a.shape, a.dtype) for a, md in zip(arrs, modes)]


def _exchange_sems(n):
    return [pltpu.SemaphoreType.DMA((n, N_DEV - 1)), pltpu.SemaphoreType.DMA((n, N_DEV - 1)),
            pltpu.SemaphoreType.DMA((n,))]


def _exchange_copies(ins, outs, modes, send_sems, recv_sems, local_sems):
    x, y, c = lax.axis_index("x"), lax.axis_index("y"), lax.axis_index("c")
    me = 4 * x + 2 * y + c
    copies = []
    for a in range(len(ins)):
        ag = modes[a] == 'ag'
        copies.append(pltpu.make_async_copy(ins[a] if ag else ins[a].at[me], outs[a].at[me], local_sems.at[a]))
        for k in range(1, N_DEV):
            px = 1 - x if k & 4 else x
            py = 1 - y if k & 2 else y
            pc = 1 - c if k & 1 else c
            src = ins[a] if ag else ins[a].at[4 * px + 2 * py + pc]
            copies.append(pltpu.make_async_remote_copy(
                src_ref=src, dst_ref=outs[a].at[me], send_sem=send_sems.at[a, k - 1],
                recv_sem=recv_sems.at[a, k - 1], device_id=(px, py, pc), device_id_type=MESH))
    return copies


def _exchange(name, arrs, modes):
    n = len(arrs)

    def body(*refs):
        copies = _exchange_copies(refs[:n], refs[n:2 * n], modes, *refs[2 * n:])
        for cp in copies:
            cp.start()
        for cp in copies:
            cp.wait()

    return pl.pallas_call(
        body, name=name,
        in_specs=[pl.BlockSpec(memory_space=pl.ANY)] * n,
        out_specs=[pl.BlockSpec(memory_space=pl.ANY)] * n,
        out_shape=_exchange_shapes(arrs, modes),
        scratch_shapes=_exchange_sems(n),
        compiler_params=pltpu.CompilerParams(has_side_effects=True),
    )(*arrs)


def _riding_exchange(refs, n_in, n_out, n, modes, first, last):
    ins = refs[n_in:n_in + n]
    outs = refs[n_in + n + n_out:n_in + 2 * n + n_out]
    sems = refs[n_in + 2 * n + n_out:n_in + 2 * n + n_out + 3]

    @pl.when(first)
    def _():
        for cp in _exchange_copies(ins, outs, modes, *sems):
            cp.start()

    def finish():
        @pl.when(last)
        def _():
            for cp in _exchange_copies(ins, outs, modes, *sems):
                cp.wait()

    return finish


def _adamw(name, w, m, v, gparts):
    r, c = w.shape
    npart = gparts.shape[0]
    tr = _pick(r, (256, 128, 64, 32, 16, 8))
    spec = pl.BlockSpec((tr, c), lambda i: (i, 0))

    def body(w_ref, m_ref, v_ref, g_ref, go_ref, d_ref, mo_ref, vo_ref):
        g = g_ref[0]
        for p in range(1, npart):
            g = g + g_ref[p]
        m1 = ADAM_B1 * m_ref[...] + (1.0 - ADAM_B1) * g
        v1 = ADAM_B2 * v_ref[...] + (1.0 - ADAM_B2) * (g * g)
        m_hat = m1 / (1.0 - ADAM_B1 ** ADAM_STEP)
        v_hat = v1 / (1.0 - ADAM_B2 ** ADAM_STEP)
        go_ref[...] = g
        d_ref[...] = -ADAM_LR * (m_hat / (jnp.sqrt(v_hat) + ADAM_EPS) + ADAM_WD * w_ref[...])
        mo_ref[...] = m1
        vo_ref[...] = v1

    return pl.pallas_call(
        body, name=name, grid=(r // tr,),
        in_specs=[spec, spec, spec, pl.BlockSpec((npart, tr, c), lambda i: (0, i, 0))],
        out_specs=[spec] * 4, out_shape=[jax.ShapeDtypeStruct((r, c), F32)] * 4,
        compiler_params=_params(("parallel",)),
    )(w, m, v, gparts)


def _pack(arrs, rows):
    flat = jnp.concatenate([a.reshape(-1) for a in arrs])
    return jnp.pad(flat, (0, rows * LANES - flat.shape[0])).reshape(rows, LANES)


def _unpack(packed, shapes):
    flat, out, off = packed.reshape(-1), [], 0
    for s in shapes:
        n = math.prod(s)
        out.append(flat[off:off + n].reshape(s))
        off += n
    return out


def _pack_rows(n_elems):
    return -(-n_elems // (SUBLANES * LANES)) * SUBLANES


def _cols_from_shards(g):
    return g.transpose(1, 0, 2).reshape(g.shape[1], N_DEV * g.shape[2])


def _cols_to_shards(w):
    r, c = w.shape
    return w.reshape(r, N_DEV, c // N_DEV).transpose(1, 0, 2)


def _rope_tables(n_lat, n_ctx):
    rows = n_lat // GRID_W
    inv = ROPE_BASE ** (-jnp.arange(ROPE_PAIRS, dtype=F32) / ROPE_PAIRS)
    ang_r = jnp.arange(rows, dtype=F32)[:, None] * inv
    ang_c = jnp.arange(GRID_W, dtype=F32)[:, None] * inv
    cr, sr = jnp.repeat(jnp.cos(ang_r), GRID_W, axis=0), jnp.repeat(jnp.sin(ang_r), GRID_W, axis=0)
    cc, sc = jnp.tile(jnp.cos(ang_c), (rows, 1)), jnp.tile(jnp.sin(ang_c), (rows, 1))
    one, zero = jnp.ones((n_lat, 1), F32), jnp.zeros((n_lat, 1), F32)
    z8 = jnp.zeros((n_lat, ROPE_PAIRS), F32)
    cos_t = jnp.concatenate([jnp.tile(one, (1, NOPE)), cr, cr, cc, cc, jnp.tile(one, (1, HEAD_PAD - QK))], 1)
    sin_up = jnp.concatenate([jnp.tile(zero, (1, NOPE)), -sr, z8, -sc, z8, jnp.tile(zero, (1, HEAD_PAD - QK))], 1)
    sin_dn = jnp.concatenate([jnp.tile(zero, (1, NOPE)), z8, sr, z8, sc, jnp.tile(zero, (1, HEAD_PAD - QK))], 1)
    if n_ctx:
        cos_t = jnp.concatenate([jnp.ones((n_ctx, HEAD_PAD), F32), cos_t])
        sin_up = jnp.concatenate([jnp.zeros((n_ctx, HEAD_PAD), F32), sin_up])
        sin_dn = jnp.concatenate([jnp.zeros((n_ctx, HEAD_PAD), F32), sin_dn])
    return cos_t, sin_up, sin_dn


def _rope(x, cos_t, sin_up, sin_dn):
    return x * cos_t + pltpu.roll(x, HEAD_PAD - ROPE_PAIRS, 1) * sin_up + pltpu.roll(x, ROPE_PAIRS, 1) * sin_dn


def _rope_t(dy, cos_t, sin_up, sin_dn):
    return (dy * cos_t + pltpu.roll(dy * sin_up, ROPE_PAIRS, 1)
            + pltpu.roll(dy * sin_dn, HEAD_PAD - ROPE_PAIRS, 1))


def _softplus(x):
    return jnp.maximum(x, 0.0) + jnp.log(1.0 + jnp.exp(-jnp.abs(x)))


def _gates(z, xcv, lam_sp):
    outs = []
    for d in range(2):
        r = jax.nn.sigmoid(z[:, (2 * d) * LRU_W:(2 * d + 1) * LRU_W])
        ig = jax.nn.sigmoid(z[:, (2 * d + 1) * LRU_W:(2 * d + 2) * LRU_W])
        log_a = -LRU_C * r * lam_sp[:, d * LRU_W:(d + 1) * LRU_W]
        a = jnp.exp(log_a)
        u = jnp.sqrt(-jnp.tanh(log_a) * (a * a + 1.0)) * (ig * xcv)
        outs += [a, u]
    return tuple(outs)


def kernel(x, c, ctx, c_ctx, w_mod, b_mod, g_pre_mix, g_post_mix, g_pre_ffn, g_post_ffn, w_in, lru_conv_w, lru_conv_b, lru_w_a, lru_b_a, lru_w_x, lru_b_x, lru_lambda, mla_g_q, mla_w_uq, mla_g_kv, mla_w_ukv, w_out, ffn_w_up, ffn_conv_w, ffn_conv_b, ffn_w_down, loss_target, m_c_ctx, m_w_mod, m_b_mod, m_g_pre_mix, m_g_post_mix, m_g_pre_ffn, m_g_post_ffn, m_w_in, m_lru_conv_w, m_lru_conv_b, m_lru_w_a, m_lru_b_a, m_lru_w_x, m_lru_b_x, m_lru_lambda, m_mla_g_q, m_mla_w_uq, m_mla_g_kv, m_mla_w_ukv, m_w_out, m_ffn_w_up, m_ffn_conv_w, m_ffn_conv_b, m_ffn_w_down, v_c_ctx, v_w_mod, v_b_mod, v_g_pre_mix, v_g_post_mix, v_g_pre_ffn, v_g_post_ffn, v_w_in, v_lru_conv_w, v_lru_conv_b, v_lru_w_a, v_lru_b_a, v_lru_w_x, v_lru_b_x, v_lru_lambda, v_mla_g_q, v_mla_w_uq, v_mla_g_kv, v_mla_w_ukv, v_w_out, v_ffn_w_up, v_ffn_conv_w, v_ffn_conv_b, v_ffn_w_down):
    W = dict(c_ctx=c_ctx, w_mod=w_mod, b_mod=b_mod, g_pre_mix=g_pre_mix, g_post_mix=g_post_mix, g_pre_ffn=g_pre_ffn,
             g_post_ffn=g_post_ffn, w_in=w_in, lru_conv_w=lru_conv_w, lru_conv_b=lru_conv_b, lru_w_a=lru_w_a,
             lru_b_a=lru_b_a, lru_w_x=lru_w_x, lru_b_x=lru_b_x, lru_lambda=lru_lambda, mla_g_q=mla_g_q,
             mla_w_uq=mla_w_uq, mla_g_kv=mla_g_kv, mla_w_ukv=mla_w_ukv, w_out=w_out, ffn_w_up=ffn_w_up,
             ffn_conv_w=ffn_conv_w, ffn_conv_b=ffn_conv_b, ffn_w_down=ffn_w_down)
    M = dict(c_ctx=m_c_ctx, w_mod=m_w_mod, b_mod=m_b_mod, g_pre_mix=m_g_pre_mix, g_post_mix=m_g_post_mix,
             g_pre_ffn=m_g_pre_ffn, g_post_ffn=m_g_post_ffn, w_in=m_w_in, lru_conv_w=m_lru_conv_w,
             lru_conv_b=m_lru_conv_b, lru_w_a=m_lru_w_a, lru_b_a=m_lru_b_a, lru_w_x=m_lru_w_x, lru_b_x=m_lru_b_x,
             lru_lambda=m_lru_lambda, mla_g_q=m_mla_g_q, mla_w_uq=m_mla_w_uq, mla_g_kv=m_mla_g_kv,
             mla_w_ukv=m_mla_w_ukv, w_out=m_w_out, ffn_w_up=m_ffn_w_up, ffn_conv_w=m_ffn_conv_w,
             ffn_conv_b=m_ffn_conv_b, ffn_w_down=m_ffn_w_down)
    V = dict(c_ctx=v_c_ctx, w_mod=v_w_mod, b_mod=v_b_mod, g_pre_mix=v_g_pre_mix, g_post_mix=v_g_post_mix,
             g_pre_ffn=v_g_pre_ffn, g_post_ffn=v_g_post_ffn, w_in=v_w_in, lru_conv_w=v_lru_conv_w,
             lru_conv_b=v_lru_conv_b, lru_w_a=v_lru_w_a, lru_b_a=v_lru_b_a, lru_w_x=v_lru_w_x, lru_b_x=v_lru_b_x,
             lru_lambda=v_lru_lambda, mla_g_q=v_mla_g_q, mla_w_uq=v_mla_w_uq, mla_g_kv=v_mla_g_kv,
             mla_w_ukv=v_mla_w_ukv, w_out=v_w_out, ffn_w_up=v_ffn_w_up, ffn_conv_w=v_ffn_conv_w,
             ffn_conv_b=v_ffn_conv_b, ffn_w_down=v_ffn_w_down)

    D = D_MODEL
    S, CN = x.shape[1], ctx.shape[1]
    T = S + CN
    TM = ROW_TILE
    ct, ns, nt = CN // TM, S // TM, T // TM
    me = 4 * lax.axis_index("x") + 2 * lax.axis_index("y") + lax.axis_index("c")

    lat = lambda i: i + ct
    swp = lambda i: jnp.where(i < ct, i + ns, i - ct)
    lat_or_0 = lambda i: jnp.maximum(i - ct, 0)

    small_shapes = [W[n].shape[1:] for n in SMALL_SHARDED] + [(D,)]
    n_small = sum(math.prod(s) for s in small_shapes)
    small_rows = _pack_rows(n_small)
    small_loc = _pack([W[n][0] for n in SMALL_SHARDED] + [c[0]], small_rows)
    early = ['w_in', 'mla_w_uq', 'mla_w_ukv']
    late = ['w_out', 'ffn_w_up', 'ffn_w_down']
    big = early + late
    gathered = _exchange("gather_weights", [W[n][0].astype(BF16) for n in early] + [small_loc], ['ag'] * 4)
    gw = dict(zip(early, gathered[:3]))
    small_all = [_unpack(gathered[3][d], small_shapes) for d in range(N_DEV)]
    full_small = {n: jnp.concatenate([small_all[d][j] for d in range(N_DEV)], axis=-1)
                  for j, n in enumerate(SMALL_SHARDED)}
    c_all = jnp.stack([small_all[d][-1] for d in range(N_DEV)])

    w_in_f = _cols_from_shards(gw['w_in'])
    w_in_p = jnp.concatenate([w_in_f[:, :OFF_KR], jnp.zeros((D, NOPE), BF16), w_in_f[:, OFF_KR:],
                              jnp.zeros((D, HEAD_PAD - QK), BF16)], axis=1)
    w_uq_f = _cols_from_shards(gw['mla_w_uq']).reshape(Q_RANK, HEADS, QK)
    wq_p = jnp.pad(w_uq_f, ((0, 0), (0, 0), (0, HEAD_PAD - QK))).reshape(Q_RANK, HEADS * HEAD_PAD)
    w_ukv_f = _cols_from_shards(gw['mla_w_ukv']).reshape(KV_RANK, HEADS, NOPE + VDIM)
    wk_p = jnp.pad(w_ukv_f[:, :, :NOPE], ((0, 0), (0, 0), (0, HEAD_PAD - NOPE))).reshape(KV_RANK, HEADS * HEAD_PAD)
    wv_f = w_ukv_f[:, :, NOPE:].reshape(KV_RANK, HEADS * VDIM)

    lru_cw, lru_ba, lru_bx, lru_lam, ffn_cw = [full_small[n] for n in SMALL_SHARDED]
    ffn_cw_t, ffn_cb_t = _ff_to_tiles(ffn_cw), _ff_to_tiles(ffn_conv_b)

    def block_diag(w):
        eye = jnp.eye(LRU_HEADS, dtype=w.dtype)
        return jnp.einsum('hij,hg->higj', w, eye).reshape(LRU_W, LRU_W)

    w_gate = jnp.concatenate([block_diag(lru_w_a[0, 0]), block_diag(lru_w_x[0, 0]),
                              block_diag(lru_w_a[0, 1]), block_diag(lru_w_x[0, 1])], axis=1).astype(BF16)
    b_gate = jnp.concatenate([lru_ba[0], lru_bx[0], lru_ba[1], lru_bx[1]])[None]
    lam_row = lru_lam.reshape(1, 2 * LRU_W)

    c16 = jnp.concatenate([c_all, c_ctx[None], jnp.zeros((2 * SUBLANES - N_DEV - 1, D), F32)])
    ncol = w_mod.shape[2]
    b_mod_loc = lax.dynamic_slice(b_mod, (0, me * ncol), (1, ncol))

    def mod_fwd(c16_r, w_r, b_r):
        c16_v = c16_r[...]
        sl = c16_v * jax.nn.sigmoid(c16_v)
        return (jnp.dot(sl.astype(BF16), w_r[...].astype(BF16), preferred_element_type=F32) + b_r[...],)

    (mod_part,) = _single("mod_fwd", mod_fwd, [c16, w_mod[0], b_mod_loc], [((2 * SUBLANES, ncol), F32)])
    (mod_g,) = _exchange("gather_mod", [mod_part], ['ag'])
    mod_all = _cols_from_shards(mod_g)
    mod_lat = lax.dynamic_slice(mod_all, (me, 0), (1, N_MOD * D)).reshape(N_MOD, D)
    mod_ctx = mod_all[N_DEV].reshape(N_MOD, D)

    xs, tgt = x[0], loss_target[0]
    xa_rows = [_row(ctx[0], lambda i: jnp.minimum(i, ct - 1)), _row(xs, lat_or_0)]

    def sel_mod(i, ml, mc, r0):
        sh = jnp.where(i < ct, mc[r0:r0 + 1, :], ml[r0:r0 + 1, :])
        sc = jnp.where(i < ct, mc[r0 + 1:r0 + 2, :], ml[r0 + 1:r0 + 2, :])
        return sh, sc

    def pre_fn(xv, g, sh, sc):
        return _rms(xv, g) * (1.0 + sc) + sh

    def k_pre(i, rv, bv):
        sh, sc = sel_mod(i, bv[1], bv[2], 0)
        return (pre_fn(jnp.where(i < ct, rv[0], rv[1]), bv[0][...], sh, sc),), ()

    (h_pre,) = _rowwise("pre_mix", k_pre, nt, xa_rows, [g_pre_mix, mod_lat, mod_ctx], [((T, D), BF16, None)])
    proj = _mm("in_proj", h_pre, w_in_p, 'nn')

    xcv = _dwconv_fwd("lru_conv", proj, lru_cw, lru_conv_b, LRU_CONV_LEFT, (0, ct))

    def k_gates(i, rv, bv):
        xv = rv[0]
        z = jnp.dot(xv.astype(BF16), bv[0][...], preferred_element_type=F32) + bv[1][...]
        return _gates(z, xv, _bc(_softplus(-bv[2][...]), TM)), ()

    a0, u0, a1, u1 = _rowwise("lru_gates", k_gates, nt, [_row(xcv)], [w_gate, b_gate, lam_row],
                              [((T, LRU_W), F32, None), ((T, LRU_W), F32, None),
                               ((T, LRU_W), F32, swp), ((T, LRU_W), F32, swp)])
    h0 = _scan("lru_scan_f", a0, u0, False)
    h1 = _scan("lru_scan_r", a1, u1, True)

    cos_q, sup_q, sdn_q = _rope_tables(S, 0)
    cos_k, sup_k, sdn_k = _rope_tables(S, CN)
    cq_row = _row(proj, lat, (Q_RANK, OFF_CQ // Q_RANK))
    ckv_row = _row(proj, None, (KV_RANK, OFF_CKV // KV_RANK))

    def heads_of(xl):
        return [xl[:, hh * HEAD_PAD:(hh + 1) * HEAD_PAD] for hh in range(HEADS)]

    def k_q_path(i, rv, bv):
        cqv, ctb, sub, sdb = rv
        cqn_v = _rms(cqv, bv[0][...]).astype(BF16)
        ql = jnp.dot(cqn_v, bv[1][...], preferred_element_type=F32)
        return (cqn_v, jnp.stack([_rope(qh, ctb, sub, sdb) for qh in heads_of(ql)])), ()

    cqn, q = _rowwise("q_path", k_q_path, ns, [cq_row, _row(cos_q), _row(sup_q), _row(sdn_q)], [mla_g_q, wq_p],
                      [((S, Q_RANK), BF16, None), ((HEADS, S, HEAD_PAD), BF16, None)])

    def k_kv_path(i, rv, bv):
        ckv_v, krp, ctb, sub, sdb = rv
        ckvn_v = _rms(ckv_v, bv[0][...]).astype(BF16)
        kl = jnp.dot(ckvn_v, bv[1][...], preferred_element_type=F32)
        vl = jnp.dot(ckvn_v, bv[2][...], preferred_element_type=F32)
        kr = _rope(krp, ctb, sub, sdb)
        return (ckvn_v, jnp.stack([kh + kr for kh in heads_of(kl)]), vl), ()

    ckvn, k, v_lin = _rowwise(
        "kv_path", k_kv_path, nt,
        [ckv_row, _row(proj, None, (HEAD_PAD, OFF_KR // HEAD_PAD)), _row(cos_k), _row(sup_k), _row(sdn_k)],
        [mla_g_kv, wk_p, wv_f],
        [((T, KV_RANK), BF16, None), ((HEADS, T, HEAD_PAD), BF16, None), ((T, HEADS * VDIM), BF16, None)])
    v = v_lin.reshape(T, HEADS, VDIM).transpose(1, 0, 2)

    o, lse, late_g = _attn_fwd(q, k, v, [W[n][0].astype(BF16) for n in late], ['ag'] * 3)
    w_out_f = late_g[0].reshape(D, D)
    w_up_t = _ff_permute("w_up_to_tiles", _cols_from_shards(late_g[1]), True)
    w_down_f = late_g[2].reshape(D_FF, D)
    o_t = o.transpose(1, 0, 2).reshape(S, HEADS * VDIM)

    def lru_out_fn(hf, hr, gr):
        return (hf + hr) * jax.nn.gelu(gr)

    def k_mix_in(i, rv, bv):
        return (jnp.concatenate([lru_out_fn(rv[0], rv[1], rv[2]), rv[3]], axis=1),), ()

    gr_row = _row(proj, lat, (LRU_W, OFF_GR // LRU_W))
    y_in = _rowwise("mix_in", k_mix_in, ns, [_row(h0, lat), _row(h1), gr_row, _row(o_t)], [],
                    [((S, D), BF16, None)])[0]
    y = _mm("out_proj", y_in, w_out_f, 'nn')

    def post_mix_fn(xv, yv, gt, g):
        return xv + gt * _rms(yv, g)

    def k_post_mix(i, rv, bv):
        ml = bv[0]
        x1v = post_mix_fn(rv[0], rv[1], ml[2:3, :], bv[1][...])
        return (x1v, pre_fn(x1v, bv[2][...], ml[3:4, :], ml[4:5, :])), ()

    x1, h2 = _rowwise("post_mix", k_post_mix, ns, [_row(xs), _row(y)], [mod_lat, g_post_mix, g_pre_ffn],
                      [((S, D), F32, None), ((S, D), BF16, None)])
    up = _mm("ffn_up", h2, w_up_t, 'nn', out_dtype=BF16)
    act = _ffn_mid_fwd(up, ffn_cw_t, ffn_cb_t)
    f = _mm("ffn_down", act, w_down_f, 'nn')

    def loss_fn(x1v, fv, gt, g, tg):
        x2 = x1v + gt * _rms(fv, g)
        err = x2 - tg
        return 0.5 * jnp.sum(jnp.mean(err * err, axis=-1))

    def k_loss(i, rv, bv):
        gtb, gb = _bc(bv[0][5:6, :], TM), _bc(bv[1][...], TM)
        val, (dx1v, dfv, dgt, dg) = jax.value_and_grad(loss_fn, argnums=(0, 1, 2, 3))(rv[0], rv[1], gtb, gb, rv[2])
        return (dx1v, dfv), (jnp.full((1, LANES), val, F32), _rs(dgt), _rs(dg))

    dx1_a, df, loss_acc, d_gt2, d_g_post_ffn = _rowwise(
        "loss_bwd", k_loss, ns, [_row(x1), _row(f), _row(tgt)], [mod_lat, g_post_ffn],
        [((S, D), F32, None), ((S, D), BF16, None)], [(1, LANES), (1, D), (1, D)])

    d_act = _mm("ffn_down_dx", df, w_down_f, 'nt', out_dtype=BF16)
    d_w_down = _mm("ffn_down_dw", act, df, 'tn')
    d_up, d_ffn_cw_t, d_ffn_cb_t = _ffn_mid_bwd(up, d_act, ffn_cw_t, ffn_cb_t)
    d_ffn_cw, d_ffn_cb = _ff_from_tiles(d_ffn_cw_t), _ff_from_tiles(d_ffn_cb_t)
    d_h2 = _mm("ffn_up_dx", d_up, w_up_t, 'nt')
    d_w_up = _ff_permute("d_w_up_from_tiles", _mm("ffn_up_dw", h2, d_up, 'tn'), False)

    def k_pre_ffn_bwd(i, rv, bv):
        ml = bv[0]
        gb, shb, scb = _bc(bv[1][...], TM), _bc(ml[3:4, :], TM), _bc(ml[4:5, :], TM)
        _, pull = jax.vjp(pre_fn, rv[0], gb, shb, scb)
        dxv, dg, dsh, dsc = pull(rv[1])
        return (rv[2] + dxv,), (_rs(dg), _rs(dsh), _rs(dsc))

    dx1, d_g_pre_ffn, d_sh2, d_sc2 = _rowwise(
        "pre_ffn_bwd", k_pre_ffn_bwd, ns, [_row(x1), _row(d_h2), _row(dx1_a)], [mod_lat, g_pre_ffn],
        [((S, D), F32, None)], [(1, D), (1, D), (1, D)])

    def k_post_mix_bwd(i, rv, bv):
        gtb, gb = _bc(bv[0][2:3, :], TM), _bc(bv[1][...], TM)
        _, pull = jax.vjp(post_mix_fn, rv[0], rv[1], gtb, gb)
        _, dyv, dgt, dg = pull(rv[2])
        return (dyv,), (_rs(dgt), _rs(dg))

    dy, d_gt1, d_g_post_mix = _rowwise(
        "post_mix_bwd", k_post_mix_bwd, ns, [_row(xs), _row(y), _row(dx1)], [mod_lat, g_post_mix],
        [((S, D), BF16, None)], [(1, D), (1, D)])
    d_y_in = _mm("out_proj_dx", dy, w_out_f, 'nt')
    d_w_out = _mm("out_proj_dw", y_in, dy, 'tn')

    def k_lru_out_bwd(i, rv, bv):
        _, pull = jax.vjp(lru_out_fn, rv[0], rv[1], rv[2])
        dhf, _, dgr = pull(rv[3])
        return (dhf, dgr), ()

    d_hsum, d_gr = _rowwise("lru_out_bwd", k_lru_out_bwd, ns,
                            [_row(h0, lat), _row(h1), gr_row, _row(d_y_in, None, (LRU_W, 0))], [],
                            [((S, LRU_W), F32, None), ((S, LRU_W), F32, None)])

    do_h = d_y_in[:, LRU_W:].reshape(S, HEADS, VDIM).transpose(1, 0, 2)

    def k_delta(i, rv, bv):
        return (jnp.sum(rv[0] * rv[1], axis=-1, keepdims=True), rv[1]), ()

    delta, do_b = _rowwise("attn_delta", k_delta, ns, [_row(o), _row(do_h)], [],
                           [((HEADS, S, 1), F32, None), ((HEADS, S, VDIM), BF16, None)])
    late_sends = [d_w_out.reshape(N_DEV, D // N_DEV, D), _cols_to_shards(d_w_up),
                  d_w_down.reshape(N_DEV, D_FF // N_DEV, D)]
    dq, dk, dv, late_recv = _attn_bwd(q, k, v, do_b, lse.reshape(HEADS, 1, S), delta.reshape(HEADS, 1, S),
                                      late_sends, ['a2a'] * 3)

    def rms_bwd(xv, g, dy):
        _, pull = jax.vjp(_rms, xv, _bc(g, TM))
        dxv, dg = pull(dy)
        return dxv, _rs(dg)

    def k_q_path_bwd(i, rv, bv):
        dqv, ctb, sub, sdb, cqv = rv
        dql = jnp.concatenate([_rope_t(dqv[hh] * MLA_SCALE, ctb, sub, sdb) for hh in range(HEADS)], axis=1).astype(BF16)
        d_cqn_v = lax.dot_general(dql, bv[1][...], NT_DIMS, preferred_element_type=F32)
        dxv, dg = rms_bwd(cqv, bv[0][...], d_cqn_v)
        return (dql, dxv), (dg,)

    dq_lin, d_cq, d_g_q = _rowwise(
        "q_path_bwd", k_q_path_bwd, ns, [_row(dq), _row(cos_q), _row(sup_q), _row(sdn_q), cq_row], [mla_g_q, wq_p],
        [((S, HEADS * HEAD_PAD), BF16, None), ((S, Q_RANK), F32, None)], [(1, Q_RANK)])
    d_wq_p = _mm("q_proj_dw", cqn, dq_lin, 'tn')

    dv_lin = dv.transpose(1, 0, 2).reshape(T, HEADS * VDIM).astype(BF16)

    def k_kv_path_bwd(i, rv, bv):
        dkv_, ctb, sub, sdb, dvl, ckv_v = rv
        tot = dkv_[0]
        for hh in range(1, HEADS):
            tot = tot + dkv_[hh]
        lane = lax.broadcasted_iota(jnp.int32, tot.shape, 1)
        tot = jnp.where((lane >= NOPE) & (lane < QK), tot, 0.0)
        dkl = jnp.concatenate([dkv_[hh] for hh in range(HEADS)], axis=1).astype(BF16)
        d_ckvn_v = (lax.dot_general(dkl, bv[1][...], NT_DIMS, preferred_element_type=F32)
                    + lax.dot_general(dvl, bv[2][...], NT_DIMS, preferred_element_type=F32))
        dxv, dg = rms_bwd(ckv_v, bv[0][...], d_ckvn_v)
        return (dkl, _rope_t(tot, ctb, sub, sdb), dxv), (dg,)

    dk_lin, d_krp, d_ckv, d_g_kv = _rowwise(
        "kv_path_bwd", k_kv_path_bwd, nt,
        [_row(dk), _row(cos_k), _row(sup_k), _row(sdn_k), _row(dv_lin), ckv_row], [mla_g_kv, wk_p, wv_f],
        [((T, HEADS * HEAD_PAD), BF16, None), ((T, HEAD_PAD), F32, None), ((T, KV_RANK), F32, None)], [(1, KV_RANK)])
    d_wk_p = _mm("k_proj_dw", ckvn, dk_lin, 'tn')
    d_wv = _mm("v_proj_dw", ckvn, dv_lin, 'tn')

    zero_row = jnp.zeros((1, LRU_W), F32)
    lam0 = _scan("lru_scan_f_bwd", a0, d_hsum, True, shifted=True, u_off=CN)
    lam1 = _scan("lru_scan_r_bwd", a1, d_hsum, False, shifted=True, u_off=0)
    hprev0 = jnp.concatenate([zero_row, h0[:-1]])
    hprev1 = jnp.concatenate([h1[1:], zero_row])

    def k_gates_bwd(i, rv, bv):
        xv, l0, hp0, l1, hp1 = rv
        wg, bg, lamv = [b[...] for b in bv]
        xb = xv.astype(BF16)
        z = jnp.dot(xb, wg, preferred_element_type=F32) + bg
        spb = _bc(_softplus(-lamv), TM)
        _, pull = jax.vjp(_gates, z, xv, spb)
        dz, dxv, dsp = pull((l0 * hp0, l0, l1 * hp1, l1))
        dzb = dz.astype(BF16)
        dxv = dxv + lax.dot_general(dzb, wg, NT_DIMS, preferred_element_type=F32)
        dwg = lax.dot_general(xb, dzb, (((0,), (0,)), ((), ())), preferred_element_type=F32)
        dlam = -_rs(dsp) * jax.nn.sigmoid(-lamv)
        return (dxv,), (dwg, _rs(dz), dlam)

    d_xcv, d_w_gate, d_b_gate, d_lam = _rowwise(
        "lru_gates_bwd", k_gates_bwd, nt,
        [_row(xcv), _row(lam0), _row(hprev0), _row(lam1, swp), _row(hprev1, swp)], [w_gate, b_gate, lam_row],
        [((T, LRU_W), F32, None)], [(LRU_W, 4 * LRU_W), (1, 4 * LRU_W), (1, 2 * LRU_W)])
    d_xr, d_lru_cw, d_lru_cb = _dwconv_bwd("lru_conv_bwd", proj, d_xcv, lru_cw, LRU_CONV_LEFT, (0, ct))

    def k_dproj(i, rv, bv):
        is_lat = i >= ct
        return (jnp.concatenate([rv[0], jnp.where(is_lat, rv[1], 0.0), jnp.where(is_lat, rv[2], 0.0), rv[3], rv[4]],
                                axis=1),), ()

    d_proj = _rowwise("d_proj", k_dproj, nt,
                      [_row(d_xr), _row(d_gr, lat_or_0), _row(d_cq, lat_or_0), _row(d_ckv), _row(d_krp)], [],
                      [((T, IN_W_PAD), BF16, None)])[0]
    d_h_pre = _mm("in_proj_dx", d_proj, w_in_p, 'nt')
    d_w_in_p = _mm("in_proj_dw", h_pre, d_proj, 'tn')

    def k_pre_bwd(i, rv, bv):
        g, ml, mc = bv[0][...], bv[1], bv[2]
        sh, sc = sel_mod(i, ml, mc, 0)
        _, pull = jax.vjp(pre_fn, jnp.where(i < ct, rv[0], rv[1]), _bc(g, TM), _bc(sh, TM), _bc(sc, TM))
        dxv, dg, dsh, dsc = pull(rv[2])
        is_lat = i >= ct
        dsh, dsc = _rs(dsh), _rs(dsc)
        zero = jnp.zeros_like(dsh)
        return ((dxv + rv[3],),
                (_rs(dg), jnp.where(is_lat, dsh, zero), jnp.where(is_lat, dsc, zero),
                 jnp.where(is_lat, zero, dsh), jnp.where(is_lat, zero, dsc)))

    dxl, d_g_pre_mix, d_sh1, d_sc1, d_csh1, d_csc1 = _rowwise(
        "pre_mix_bwd", k_pre_bwd, nt, xa_rows + [_row(d_h_pre), _row(dx1, lat_or_0)],
        [g_pre_mix, mod_lat, mod_ctx], [((S, D), F32, lat_or_0)], [(1, D)] * 5)
    grad_x = dxl[None]

    zrow = jnp.zeros((1, D), F32)
    d_mod_lat = jnp.concatenate([d_sh1, d_sc1, d_gt1, d_sh2, d_sc2, d_gt2], axis=1)
    d_mod_ctx = jnp.concatenate([d_csh1, d_csc1, zrow, zrow, zrow, zrow], axis=1)
    loss_row = jnp.pad(loss_acc, ((0, 0), (0, N_MOD * D - LANES)))
    d_mod_mine = jnp.concatenate([d_mod_lat, d_mod_ctx, loss_row, jnp.zeros((SUBLANES - 3, N_MOD * D), F32)])
    (d_mod_all,) = _exchange("gather_dmod", [d_mod_mine], ['ag'])
    loss = jnp.sum(d_mod_all[:, 2, 0])
    dm_lat_loc = lax.dynamic_slice(d_mod_all[:, 0], (0, me * ncol), (N_DEV, ncol))
    dm_ctx_loc = lax.dynamic_slice(d_mod_all[:, 1], (0, me * ncol), (N_DEV, ncol))

    def mod_bwd(c16_r, w_r, dml_r, dmc_r):
        c16_v = c16_r[...]
        sig = jax.nn.sigmoid(c16_v)
        sl = c16_v * sig
        dctx = dmc_r[0:1, :]
        for d in range(1, N_DEV):
            dctx = dctx + dmc_r[d:d + 1, :]
        row = lax.broadcasted_iota(jnp.int32, (2 * SUBLANES, ncol), 0)
        dm16 = dml_r[...] + jnp.where(row == N_DEV, _bc(dctx, 2 * SUBLANES), 0.0)
        dw = lax.dot_general(sl.astype(BF16), dm16.astype(BF16), (((0,), (0,)), ((), ())), preferred_element_type=F32)
        dsl = lax.dot_general(dm16.astype(BF16), w_r[...].astype(BF16), NT_DIMS, preferred_element_type=F32)
        dc = dsl * (sig * (1.0 + c16_v * (1.0 - sig)))
        return dw, dc

    dm_lat16 = jnp.concatenate([dm_lat_loc, jnp.zeros((2 * SUBLANES - N_DEV, ncol), F32)])
    g_w_mod, dc16 = _single("mod_bwd", mod_bwd, [c16, w_mod[0], dm_lat16, dm_ctx_loc],
                            [((D, ncol), F32), ((2 * SUBLANES, D), F32)])
    d_c_ctx_part = dc16[N_DEV]

    def diag_blocks(dw):
        return jnp.stack([dw[hh * 64:(hh + 1) * 64, hh * 64:(hh + 1) * 64] for hh in range(LRU_HEADS)])

    d_lru_w_a = jnp.stack([diag_blocks(d_w_gate[:, 0:LRU_W]), diag_blocks(d_w_gate[:, 2 * LRU_W:3 * LRU_W])])[None]
    d_lru_w_x = jnp.stack([diag_blocks(d_w_gate[:, LRU_W:2 * LRU_W]), diag_blocks(d_w_gate[:, 3 * LRU_W:])])[None]
    d_b_a = jnp.stack([d_b_gate[0, 0:LRU_W], d_b_gate[0, 2 * LRU_W:3 * LRU_W]])
    d_b_x = jnp.stack([d_b_gate[0, LRU_W:2 * LRU_W], d_b_gate[0, 3 * LRU_W:]])

    rep_part = dict(c_ctx=d_c_ctx_part, b_mod=d_mod_lat + d_mod_ctx, g_pre_mix=d_g_pre_mix, g_post_mix=d_g_post_mix,
                    g_pre_ffn=d_g_pre_ffn, g_post_ffn=d_g_post_ffn, lru_conv_b=d_lru_cb, lru_w_a=d_lru_w_a,
                    lru_w_x=d_lru_w_x, mla_g_q=d_g_q, mla_g_kv=d_g_kv, ffn_conv_b=d_ffn_cb)
    rep_shapes = [W[n].shape for n in REPLICATED]
    rep_rows = -(-_pack_rows(sum(W[n].size for n in REPLICATED)) // ROW_TILE) * ROW_TILE
    rep_loc = _pack([rep_part[n] for n in REPLICATED], rep_rows)

    d_w_in = jnp.concatenate([d_w_in_p[:, :OFF_KR], d_w_in_p[:, OFF_KR + NOPE:OFF_KR + QK]], axis=1)
    d_w_uq = d_wq_p.reshape(Q_RANK, HEADS, HEAD_PAD)[:, :, :QK].reshape(Q_RANK, HEADS * QK)
    d_w_ukv = jnp.concatenate([d_wk_p.reshape(KV_RANK, HEADS, HEAD_PAD)[:, :, :NOPE],
                               d_wv.reshape(KV_RANK, HEADS, VDIM)], axis=2).reshape(KV_RANK, HEADS * (NOPE + VDIM))
    small_full = dict(lru_conv_w=d_lru_cw, lru_b_a=d_b_a, lru_b_x=d_b_x, lru_lambda=d_lam.reshape(2, LRU_W),
                      ffn_conv_w=d_ffn_cw)
    small_sh = jnp.concatenate([_cols_to_shards(small_full[n]).reshape(N_DEV, -1) for n in SMALL_SHARDED], axis=1)
    n_sh = small_sh.shape[1]
    sh_rows = _pack_rows(n_sh)
    small_sh = jnp.pad(small_sh, ((0, 0), (0, sh_rows * LANES - n_sh))).reshape(N_DEV, sh_rows, LANES)

    sends = [_cols_to_shards(d_w_in), _cols_to_shards(d_w_uq), _cols_to_shards(d_w_ukv), small_sh]
    recv = _exchange("exchange_grads", sends + [rep_loc], ['a2a'] * 4 + ['ag'])
    big_parts = list(recv[:3]) + list(late_recv)

    res = {}

    def adam(name, w2, m2, v2, parts):
        return _adamw("adamw_" + name, w2, m2, v2, parts)

    for n, parts in zip(big, big_parts):
        shp = W[n].shape
        outs = adam(n, W[n][0], M[n][0], V[n][0], parts)
        res[n] = [o_.reshape(shp) for o_ in outs]
    outs = adam('w_mod', w_mod[0], m_w_mod[0], v_w_mod[0], g_w_mod[None])
    res['w_mod'] = [o_.reshape(w_mod.shape) for o_ in outs]

    sh_shapes = [W[n].shape for n in SMALL_SHARDED]
    pk = lambda dct: _pack([dct[n] for n in SMALL_SHARDED], sh_rows)
    outs = adam('small_sharded', pk(W), pk(M), pk(V), recv[3])
    for n, vals in zip(SMALL_SHARDED, zip(*[_unpack(o_, sh_shapes) for o_ in outs])):
        res[n] = list(vals)

    pr = lambda dct: _pack([dct[n] for n in REPLICATED], rep_rows)
    outs = adam('replicated', pr(W), pr(M), pr(V), recv[4])
    for n, vals in zip(REPLICATED, zip(*[_unpack(o_, rep_shapes) for o_ in outs])):
        res[n] = list(vals)

    return (loss, grad_x, *[res[n][0] for n in WEIGHTS], *[res[n][1] for n in WEIGHTS],
            *[res[n][2] for n in WEIGHTS], *[res[n][3] for n in WEIGHTS])
```

```python
import functools
import math

import jax
import jax.numpy as jnp
from jax import lax
from jax.experimental import pallas as pl
from jax.experimental.pallas import tpu as pltpu

F32 = jnp.float32
BF16 = jnp.bfloat16
MESH = pl.DeviceIdType.MESH

N_DEV = 8
ROW_TILE = 256
SUBLANES = 8
LANES = 128
VMEM_LIMIT = 56 * 1024 * 1024

D_MODEL = 1024
LRU_W = 512
LRU_HEADS = 8
LRU_CONV_K = 4
LRU_CONV_LEFT = 2
LRU_C = 8.0
HEADS = 8
NOPE = 64
ROPE = 32
VDIM = 64
QK = NOPE + ROPE
HEAD_PAD = 128
Q_RANK = 256
KV_RANK = 128
MLA_SCALE = QK ** -0.5
ROPE_PAIRS = ROPE // 4
ROPE_BASE = 10000.0
GRID_W = 64
D_FF = 2816
FFN_CONV_K = 3
FFN_CONV_LEFT = 1
N_MOD = 6
EPS = 1e-6
IN_W = 2 * LRU_W + Q_RANK + KV_RANK + ROPE
IN_W_PAD = 2 * LRU_W + Q_RANK + KV_RANK + HEAD_PAD
OFF_GR, OFF_CQ, OFF_CKV, OFF_KR = LRU_W, 2 * LRU_W, 2 * LRU_W + Q_RANK, 2 * LRU_W + Q_RANK + KV_RANK

ADAM_LR, ADAM_B1, ADAM_B2, ADAM_EPS, ADAM_WD, ADAM_STEP = 0.001, 0.9, 0.999, 1e-08, 0.01, 10

WEIGHTS = ['c_ctx', 'w_mod', 'b_mod', 'g_pre_mix', 'g_post_mix', 'g_pre_ffn', 'g_post_ffn', 'w_in', 'lru_conv_w',
           'lru_conv_b', 'lru_w_a', 'lru_b_a', 'lru_w_x', 'lru_b_x', 'lru_lambda', 'mla_g_q', 'mla_w_uq', 'mla_g_kv',
           'mla_w_ukv', 'w_out', 'ffn_w_up', 'ffn_conv_w', 'ffn_conv_b', 'ffn_w_down']
REPLICATED = ['c_ctx', 'b_mod', 'g_pre_mix', 'g_post_mix', 'g_pre_ffn', 'g_post_ffn', 'lru_conv_b', 'lru_w_a',
              'lru_w_x', 'mla_g_q', 'mla_g_kv', 'ffn_conv_b']
SMALL_SHARDED = ['lru_conv_w', 'lru_b_a', 'lru_b_x', 'lru_lambda', 'ffn_conv_w']


def _pick(d, prefs):
    for p in prefs:
        if d % p == 0:
            return p
    return d


def _params(sem=None):
    return pltpu.CompilerParams(dimension_semantics=sem, vmem_limit_bytes=VMEM_LIMIT)


MM_TILES = (1024, 1408, 768, 512, 256, 128)


def _mm(name, a, b, mode, out_dtype=F32):
    if mode == 'nn':
        (m, k), (_, n) = a.shape, b.shape
    elif mode == 'nt':
        (m, k), (n, _) = a.shape, b.shape
    else:
        (k, m), (_, n) = a.shape, b.shape
    tm = _pick(m, MM_TILES)
    tn = _pick(n, MM_TILES)
    tk = _pick(k, MM_TILES)
    nk = k // tk
    if mode == 'nn':
        a_spec = pl.BlockSpec((tm, tk), lambda i, j, kk: (i, kk))
        b_spec = pl.BlockSpec((tk, tn), lambda i, j, kk: (kk, j))
        dn = (((1,), (0,)), ((), ()))
    elif mode == 'nt':
        a_spec = pl.BlockSpec((tm, tk), lambda i, j, kk: (i, kk))
        b_spec = pl.BlockSpec((tn, tk), lambda i, j, kk: (j, kk))
        dn = (((1,), (1,)), ((), ()))
    else:
        a_spec = pl.BlockSpec((tk, tm), lambda i, j, kk: (kk, i))
        b_spec = pl.BlockSpec((tk, tn), lambda i, j, kk: (kk, j))
        dn = (((0,), (0,)), ((), ()))

    def body(a_ref, b_ref, o_ref, acc_ref):
        kk = pl.program_id(2)

        @pl.when(kk == 0)
        def _():
            acc_ref[...] = jnp.zeros_like(acc_ref)

        acc_ref[...] += lax.dot_general(a_ref[...].astype(BF16), b_ref[...].astype(BF16), dn,
                                        preferred_element_type=F32)

        @pl.when(kk == nk - 1)
        def _():
            o_ref[...] = acc_ref[...].astype(o_ref.dtype)

    return pl.pallas_call(
        body, name=name, grid=(m // tm, n // tn, nk),
        in_specs=[a_spec, b_spec], out_specs=pl.BlockSpec((tm, tn), lambda i, j, kk: (i, j)),
        out_shape=jax.ShapeDtypeStruct((m, n), out_dtype),
        scratch_shapes=[pltpu.VMEM((tm, tn), F32)],
        compiler_params=_params(("parallel", "parallel", "arbitrary")),
    )(a, b)


def _row(a, idx=None, col=None):
    return dict(a=a, idx=idx, col=col)


def _rowwise(name, fn, n_tiles, rows, bcast, out_rows, out_acc=(), tm=ROW_TILE, ride_arrs=(), ride_modes=()):
    in_specs = []
    for r in rows:
        a, idx, col = r['a'], r['idx'] or (lambda i: i), r['col']
        if a.ndim == 2:
            w, ci = col if col else (a.shape[1], 0)
            in_specs.append(pl.BlockSpec((tm, w), lambda i, idx=idx, ci=ci: (idx(i), ci)))
        else:
            in_specs.append(pl.BlockSpec((a.shape[0], tm, a.shape[2]), lambda i, idx=idx: (0, idx(i), 0)))
    for b in bcast:
        in_specs.append(pl.BlockSpec(b.shape, lambda i, nd=b.ndim: (0,) * nd))
    out_specs, out_shape = [], []
    for shape, dtype, idx in out_rows:
        idx = idx or (lambda i: i)
        if len(shape) == 2:
            out_specs.append(pl.BlockSpec((tm, shape[1]), lambda i, idx=idx: (idx(i), 0)))
        else:
            out_specs.append(pl.BlockSpec((shape[0], tm, shape[2]), lambda i, idx=idx: (0, idx(i), 0)))
        out_shape.append(jax.ShapeDtypeStruct(shape, dtype))
    for shape in out_acc:
        out_specs.append(pl.BlockSpec(shape, lambda i, nd=len(shape): (0,) * nd))
        out_shape.append(jax.ShapeDtypeStruct(shape, F32))
    nr, nb, no, na, n = len(rows), len(bcast), len(out_rows), len(out_acc), len(ride_arrs)
    any_spec = pl.BlockSpec(memory_space=pl.ANY)

    def body(*refs):
        i = pl.program_id(0)
        finish = _riding_exchange(refs, nr + nb, no + na, n, ride_modes, i == 0, i == n_tiles - 1) if n else None
        rvals = [r[...] for r in refs[:nr]]
        bvals = list(refs[nr:nr + nb])
        o_rows, o_acc = fn(i, rvals, bvals)
        outs = refs[nr + nb + n:]
        for ref, v in zip(outs[:no], o_rows):
            ref[...] = v.astype(ref.dtype)
        acc_refs = outs[no:no + na]
        if acc_refs:
            @pl.when(i == 0)
            def _():
                for ref in acc_refs:
                    ref[...] = jnp.zeros_like(ref)
            for ref, v in zip(acc_refs, o_acc):
                ref[...] += v
        if n:
            finish()

    res = pl.pallas_call(
        body, name=name, grid=(n_tiles,), in_specs=in_specs + [any_spec] * n, out_specs=out_specs + [any_spec] * n,
        out_shape=out_shape + _exchange_shapes(ride_arrs, ride_modes),
        scratch_shapes=_exchange_sems(n) if n else [],
        compiler_params=pltpu.CompilerParams(dimension_semantics=("arbitrary",), vmem_limit_bytes=VMEM_LIMIT,
                                             has_side_effects=bool(n)),
    )(*[r['a'] for r in rows], *bcast, *ride_arrs)
    return (res[:no + na], res[no + na:]) if n else res


def _single(name, fn, ins, out_shapes):
    def body(*refs):
        outs = fn(*refs[:len(ins)])
        for ref, v in zip(refs[len(ins):], outs):
            ref[...] = v.astype(ref.dtype)

    return pl.pallas_call(
        body, name=name,
        in_specs=[pl.BlockSpec(memory_space=pltpu.VMEM)] * len(ins),
        out_specs=[pl.BlockSpec(memory_space=pltpu.VMEM)] * len(out_shapes),
        out_shape=[jax.ShapeDtypeStruct(s, d) for s, d in out_shapes],
        compiler_params=_params(),
    )(*ins)


def _bc(p, n):
    return jnp.broadcast_to(p, (n, p.shape[-1]))


def _rs(g):
    return jnp.sum(g, axis=0, keepdims=True)


def _rms(x, g):
    return x * lax.rsqrt(jnp.mean(x * x, axis=-1, keepdims=True) + EPS) * g


def _conv_specs(r, cw, tm, halo=SUBLANES):
    th = tm // halo
    last = r // halo - 1
    prev = pl.BlockSpec((halo, cw), lambda c, i: (jnp.maximum(i * th - 1, 0), c))
    cur = pl.BlockSpec((tm, cw), lambda c, i: (i, c))
    nxt = pl.BlockSpec((halo, cw), lambda c, i: (jnp.minimum((i + 1) * th, last), c))
    return [prev, cur, nxt]


def _fill_ext(ext_ref, prev_ref, cur_ref, next_ref, i, n_tiles, seg_starts, tm):
    prev_ok = functools.reduce(jnp.logical_and, [i != s for s in seg_starts])
    next_ok = functools.reduce(jnp.logical_and, [i + 1 != s for s in seg_starts] + [i + 1 < n_tiles])
    ext_ref[0:SUBLANES, :] = jnp.where(prev_ok, prev_ref[...].astype(F32), 0.0)
    ext_ref[SUBLANES:SUBLANES + tm, :] = cur_ref[...].astype(F32)
    ext_ref[SUBLANES + tm:, :] = jnp.where(next_ok, next_ref[...].astype(F32), 0.0)


def _dwconv_fwd(name, x, w, b, left, seg_starts, cw=512, tm=ROW_TILE):
    r, c = x.shape[0], w.shape[1]
    kw = w.shape[0]
    n_tiles = r // tm

    def body(prev_ref, cur_ref, next_ref, w_ref, b_ref, o_ref, ext_ref):
        i = pl.program_id(1)
        _fill_ext(ext_ref, prev_ref, cur_ref, next_ref, i, n_tiles, seg_starts, tm)
        out = jnp.broadcast_to(b_ref[...], (tm, cw))
        for k in range(kw):
            out = out + ext_ref[pl.ds(SUBLANES + k - left, tm), :] * w_ref[k:k + 1, :]
        o_ref[...] = out

    return pl.pallas_call(
        body, name=name, grid=(c // cw, n_tiles),
        in_specs=_conv_specs(r, cw, tm) + [pl.BlockSpec((kw, cw), lambda c_, i: (0, c_)),
                                           pl.BlockSpec((1, cw), lambda c_, i: (0, c_))],
        out_specs=pl.BlockSpec((tm, cw), lambda c_, i: (i, c_)),
        out_shape=jax.ShapeDtypeStruct((r, c), F32),
        scratch_shapes=[pltpu.VMEM((tm + 2 * SUBLANES, cw), F32)],
        compiler_params=_params(("parallel", "arbitrary")),
    )(x, x, x, w, b)


def _dwconv_bwd(name, x, dy, w, left, seg_starts, out_dtype=F32, cw=512, tm=ROW_TILE):
    r, c = dy.shape
    kw = w.shape[0]
    n_tiles = r // tm

    def body(xp, xc, xn, dp, dc, dn, w_ref, dx_ref, dw_ref, db_ref, xe_ref, de_ref):
        i = pl.program_id(1)
        _fill_ext(xe_ref, xp, xc, xn, i, n_tiles, seg_starts, tm)
        _fill_ext(de_ref, dp, dc, dn, i, n_tiles, seg_starts, tm)
        dyc = dc[...].astype(F32)
        dx = jnp.zeros((tm, cw), F32)
        dws = []
        for k in range(kw):
            dx = dx + de_ref[pl.ds(SUBLANES - k + left, tm), :] * w_ref[k:k + 1, :]
            dws.append(jnp.sum(dyc * xe_ref[pl.ds(SUBLANES + k - left, tm), :], axis=0, keepdims=True))
        dx_ref[...] = dx.astype(dx_ref.dtype)

        @pl.when(i == 0)
        def _():
            dw_ref[...] = jnp.zeros_like(dw_ref)
            db_ref[...] = jnp.zeros_like(db_ref)

        for k in range(kw):
            dw_ref[k:k + 1, :] += dws[k]
        db_ref[...] += jnp.sum(dyc, axis=0, keepdims=True)

    return pl.pallas_call(
        body, name=name, grid=(c // cw, n_tiles),
        in_specs=_conv_specs(r, cw, tm) + _conv_specs(r, cw, tm) + [pl.BlockSpec((kw, cw), lambda c_, i: (0, c_))],
        out_specs=[pl.BlockSpec((tm, cw), lambda c_, i: (i, c_)),
                   pl.BlockSpec((kw, cw), lambda c_, i: (0, c_)),
                   pl.BlockSpec((1, cw), lambda c_, i: (0, c_))],
        out_shape=[jax.ShapeDtypeStruct((r, c), out_dtype), jax.ShapeDtypeStruct((kw, c), F32),
                   jax.ShapeDtypeStruct((1, c), F32)],
        scratch_shapes=[pltpu.VMEM((tm + 2 * SUBLANES, cw), F32), pltpu.VMEM((tm + 2 * SUBLANES, cw), F32)],
        compiler_params=_params(("parallel", "arbitrary")),
    )(x, x, x, dy, dy, dy, w)


FF_TILE = 256
FF_HALO = 16
FF_STRIP = 32


def _ffn_fill(ext_ref, prev_ref, cur_ref, next_ref, i, n_tiles, tm):
    ext_ref[0:FF_HALO, :] = jnp.where(i > 0, prev_ref[...].astype(F32), 0.0)
    ext_ref[FF_HALO:FF_HALO + tm, :] = cur_ref[...].astype(F32)
    ext_ref[FF_HALO + tm:, :] = jnp.where(i + 1 < n_tiles, next_ref[...].astype(F32), 0.0)


def _ffn_conv(ext_ref, w_ref, b_ref, start, rows):
    out = jnp.broadcast_to(b_ref[...], (rows, 2 * FF_TILE))
    for k in range(FFN_CONV_K):
        out = out + ext_ref[pl.ds(start + k - FFN_CONV_LEFT, rows), :] * w_ref[k:k + 1, :]
    return out


def _ffn_mid_fwd(up, w, b):
    s, c2 = up.shape
    tm = _pick(s, (2 * ROW_TILE, ROW_TILE))
    n_tiles = s // tm
    cw = 2 * FF_TILE

    def body(prev_ref, cur_ref, next_ref, w_ref, b_ref, o_ref, ext_ref):
        i = pl.program_id(1)
        _ffn_fill(ext_ref, prev_ref, cur_ref, next_ref, i, n_tiles, tm)
        for r0 in range(0, tm, FF_STRIP):
            upc = _ffn_conv(ext_ref, w_ref, b_ref, FF_HALO + r0, FF_STRIP)
            uv, gv = upc[:, :FF_TILE], upc[:, FF_TILE:]
            o_ref[r0:r0 + FF_STRIP, :] = (gv * jax.nn.sigmoid(gv) * uv).astype(o_ref.dtype)

    return pl.pallas_call(
        body, name="ffn_mid", grid=(c2 // cw, n_tiles),
        in_specs=_conv_specs(s, cw, tm, FF_HALO) + [pl.BlockSpec((FFN_CONV_K, cw), lambda c_, i: (0, c_)),
                                                   pl.BlockSpec((1, cw), lambda c_, i: (0, c_))],
        out_specs=pl.BlockSpec((tm, FF_TILE), lambda c_, i: (i, c_)),
        out_shape=jax.ShapeDtypeStruct((s, c2 // 2), BF16),
        scratch_shapes=[pltpu.VMEM((tm + 2 * FF_HALO, cw), F32)],
        compiler_params=_params(("parallel", "arbitrary")),
    )(up, up, up, w, b)


def _ffn_mid_bwd(up, d_act, w, b, tm=ROW_TILE):
    s, c2 = up.shape
    n_tiles = s // tm
    cw = 2 * FF_TILE
    h8 = SUBLANES

    def gate_bwd(upc, dact):
        uv, gv = upc[:, :FF_TILE], upc[:, FF_TILE:]
        sg = jax.nn.sigmoid(gv)
        return jnp.concatenate([dact * (gv * sg), dact * uv * (sg * (1.0 + gv * (1.0 - sg)))], axis=1)

    def body(up_p, up_c, up_n, da_p, da_c, da_n, w_ref, b_ref, dup_ref, dw_ref, db_ref, ext_ref, dext_ref):
        i = pl.program_id(1)
        _ffn_fill(ext_ref, up_p, up_c, up_n, i, n_tiles, tm)
        dws = [jnp.zeros((1, cw), F32) for _ in range(FFN_CONV_K)]
        dbs = jnp.zeros((1, cw), F32)
        for r0 in range(0, tm, FF_STRIP):
            d_c = gate_bwd(_ffn_conv(ext_ref, w_ref, b_ref, FF_HALO + r0, FF_STRIP),
                           da_c[r0:r0 + FF_STRIP, :].astype(F32))
            dext_ref[FF_HALO + r0:FF_HALO + r0 + FF_STRIP, :] = d_c
            for k in range(FFN_CONV_K):
                xk = ext_ref[pl.ds(FF_HALO + r0 + k - FFN_CONV_LEFT, FF_STRIP), :]
                dws[k] = dws[k] + jnp.sum(d_c * xk, axis=0, keepdims=True)
            dbs = dbs + jnp.sum(d_c, axis=0, keepdims=True)
        da_prev = jnp.where(i > 0, da_p[...].astype(F32)[FF_HALO - h8:, :], 0.0)
        da_next = jnp.where(i + 1 < n_tiles, da_n[...].astype(F32)[:h8, :], 0.0)
        dext_ref[FF_HALO - h8:FF_HALO, :] = gate_bwd(_ffn_conv(ext_ref, w_ref, b_ref, FF_HALO - h8, h8), da_prev)
        dext_ref[FF_HALO + tm:FF_HALO + tm + h8, :] = gate_bwd(_ffn_conv(ext_ref, w_ref, b_ref, FF_HALO + tm, h8), da_next)
        for r0 in range(0, tm, FF_STRIP):
            dup = jnp.zeros((FF_STRIP, cw), F32)
            for k in range(FFN_CONV_K):
                dup = dup + dext_ref[pl.ds(FF_HALO + r0 - k + FFN_CONV_LEFT, FF_STRIP), :] * w_ref[k:k + 1, :]
            dup_ref[r0:r0 + FF_STRIP, :] = dup.astype(dup_ref.dtype)

        @pl.when(i == 0)
        def _():
            dw_ref[...] = jnp.zeros_like(dw_ref)
            db_ref[...] = jnp.zeros_like(db_ref)

        for k in range(FFN_CONV_K):
            dw_ref[k:k + 1, :] += dws[k]
        db_ref[...] += dbs

    def half_specs():
        th = tm // FF_HALO
        last = s // FF_HALO - 1
        return [pl.BlockSpec((FF_HALO, FF_TILE), lambda c_, i: (jnp.maximum(i * th - 1, 0), c_)),
                pl.BlockSpec((tm, FF_TILE), lambda c_, i: (i, c_)),
                pl.BlockSpec((FF_HALO, FF_TILE), lambda c_, i: (jnp.minimum((i + 1) * th, last), c_))]

    return pl.pallas_call(
        body, name="ffn_mid_bwd", grid=(c2 // cw, n_tiles),
        in_specs=_conv_specs(s, cw, tm, FF_HALO) + half_specs() + [
            pl.BlockSpec((FFN_CONV_K, cw), lambda c_, i: (0, c_)), pl.BlockSpec((1, cw), lambda c_, i: (0, c_))],
        out_specs=[pl.BlockSpec((tm, cw), lambda c_, i: (i, c_)),
                   pl.BlockSpec((FFN_CONV_K, cw), lambda c_, i: (0, c_)),
                   pl.BlockSpec((1, cw), lambda c_, i: (0, c_))],
        out_shape=[jax.ShapeDtypeStruct((s, c2), BF16), jax.ShapeDtypeStruct((FFN_CONV_K, c2), F32),
                   jax.ShapeDtypeStruct((1, c2), F32)],
        scratch_shapes=[pltpu.VMEM((tm + 2 * FF_HALO, cw), F32), pltpu.VMEM((tm + 2 * FF_HALO, cw), F32)],
        compiler_params=_params(("parallel", "arbitrary")),
    )(up, up, up, d_act, d_act, d_act, w, b)


def _ff_permute(name, w, to_tiles):
    r = w.shape[0]
    nb = D_FF // FF_TILE
    natural = pl.BlockSpec((r, FF_TILE), lambda j, half: (0, half * nb + j))
    tiled = pl.BlockSpec((r, FF_TILE), lambda j, half: (0, 2 * j + half))

    def body(x_ref, o_ref):
        o_ref[...] = x_ref[...]

    return pl.pallas_call(
        body, name=name, grid=(nb, 2),
        in_specs=[natural if to_tiles else tiled], out_specs=tiled if to_tiles else natural,
        out_shape=jax.ShapeDtypeStruct(w.shape, w.dtype), compiler_params=_params(("parallel", "parallel")),
    )(w)


def _ff_to_tiles(w):
    r = w.shape[0]
    return w.reshape(r, 2, D_FF // FF_TILE, FF_TILE).transpose(0, 2, 1, 3).reshape(r, 2 * D_FF)


def _ff_from_tiles(w):
    r = w.shape[0]
    return w.reshape(r, D_FF // FF_TILE, 2, FF_TILE).transpose(0, 2, 1, 3).reshape(r, 2 * D_FF)


SCAN_UNROLL = 8


def _scan(name, a, u, reverse, shifted=False, u_off=0):
    t, c = a.shape
    us = u.shape[0]
    n8 = t // SUBLANES
    lo, hi = 0, SUBLANES - 1

    def body(a_ref, u_ref, h_ref):
        row = lax.broadcasted_iota(jnp.int32, (SUBLANES, LANES), 0)
        last = lo if reverse else hi

        def tile(ref, base):
            return ref[pl.ds(pl.multiple_of(base, SUBLANES), SUBLANES), :]

        def local(blk):
            base = blk * SUBLANES
            av = tile(a_ref, base)
            if shifted and reverse:
                nb = tile(a_ref, jnp.minimum(base + SUBLANES, t - SUBLANES))
                edge = jnp.where(base + SUBLANES >= t, 1.0, pltpu.roll(nb, hi, 0))
                av = jnp.where(row < hi, pltpu.roll(av, hi, 0), edge)
            elif shifted:
                pb = tile(a_ref, jnp.maximum(base - SUBLANES, 0))
                edge = jnp.where(base == 0, 1.0, pltpu.roll(pb, 1, 0))
                av = jnp.where(row >= 1, pltpu.roll(av, 1, 0), edge)
            ub = base - u_off
            hv = jnp.where((ub >= 0) & (ub < us), tile(u_ref, jnp.clip(ub, 0, us - SUBLANES)), 0.0)
            for s in (1, 2, 4):
                shift = SUBLANES - s if reverse else s
                ok = (row < SUBLANES - s) if reverse else (row >= s)
                a_sh = jnp.where(ok, pltpu.roll(av, shift, 0), 1.0)
                h_sh = jnp.where(ok, pltpu.roll(hv, shift, 0), 0.0)
                hv = av * h_sh + hv
                av = av * a_sh
            a_last = jnp.sum(jnp.where(row == last, av, 0.0), axis=0, keepdims=True)
            h_last = jnp.sum(jnp.where(row == last, hv, 0.0), axis=0, keepdims=True)
            return base, av, hv, a_last, h_last

        def step(j, carry):
            parts = []
            for k in range(SCAN_UNROLL):
                idx = j * SCAN_UNROLL + k
                parts.append(local((n8 - 1 - idx) if reverse else idx))
            for base, av, hv, a_last, h_last in parts:
                h_ref[pl.ds(pl.multiple_of(base, SUBLANES), SUBLANES), :] = av * carry + hv
                carry = a_last * carry + h_last
            return carry

        lax.fori_loop(0, n8 // SCAN_UNROLL, step, jnp.zeros((1, LANES), F32))

    return pl.pallas_call(
        body, name=name, grid=(c // LANES,),
        in_specs=[pl.BlockSpec((t, LANES), lambda j: (0, j)), pl.BlockSpec((us, LANES), lambda j: (0, j))],
        out_specs=pl.BlockSpec((t, LANES), lambda j: (0, j)),
        out_shape=jax.ShapeDtypeStruct((t, c), F32),
        compiler_params=_params(("parallel",)),
    )(a, u)


NT_DIMS = (((1,), (1,)), ((), ()))


LOG2E = 1.4426950408889634
SCALE2 = MLA_SCALE * LOG2E
ATTN_TILES = (512, 256, 128)
KEY_CHUNKS = (768, 512, 256, 128)
QUERY_CHUNKS = (1024, 512, 256, 128)


def _attn_fwd(q, k, v, ride_arrs, ride_modes):
    h, s, _ = q.shape
    t = k.shape[1]
    tq = _pick(s, ATTN_TILES)
    ck = _pick(t, KEY_CHUNKS)
    n = len(ride_arrs)

    def body(*refs):
        q_ref, k_ref, v_ref = refs[:3]
        o_ref, lse_ref = refs[3 + n:5 + n]
        hh, i = pl.program_id(0), pl.program_id(1)
        finish = _riding_exchange(refs, 3, 2, n, ride_modes, (hh == 0) & (i == 0),
                                  (hh == h - 1) & (i == s // tq - 1))
        qv = q_ref[0]
        m = l = acc = None
        for j in range(t // ck):
            kj, vj = k_ref[0, j * ck:(j + 1) * ck, :], v_ref[0, j * ck:(j + 1) * ck, :]
            s2 = lax.dot_general(qv, kj, NT_DIMS, preferred_element_type=F32) * SCALE2
            mj = jnp.max(s2, axis=-1, keepdims=True)
            m_new = mj if j == 0 else jnp.maximum(m, mj)
            p = jnp.exp2(s2 - m_new)
            lj = jnp.sum(p, axis=-1, keepdims=True)
            pv = jnp.dot(p.astype(BF16), vj, preferred_element_type=F32)
            if j == 0:
                l, acc = lj, pv
            else:
                alpha = jnp.exp2(m - m_new)
                l, acc = alpha * l + lj, alpha * acc + pv
            m = m_new
        o_ref[0] = acc / l
        lse_ref[0] = m + jnp.log2(l)
        finish()

    any_spec = pl.BlockSpec(memory_space=pl.ANY)
    res = pl.pallas_call(
        body, name="attn_fwd", grid=(h, s // tq),
        in_specs=[pl.BlockSpec((1, tq, HEAD_PAD), lambda hh, i: (hh, i, 0)),
                  pl.BlockSpec((1, t, HEAD_PAD), lambda hh, i: (hh, 0, 0)),
                  pl.BlockSpec((1, t, VDIM), lambda hh, i: (hh, 0, 0))] + [any_spec] * n,
        out_specs=[pl.BlockSpec((1, tq, VDIM), lambda hh, i: (hh, i, 0)),
                   pl.BlockSpec((1, tq, 1), lambda hh, i: (hh, i, 0))] + [any_spec] * n,
        out_shape=[jax.ShapeDtypeStruct((h, s, VDIM), F32), jax.ShapeDtypeStruct((h, s, 1), F32)]
        + _exchange_shapes(ride_arrs, ride_modes),
        scratch_shapes=_exchange_sems(n),
        compiler_params=pltpu.CompilerParams(dimension_semantics=("arbitrary", "arbitrary"),
                                             vmem_limit_bytes=VMEM_LIMIT, has_side_effects=True),
    )(q, k, v, *ride_arrs)
    return res[0], res[1], res[2:]


TN_DIMS = (((0,), (0,)), ((), ()))


def _attn_bwd(q, k, v, do, lse_row, delta_row, ride_arrs, ride_modes):
    h, s, _ = q.shape
    t = k.shape[1]
    tk = _pick(t, (768,) + ATTN_TILES)
    cq = _pick(s, QUERY_CHUNKS)
    n = len(ride_arrs)

    def body(*refs):
        q_ref, k_ref, v_ref, do_ref, lse_ref, delta_ref = refs[:6]
        dq_ref, dk_ref, dv_ref = refs[6 + n:9 + n]
        hh, i = pl.program_id(0), pl.program_id(1)
        finish = _riding_exchange(refs, 6, 3, n, ride_modes, (hh == 0) & (i == 0),
                                  (hh == h - 1) & (i == t // tk - 1))

        @pl.when(i == 0)
        def _():
            dq_ref[...] = jnp.zeros_like(dq_ref)

        kt, vt = k_ref[0], v_ref[0]
        dk = dv = None
        for j in range(s // cq):
            rows = slice(j * cq, (j + 1) * cq)
            qj, doj = q_ref[0, rows, :], do_ref[0, rows, :]
            pt = jnp.exp2(lax.dot_general(kt, qj, NT_DIMS, preferred_element_type=F32) * SCALE2 - lse_ref[0, :, rows])
            dv_j = jnp.dot(pt.astype(BF16), doj, preferred_element_type=F32)
            dpt = lax.dot_general(vt, doj, NT_DIMS, preferred_element_type=F32)
            dst = (pt * (dpt - delta_ref[0, :, rows])).astype(BF16)
            dk_j = jnp.dot(dst, qj, preferred_element_type=F32)
            dq_ref[0, rows, :] += lax.dot_general(dst, kt, TN_DIMS, preferred_element_type=F32)
            dk, dv = (dk_j, dv_j) if j == 0 else (dk + dk_j, dv + dv_j)
        dk_ref[0] = dk * MLA_SCALE
        dv_ref[0] = dv
        finish()

    any_spec = pl.BlockSpec(memory_space=pl.ANY)
    res = pl.pallas_call(
        body, name="attn_bwd", grid=(h, t // tk),
        in_specs=[pl.BlockSpec((1, s, HEAD_PAD), lambda hh, i: (hh, 0, 0)),
                  pl.BlockSpec((1, tk, HEAD_PAD), lambda hh, i: (hh, i, 0)),
                  pl.BlockSpec((1, tk, VDIM), lambda hh, i: (hh, i, 0)),
                  pl.BlockSpec((1, s, VDIM), lambda hh, i: (hh, 0, 0)),
                  pl.BlockSpec((1, 1, s), lambda hh, i: (hh, 0, 0)),
                  pl.BlockSpec((1, 1, s), lambda hh, i: (hh, 0, 0))] + [any_spec] * n,
        out_specs=[pl.BlockSpec((1, s, HEAD_PAD), lambda hh, i: (hh, 0, 0)),
                   pl.BlockSpec((1, tk, HEAD_PAD), lambda hh, i: (hh, i, 0)),
                   pl.BlockSpec((1, tk, VDIM), lambda hh, i: (hh, i, 0))] + [any_spec] * n,
        out_shape=[jax.ShapeDtypeStruct((h, s, HEAD_PAD), F32), jax.ShapeDtypeStruct((h, t, HEAD_PAD), F32),
                   jax.ShapeDtypeStruct((h, t, VDIM), F32)] + _exchange_shapes(ride_arrs, ride_modes),
        scratch_shapes=_exchange_sems(n),
        compiler_params=pltpu.CompilerParams(dimension_semantics=("arbitrary", "arbitrary"),
                                             vmem_limit_bytes=VMEM_LIMIT, has_side_effects=True),
    )(q, k, v, do, lse_row, delta_row, *ride_arrs)
    return res[0], res[1], res[2], res[3:]


def _exchange_shapes(arrs, modes):
    return [jax.ShapeDtypeStruct((N_DEV,) + a.shape if md == 'ag' else a.shape, a.dtype) for a, md in zip(arrs, modes)]


def _exchange_sems(n):
    return [pltpu.SemaphoreType.DMA((n, N_DEV - 1)), pltpu.SemaphoreType.DMA((n, N_DEV - 1)),
            pltpu.SemaphoreType.DMA((n,))]


def _exchange_copies(ins, outs, modes, send_sems, recv_sems, local_sems):
    x, y, c = lax.axis_index("x"), lax.axis_index("y"), lax.axis_index("c")
    me = 4 * x + 2 * y + c
    copies = []
    for a in range(len(ins)):
        ag = modes[a] == 'ag'
        copies.append(pltpu.make_async_copy(ins[a] if ag else ins[a].at[me], outs[a].at[me], local_sems.at[a]))
        for k in range(1, N_DEV):
            px = 1 - x if k & 4 else x
            py = 1 - y if k & 2 else y
            pc = 1 - c if k & 1 else c
            src = ins[a] if ag else ins[a].at[4 * px + 2 * py + pc]
            copies.append(pltpu.make_async_remote_copy(
                src_ref=src, dst_ref=outs[a].at[me], send_sem=send_sems.at[a, k - 1],
                recv_sem=recv_sems.at[a, k - 1], device_id=(px, py, pc), device_id_type=MESH))
    return copies


def _exchange(name, arrs, modes):
    n = len(arrs)

    def body(*refs):
        copies = _exchange_copies(refs[:n], refs[n:2 * n], modes, *refs[2 * n:])
        for cp in copies:
            cp.start()
        for cp in copies:
            cp.wait()

    return pl.pallas_call(
        body, name=name,
        in_specs=[pl.BlockSpec(memory_space=pl.ANY)] * n,
        out_specs=[pl.BlockSpec(memory_space=pl.ANY)] * n,
        out_shape=_exchange_shapes(arrs, modes),
        scratch_shapes=_exchange_sems(n),
        compiler_params=pltpu.CompilerParams(has_side_effects=True),
    )(*arrs)


def _riding_exchange(refs, n_in, n_out, n, modes, first, last):
    ins = refs[n_in:n_in + n]
    outs = refs[n_in + n + n_out:n_in + 2 * n + n_out]
    sems = refs[n_in + 2 * n + n_out:n_in + 2 * n + n_out + 3]

    @pl.when(first)
    def _():
        for cp in _exchange_copies(ins, outs, modes, *sems):
            cp.start()

    def finish():
        @pl.when(last)
        def _():
            for cp in _exchange_copies(ins, outs, modes, *sems):
                cp.wait()

    return finish


def _adamw(name, w, m, v, gparts):
    r, c = w.shape
    npart = gparts.shape[0]
    tr = _pick(r, (256, 128, 64, 32, 16, 8))
    spec = pl.BlockSpec((tr, c), lambda i: (i, 0))

    def body(w_ref, m_ref, v_ref, g_ref, go_ref, d_ref, mo_ref, vo_ref):
        g = g_ref[0]
        for p in range(1, npart):
            g = g + g_ref[p]
        m1 = ADAM_B1 * m_ref[...] + (1.0 - ADAM_B1) * g
        v1 = ADAM_B2 * v_ref[...] + (1.0 - ADAM_B2) * (g * g)
        m_hat = m1 / (1.0 - ADAM_B1 ** ADAM_STEP)
        v_hat = v1 / (1.0 - ADAM_B2 ** ADAM_STEP)
        go_ref[...] = g
        d_ref[...] = -ADAM_LR * (m_hat / (jnp.sqrt(v_hat) + ADAM_EPS) + ADAM_WD * w_ref[...])
        mo_ref[...] = m1
        vo_ref[...] = v1

    return pl.pallas_call(
        body, name=name, grid=(r // tr,),
        in_specs=[spec, spec, spec, pl.BlockSpec((npart, tr, c), lambda i: (0, i, 0))],
        out_specs=[spec] * 4, out_shape=[jax.ShapeDtypeStruct((r, c), F32)] * 4,
        compiler_params=_params(("parallel",)),
    )(w, m, v, gparts)


def _pack(arrs, rows):
    flat = jnp.concatenate([a.reshape(-1) for a in arrs])
    return jnp.pad(flat, (0, rows * LANES - flat.shape[0])).reshape(rows, LANES)


def _unpack(packed, shapes):
    flat, out, off = packed.reshape(-1), [], 0
    for s in shapes:
        n = math.prod(s)
        out.append(flat[off:off + n].reshape(s))
        off += n
    return out


def _pack_rows(n_elems):
    return -(-n_elems // (SUBLANES * LANES)) * SUBLANES


def _cols_from_shards(g):
    return g.transpose(1, 0, 2).reshape(g.shape[1], N_DEV * g.shape[2])


def _cols_to_shards(w):
    r, c = w.shape
    return w.reshape(r, N_DEV, c // N_DEV).transpose(1, 0, 2)


def _rope_tables(n_lat, n_ctx):
    rows = n_lat // GRID_W
    inv = ROPE_BASE ** (-jnp.arange(ROPE_PAIRS, dtype=F32) / ROPE_PAIRS)
    ang_r = jnp.arange(rows, dtype=F32)[:, None] * inv
    ang_c = jnp.arange(GRID_W, dtype=F32)[:, None] * inv
    cr, sr = jnp.repeat(jnp.cos(ang_r), GRID_W, axis=0), jnp.repeat(jnp.sin(ang_r), GRID_W, axis=0)
    cc, sc = jnp.tile(jnp.cos(ang_c), (rows, 1)), jnp.tile(jnp.sin(ang_c), (rows, 1))
    one, zero = jnp.ones((n_lat, 1), F32), jnp.zeros((n_lat, 1), F32)
    z8 = jnp.zeros((n_lat, ROPE_PAIRS), F32)
    cos_t = jnp.concatenate([jnp.tile(one, (1, NOPE)), cr, cr, cc, cc, jnp.tile(one, (1, HEAD_PAD - QK))], 1)
    sin_up = jnp.concatenate([jnp.tile(zero, (1, NOPE)), -sr, z8, -sc, z8, jnp.tile(zero, (1, HEAD_PAD - QK))], 1)
    sin_dn = jnp.concatenate([jnp.tile(zero, (1, NOPE)), z8, sr, z8, sc, jnp.tile(zero, (1, HEAD_PAD - QK))], 1)
    if n_ctx:
        cos_t = jnp.concatenate([jnp.ones((n_ctx, HEAD_PAD), F32), cos_t])
        sin_up = jnp.concatenate([jnp.zeros((n_ctx, HEAD_PAD), F32), sin_up])
        sin_dn = jnp.concatenate([jnp.zeros((n_ctx, HEAD_PAD), F32), sin_dn])
    return cos_t, sin_up, sin_dn


def _rope(x, cos_t, sin_up, sin_dn):
    return x * cos_t + pltpu.roll(x, HEAD_PAD - ROPE_PAIRS, 1) * sin_up + pltpu.roll(x, ROPE_PAIRS, 1) * sin_dn


def _rope_t(dy, cos_t, sin_up, sin_dn):
    return (dy * cos_t + pltpu.roll(dy * sin_up, ROPE_PAIRS, 1)
            + pltpu.roll(dy * sin_dn, HEAD_PAD - ROPE_PAIRS, 1))


def _softplus(x):
    return jnp.maximum(x, 0.0) + jnp.log(1.0 + jnp.exp(-jnp.abs(x)))


def _gates(z, xcv, lam_sp):
    outs = []
    for d in range(2):
        r = jax.nn.sigmoid(z[:, (2 * d) * LRU_W:(2 * d + 1) * LRU_W])
        ig = jax.nn.sigmoid(z[:, (2 * d + 1) * LRU_W:(2 * d + 2) * LRU_W])
        log_a = -LRU_C * r * lam_sp[:, d * LRU_W:(d + 1) * LRU_W]
        a = jnp.exp(log_a)
        u = jnp.sqrt(-jnp.tanh(log_a) * (a * a + 1.0)) * (ig * xcv)
        outs += [a, u]
    return tuple(outs)


def kernel(x, c, ctx, c_ctx, w_mod, b_mod, g_pre_mix, g_post_mix, g_pre_ffn, g_post_ffn, w_in, lru_conv_w, lru_conv_b, lru_w_a, lru_b_a, lru_w_x, lru_b_x, lru_lambda, mla_g_q, mla_w_uq, mla_g_kv, mla_w_ukv, w_out, ffn_w_up, ffn_conv_w, ffn_conv_b, ffn_w_down, loss_target, m_c_ctx, m_w_mod, m_b_mod, m_g_pre_mix, m_g_post_mix, m_g_pre_ffn, m_g_post_ffn, m_w_in, m_lru_conv_w, m_lru_conv_b, m_lru_w_a, m_lru_b_a, m_lru_w_x, m_lru_b_x, m_lru_lambda, m_mla_g_q, m_mla_w_uq, m_mla_g_kv, m_mla_w_ukv, m_w_out, m_ffn_w_up, m_ffn_conv_w, m_ffn_conv_b, m_ffn_w_down, v_c_ctx, v_w_mod, v_b_mod, v_g_pre_mix, v_g_post_mix, v_g_pre_ffn, v_g_post_ffn, v_w_in, v_lru_conv_w, v_lru_conv_b, v_lru_w_a, v_lru_b_a, v_lru_w_x, v_lru_b_x, v_lru_lambda, v_mla_g_q, v_mla_w_uq, v_mla_g_kv, v_mla_w_ukv, v_w_out, v_ffn_w_up, v_ffn_conv_w, v_ffn_conv_b, v_ffn_w_down):
    W = dict(c_ctx=c_ctx, w_mod=w_mod, b_mod=b_mod, g_pre_mix=g_pre_mix, g_post_mix=g_post_mix, g_pre_ffn=g_pre_ffn,
             g_post_ffn=g_post_ffn, w_in=w_in, lru_conv_w=lru_conv_w, lru_conv_b=lru_conv_b, lru_w_a=lru_w_a,
             lru_b_a=lru_b_a, lru_w_x=lru_w_x, lru_b_x=lru_b_x, lru_lambda=lru_lambda, mla_g_q=mla_g_q,
             mla_w_uq=mla_w_uq, mla_g_kv=mla_g_kv, mla_w_ukv=mla_w_ukv, w_out=w_out, ffn_w_up=ffn_w_up,
             ffn_conv_w=ffn_conv_w, ffn_conv_b=ffn_conv_b, ffn_w_down=ffn_w_down)
    M = dict(c_ctx=m_c_ctx, w_mod=m_w_mod, b_mod=m_b_mod, g_pre_mix=m_g_pre_mix, g_post_mix=m_g_post_mix,
             g_pre_ffn=m_g_pre_ffn, g_post_ffn=m_g_post_ffn, w_in=m_w_in, lru_conv_w=m_lru_conv_w,
             lru_conv_b=m_lru_conv_b, lru_w_a=m_lru_w_a, lru_b_a=m_lru_b_a, lru_w_x=m_lru_w_x, lru_b_x=m_lru_b_x,
             lru_lambda=m_lru_lambda, mla_g_q=m_mla_g_q, mla_w_uq=m_mla_w_uq, mla_g_kv=m_mla_g_kv,
             mla_w_ukv=m_mla_w_ukv, w_out=m_w_out, ffn_w_up=m_ffn_w_up, ffn_conv_w=m_ffn_conv_w,
             ffn_conv_b=m_ffn_conv_b, ffn_w_down=m_ffn_w_down)
    V = dict(c_ctx=v_c_ctx, w_mod=v_w_mod, b_mod=v_b_mod, g_pre_mix=v_g_pre_mix, g_post_mix=v_g_post_mix,
             g_pre_ffn=v_g_pre_ffn, g_post_ffn=v_g_post_ffn, w_in=v_w_in, lru_conv_w=v_lru_conv_w,
             lru_conv_b=v_lru_conv_b, lru_w_a=v_lru_w_a, lru_b_a=v_lru_b_a, lru_w_x=v_lru_w_x, lru_b_x=v_lru_b_x,
             lru_lambda=v_lru_lambda, mla_g_q=v_mla_g_q, mla_w_uq=v_mla_w_uq, mla_g_kv=v_mla_g_kv,
             mla_w_ukv=v_mla_w_ukv, w_out=v_w_out, ffn_w_up=v_ffn_w_up, ffn_conv_w=v_ffn_conv_w,
             ffn_conv_b=v_ffn_conv_b, ffn_w_down=v_ffn_w_down)

    D = D_MODEL
    S, CN = x.shape[1], ctx.shape[1]
    T = S + CN
    TM = ROW_TILE
    ct, ns, nt = CN // TM, S // TM, T // TM
    me = 4 * lax.axis_index("x") + 2 * lax.axis_index("y") + lax.axis_index("c")

    lat = lambda i: i + ct
    swp = lambda i: jnp.where(i < ct, i + ns, i - ct)
    lat_or_0 = lambda i: jnp.maximum(i - ct, 0)

    small_shapes = [W[n].shape[1:] for n in SMALL_SHARDED] + [(D,)]
    n_small = sum(math.prod(s) for s in small_shapes)
    small_rows = _pack_rows(n_small)
    small_loc = _pack([W[n][0] for n in SMALL_SHARDED] + [c[0]], small_rows)
    early = ['w_in', 'mla_w_uq', 'mla_w_ukv']
    late = ['w_out', 'ffn_w_up', 'ffn_w_down']
    big = early + late
    gathered = _exchange("gather_weights", [W[n][0].astype(BF16) for n in early] + [small_loc], ['ag'] * 4)
    gw = dict(zip(early, gathered[:3]))
    small_all = [_unpack(gathered[3][d], small_shapes) for d in range(N_DEV)]
    full_small = {n: jnp.concatenate([small_all[d][j] for d in range(N_DEV)], axis=-1)
                  for j, n in enumerate(SMALL_SHARDED)}
    c_all = jnp.stack([small_all[d][-1] for d in range(N_DEV)])

    w_in_f = _cols_from_shards(gw['w_in'])
    w_in_p = jnp.concatenate([w_in_f[:, :OFF_KR], jnp.zeros((D, NOPE), BF16), w_in_f[:, OFF_KR:],
                              jnp.zeros((D, HEAD_PAD - QK), BF16)], axis=1)
    w_uq_f = _cols_from_shards(gw['mla_w_uq']).reshape(Q_RANK, HEADS, QK)
    wq_p = jnp.pad(w_uq_f, ((0, 0), (0, 0), (0, HEAD_PAD - QK))).reshape(Q_RANK, HEADS * HEAD_PAD)
    w_ukv_f = _cols_from_shards(gw['mla_w_ukv']).reshape(KV_RANK, HEADS, NOPE + VDIM)
    wk_p = jnp.pad(w_ukv_f[:, :, :NOPE], ((0, 0), (0, 0), (0, HEAD_PAD - NOPE))).reshape(KV_RANK, HEADS * HEAD_PAD)
    wv_f = w_ukv_f[:, :, NOPE:].reshape(KV_RANK, HEADS * VDIM)

    lru_cw, lru_ba, lru_bx, lru_lam, ffn_cw = [full_small[n] for n in SMALL_SHARDED]
    ffn_cw_t, ffn_cb_t = _ff_to_tiles(ffn_cw), _ff_to_tiles(ffn_conv_b)

    def block_diag(w):
        eye = jnp.eye(LRU_HEADS, dtype=w.dtype)
        return jnp.einsum('hij,hg->higj', w, eye).reshape(LRU_W, LRU_W)

    w_gate = jnp.concatenate([block_diag(lru_w_a[0, 0]), block_diag(lru_w_x[0, 0]),
                              block_diag(lru_w_a[0, 1]), block_diag(lru_w_x[0, 1])], axis=1).astype(BF16)
    b_gate = jnp.concatenate([lru_ba[0], lru_bx[0], lru_ba[1], lru_bx[1]])[None]
    lam_row = lru_lam.reshape(1, 2 * LRU_W)

    c16 = jnp.concatenate([c_all, c_ctx[None], jnp.zeros((2 * SUBLANES - N_DEV - 1, D), F32)])
    ncol = w_mod.shape[2]
    b_mod_loc = lax.dynamic_slice(b_mod, (0, me * ncol), (1, ncol))

    def mod_fwd(c16_r, w_r, b_r):
        c16_v = c16_r[...]
        sl = c16_v * jax.nn.sigmoid(c16_v)
        return (jnp.dot(sl.astype(BF16), w_r[...].astype(BF16), preferred_element_type=F32) + b_r[...],)

    (mod_part,) = _single("mod_fwd", mod_fwd, [c16, w_mod[0], b_mod_loc], [((2 * SUBLANES, ncol), F32)])
    (mod_g,) = _exchange("gather_mod", [mod_part], ['ag'])
    mod_all = _cols_from_shards(mod_g)
    mod_lat = lax.dynamic_slice(mod_all, (me, 0), (1, N_MOD * D)).reshape(N_MOD, D)
    mod_ctx = mod_all[N_DEV].reshape(N_MOD, D)

    xs, tgt = x[0], loss_target[0]
    xa_rows = [_row(ctx[0], lambda i: jnp.minimum(i, ct - 1)), _row(xs, lat_or_0)]

    def sel_mod(i, ml, mc, r0):
        sh = jnp.where(i < ct, mc[r0:r0 + 1, :], ml[r0:r0 + 1, :])
        sc = jnp.where(i < ct, mc[r0 + 1:r0 + 2, :], ml[r0 + 1:r0 + 2, :])
        return sh, sc

    def pre_fn(xv, g, sh, sc):
        return _rms(xv, g) * (1.0 + sc) + sh

    def k_pre(i, rv, bv):
        sh, sc = sel_mod(i, bv[1], bv[2], 0)
        return (pre_fn(jnp.where(i < ct, rv[0], rv[1]), bv[0][...], sh, sc),), ()

    (h_pre,) = _rowwise("pre_mix", k_pre, nt, xa_rows, [g_pre_mix, mod_lat, mod_ctx], [((T, D), BF16, None)])
    proj = _mm("in_proj", h_pre, w_in_p, 'nn')

    xcv = _dwconv_fwd("lru_conv", proj, lru_cw, lru_conv_b, LRU_CONV_LEFT, (0, ct))

    def k_gates(i, rv, bv):
        xv = rv[0]
        z = jnp.dot(xv.astype(BF16), bv[0][...], preferred_element_type=F32) + bv[1][...]
        return _gates(z, xv, _bc(_softplus(-bv[2][...]), TM)), ()

    a0, u0, a1, u1 = _rowwise("lru_gates", k_gates, nt, [_row(xcv)], [w_gate, b_gate, lam_row],
                              [((T, LRU_W), F32, None), ((T, LRU_W), F32, None),
                               ((T, LRU_W), F32, swp), ((T, LRU_W), F32, swp)])
    h0 = _scan("lru_scan_f", a0, u0, False)
    h1 = _scan("lru_scan_r", a1, u1, True)

    cos_q, sup_q, sdn_q = _rope_tables(S, 0)
    cos_k, sup_k, sdn_k = _rope_tables(S, CN)
    cq_row = _row(proj, lat, (Q_RANK, OFF_CQ // Q_RANK))
    ckv_row = _row(proj, None, (KV_RANK, OFF_CKV // KV_RANK))

    def heads_of(xl):
        return [xl[:, hh * HEAD_PAD:(hh + 1) * HEAD_PAD] for hh in range(HEADS)]

    def k_q_path(i, rv, bv):
        cqv, ctb, sub, sdb = rv
        cqn_v = _rms(cqv, bv[0][...]).astype(BF16)
        ql = jnp.dot(cqn_v, bv[1][...], preferred_element_type=F32)
        return (cqn_v, jnp.stack([_rope(qh, ctb, sub, sdb) for qh in heads_of(ql)])), ()

    cqn, q = _rowwise("q_path", k_q_path, ns, [cq_row, _row(cos_q), _row(sup_q), _row(sdn_q)], [mla_g_q, wq_p],
                      [((S, Q_RANK), BF16, None), ((HEADS, S, HEAD_PAD), BF16, None)])

    def k_kv_path(i, rv, bv):
        ckv_v, krp, ctb, sub, sdb = rv
        ckvn_v = _rms(ckv_v, bv[0][...]).astype(BF16)
        kl = jnp.dot(ckvn_v, bv[1][...], preferred_element_type=F32)
        vl = jnp.dot(ckvn_v, bv[2][...], preferred_element_type=F32)
        kr = _rope(krp, ctb, sub, sdb)
        return (ckvn_v, jnp.stack([kh + kr for kh in heads_of(kl)]), vl), ()

    ckvn, k, v_lin = _rowwise(
        "kv_path", k_kv_path, nt,
        [ckv_row, _row(proj, None, (HEAD_PAD, OFF_KR // HEAD_PAD)), _row(cos_k), _row(sup_k), _row(sdn_k)],
        [mla_g_kv, wk_p, wv_f],
        [((T, KV_RANK), BF16, None), ((HEADS, T, HEAD_PAD), BF16, None), ((T, HEADS * VDIM), BF16, None)])
    v = v_lin.reshape(T, HEADS, VDIM).transpose(1, 0, 2)

    o, lse, late_g = _attn_fwd(q, k, v, [W[n][0].astype(BF16) for n in late], ['ag'] * 3)
    w_out_f = late_g[0].reshape(D, D)
    w_up_t = _ff_permute("w_up_to_tiles", _cols_from_shards(late_g[1]), True)
    w_down_f = late_g[2].reshape(D_FF, D)
    o_t = o.transpose(1, 0, 2).reshape(S, HEADS * VDIM)

    def lru_out_fn(hf, hr, gr):
        return (hf + hr) * jax.nn.gelu(gr)

    def k_mix_in(i, rv, bv):
        return (jnp.concatenate([lru_out_fn(rv[0], rv[1], rv[2]), rv[3]], axis=1),), ()

    gr_row = _row(proj, lat, (LRU_W, OFF_GR // LRU_W))
    y_in = _rowwise("mix_in", k_mix_in, ns, [_row(h0, lat), _row(h1), gr_row, _row(o_t)], [],
                    [((S, D), BF16, None)])[0]
    y = _mm("out_proj", y_in, w_out_f, 'nn')

    def post_mix_fn(xv, yv, gt, g):
        return xv + gt * _rms(yv, g)

    def k_post_mix(i, rv, bv):
        ml = bv[0]
        x1v = post_mix_fn(rv[0], rv[1], ml[2:3, :], bv[1][...])
        return (x1v, pre_fn(x1v, bv[2][...], ml[3:4, :], ml[4:5, :])), ()

    x1, h2 = _rowwise("post_mix", k_post_mix, ns, [_row(xs), _row(y)], [mod_lat, g_post_mix, g_pre_ffn],
                      [((S, D), F32, None), ((S, D), BF16, None)])
    up = _mm("ffn_up", h2, w_up_t, 'nn', out_dtype=BF16)
    act = _ffn_mid_fwd(up, ffn_cw_t, ffn_cb_t)
    f = _mm("ffn_down", act, w_down_f, 'nn')

    def loss_fn(x1v, fv, gt, g, tg):
        x2 = x1v + gt * _rms(fv, g)
        err = x2 - tg
        return 0.5 * jnp.sum(jnp.mean(err * err, axis=-1))

    def k_loss(i, rv, bv):
        gtb, gb = _bc(bv[0][5:6, :], TM), _bc(bv[1][...], TM)
        val, (dx1v, dfv, dgt, dg) = jax.value_and_grad(loss_fn, argnums=(0, 1, 2, 3))(rv[0], rv[1], gtb, gb, rv[2])
        return (dx1v, dfv), (jnp.full((1, LANES), val, F32), _rs(dgt), _rs(dg))

    dx1_a, df, loss_acc, d_gt2, d_g_post_ffn = _rowwise(
        "loss_bwd", k_loss, ns, [_row(x1), _row(f), _row(tgt)], [mod_lat, g_post_ffn],
        [((S, D), F32, None), ((S, D), BF16, None)], [(1, LANES), (1, D), (1, D)])

    d_act = _mm("ffn_down_dx", df, w_down_f, 'nt', out_dtype=BF16)
    d_w_down = _mm("ffn_down_dw", act, df, 'tn')
    d_up, d_ffn_cw_t, d_ffn_cb_t = _ffn_mid_bwd(up, d_act, ffn_cw_t, ffn_cb_t)
    d_ffn_cw, d_ffn_cb = _ff_from_tiles(d_ffn_cw_t), _ff_from_tiles(d_ffn_cb_t)
    d_h2 = _mm("ffn_up_dx", d_up, w_up_t, 'nt')
    d_w_up = _ff_permute("d_w_up_from_tiles", _mm("ffn_up_dw", h2, d_up, 'tn'), False)

    def k_pre_ffn_bwd(i, rv, bv):
        ml = bv[0]
        gb, shb, scb = _bc(bv[1][...], TM), _bc(ml[3:4, :], TM), _bc(ml[4:5, :], TM)
        _, pull = jax.vjp(pre_fn, rv[0], gb, shb, scb)
        dxv, dg, dsh, dsc = pull(rv[1])
        return (rv[2] + dxv,), (_rs(dg), _rs(dsh), _rs(dsc))

    dx1, d_g_pre_ffn, d_sh2, d_sc2 = _rowwise(
        "pre_ffn_bwd", k_pre_ffn_bwd, ns, [_row(x1), _row(d_h2), _row(dx1_a)], [mod_lat, g_pre_ffn],
        [((S, D), F32, None)], [(1, D), (1, D), (1, D)])

    def k_post_mix_bwd(i, rv, bv):
        gtb, gb = _bc(bv[0][2:3, :], TM), _bc(bv[1][...], TM)
        _, pull = jax.vjp(post_mix_fn, rv[0], rv[1], gtb, gb)
        _, dyv, dgt, dg = pull(rv[2])
        return (dyv,), (_rs(dgt), _rs(dg))

    dy, d_gt1, d_g_post_mix = _rowwise(
        "post_mix_bwd", k_post_mix_bwd, ns, [_row(xs), _row(y), _row(dx1)], [mod_lat, g_post_mix],
        [((S, D), BF16, None)], [(1, D), (1, D)])
    d_y_in = _mm("out_proj_dx", dy, w_out_f, 'nt')
    d_w_out = _mm("out_proj_dw", y_in, dy, 'tn')

    def k_lru_out_bwd(i, rv, bv):
        _, pull = jax.vjp(lru_out_fn, rv[0], rv[1], rv[2])
        dhf, _, dgr = pull(rv[3])
        return (dhf, dgr), ()

    d_hsum, d_gr = _rowwise("lru_out_bwd", k_lru_out_bwd, ns,
                            [_row(h0, lat), _row(h1), gr_row, _row(d_y_in, None, (LRU_W, 0))], [],
                            [((S, LRU_W), F32, None), ((S, LRU_W), F32, None)])

    do_h = d_y_in[:, LRU_W:].reshape(S, HEADS, VDIM).transpose(1, 0, 2)

    def k_delta(i, rv, bv):
        return (jnp.sum(rv[0] * rv[1], axis=-1, keepdims=True), rv[1]), ()

    delta, do_b = _rowwise("attn_delta", k_delta, ns, [_row(o), _row(do_h)], [],
                           [((HEADS, S, 1), F32, None), ((HEADS, S, VDIM), BF16, None)])
    late_sends = [d_w_out.reshape(N_DEV, D // N_DEV, D), _cols_to_shards(d_w_up),
                  d_w_down.reshape(N_DEV, D_FF // N_DEV, D)]
    dq, dk, dv, late_recv = _attn_bwd(q, k, v, do_b, lse.reshape(HEADS, 1, S), delta.reshape(HEADS, 1, S),
                                      late_sends, ['a2a'] * 3)

    def rms_bwd(xv, g, dy):
        _, pull = jax.vjp(_rms, xv, _bc(g, TM))
        dxv, dg = pull(dy)
        return dxv, _rs(dg)

    def k_q_path_bwd(i, rv, bv):
        dqv, ctb, sub, sdb, cqv = rv
        dql = jnp.concatenate([_rope_t(dqv[hh] * MLA_SCALE, ctb, sub, sdb) for hh in range(HEADS)], axis=1).astype(BF16)
        d_cqn_v = lax.dot_general(dql, bv[1][...], NT_DIMS, preferred_element_type=F32)
        dxv, dg = rms_bwd(cqv, bv[0][...], d_cqn_v)
        return (dql, dxv), (dg,)

    dq_lin, d_cq, d_g_q = _rowwise(
        "q_path_bwd", k_q_path_bwd, ns, [_row(dq), _row(cos_q), _row(sup_q), _row(sdn_q), cq_row], [mla_g_q, wq_p],
        [((S, HEADS * HEAD_PAD), BF16, None), ((S, Q_RANK), F32, None)], [(1, Q_RANK)])
    d_wq_p = _mm("q_proj_dw", cqn, dq_lin, 'tn')

    dv_lin = dv.transpose(1, 0, 2).reshape(T, HEADS * VDIM).astype(BF16)

    def k_kv_path_bwd(i, rv, bv):
        dkv_, ctb, sub, sdb, dvl, ckv_v = rv
        tot = dkv_[0]
        for hh in range(1, HEADS):
            tot = tot + dkv_[hh]
        lane = lax.broadcasted_iota(jnp.int32, tot.shape, 1)
        tot = jnp.where((lane >= NOPE) & (lane < QK), tot, 0.0)
        dkl = jnp.concatenate([dkv_[hh] for hh in range(HEADS)], axis=1).astype(BF16)
        d_ckvn_v = (lax.dot_general(dkl, bv[1][...], NT_DIMS, preferred_element_type=F32)
                    + lax.dot_general(dvl, bv[2][...], NT_DIMS, preferred_element_type=F32))
        dxv, dg = rms_bwd(ckv_v, bv[0][...], d_ckvn_v)
        return (dkl, _rope_t(tot, ctb, sub, sdb), dxv), (dg,)

    dk_lin, d_krp, d_ckv, d_g_kv = _rowwise(
        "kv_path_bwd", k_kv_path_bwd, nt,
        [_row(dk), _row(cos_k), _row(sup_k), _row(sdn_k), _row(dv_lin), ckv_row], [mla_g_kv, wk_p, wv_f],
        [((T, HEADS * HEAD_PAD), BF16, None), ((T, HEAD_PAD), F32, None), ((T, KV_RANK), F32, None)], [(1, KV_RANK)])
    d_wk_p = _mm("k_proj_dw", ckvn, dk_lin, 'tn')
    d_wv = _mm("v_proj_dw", ckvn, dv_lin, 'tn')

    zero_row = jnp.zeros((1, LRU_W), F32)
    lam0 = _scan("lru_scan_f_bwd", a0, d_hsum, True, shifted=True, u_off=CN)
    lam1 = _scan("lru_scan_r_bwd", a1, d_hsum, False, shifted=True, u_off=0)
    hprev0 = jnp.concatenate([zero_row, h0[:-1]])
    hprev1 = jnp.concatenate([h1[1:], zero_row])

    def k_gates_bwd(i, rv, bv):
        xv, l0, hp0, l1, hp1 = rv
        wg, bg, lamv = [b[...] for b in bv]
        xb = xv.astype(BF16)
        z = jnp.dot(xb, wg, preferred_element_type=F32) + bg
        spb = _bc(_softplus(-lamv), TM)
        _, pull = jax.vjp(_gates, z, xv, spb)
        dz, dxv, dsp = pull((l0 * hp0, l0, l1 * hp1, l1))
        dzb = dz.astype(BF16)
        dxv = dxv + lax.dot_general(dzb, wg, NT_DIMS, preferred_element_type=F32)
        dwg = lax.dot_general(xb, dzb, (((0,), (0,)), ((), ())), preferred_element_type=F32)
        dlam = -_rs(dsp) * jax.nn.sigmoid(-lamv)
        return (dxv,), (dwg, _rs(dz), dlam)

    d_xcv, d_w_gate, d_b_gate, d_lam = _rowwise(
        "lru_gates_bwd", k_gates_bwd, nt,
        [_row(xcv), _row(lam0), _row(hprev0), _row(lam1, swp), _row(hprev1, swp)], [w_gate, b_gate, lam_row],
        [((T, LRU_W), F32, None)], [(LRU_W, 4 * LRU_W), (1, 4 * LRU_W), (1, 2 * LRU_W)])
    d_xr, d_lru_cw, d_lru_cb = _dwconv_bwd("lru_conv_bwd", proj, d_xcv, lru_cw, LRU_CONV_LEFT, (0, ct))

    def k_dproj(i, rv, bv):
        is_lat = i >= ct
        return (jnp.concatenate([rv[0], jnp.where(is_lat, rv[1], 0.0), jnp.where(is_lat, rv[2], 0.0), rv[3], rv[4]],
                                axis=1),), ()

    d_proj = _rowwise("d_proj", k_dproj, nt,
                      [_row(d_xr), _row(d_gr, lat_or_0), _row(d_cq, lat_or_0), _row(d_ckv), _row(d_krp)], [],
                      [((T, IN_W_PAD), BF16, None)])[0]
    d_w_in_p = _mm("in_proj_dw", h_pre, d_proj, 'tn')

    d_b_a = jnp.stack([d_b_gate[0, 0:LRU_W], d_b_gate[0, 2 * LRU_W:3 * LRU_W]])
    d_b_x = jnp.stack([d_b_gate[0, LRU_W:2 * LRU_W], d_b_gate[0, 3 * LRU_W:]])
    d_w_in = jnp.concatenate([d_w_in_p[:, :OFF_KR], d_w_in_p[:, OFF_KR + NOPE:OFF_KR + QK]], axis=1)
    d_w_uq = d_wq_p.reshape(Q_RANK, HEADS, HEAD_PAD)[:, :, :QK].reshape(Q_RANK, HEADS * QK)
    d_w_ukv = jnp.concatenate([d_wk_p.reshape(KV_RANK, HEADS, HEAD_PAD)[:, :, :NOPE],
                               d_wv.reshape(KV_RANK, HEADS, VDIM)], axis=2).reshape(KV_RANK, HEADS * (NOPE + VDIM))
    small_full = dict(lru_conv_w=d_lru_cw, lru_b_a=d_b_a, lru_b_x=d_b_x, lru_lambda=d_lam.reshape(2, LRU_W),
                      ffn_conv_w=d_ffn_cw)
    small_sh = jnp.concatenate([_cols_to_shards(small_full[n]).reshape(N_DEV, -1) for n in SMALL_SHARDED], axis=1)
    n_sh = small_sh.shape[1]
    sh_rows = _pack_rows(n_sh)
    small_sh = jnp.pad(small_sh, ((0, 0), (0, sh_rows * LANES - n_sh))).reshape(N_DEV, sh_rows, LANES)
    early_sends = [_cols_to_shards(d_w_in), _cols_to_shards(d_w_uq), _cols_to_shards(d_w_ukv), small_sh]

    def k_pre_bwd(i, rv, bv):
        g, ml, mc = bv[0][...], bv[1], bv[2]
        sh, sc = sel_mod(i, ml, mc, 0)
        d_h = lax.dot_general(rv[2], bv[3][...], NT_DIMS, preferred_element_type=F32)
        _, pull = jax.vjp(pre_fn, jnp.where(i < ct, rv[0], rv[1]), _bc(g, TM), _bc(sh, TM), _bc(sc, TM))
        dxv, dg, dsh, dsc = pull(d_h)
        is_lat = i >= ct
        dsh, dsc = _rs(dsh), _rs(dsc)
        zero = jnp.zeros_like(dsh)
        return ((dxv + rv[3],),
                (_rs(dg), jnp.where(is_lat, dsh, zero), jnp.where(is_lat, dsc, zero),
                 jnp.where(is_lat, zero, dsh), jnp.where(is_lat, zero, dsc)))

    (dxl, d_g_pre_mix, d_sh1, d_sc1, d_csh1, d_csc1), early_recv = _rowwise(
        "pre_mix_bwd", k_pre_bwd, nt, xa_rows + [_row(d_proj), _row(dx1, lat_or_0)],
        [g_pre_mix, mod_lat, mod_ctx, w_in_p], [((S, D), F32, lat_or_0)], [(1, D)] * 5,
        ride_arrs=early_sends, ride_modes=['a2a'] * 4)
    grad_x = dxl[None]

    zrow = jnp.zeros((1, D), F32)
    d_mod_lat = jnp.concatenate([d_sh1, d_sc1, d_gt1, d_sh2, d_sc2, d_gt2], axis=1)
    d_mod_ctx = jnp.concatenate([d_csh1, d_csc1, zrow, zrow, zrow, zrow], axis=1)
    loss_row = jnp.pad(loss_acc, ((0, 0), (0, N_MOD * D - LANES)))
    d_mod_mine = jnp.concatenate([d_mod_lat, d_mod_ctx, loss_row, jnp.zeros((SUBLANES - 3, N_MOD * D), F32)])

    def diag_blocks(dw):
        return jnp.stack([dw[hh * 64:(hh + 1) * 64, hh * 64:(hh + 1) * 64] for hh in range(LRU_HEADS)])

    d_lru_w_a = jnp.stack([diag_blocks(d_w_gate[:, 0:LRU_W]), diag_blocks(d_w_gate[:, 2 * LRU_W:3 * LRU_W])])[None]
    d_lru_w_x = jnp.stack([diag_blocks(d_w_gate[:, LRU_W:2 * LRU_W]), diag_blocks(d_w_gate[:, 3 * LRU_W:])])[None]
    rep_part = dict(b_mod=d_mod_lat + d_mod_ctx, g_pre_mix=d_g_pre_mix, g_post_mix=d_g_post_mix,
                    g_pre_ffn=d_g_pre_ffn, g_post_ffn=d_g_post_ffn, lru_conv_b=d_lru_cb, lru_w_a=d_lru_w_a,
                    lru_w_x=d_lru_w_x, mla_g_q=d_g_q, mla_g_kv=d_g_kv, ffn_conv_b=d_ffn_cb)
    rep_names = [n for n in REPLICATED if n != 'c_ctx']
    rep_shapes = [W[n].shape for n in rep_names]
    rep_rows = -(-_pack_rows(sum(W[n].size for n in rep_names)) // ROW_TILE) * ROW_TILE
    rep_loc = _pack([rep_part[n] for n in rep_names], rep_rows)
    d_mod_all, rep_all = _exchange("gather_dmod", [d_mod_mine, rep_loc], ['ag'] * 2)
    loss = jnp.sum(d_mod_all[:, 2, 0])
    dm_lat_loc = lax.dynamic_slice(d_mod_all[:, 0], (0, me * ncol), (N_DEV, ncol))
    dm_ctx_loc = lax.dynamic_slice(d_mod_all[:, 1], (0, me * ncol), (N_DEV, ncol))

    def mod_bwd(c16_r, w_r, dml_r, dmc_r):
        c16_v = c16_r[...]
        sig = jax.nn.sigmoid(c16_v)
        sl = c16_v * sig
        dctx = dmc_r[0:1, :]
        for d in range(1, N_DEV):
            dctx = dctx + dmc_r[d:d + 1, :]
        row = lax.broadcasted_iota(jnp.int32, (2 * SUBLANES, ncol), 0)
        dm16 = dml_r[...] + jnp.where(row == N_DEV, _bc(dctx, 2 * SUBLANES), 0.0)
        dw = lax.dot_general(sl.astype(BF16), dm16.astype(BF16), (((0,), (0,)), ((), ())), preferred_element_type=F32)
        dsl = lax.dot_general(dm16.astype(BF16), w_r[...].astype(BF16), NT_DIMS, preferred_element_type=F32)
        dc = dsl * (sig * (1.0 + c16_v * (1.0 - sig)))
        return dw, dc

    dm_lat16 = jnp.concatenate([dm_lat_loc, jnp.zeros((2 * SUBLANES - N_DEV, ncol), F32)])
    g_w_mod, dc16 = _single("mod_bwd", mod_bwd, [c16, w_mod[0], dm_lat16, dm_ctx_loc],
                            [((D, ncol), F32), ((2 * SUBLANES, D), F32)])
    d_c_ctx_part = dc16[N_DEV]

    (c_ctx_all,) = _exchange("gather_d_c_ctx", [d_c_ctx_part.reshape(SUBLANES, D // SUBLANES)], ['ag'])
    big_parts = list(early_recv[:3]) + list(late_recv)

    res = {}

    def adam(name, w2, m2, v2, parts):
        return _adamw("adamw_" + name, w2, m2, v2, parts)

    for n, parts in zip(big, big_parts):
        shp = W[n].shape
        outs = adam(n, W[n][0], M[n][0], V[n][0], parts)
        res[n] = [o_.reshape(shp) for o_ in outs]
    outs = adam('w_mod', w_mod[0], m_w_mod[0], v_w_mod[0], g_w_mod[None])
    res['w_mod'] = [o_.reshape(w_mod.shape) for o_ in outs]

    sh_shapes = [W[n].shape for n in SMALL_SHARDED]
    pk = lambda dct: _pack([dct[n] for n in SMALL_SHARDED], sh_rows)
    outs = adam('small_sharded', pk(W), pk(M), pk(V), early_recv[3])
    for n, vals in zip(SMALL_SHARDED, zip(*[_unpack(o_, sh_shapes) for o_ in outs])):
        res[n] = list(vals)

    pr = lambda dct: _pack([dct[n] for n in rep_names], rep_rows)
    outs = adam('replicated', pr(W), pr(M), pr(V), rep_all)
    for n, vals in zip(rep_names, zip(*[_unpack(o_, rep_shapes) for o_ in outs])):
        res[n] = list(vals)
    as_tile = lambda a: a.reshape(SUBLANES, D // SUBLANES)
    outs = adam('c_ctx', as_tile(c_ctx), as_tile(m_c_ctx), as_tile(v_c_ctx), c_ctx_all)
    res['c_ctx'] = [o_.reshape(c_ctx.shape) for o_ in outs]

    return (loss, grad_x, *[res[n][0] for n in WEIGHTS], *[res[n][1] for n in WEIGHTS],
            *[res[n][2] for n in WEIGHTS], *[res[n][3] for n in WEIGHTS])
```

```python
import functools
import math

import jax
import jax.numpy as jnp
from jax import lax
from jax.experimental import pallas as pl
from jax.experimental.pallas import tpu as pltpu

F32 = jnp.float32
BF16 = jnp.bfloat16
MESH = pl.DeviceIdType.MESH

N_DEV = 8
ROW_TILE = 256
SUBLANES = 8
LANES = 128
VMEM_LIMIT = 56 * 1024 * 1024

D_MODEL = 1024
LRU_W = 512
LRU_HEADS = 8
LRU_CONV_K = 4
LRU_CONV_LEFT = 2
LRU_C = 8.0
HEADS = 8
NOPE = 64
ROPE = 32
VDIM = 64
QK = NOPE + ROPE
HEAD_PAD = 128
Q_RANK = 256
KV_RANK = 128
MLA_SCALE = QK ** -0.5
ROPE_PAIRS = ROPE // 4
ROPE_BASE = 10000.0
GRID_W = 64
D_FF = 2816
FFN_CONV_K = 3
FFN_CONV_LEFT = 1
N_MOD = 6
EPS = 1e-6
IN_W = 2 * LRU_W + Q_RANK + KV_RANK + ROPE
IN_W_PAD = 2 * LRU_W + Q_RANK + KV_RANK + HEAD_PAD
OFF_GR, OFF_CQ, OFF_CKV, OFF_KR = LRU_W, 2 * LRU_W, 2 * LRU_W + Q_RANK, 2 * LRU_W + Q_RANK + KV_RANK

ADAM_LR, ADAM_B1, ADAM_B2, ADAM_EPS, ADAM_WD, ADAM_STEP = 0.001, 0.9, 0.999, 1e-08, 0.01, 10

WEIGHTS = ['c_ctx', 'w_mod', 'b_mod', 'g_pre_mix', 'g_post_mix', 'g_pre_ffn', 'g_post_ffn', 'w_in', 'lru_conv_w',
           'lru_conv_b', 'lru_w_a', 'lru_b_a', 'lru_w_x', 'lru_b_x', 'lru_lambda', 'mla_g_q', 'mla_w_uq', 'mla_g_kv',
           'mla_w_ukv', 'w_out', 'ffn_w_up', 'ffn_conv_w', 'ffn_conv_b', 'ffn_w_down']
REPLICATED = ['c_ctx', 'b_mod', 'g_pre_mix', 'g_post_mix', 'g_pre_ffn', 'g_post_ffn', 'lru_conv_b', 'lru_w_a',
              'lru_w_x', 'mla_g_q', 'mla_g_kv', 'ffn_conv_b']
SMALL_SHARDED = ['lru_conv_w', 'lru_b_a', 'lru_b_x', 'lru_lambda', 'ffn_conv_w']


def _pick(d, prefs):
    for p in prefs:
        if d % p == 0:
            return p
    return d


def _params(sem=None):
    return pltpu.CompilerParams(dimension_semantics=sem, vmem_limit_bytes=VMEM_LIMIT)


MM_TILES = (1024, 1408, 768, 512, 256, 128)


def _mm(name, a, b, mode, out_dtype=F32):
    if mode == 'nn':
        (m, k), (_, n) = a.shape, b.shape
    elif mode == 'nt':
        (m, k), (n, _) = a.shape, b.shape
    else:
        (k, m), (_, n) = a.shape, b.shape
    tm = _pick(m, MM_TILES)
    tn = _pick(n, MM_TILES)
    tk = _pick(k, MM_TILES)
    nk = k // tk
    if mode == 'nn':
        a_spec = pl.BlockSpec((tm, tk), lambda i, j, kk: (i, kk))
        b_spec = pl.BlockSpec((tk, tn), lambda i, j, kk: (kk, j))
        dn = (((1,), (0,)), ((), ()))
    elif mode == 'nt':
        a_spec = pl.BlockSpec((tm, tk), lambda i, j, kk: (i, kk))
        b_spec = pl.BlockSpec((tn, tk), lambda i, j, kk: (j, kk))
        dn = (((1,), (1,)), ((), ()))
    else:
        a_spec = pl.BlockSpec((tk, tm), lambda i, j, kk: (kk, i))
        b_spec = pl.BlockSpec((tk, tn), lambda i, j, kk: (kk, j))
        dn = (((0,), (0,)), ((), ()))

    def body(a_ref, b_ref, o_ref, acc_ref):
        kk = pl.program_id(2)

        @pl.when(kk == 0)
        def _():
            acc_ref[...] = jnp.zeros_like(acc_ref)

        acc_ref[...] += lax.dot_general(a_ref[...].astype(BF16), b_ref[...].astype(BF16), dn,
                                        preferred_element_type=F32)

        @pl.when(kk == nk - 1)
        def _():
            o_ref[...] = acc_ref[...].astype(o_ref.dtype)

    return pl.pallas_call(
        body, name=name, grid=(m // tm, n // tn, nk),
        in_specs=[a_spec, b_spec], out_specs=pl.BlockSpec((tm, tn), lambda i, j, kk: (i, j)),
        out_shape=jax.ShapeDtypeStruct((m, n), out_dtype),
        scratch_shapes=[pltpu.VMEM((tm, tn), F32)],
        compiler_params=_params(("parallel", "parallel", "arbitrary")),
    )(a, b)


def _row(a, idx=None, col=None):
    return dict(a=a, idx=idx, col=col)


def _rowwise(name, fn, n_tiles, rows, bcast, out_rows, out_acc=(), tm=ROW_TILE, ride_arrs=(), ride_modes=()):
    in_specs = []
    for r in rows:
        a, idx, col = r['a'], r['idx'] or (lambda i: i), r['col']
        if a.ndim == 2:
            w, ci = col if col else (a.shape[1], 0)
            in_specs.append(pl.BlockSpec((tm, w), lambda i, idx=idx, ci=ci: (idx(i), ci)))
        else:
            in_specs.append(pl.BlockSpec((a.shape[0], tm, a.shape[2]), lambda i, idx=idx: (0, idx(i), 0)))
    for b in bcast:
        in_specs.append(pl.BlockSpec(b.shape, lambda i, nd=b.ndim: (0,) * nd))
    out_specs, out_shape = [], []
    for shape, dtype, idx in out_rows:
        idx = idx or (lambda i: i)
        if len(shape) == 2:
            out_specs.append(pl.BlockSpec((tm, shape[1]), lambda i, idx=idx: (idx(i), 0)))
        else:
            out_specs.append(pl.BlockSpec((shape[0], tm, shape[2]), lambda i, idx=idx: (0, idx(i), 0)))
        out_shape.append(jax.ShapeDtypeStruct(shape, dtype))
    for shape in out_acc:
        out_specs.append(pl.BlockSpec(shape, lambda i, nd=len(shape): (0,) * nd))
        out_shape.append(jax.ShapeDtypeStruct(shape, F32))
    nr, nb, no, na, n = len(rows), len(bcast), len(out_rows), len(out_acc), len(ride_arrs)
    any_spec = pl.BlockSpec(memory_space=pl.ANY)

    def body(*refs):
        i = pl.program_id(0)
        finish = _riding_exchange(refs, nr + nb, no + na, n, ride_modes, i == 0, i == n_tiles - 1) if n else None
        rvals = [r[...] for r in refs[:nr]]
        bvals = list(refs[nr:nr + nb])
        o_rows, o_acc = fn(i, rvals, bvals)
        outs = refs[nr + nb + n:]
        for ref, v in zip(outs[:no], o_rows):
            ref[...] = v.astype(ref.dtype)
        acc_refs = outs[no:no + na]
        if acc_refs:
            @pl.when(i == 0)
            def _():
                for ref in acc_refs:
                    ref[...] = jnp.zeros_like(ref)
            for ref, v in zip(acc_refs, o_acc):
                ref[...] += v
        if n:
            finish()

    res = pl.pallas_call(
        body, name=name, grid=(n_tiles,), in_specs=in_specs + [any_spec] * n, out_specs=out_specs + [any_spec] * n,
        out_shape=out_shape + _exchange_shapes(ride_arrs, ride_modes),
        scratch_shapes=_exchange_sems(n) if n else [],
        compiler_params=pltpu.CompilerParams(dimension_semantics=("arbitrary",), vmem_limit_bytes=VMEM_LIMIT,
                                             has_side_effects=bool(n)),
    )(*[r['a'] for r in rows], *bcast, *ride_arrs)
    return (res[:no + na], res[no + na:]) if n else res


def _single(name, fn, ins, out_shapes):
    def body(*refs):
        outs = fn(*refs[:len(ins)])
        for ref, v in zip(refs[len(ins):], outs):
            ref[...] = v.astype(ref.dtype)

    return pl.pallas_call(
        body, name=name,
        in_specs=[pl.BlockSpec(memory_space=pltpu.VMEM)] * len(ins),
        out_specs=[pl.BlockSpec(memory_space=pltpu.VMEM)] * len(out_shapes),
        out_shape=[jax.ShapeDtypeStruct(s, d) for s, d in out_shapes],
        compiler_params=_params(),
    )(*ins)


def _bc(p, n):
    return jnp.broadcast_to(p, (n, p.shape[-1]))


def _rs(g):
    return jnp.sum(g, axis=0, keepdims=True)


def _rms(x, g):
    return x * lax.rsqrt(jnp.mean(x * x, axis=-1, keepdims=True) + EPS) * g


def _conv_specs(r, cw, tm, halo=SUBLANES):
    th = tm // halo
    last = r // halo - 1
    prev = pl.BlockSpec((halo, cw), lambda c, i: (jnp.maximum(i * th - 1, 0), c))
    cur = pl.BlockSpec((tm, cw), lambda c, i: (i, c))
    nxt = pl.BlockSpec((halo, cw), lambda c, i: (jnp.minimum((i + 1) * th, last), c))
    return [prev, cur, nxt]


def _fill_ext(ext_ref, prev_ref, cur_ref, next_ref, i, n_tiles, seg_starts, tm):
    prev_ok = functools.reduce(jnp.logical_and, [i != s for s in seg_starts])
    next_ok = functools.reduce(jnp.logical_and, [i + 1 != s for s in seg_starts] + [i + 1 < n_tiles])
    ext_ref[0:SUBLANES, :] = jnp.where(prev_ok, prev_ref[...].astype(F32), 0.0)
    ext_ref[SUBLANES:SUBLANES + tm, :] = cur_ref[...].astype(F32)
    ext_ref[SUBLANES + tm:, :] = jnp.where(next_ok, next_ref[...].astype(F32), 0.0)


def _dwconv_fwd(name, x, w, b, left, seg_starts, cw=512, tm=ROW_TILE):
    r, c = x.shape[0], w.shape[1]
    kw = w.shape[0]
    n_tiles = r // tm

    def body(prev_ref, cur_ref, next_ref, w_ref, b_ref, o_ref, ext_ref):
        i = pl.program_id(1)
        _fill_ext(ext_ref, prev_ref, cur_ref, next_ref, i, n_tiles, seg_starts, tm)
        out = jnp.broadcast_to(b_ref[...], (tm, cw))
        for k in range(kw):
            out = out + ext_ref[pl.ds(SUBLANES + k - left, tm), :] * w_ref[k:k + 1, :]
        o_ref[...] = out

    return pl.pallas_call(
        body, name=name, grid=(c // cw, n_tiles),
        in_specs=_conv_specs(r, cw, tm) + [pl.BlockSpec((kw, cw), lambda c_, i: (0, c_)),
                                           pl.BlockSpec((1, cw), lambda c_, i: (0, c_))],
        out_specs=pl.BlockSpec((tm, cw), lambda c_, i: (i, c_)),
        out_shape=jax.ShapeDtypeStruct((r, c), F32),
        scratch_shapes=[pltpu.VMEM((tm + 2 * SUBLANES, cw), F32)],
        compiler_params=_params(("parallel", "arbitrary")),
    )(x, x, x, w, b)


def _dwconv_bwd(name, x, dy, w, left, seg_starts, out_dtype=F32, cw=512, tm=ROW_TILE):
    r, c = dy.shape
    kw = w.shape[0]
    n_tiles = r // tm

    def body(xp, xc, xn, dp, dc, dn, w_ref, dx_ref, dw_ref, db_ref, xe_ref, de_ref):
        i = pl.program_id(1)
        _fill_ext(xe_ref, xp, xc, xn, i, n_tiles, seg_starts, tm)
        _fill_ext(de_ref, dp, dc, dn, i, n_tiles, seg_starts, tm)
        dyc = dc[...].astype(F32)
        dx = jnp.zeros((tm, cw), F32)
        dws = []
        for k in range(kw):
            dx = dx + de_ref[pl.ds(SUBLANES - k + left, tm), :] * w_ref[k:k + 1, :]
            dws.append(jnp.sum(dyc * xe_ref[pl.ds(SUBLANES + k - left, tm), :], axis=0, keepdims=True))
        dx_ref[...] = dx.astype(dx_ref.dtype)

        @pl.when(i == 0)
        def _():
            dw_ref[...] = jnp.zeros_like(dw_ref)
            db_ref[...] = jnp.zeros_like(db_ref)

        for k in range(kw):
            dw_ref[k:k + 1, :] += dws[k]
        db_ref[...] += jnp.sum(dyc, axis=0, keepdims=True)

    return pl.pallas_call(
        body, name=name, grid=(c // cw, n_tiles),
        in_specs=_conv_specs(r, cw, tm) + _conv_specs(r, cw, tm) + [pl.BlockSpec((kw, cw), lambda c_, i: (0, c_))],
        out_specs=[pl.BlockSpec((tm, cw), lambda c_, i: (i, c_)),
                   pl.BlockSpec((kw, cw), lambda c_, i: (0, c_)),
                   pl.BlockSpec((1, cw), lambda c_, i: (0, c_))],
        out_shape=[jax.ShapeDtypeStruct((r, c), out_dtype), jax.ShapeDtypeStruct((kw, c), F32),
                   jax.ShapeDtypeStruct((1, c), F32)],
        scratch_shapes=[pltpu.VMEM((tm + 2 * SUBLANES, cw), F32), pltpu.VMEM((tm + 2 * SUBLANES, cw), F32)],
        compiler_params=_params(("parallel", "arbitrary")),
    )(x, x, x, dy, dy, dy, w)


FF_TILE = 256
FF_HALO = 16
FF_STRIP = 32


def _ffn_fill(ext_ref, prev_ref, cur_ref, next_ref, i, n_tiles, tm):
    ext_ref[0:FF_HALO, :] = jnp.where(i > 0, prev_ref[...].astype(F32), 0.0)
    ext_ref[FF_HALO:FF_HALO + tm, :] = cur_ref[...].astype(F32)
    ext_ref[FF_HALO + tm:, :] = jnp.where(i + 1 < n_tiles, next_ref[...].astype(F32), 0.0)


def _ffn_conv(ext_ref, w_ref, b_ref, start, rows):
    out = jnp.broadcast_to(b_ref[...], (rows, 2 * FF_TILE))
    for k in range(FFN_CONV_K):
        out = out + ext_ref[pl.ds(start + k - FFN_CONV_LEFT, rows), :] * w_ref[k:k + 1, :]
    return out


def _ffn_mid_fwd(up, w, b):
    s, c2 = up.shape
    tm = _pick(s, (2 * ROW_TILE, ROW_TILE))
    n_tiles = s // tm
    cw = 2 * FF_TILE

    def body(prev_ref, cur_ref, next_ref, w_ref, b_ref, o_ref, ext_ref):
        i = pl.program_id(1)
        _ffn_fill(ext_ref, prev_ref, cur_ref, next_ref, i, n_tiles, tm)
        for r0 in range(0, tm, FF_STRIP):
            upc = _ffn_conv(ext_ref, w_ref, b_ref, FF_HALO + r0, FF_STRIP)
            uv, gv = upc[:, :FF_TILE], upc[:, FF_TILE:]
            o_ref[r0:r0 + FF_STRIP, :] = (gv * jax.nn.sigmoid(gv) * uv).astype(o_ref.dtype)

    return pl.pallas_call(
        body, name="ffn_mid", grid=(c2 // cw, n_tiles),
        in_specs=_conv_specs(s, cw, tm, FF_HALO) + [pl.BlockSpec((FFN_CONV_K, cw), lambda c_, i: (0, c_)),
                                                   pl.BlockSpec((1, cw), lambda c_, i: (0, c_))],
        out_specs=pl.BlockSpec((tm, FF_TILE), lambda c_, i: (i, c_)),
        out_shape=jax.ShapeDtypeStruct((s, c2 // 2), BF16),
        scratch_shapes=[pltpu.VMEM((tm + 2 * FF_HALO, cw), F32)],
        compiler_params=_params(("parallel", "arbitrary")),
    )(up, up, up, w, b)


def _ffn_mid_bwd(up, d_act, w, b, tm=ROW_TILE):
    s, c2 = up.shape
    n_tiles = s // tm
    cw = 2 * FF_TILE
    h8 = SUBLANES

    def gate_bwd(upc, dact):
        uv, gv = upc[:, :FF_TILE], upc[:, FF_TILE:]
        sg = jax.nn.sigmoid(gv)
        return jnp.concatenate([dact * (gv * sg), dact * uv * (sg * (1.0 + gv * (1.0 - sg)))], axis=1)

    def body(up_p, up_c, up_n, da_p, da_c, da_n, w_ref, b_ref, dup_ref, dw_ref, db_ref, ext_ref, dext_ref):
        i = pl.program_id(1)
        _ffn_fill(ext_ref, up_p, up_c, up_n, i, n_tiles, tm)
        dws = [jnp.zeros((1, cw), F32) for _ in range(FFN_CONV_K)]
        dbs = jnp.zeros((1, cw), F32)
        for r0 in range(0, tm, FF_STRIP):
            d_c = gate_bwd(_ffn_conv(ext_ref, w_ref, b_ref, FF_HALO + r0, FF_STRIP),
                           da_c[r0:r0 + FF_STRIP, :].astype(F32))
            dext_ref[FF_HALO + r0:FF_HALO + r0 + FF_STRIP, :] = d_c
            for k in range(FFN_CONV_K):
                xk = ext_ref[pl.ds(FF_HALO + r0 + k - FFN_CONV_LEFT, FF_STRIP), :]
                dws[k] = dws[k] + jnp.sum(d_c * xk, axis=0, keepdims=True)
            dbs = dbs + jnp.sum(d_c, axis=0, keepdims=True)
        da_prev = jnp.where(i > 0, da_p[...].astype(F32)[FF_HALO - h8:, :], 0.0)
        da_next = jnp.where(i + 1 < n_tiles, da_n[...].astype(F32)[:h8, :], 0.0)
        dext_ref[FF_HALO - h8:FF_HALO, :] = gate_bwd(_ffn_conv(ext_ref, w_ref, b_ref, FF_HALO - h8, h8), da_prev)
        dext_ref[FF_HALO + tm:FF_HALO + tm + h8, :] = gate_bwd(_ffn_conv(ext_ref, w_ref, b_ref, FF_HALO + tm, h8), da_next)
        for r0 in range(0, tm, FF_STRIP):
            dup = jnp.zeros((FF_STRIP, cw), F32)
            for k in range(FFN_CONV_K):
                dup = dup + dext_ref[pl.ds(FF_HALO + r0 - k + FFN_CONV_LEFT, FF_STRIP), :] * w_ref[k:k + 1, :]
            dup_ref[r0:r0 + FF_STRIP, :] = dup.astype(dup_ref.dtype)

        @pl.when(i == 0)
        def _():
            dw_ref[...] = jnp.zeros_like(dw_ref)
            db_ref[...] = jnp.zeros_like(db_ref)

        for k in range(FFN_CONV_K):
            dw_ref[k:k + 1, :] += dws[k]
        db_ref[...] += dbs

    def half_specs():
        th = tm // FF_HALO
        last = s // FF_HALO - 1
        return [pl.BlockSpec((FF_HALO, FF_TILE), lambda c_, i: (jnp.maximum(i * th - 1, 0), c_)),
                pl.BlockSpec((tm, FF_TILE), lambda c_, i: (i, c_)),
                pl.BlockSpec((FF_HALO, FF_TILE), lambda c_, i: (jnp.minimum((i + 1) * th, last), c_))]

    return pl.pallas_call(
        body, name="ffn_mid_bwd", grid=(c2 // cw, n_tiles),
        in_specs=_conv_specs(s, cw, tm, FF_HALO) + half_specs() + [
            pl.BlockSpec((FFN_CONV_K, cw), lambda c_, i: (0, c_)), pl.BlockSpec((1, cw), lambda c_, i: (0, c_))],
        out_specs=[pl.BlockSpec((tm, cw), lambda c_, i: (i, c_)),
                   pl.BlockSpec((FFN_CONV_K, cw), lambda c_, i: (0, c_)),
                   pl.BlockSpec((1, cw), lambda c_, i: (0, c_))],
        out_shape=[jax.ShapeDtypeStruct((s, c2), BF16), jax.ShapeDtypeStruct((FFN_CONV_K, c2), F32),
                   jax.ShapeDtypeStruct((1, c2), F32)],
        scratch_shapes=[pltpu.VMEM((tm + 2 * FF_HALO, cw), F32), pltpu.VMEM((tm + 2 * FF_HALO, cw), F32)],
        compiler_params=_params(("parallel", "arbitrary")),
    )(up, up, up, d_act, d_act, d_act, w, b)


def _ff_permute(name, w, to_tiles):
    r = w.shape[0]
    nb = D_FF // FF_TILE
    natural = pl.BlockSpec((r, FF_TILE), lambda j, half: (0, half * nb + j))
    tiled = pl.BlockSpec((r, FF_TILE), lambda j, half: (0, 2 * j + half))

    def body(x_ref, o_ref):
        o_ref[...] = x_ref[...]

    return pl.pallas_call(
        body, name=name, grid=(nb, 2),
        in_specs=[natural if to_tiles else tiled], out_specs=tiled if to_tiles else natural,
        out_shape=jax.ShapeDtypeStruct(w.shape, w.dtype), compiler_params=_params(("parallel", "parallel")),
    )(w)


def _ff_to_tiles(w):
    r = w.shape[0]
    return w.reshape(r, 2, D_FF // FF_TILE, FF_TILE).transpose(0, 2, 1, 3).reshape(r, 2 * D_FF)


def _ff_from_tiles(w):
    r = w.shape[0]
    return w.reshape(r, D_FF // FF_TILE, 2, FF_TILE).transpose(0, 2, 1, 3).reshape(r, 2 * D_FF)


SCAN_UNROLL = 8


def _scan(name, a, u, reverse, shifted=False, u_off=0, with_prev=False):
    t, c = a.shape
    us = u.shape[0]
    n8 = t // SUBLANES
    lo, hi = 0, SUBLANES - 1

    def body(a_ref, u_ref, h_ref, *prev_ref):
        row = lax.broadcasted_iota(jnp.int32, (SUBLANES, LANES), 0)
        last = lo if reverse else hi

        def tile(ref, base):
            return ref[pl.ds(pl.multiple_of(base, SUBLANES), SUBLANES), :]

        def local(blk):
            base = blk * SUBLANES
            av = tile(a_ref, base)
            if shifted and reverse:
                nb = tile(a_ref, jnp.minimum(base + SUBLANES, t - SUBLANES))
                edge = jnp.where(base + SUBLANES >= t, 1.0, pltpu.roll(nb, hi, 0))
                av = jnp.where(row < hi, pltpu.roll(av, hi, 0), edge)
            elif shifted:
                pb = tile(a_ref, jnp.maximum(base - SUBLANES, 0))
                edge = jnp.where(base == 0, 1.0, pltpu.roll(pb, 1, 0))
                av = jnp.where(row >= 1, pltpu.roll(av, 1, 0), edge)
            ub = base - u_off
            hv = jnp.where((ub >= 0) & (ub < us), tile(u_ref, jnp.clip(ub, 0, us - SUBLANES)), 0.0)
            for s in (1, 2, 4):
                shift = SUBLANES - s if reverse else s
                ok = (row < SUBLANES - s) if reverse else (row >= s)
                a_sh = jnp.where(ok, pltpu.roll(av, shift, 0), 1.0)
                h_sh = jnp.where(ok, pltpu.roll(hv, shift, 0), 0.0)
                hv = av * h_sh + hv
                av = av * a_sh
            a_last = jnp.sum(jnp.where(row == last, av, 0.0), axis=0, keepdims=True)
            h_last = jnp.sum(jnp.where(row == last, hv, 0.0), axis=0, keepdims=True)
            return base, av, hv, a_last, h_last

        def step(j, carry):
            parts = []
            for k in range(SCAN_UNROLL):
                idx = j * SCAN_UNROLL + k
                parts.append(local((n8 - 1 - idx) if reverse else idx))
            for base, av, hv, a_last, h_last in parts:
                rows = pl.ds(pl.multiple_of(base, SUBLANES), SUBLANES)
                h_true = av * carry + hv
                h_ref[rows, :] = h_true
                if with_prev:
                    if reverse:
                        prev_ref[0][rows, :] = jnp.where(row < hi, pltpu.roll(h_true, hi, 0), carry)
                    else:
                        prev_ref[0][rows, :] = jnp.where(row >= 1, pltpu.roll(h_true, 1, 0), carry)
                carry = a_last * carry + h_last
            return carry

        lax.fori_loop(0, n8 // SCAN_UNROLL, step, jnp.zeros((1, LANES), F32))

    spec = pl.BlockSpec((t, LANES), lambda j: (0, j))
    n_out = 2 if with_prev else 1
    res = pl.pallas_call(
        body, name=name, grid=(c // LANES,),
        in_specs=[spec, pl.BlockSpec((us, LANES), lambda j: (0, j))],
        out_specs=[spec] * n_out, out_shape=[jax.ShapeDtypeStruct((t, c), F32)] * n_out,
        compiler_params=_params(("parallel",)),
    )(a, u)
    return res if with_prev else res[0]


NT_DIMS = (((1,), (1,)), ((), ()))


LOG2E = 1.4426950408889634
SCALE2 = MLA_SCALE * LOG2E
ATTN_TILES = (512, 256, 128)
KEY_CHUNKS = (768, 512, 256, 128)
QUERY_CHUNKS = (1024, 512, 256, 128)


def _attn_fwd(q, k, v, ride_arrs, ride_modes):
    h, s, _ = q.shape
    t = k.shape[1]
    tq = _pick(s, ATTN_TILES)
    ck = _pick(t, KEY_CHUNKS)
    n = len(ride_arrs)

    def body(*refs):
        q_ref, k_ref, v_ref = refs[:3]
        o_ref, lse_ref = refs[3 + n:5 + n]
        hh, i = pl.program_id(0), pl.program_id(1)
        finish = _riding_exchange(refs, 3, 2, n, ride_modes, (hh == 0) & (i == 0),
                                  (hh == h - 1) & (i == s // tq - 1))
        qv = q_ref[0]
        def scores(j):
            return lax.dot_general(qv, k_ref[0, j * ck:(j + 1) * ck, :], NT_DIMS, preferred_element_type=F32)

        m = l = acc = None
        s_next = scores(0)
        for j in range(t // ck):
            vj = v_ref[0, j * ck:(j + 1) * ck, :]
            s2, s_next = s_next, (scores(j + 1) if j + 1 < t // ck else None)
            mj = jnp.max(s2, axis=-1, keepdims=True)
            m_new = mj if j == 0 else jnp.maximum(m, mj)
            p = jnp.exp2(s2 - m_new)
            lj = jnp.sum(p, axis=-1, keepdims=True)
            pv = jnp.dot(p.astype(BF16), vj, preferred_element_type=F32)
            if j == 0:
                l, acc = lj, pv
            else:
                alpha = jnp.exp2(m - m_new)
                l, acc = alpha * l + lj, alpha * acc + pv
            m = m_new
        o_ref[0] = acc / l
        lse_ref[0] = m + jnp.log2(l)
        finish()

    any_spec = pl.BlockSpec(memory_space=pl.ANY)
    res = pl.pallas_call(
        body, name="attn_fwd", grid=(h, s // tq),
        in_specs=[pl.BlockSpec((1, tq, HEAD_PAD), lambda hh, i: (hh, i, 0)),
                  pl.BlockSpec((1, t, HEAD_PAD), lambda hh, i: (hh, 0, 0)),
                  pl.BlockSpec((1, t, VDIM), lambda hh, i: (hh, 0, 0))] + [any_spec] * n,
        out_specs=[pl.BlockSpec((1, tq, VDIM), lambda hh, i: (hh, i, 0)),
                   pl.BlockSpec((1, tq, 1), lambda hh, i: (hh, i, 0))] + [any_spec] * n,
        out_shape=[jax.ShapeDtypeStruct((h, s, VDIM), F32), jax.ShapeDtypeStruct((h, s, 1), F32)]
        + _exchange_shapes(ride_arrs, ride_modes),
        scratch_shapes=_exchange_sems(n),
        compiler_params=pltpu.CompilerParams(dimension_semantics=("arbitrary", "arbitrary"),
                                             vmem_limit_bytes=VMEM_LIMIT, has_side_effects=True),
    )(q, k, v, *ride_arrs)
    return res[0], res[1], res[2:]


TN_DIMS = (((0,), (0,)), ((), ()))


def _attn_bwd(q, k, v, do, lse_row, delta_row, ride_arrs, ride_modes):
    h, s, _ = q.shape
    t = k.shape[1]
    tk = _pick(t, (768,) + ATTN_TILES)
    cq = _pick(s, QUERY_CHUNKS)
    n = len(ride_arrs)

    def body(*refs):
        q_ref, k_ref, v_ref, do_ref, lse_ref, delta_ref = refs[:6]
        dq_ref, dk_ref, dv_ref = refs[6 + n:9 + n]
        hh, i = pl.program_id(0), pl.program_id(1)
        finish = _riding_exchange(refs, 6, 3, n, ride_modes, (hh == 0) & (i == 0),
                                  (hh == h - 1) & (i == t // tk - 1))

        @pl.when(i == 0)
        def _():
            dq_ref[...] = jnp.zeros_like(dq_ref)

        kt, vt = k_ref[0], v_ref[0]
        dk = dv = None
        for j in range(s // cq):
            rows = slice(j * cq, (j + 1) * cq)
            qj, doj = q_ref[0, rows, :], do_ref[0, rows, :]
            pt = jnp.exp2(lax.dot_general(kt, qj, NT_DIMS, preferred_element_type=F32) - lse_ref[0, :, rows])
            dv_j = jnp.dot(pt.astype(BF16), doj, preferred_element_type=F32)
            dpt = lax.dot_general(vt, doj, NT_DIMS, preferred_element_type=F32)
            dst = (pt * (dpt - delta_ref[0, :, rows])).astype(BF16)
            dk_j = jnp.dot(dst, qj, preferred_element_type=F32)
            dq_ref[0, rows, :] += lax.dot_general(dst, kt, TN_DIMS, preferred_element_type=F32)
            dk, dv = (dk_j, dv_j) if j == 0 else (dk + dk_j, dv + dv_j)
        dk_ref[0] = dk * (1.0 / LOG2E)
        dv_ref[0] = dv
        finish()

    any_spec = pl.BlockSpec(memory_space=pl.ANY)
    res = pl.pallas_call(
        body, name="attn_bwd", grid=(h, t // tk),
        in_specs=[pl.BlockSpec((1, s, HEAD_PAD), lambda hh, i: (hh, 0, 0)),
                  pl.BlockSpec((1, tk, HEAD_PAD), lambda hh, i: (hh, i, 0)),
                  pl.BlockSpec((1, tk, VDIM), lambda hh, i: (hh, i, 0)),
                  pl.BlockSpec((1, s, VDIM), lambda hh, i: (hh, 0, 0)),
                  pl.BlockSpec((1, 1, s), lambda hh, i: (hh, 0, 0)),
                  pl.BlockSpec((1, 1, s), lambda hh, i: (hh, 0, 0))] + [any_spec] * n,
        out_specs=[pl.BlockSpec((1, s, HEAD_PAD), lambda hh, i: (hh, 0, 0)),
                   pl.BlockSpec((1, tk, HEAD_PAD), lambda hh, i: (hh, i, 0)),
                   pl.BlockSpec((1, tk, VDIM), lambda hh, i: (hh, i, 0))] + [any_spec] * n,
        out_shape=[jax.ShapeDtypeStruct((h, s, HEAD_PAD), F32), jax.ShapeDtypeStruct((h, t, HEAD_PAD), F32),
                   jax.ShapeDtypeStruct((h, t, VDIM), F32)] + _exchange_shapes(ride_arrs, ride_modes),
        scratch_shapes=_exchange_sems(n),
        compiler_params=pltpu.CompilerParams(dimension_semantics=("arbitrary", "arbitrary"),
                                             vmem_limit_bytes=VMEM_LIMIT, has_side_effects=True),
    )(q, k, v, do, lse_row, delta_row, *ride_arrs)
    return res[0], res[1], res[2], res[3:]


def _exchange_shapes(arrs, modes):
    return [jax.ShapeDtypeStruct((N_DEV,) + a.shape if md == 'ag' else a.shape, a.dtype) for a, md in zip(arrs, modes)]


def _exchange_sems(n):
    return [pltpu.SemaphoreType.DMA((n, N_DEV - 1)), pltpu.SemaphoreType.DMA((n, N_DEV - 1)),
            pltpu.SemaphoreType.DMA((n,))]


def _exchange_copies(ins, outs, modes, send_sems, recv_sems, local_sems):
    x, y, c = lax.axis_index("x"), lax.axis_index("y"), lax.axis_index("c")
    me = 4 * x + 2 * y + c
    copies = []
    for a in range(len(ins)):
        ag = modes[a] == 'ag'
        copies.append(pltpu.make_async_copy(ins[a] if ag else ins[a].at[me], outs[a].at[me], local_sems.at[a]))
        for k in range(1, N_DEV):
            px = 1 - x if k & 4 else x
            py = 1 - y if k & 2 else y
            pc = 1 - c if k & 1 else c
            src = ins[a] if ag else ins[a].at[4 * px + 2 * py + pc]
            copies.append(pltpu.make_async_remote_copy(
                src_ref=src, dst_ref=outs[a].at[me], send_sem=send_sems.at[a, k - 1],
                recv_sem=recv_sems.at[a, k - 1], device_id=(px, py, pc), device_id_type=MESH))
    return copies


def _exchange(name, arrs, modes):
    n = len(arrs)

    def body(*refs):
        copies = _exchange_copies(refs[:n], refs[n:2 * n], modes, *refs[2 * n:])
        for cp in copies:
            cp.start()
        for cp in copies:
            cp.wait()

    return pl.pallas_call(
        body, name=name,
        in_specs=[pl.BlockSpec(memory_space=pl.ANY)] * n,
        out_specs=[pl.BlockSpec(memory_space=pl.ANY)] * n,
        out_shape=_exchange_shapes(arrs, modes),
        scratch_shapes=_exchange_sems(n),
        compiler_params=pltpu.CompilerParams(has_side_effects=True),
    )(*arrs)


def _riding_exchange(refs, n_in, n_out, n, modes, first, last):
    ins = refs[n_in:n_in + n]
    outs = refs[n_in + n + n_out:n_in + 2 * n + n_out]
    sems = refs[n_in + 2 * n + n_out:n_in + 2 * n + n_out + 3]

    @pl.when(first)
    def _():
        for cp in _exchange_copies(ins, outs, modes, *sems):
            cp.start()

    def finish():
        @pl.when(last)
        def _():
            for cp in _exchange_copies(ins, outs, modes, *sems):
                cp.wait()

    return finish


def _adamw(name, w, m, v, gparts):
    r, c = w.shape
    npart = gparts.shape[0]
    tr = _pick(r, (256, 128, 64, 32, 16, 8))
    spec = pl.BlockSpec((tr, c), lambda i: (i, 0))

    def body(w_ref, m_ref, v_ref, g_ref, go_ref, d_ref, mo_ref, vo_ref):
        g = g_ref[0]
        for p in range(1, npart):
            g = g + g_ref[p]
        m1 = ADAM_B1 * m_ref[...] + (1.0 - ADAM_B1) * g
        v1 = ADAM_B2 * v_ref[...] + (1.0 - ADAM_B2) * (g * g)
        m_hat = m1 / (1.0 - ADAM_B1 ** ADAM_STEP)
        v_hat = v1 / (1.0 - ADAM_B2 ** ADAM_STEP)
        go_ref[...] = g
        d_ref[...] = -ADAM_LR * (m_hat / (jnp.sqrt(v_hat) + ADAM_EPS) + ADAM_WD * w_ref[...])
        mo_ref[...] = m1
        vo_ref[...] = v1

    return pl.pallas_call(
        body, name=name, grid=(r // tr,),
        in_specs=[spec, spec, spec, pl.BlockSpec((npart, tr, c), lambda i: (0, i, 0))],
        out_specs=[spec] * 4, out_shape=[jax.ShapeDtypeStruct((r, c), F32)] * 4,
        compiler_params=_params(("parallel",)),
    )(w, m, v, gparts)


def _pack(arrs, rows):
    flat = jnp.concatenate([a.reshape(-1) for a in arrs])
    return jnp.pad(flat, (0, rows * LANES - flat.shape[0])).reshape(rows, LANES)


def _unpack(packed, shapes):
    flat, out, off = packed.reshape(-1), [], 0
    for s in shapes:
        n = math.prod(s)
        out.append(flat[off:off + n].reshape(s))
        off += n
    return out


def _pack_rows(n_elems):
    return -(-n_elems // (SUBLANES * LANES)) * SUBLANES


def _cols_from_shards(g):
    return g.transpose(1, 0, 2).reshape(g.shape[1], N_DEV * g.shape[2])


def _cols_to_shards(w):
    r, c = w.shape
    return w.reshape(r, N_DEV, c // N_DEV).transpose(1, 0, 2)


def _rope_tables(n_lat, n_ctx):
    rows = n_lat // GRID_W
    inv = ROPE_BASE ** (-jnp.arange(ROPE_PAIRS, dtype=F32) / ROPE_PAIRS)
    ang_r = jnp.arange(rows, dtype=F32)[:, None] * inv
    ang_c = jnp.arange(GRID_W, dtype=F32)[:, None] * inv
    cr, sr = jnp.repeat(jnp.cos(ang_r), GRID_W, axis=0), jnp.repeat(jnp.sin(ang_r), GRID_W, axis=0)
    cc, sc = jnp.tile(jnp.cos(ang_c), (rows, 1)), jnp.tile(jnp.sin(ang_c), (rows, 1))
    one, zero = jnp.ones((n_lat, 1), F32), jnp.zeros((n_lat, 1), F32)
    z8 = jnp.zeros((n_lat, ROPE_PAIRS), F32)
    cos_t = jnp.concatenate([jnp.tile(one, (1, NOPE)), cr, cr, cc, cc, jnp.tile(one, (1, HEAD_PAD - QK))], 1)
    sin_up = jnp.concatenate([jnp.tile(zero, (1, NOPE)), -sr, z8, -sc, z8, jnp.tile(zero, (1, HEAD_PAD - QK))], 1)
    sin_dn = jnp.concatenate([jnp.tile(zero, (1, NOPE)), z8, sr, z8, sc, jnp.tile(zero, (1, HEAD_PAD - QK))], 1)
    if n_ctx:
        cos_t = jnp.concatenate([jnp.ones((n_ctx, HEAD_PAD), F32), cos_t])
        sin_up = jnp.concatenate([jnp.zeros((n_ctx, HEAD_PAD), F32), sin_up])
        sin_dn = jnp.concatenate([jnp.zeros((n_ctx, HEAD_PAD), F32), sin_dn])
    return cos_t, sin_up, sin_dn


def _rope(x, cos_t, sin_up, sin_dn):
    return x * cos_t + pltpu.roll(x, HEAD_PAD - ROPE_PAIRS, 1) * sin_up + pltpu.roll(x, ROPE_PAIRS, 1) * sin_dn


def _rope_t(dy, cos_t, sin_up, sin_dn):
    return (dy * cos_t + pltpu.roll(dy * sin_up, ROPE_PAIRS, 1)
            + pltpu.roll(dy * sin_dn, HEAD_PAD - ROPE_PAIRS, 1))


def _softplus(x):
    return jnp.maximum(x, 0.0) + jnp.log(1.0 + jnp.exp(-jnp.abs(x)))


def _gates(z, xcv, lam_sp):
    outs = []
    for d in range(2):
        r = jax.nn.sigmoid(z[:, (2 * d) * LRU_W:(2 * d + 1) * LRU_W])
        ig = jax.nn.sigmoid(z[:, (2 * d + 1) * LRU_W:(2 * d + 2) * LRU_W])
        log_a = -LRU_C * r * lam_sp[:, d * LRU_W:(d + 1) * LRU_W]
        a = jnp.exp(log_a)
        u = jnp.sqrt(-jnp.tanh(log_a) * (a * a + 1.0)) * (ig * xcv)
        outs += [a, u]
    return tuple(outs)


def kernel(x, c, ctx, c_ctx, w_mod, b_mod, g_pre_mix, g_post_mix, g_pre_ffn, g_post_ffn, w_in, lru_conv_w, lru_conv_b, lru_w_a, lru_b_a, lru_w_x, lru_b_x, lru_lambda, mla_g_q, mla_w_uq, mla_g_kv, mla_w_ukv, w_out, ffn_w_up, ffn_conv_w, ffn_conv_b, ffn_w_down, loss_target, m_c_ctx, m_w_mod, m_b_mod, m_g_pre_mix, m_g_post_mix, m_g_pre_ffn, m_g_post_ffn, m_w_in, m_lru_conv_w, m_lru_conv_b, m_lru_w_a, m_lru_b_a, m_lru_w_x, m_lru_b_x, m_lru_lambda, m_mla_g_q, m_mla_w_uq, m_mla_g_kv, m_mla_w_ukv, m_w_out, m_ffn_w_up, m_ffn_conv_w, m_ffn_conv_b, m_ffn_w_down, v_c_ctx, v_w_mod, v_b_mod, v_g_pre_mix, v_g_post_mix, v_g_pre_ffn, v_g_post_ffn, v_w_in, v_lru_conv_w, v_lru_conv_b, v_lru_w_a, v_lru_b_a, v_lru_w_x, v_lru_b_x, v_lru_lambda, v_mla_g_q, v_mla_w_uq, v_mla_g_kv, v_mla_w_ukv, v_w_out, v_ffn_w_up, v_ffn_conv_w, v_ffn_conv_b, v_ffn_w_down):
    W = dict(c_ctx=c_ctx, w_mod=w_mod, b_mod=b_mod, g_pre_mix=g_pre_mix, g_post_mix=g_post_mix, g_pre_ffn=g_pre_ffn,
             g_post_ffn=g_post_ffn, w_in=w_in, lru_conv_w=lru_conv_w, lru_conv_b=lru_conv_b, lru_w_a=lru_w_a,
             lru_b_a=lru_b_a, lru_w_x=lru_w_x, lru_b_x=lru_b_x, lru_lambda=lru_lambda, mla_g_q=mla_g_q,
             mla_w_uq=mla_w_uq, mla_g_kv=mla_g_kv, mla_w_ukv=mla_w_ukv, w_out=w_out, ffn_w_up=ffn_w_up,
             ffn_conv_w=ffn_conv_w, ffn_conv_b=ffn_conv_b, ffn_w_down=ffn_w_down)
    M = dict(c_ctx=m_c_ctx, w_mod=m_w_mod, b_mod=m_b_mod, g_pre_mix=m_g_pre_mix, g_post_mix=m_g_post_mix,
             g_pre_ffn=m_g_pre_ffn, g_post_ffn=m_g_post_ffn, w_in=m_w_in, lru_conv_w=m_lru_conv_w,
             lru_conv_b=m_lru_conv_b, lru_w_a=m_lru_w_a, lru_b_a=m_lru_b_a, lru_w_x=m_lru_w_x, lru_b_x=m_lru_b_x,
             lru_lambda=m_lru_lambda, mla_g_q=m_mla_g_q, mla_w_uq=m_mla_w_uq, mla_g_kv=m_mla_g_kv,
             mla_w_ukv=m_mla_w_ukv, w_out=m_w_out, ffn_w_up=m_ffn_w_up, ffn_conv_w=m_ffn_conv_w,
             ffn_conv_b=m_ffn_conv_b, ffn_w_down=m_ffn_w_down)
    V = dict(c_ctx=v_c_ctx, w_mod=v_w_mod, b_mod=v_b_mod, g_pre_mix=v_g_pre_mix, g_post_mix=v_g_post_mix,
             g_pre_ffn=v_g_pre_ffn, g_post_ffn=v_g_post_ffn, w_in=v_w_in, lru_conv_w=v_lru_conv_w,
             lru_conv_b=v_lru_conv_b, lru_w_a=v_lru_w_a, lru_b_a=v_lru_b_a, lru_w_x=v_lru_w_x, lru_b_x=v_lru_b_x,
             lru_lambda=v_lru_lambda, mla_g_q=v_mla_g_q, mla_w_uq=v_mla_w_uq, mla_g_kv=v_mla_g_kv,
             mla_w_ukv=v_mla_w_ukv, w_out=v_w_out, ffn_w_up=v_ffn_w_up, ffn_conv_w=v_ffn_conv_w,
             ffn_conv_b=v_ffn_conv_b, ffn_w_down=v_ffn_w_down)

    D = D_MODEL
    S, CN = x.shape[1], ctx.shape[1]
    T = S + CN
    TM = ROW_TILE
    ct, ns, nt = CN // TM, S // TM, T // TM
    me = 4 * lax.axis_index("x") + 2 * lax.axis_index("y") + lax.axis_index("c")

    lat = lambda i: i + ct
    swp = lambda i: jnp.where(i < ct, i + ns, i - ct)
    lat_or_0 = lambda i: jnp.maximum(i - ct, 0)

    small_shapes = [W[n].shape[1:] for n in SMALL_SHARDED] + [(D,)]
    n_small = sum(math.prod(s) for s in small_shapes)
    small_rows = _pack_rows(n_small)
    small_loc = _pack([W[n][0] for n in SMALL_SHARDED] + [c[0]], small_rows)
    early = ['w_in', 'mla_w_uq', 'mla_w_ukv']
    late = ['w_out', 'ffn_w_up', 'ffn_w_down']
    big = early + late
    gathered = _exchange("gather_weights", [W[n][0].astype(BF16) for n in early] + [small_loc], ['ag'] * 4)
    gw = dict(zip(early, gathered[:3]))
    small_all = [_unpack(gathered[3][d], small_shapes) for d in range(N_DEV)]
    full_small = {n: jnp.concatenate([small_all[d][j] for d in range(N_DEV)], axis=-1)
                  for j, n in enumerate(SMALL_SHARDED)}
    c_all = jnp.stack([small_all[d][-1] for d in range(N_DEV)])

    w_in_f = _cols_from_shards(gw['w_in'])
    w_in_p = jnp.concatenate([w_in_f[:, :OFF_KR], jnp.zeros((D, NOPE), BF16), w_in_f[:, OFF_KR:],
                              jnp.zeros((D, HEAD_PAD - QK), BF16)], axis=1)
    w_uq_f = _cols_from_shards(gw['mla_w_uq']).reshape(Q_RANK, HEADS, QK)
    wq_p = jnp.pad(w_uq_f, ((0, 0), (0, 0), (0, HEAD_PAD - QK))).reshape(Q_RANK, HEADS * HEAD_PAD)
    w_ukv_f = _cols_from_shards(gw['mla_w_ukv']).reshape(KV_RANK, HEADS, NOPE + VDIM)
    wk_p = jnp.pad(w_ukv_f[:, :, :NOPE], ((0, 0), (0, 0), (0, HEAD_PAD - NOPE))).reshape(KV_RANK, HEADS * HEAD_PAD)
    wv_f = w_ukv_f[:, :, NOPE:].reshape(KV_RANK, HEADS * VDIM)

    lru_cw, lru_ba, lru_bx, lru_lam, ffn_cw = [full_small[n] for n in SMALL_SHARDED]
    ffn_cw_t, ffn_cb_t = _ff_to_tiles(ffn_cw), _ff_to_tiles(ffn_conv_b)

    def block_diag(w):
        eye = jnp.eye(LRU_HEADS, dtype=w.dtype)
        return jnp.einsum('hij,hg->higj', w, eye).reshape(LRU_W, LRU_W)

    w_gate = jnp.concatenate([block_diag(lru_w_a[0, 0]), block_diag(lru_w_x[0, 0]),
                              block_diag(lru_w_a[0, 1]), block_diag(lru_w_x[0, 1])], axis=1).astype(BF16)
    b_gate = jnp.concatenate([lru_ba[0], lru_bx[0], lru_ba[1], lru_bx[1]])[None]
    lam_row = lru_lam.reshape(1, 2 * LRU_W)

    c16 = jnp.concatenate([c_all, c_ctx[None], jnp.zeros((2 * SUBLANES - N_DEV - 1, D), F32)])
    ncol = w_mod.shape[2]
    b_mod_loc = lax.dynamic_slice(b_mod, (0, me * ncol), (1, ncol))

    def mod_fwd(c16_r, w_r, b_r):
        c16_v = c16_r[...]
        sl = c16_v * jax.nn.sigmoid(c16_v)
        return (jnp.dot(sl.astype(BF16), w_r[...].astype(BF16), preferred_element_type=F32) + b_r[...],)

    (mod_part,) = _single("mod_fwd", mod_fwd, [c16, w_mod[0], b_mod_loc], [((2 * SUBLANES, ncol), F32)])
    (mod_g,) = _exchange("gather_mod", [mod_part], ['ag'])
    mod_all = _cols_from_shards(mod_g)
    mod_lat = lax.dynamic_slice(mod_all, (me, 0), (1, N_MOD * D)).reshape(N_MOD, D)
    mod_ctx = mod_all[N_DEV].reshape(N_MOD, D)

    xs, tgt = x[0], loss_target[0]
    xa_rows = [_row(ctx[0], lambda i: jnp.minimum(i, ct - 1)), _row(xs, lat_or_0)]

    def sel_mod(i, ml, mc, r0):
        sh = jnp.where(i < ct, mc[r0:r0 + 1, :], ml[r0:r0 + 1, :])
        sc = jnp.where(i < ct, mc[r0 + 1:r0 + 2, :], ml[r0 + 1:r0 + 2, :])
        return sh, sc

    def pre_fn(xv, g, sh, sc):
        return _rms(xv, g) * (1.0 + sc) + sh

    def k_pre(i, rv, bv):
        sh, sc = sel_mod(i, bv[1], bv[2], 0)
        return (pre_fn(jnp.where(i < ct, rv[0], rv[1]), bv[0][...], sh, sc),), ()

    (h_pre,) = _rowwise("pre_mix", k_pre, nt, xa_rows, [g_pre_mix, mod_lat, mod_ctx], [((T, D), BF16, None)])
    proj = _mm("in_proj", h_pre, w_in_p, 'nn')

    xcv = _dwconv_fwd("lru_conv", proj, lru_cw, lru_conv_b, LRU_CONV_LEFT, (0, ct))

    def k_gates(i, rv, bv):
        xv = rv[0]
        z = jnp.dot(xv.astype(BF16), bv[0][...], preferred_element_type=F32) + bv[1][...]
        return _gates(z, xv, _bc(_softplus(-bv[2][...]), TM)), ()

    a0, u0, a1, u1 = _rowwise("lru_gates", k_gates, nt, [_row(xcv)], [w_gate, b_gate, lam_row],
                              [((T, LRU_W), F32, None), ((T, LRU_W), F32, None),
                               ((T, LRU_W), F32, swp), ((T, LRU_W), F32, swp)])
    h0, hprev0 = _scan("lru_scan_f", a0, u0, False, with_prev=True)
    h1, hprev1 = _scan("lru_scan_r", a1, u1, True, with_prev=True)

    cos_q, sup_q, sdn_q = _rope_tables(S, 0)
    cos_k, sup_k, sdn_k = _rope_tables(S, CN)
    cq_row = _row(proj, lat, (Q_RANK, OFF_CQ // Q_RANK))
    ckv_row = _row(proj, None, (KV_RANK, OFF_CKV // KV_RANK))

    def heads_of(xl):
        return [xl[:, hh * HEAD_PAD:(hh + 1) * HEAD_PAD] for hh in range(HEADS)]

    def k_q_path(i, rv, bv):
        cqv, ctb, sub, sdb = rv
        cqn_v = _rms(cqv, bv[0][...]).astype(BF16)
        ql = jnp.dot(cqn_v, bv[1][...], preferred_element_type=F32)
        return (cqn_v, jnp.stack([_rope(qh, ctb, sub, sdb) * SCALE2 for qh in heads_of(ql)])), ()

    cqn, q = _rowwise("q_path", k_q_path, ns, [cq_row, _row(cos_q), _row(sup_q), _row(sdn_q)], [mla_g_q, wq_p],
                      [((S, Q_RANK), BF16, None), ((HEADS, S, HEAD_PAD), BF16, None)])

    def k_kv_path(i, rv, bv):
        ckv_v, krp, ctb, sub, sdb = rv
        ckvn_v = _rms(ckv_v, bv[0][...]).astype(BF16)
        kl = jnp.dot(ckvn_v, bv[1][...], preferred_element_type=F32)
        vl = jnp.dot(ckvn_v, bv[2][...], preferred_element_type=F32)
        kr = _rope(krp, ctb, sub, sdb)
        return (ckvn_v, jnp.stack([kh + kr for kh in heads_of(kl)]), vl), ()

    ckvn, k, v_lin = _rowwise(
        "kv_path", k_kv_path, nt,
        [ckv_row, _row(proj, None, (HEAD_PAD, OFF_KR // HEAD_PAD)), _row(cos_k), _row(sup_k), _row(sdn_k)],
        [mla_g_kv, wk_p, wv_f],
        [((T, KV_RANK), BF16, None), ((HEADS, T, HEAD_PAD), BF16, None), ((T, HEADS * VDIM), BF16, None)])
    v = v_lin.reshape(T, HEADS, VDIM).transpose(1, 0, 2)

    o, lse, late_g = _attn_fwd(q, k, v, [W[n][0].astype(BF16) for n in late], ['ag'] * 3)
    w_out_f = late_g[0].reshape(D, D)
    w_up_t = _ff_permute("w_up_to_tiles", _cols_from_shards(late_g[1]), True)
    w_down_f = late_g[2].reshape(D_FF, D)
    o_t = o.transpose(1, 0, 2).reshape(S, HEADS * VDIM)

    def lru_out_fn(hf, hr, gr):
        return (hf + hr) * jax.nn.gelu(gr)

    def k_mix_in(i, rv, bv):
        return (jnp.concatenate([lru_out_fn(rv[0], rv[1], rv[2]), rv[3]], axis=1),), ()

    gr_row = _row(proj, lat, (LRU_W, OFF_GR // LRU_W))
    y_in = _rowwise("mix_in", k_mix_in, ns, [_row(h0, lat), _row(h1), gr_row, _row(o_t)], [],
                    [((S, D), BF16, None)])[0]
    y = _mm("out_proj", y_in, w_out_f, 'nn')

    def post_mix_fn(xv, yv, gt, g):
        return xv + gt * _rms(yv, g)

    def k_post_mix(i, rv, bv):
        ml = bv[0]
        x1v = post_mix_fn(rv[0], rv[1], ml[2:3, :], bv[1][...])
        return (x1v, pre_fn(x1v, bv[2][...], ml[3:4, :], ml[4:5, :])), ()

    x1, h2 = _rowwise("post_mix", k_post_mix, ns, [_row(xs), _row(y)], [mod_lat, g_post_mix, g_pre_ffn],
                      [((S, D), F32, None), ((S, D), BF16, None)])
    up = _mm("ffn_up", h2, w_up_t, 'nn', out_dtype=BF16)
    act = _ffn_mid_fwd(up, ffn_cw_t, ffn_cb_t)
    f = _mm("ffn_down", act, w_down_f, 'nn')

    def loss_fn(x1v, fv, gt, g, tg):
        x2 = x1v + gt * _rms(fv, g)
        err = x2 - tg
        return 0.5 * jnp.sum(jnp.mean(err * err, axis=-1))

    def k_loss(i, rv, bv):
        gtb, gb = _bc(bv[0][5:6, :], TM), _bc(bv[1][...], TM)
        val, (dx1v, dfv, dgt, dg) = jax.value_and_grad(loss_fn, argnums=(0, 1, 2, 3))(rv[0], rv[1], gtb, gb, rv[2])
        return (dx1v, dfv), (jnp.full((1, LANES), val, F32), _rs(dgt), _rs(dg))

    dx1_a, df, loss_acc, d_gt2, d_g_post_ffn = _rowwise(
        "loss_bwd", k_loss, ns, [_row(x1), _row(f), _row(tgt)], [mod_lat, g_post_ffn],
        [((S, D), F32, None), ((S, D), BF16, None)], [(1, LANES), (1, D), (1, D)])

    d_act = _mm("ffn_down_dx", df, w_down_f, 'nt', out_dtype=BF16)
    d_w_down = _mm("ffn_down_dw", act, df, 'tn')
    d_up, d_ffn_cw_t, d_ffn_cb_t = _ffn_mid_bwd(up, d_act, ffn_cw_t, ffn_cb_t)
    d_ffn_cw, d_ffn_cb = _ff_from_tiles(d_ffn_cw_t), _ff_from_tiles(d_ffn_cb_t)
    d_h2 = _mm("ffn_up_dx", d_up, w_up_t, 'nt')
    d_w_up = _ff_permute("d_w_up_from_tiles", _mm("ffn_up_dw", h2, d_up, 'tn'), False)

    def k_pre_ffn_bwd(i, rv, bv):
        ml = bv[0]
        gb, shb, scb = _bc(bv[1][...], TM), _bc(ml[3:4, :], TM), _bc(ml[4:5, :], TM)
        _, pull = jax.vjp(pre_fn, rv[0], gb, shb, scb)
        dxv, dg, dsh, dsc = pull(rv[1])
        return (rv[2] + dxv,), (_rs(dg), _rs(dsh), _rs(dsc))

    dx1, d_g_pre_ffn, d_sh2, d_sc2 = _rowwise(
        "pre_ffn_bwd", k_pre_ffn_bwd, ns, [_row(x1), _row(d_h2), _row(dx1_a)], [mod_lat, g_pre_ffn],
        [((S, D), F32, None)], [(1, D), (1, D), (1, D)])

    def k_post_mix_bwd(i, rv, bv):
        gtb, gb = _bc(bv[0][2:3, :], TM), _bc(bv[1][...], TM)
        _, pull = jax.vjp(post_mix_fn, rv[0], rv[1], gtb, gb)
        _, dyv, dgt, dg = pull(rv[2])
        return (dyv,), (_rs(dgt), _rs(dg))

    dy, d_gt1, d_g_post_mix = _rowwise(
        "post_mix_bwd", k_post_mix_bwd, ns, [_row(xs), _row(y), _row(dx1)], [mod_lat, g_post_mix],
        [((S, D), BF16, None)], [(1, D), (1, D)])
    d_y_in = _mm("out_proj_dx", dy, w_out_f, 'nt')
    d_w_out = _mm("out_proj_dw", y_in, dy, 'tn')

    def k_lru_out_bwd(i, rv, bv):
        _, pull = jax.vjp(lru_out_fn, rv[0], rv[1], rv[2])
        dhf, _, dgr = pull(rv[3])
        return (dhf, dgr), ()

    d_hsum, d_gr = _rowwise("lru_out_bwd", k_lru_out_bwd, ns,
                            [_row(h0, lat), _row(h1), gr_row, _row(d_y_in, None, (LRU_W, 0))], [],
                            [((S, LRU_W), F32, None), ((S, LRU_W), F32, None)])

    do_h = d_y_in[:, LRU_W:].reshape(S, HEADS, VDIM).transpose(1, 0, 2)

    def k_delta(i, rv, bv):
        return (jnp.sum(rv[0] * rv[1], axis=-1, keepdims=True), rv[1]), ()

    delta, do_b = _rowwise("attn_delta", k_delta, ns, [_row(o), _row(do_h)], [],
                           [((HEADS, S, 1), F32, None), ((HEADS, S, VDIM), BF16, None)])
    late_sends = [d_w_out.reshape(N_DEV, D // N_DEV, D), _cols_to_shards(d_w_up),
                  d_w_down.reshape(N_DEV, D_FF // N_DEV, D)]
    dq, dk, dv, late_recv = _attn_bwd(q, k, v, do_b, lse.reshape(HEADS, 1, S), delta.reshape(HEADS, 1, S),
                                      late_sends, ['a2a'] * 3)

    def rms_bwd(xv, g, dy):
        _, pull = jax.vjp(_rms, xv, _bc(g, TM))
        dxv, dg = pull(dy)
        return dxv, _rs(dg)

    def k_q_path_bwd(i, rv, bv):
        dqv, ctb, sub, sdb, cqv = rv
        dql = jnp.concatenate([_rope_t(dqv[hh] * MLA_SCALE, ctb, sub, sdb) for hh in range(HEADS)], axis=1).astype(BF16)
        d_cqn_v = lax.dot_general(dql, bv[1][...], NT_DIMS, preferred_element_type=F32)
        dxv, dg = rms_bwd(cqv, bv[0][...], d_cqn_v)
        return (dql, dxv), (dg,)

    dq_lin, d_cq, d_g_q = _rowwise(
        "q_path_bwd", k_q_path_bwd, ns, [_row(dq), _row(cos_q), _row(sup_q), _row(sdn_q), cq_row], [mla_g_q, wq_p],
        [((S, HEADS * HEAD_PAD), BF16, None), ((S, Q_RANK), F32, None)], [(1, Q_RANK)])
    d_wq_p = _mm("q_proj_dw", cqn, dq_lin, 'tn')

    dv_lin = dv.transpose(1, 0, 2).reshape(T, HEADS * VDIM).astype(BF16)

    def k_kv_path_bwd(i, rv, bv):
        dkv_, ctb, sub, sdb, dvl, ckv_v = rv
        tot = dkv_[0]
        for hh in range(1, HEADS):
            tot = tot + dkv_[hh]
        lane = lax.broadcasted_iota(jnp.int32, tot.shape, 1)
        tot = jnp.where((lane >= NOPE) & (lane < QK), tot, 0.0)
        dkl = jnp.concatenate([dkv_[hh] for hh in range(HEADS)], axis=1).astype(BF16)
        d_ckvn_v = (lax.dot_general(dkl, bv[1][...], NT_DIMS, preferred_element_type=F32)
                    + lax.dot_general(dvl, bv[2][...], NT_DIMS, preferred_element_type=F32))
        dxv, dg = rms_bwd(ckv_v, bv[0][...], d_ckvn_v)
        return (dkl, _rope_t(tot, ctb, sub, sdb), dxv), (dg,)

    dk_lin, d_krp, d_ckv, d_g_kv = _rowwise(
        "kv_path_bwd", k_kv_path_bwd, nt,
        [_row(dk), _row(cos_k), _row(sup_k), _row(sdn_k), _row(dv_lin), ckv_row], [mla_g_kv, wk_p, wv_f],
        [((T, HEADS * HEAD_PAD), BF16, None), ((T, HEAD_PAD), F32, None), ((T, KV_RANK), F32, None)], [(1, KV_RANK)])
    d_wk_p = _mm("k_proj_dw", ckvn, dk_lin, 'tn')
    d_wv = _mm("v_proj_dw", ckvn, dv_lin, 'tn')

    lam0 = _scan("lru_scan_f_bwd", a0, d_hsum, True, shifted=True, u_off=CN)
    lam1 = _scan("lru_scan_r_bwd", a1, d_hsum, False, shifted=True, u_off=0)

    def k_gates_bwd(i, rv, bv):
        xv, l0, hp0, l1, hp1 = rv
        wg, bg, lamv = [b[...] for b in bv]
        xb = xv.astype(BF16)
        z = jnp.dot(xb, wg, preferred_element_type=F32) + bg
        spb = _bc(_softplus(-lamv), TM)
        _, pull = jax.vjp(_gates, z, xv, spb)
        dz, dxv, dsp = pull((l0 * hp0, l0, l1 * hp1, l1))
        dzb = dz.astype(BF16)
        dxv = dxv + lax.dot_general(dzb, wg, NT_DIMS, preferred_element_type=F32)
        dwg = lax.dot_general(xb, dzb, (((0,), (0,)), ((), ())), preferred_element_type=F32)
        dlam = -_rs(dsp) * jax.nn.sigmoid(-lamv)
        return (dxv,), (dwg, _rs(dz), dlam)

    d_xcv, d_w_gate, d_b_gate, d_lam = _rowwise(
        "lru_gates_bwd", k_gates_bwd, nt,
        [_row(xcv), _row(lam0), _row(hprev0), _row(lam1, swp), _row(hprev1, swp)], [w_gate, b_gate, lam_row],
        [((T, LRU_W), F32, None)], [(LRU_W, 4 * LRU_W), (1, 4 * LRU_W), (1, 2 * LRU_W)])
    d_xr, d_lru_cw, d_lru_cb = _dwconv_bwd("lru_conv_bwd", proj, d_xcv, lru_cw, LRU_CONV_LEFT, (0, ct))

    def k_dproj(i, rv, bv):
        is_lat = i >= ct
        return (jnp.concatenate([rv[0], jnp.where(is_lat, rv[1], 0.0), jnp.where(is_lat, rv[2], 0.0), rv[3], rv[4]],
                                axis=1),), ()

    d_proj = _rowwise("d_proj", k_dproj, nt,
                      [_row(d_xr), _row(d_gr, lat_or_0), _row(d_cq, lat_or_0), _row(d_ckv), _row(d_krp)], [],
                      [((T, IN_W_PAD), BF16, None)])[0]
    d_w_in_p = _mm("in_proj_dw", h_pre, d_proj, 'tn')

    d_b_a = jnp.stack([d_b_gate[0, 0:LRU_W], d_b_gate[0, 2 * LRU_W:3 * LRU_W]])
    d_b_x = jnp.stack([d_b_gate[0, LRU_W:2 * LRU_W], d_b_gate[0, 3 * LRU_W:]])
    d_w_in = jnp.concatenate([d_w_in_p[:, :OFF_KR], d_w_in_p[:, OFF_KR + NOPE:OFF_KR + QK]], axis=1)
    d_w_uq = d_wq_p.reshape(Q_RANK, HEADS, HEAD_PAD)[:, :, :QK].reshape(Q_RANK, HEADS * QK)
    d_w_ukv = jnp.concatenate([d_wk_p.reshape(KV_RANK, HEADS, HEAD_PAD)[:, :, :NOPE],
                               d_wv.reshape(KV_RANK, HEADS, VDIM)], axis=2).reshape(KV_RANK, HEADS * (NOPE + VDIM))
    small_full = dict(lru_conv_w=d_lru_cw, lru_b_a=d_b_a, lru_b_x=d_b_x, lru_lambda=d_lam.reshape(2, LRU_W),
                      ffn_conv_w=d_ffn_cw)
    small_sh = jnp.concatenate([_cols_to_shards(small_full[n]).reshape(N_DEV, -1) for n in SMALL_SHARDED], axis=1)
    n_sh = small_sh.shape[1]
    sh_rows = _pack_rows(n_sh)
    small_sh = jnp.pad(small_sh, ((0, 0), (0, sh_rows * LANES - n_sh))).reshape(N_DEV, sh_rows, LANES)
    early_sends = [_cols_to_shards(d_w_in), _cols_to_shards(d_w_uq), _cols_to_shards(d_w_ukv), small_sh]

    def k_pre_bwd(i, rv, bv):
        g, ml, mc = bv[0][...], bv[1], bv[2]
        sh, sc = sel_mod(i, ml, mc, 0)
        d_h = lax.dot_general(rv[2], bv[3][...], NT_DIMS, preferred_element_type=F32)
        _, pull = jax.vjp(pre_fn, jnp.where(i < ct, rv[0], rv[1]), _bc(g, TM), _bc(sh, TM), _bc(sc, TM))
        dxv, dg, dsh, dsc = pull(d_h)
        is_lat = i >= ct
        dsh, dsc = _rs(dsh), _rs(dsc)
        zero = jnp.zeros_like(dsh)
        return ((dxv + rv[3],),
                (_rs(dg), jnp.where(is_lat, dsh, zero), jnp.where(is_lat, dsc, zero),
                 jnp.where(is_lat, zero, dsh), jnp.where(is_lat, zero, dsc)))

    (dxl, d_g_pre_mix, d_sh1, d_sc1, d_csh1, d_csc1), early_recv = _rowwise(
        "pre_mix_bwd", k_pre_bwd, nt, xa_rows + [_row(d_proj), _row(dx1, lat_or_0)],
        [g_pre_mix, mod_lat, mod_ctx, w_in_p], [((S, D), F32, lat_or_0)], [(1, D)] * 5,
        ride_arrs=early_sends, ride_modes=['a2a'] * 4)
    grad_x = dxl[None]

    zrow = jnp.zeros((1, D), F32)
    d_mod_lat = jnp.concatenate([d_sh1, d_sc1, d_gt1, d_sh2, d_sc2, d_gt2], axis=1)
    d_mod_ctx = jnp.concatenate([d_csh1, d_csc1, zrow, zrow, zrow, zrow], axis=1)
    loss_row = jnp.pad(loss_acc, ((0, 0), (0, N_MOD * D - LANES)))
    d_mod_mine = jnp.concatenate([d_mod_lat, d_mod_ctx, loss_row, jnp.zeros((SUBLANES - 3, N_MOD * D), F32)])

    def diag_blocks(dw):
        return jnp.stack([dw[hh * 64:(hh + 1) * 64, hh * 64:(hh + 1) * 64] for hh in range(LRU_HEADS)])

    d_lru_w_a = jnp.stack([diag_blocks(d_w_gate[:, 0:LRU_W]), diag_blocks(d_w_gate[:, 2 * LRU_W:3 * LRU_W])])[None]
    d_lru_w_x = jnp.stack([diag_blocks(d_w_gate[:, LRU_W:2 * LRU_W]), diag_blocks(d_w_gate[:, 3 * LRU_W:])])[None]
    rep_part = dict(b_mod=d_mod_lat + d_mod_ctx, g_pre_mix=d_g_pre_mix, g_post_mix=d_g_post_mix,
                    g_pre_ffn=d_g_pre_ffn, g_post_ffn=d_g_post_ffn, lru_conv_b=d_lru_cb, lru_w_a=d_lru_w_a,
                    lru_w_x=d_lru_w_x, mla_g_q=d_g_q, mla_g_kv=d_g_kv, ffn_conv_b=d_ffn_cb)
    rep_names = [n for n in REPLICATED if n != 'c_ctx']
    rep_shapes = [W[n].shape for n in rep_names]
    rep_rows = -(-_pack_rows(sum(W[n].size for n in rep_names)) // ROW_TILE) * ROW_TILE
    rep_loc = _pack([rep_part[n] for n in rep_names], rep_rows)
    d_mod_all, rep_all = _exchange("gather_dmod", [d_mod_mine, rep_loc], ['ag'] * 2)
    loss = jnp.sum(d_mod_all[:, 2, 0])
    dm_lat_loc = lax.dynamic_slice(d_mod_all[:, 0], (0, me * ncol), (N_DEV, ncol))
    dm_ctx_loc = lax.dynamic_slice(d_mod_all[:, 1], (0, me * ncol), (N_DEV, ncol))

    def mod_bwd(c16_r, w_r, dml_r, dmc_r):
        c16_v = c16_r[...]
        sig = jax.nn.sigmoid(c16_v)
        sl = c16_v * sig
        dctx = dmc_r[0:1, :]
        for d in range(1, N_DEV):
            dctx = dctx + dmc_r[d:d + 1, :]
        row = lax.broadcasted_iota(jnp.int32, (2 * SUBLANES, ncol), 0)
        dm16 = dml_r[...] + jnp.where(row == N_DEV, _bc(dctx, 2 * SUBLANES), 0.0)
        dw = lax.dot_general(sl.astype(BF16), dm16.astype(BF16), (((0,), (0,)), ((), ())), preferred_element_type=F32)
        dsl = lax.dot_general(dm16.astype(BF16), w_r[...].astype(BF16), NT_DIMS, preferred_element_type=F32)
        dc = dsl * (sig * (1.0 + c16_v * (1.0 - sig)))
        return dw, dc

    dm_lat16 = jnp.concatenate([dm_lat_loc, jnp.zeros((2 * SUBLANES - N_DEV, ncol), F32)])
    g_w_mod, dc16 = _single("mod_bwd", mod_bwd, [c16, w_mod[0], dm_lat16, dm_ctx_loc],
                            [((D, ncol), F32), ((2 * SUBLANES, D), F32)])
    d_c_ctx_part = dc16[N_DEV]

    (c_ctx_all,) = _exchange("gather_d_c_ctx", [d_c_ctx_part.reshape(SUBLANES, D // SUBLANES)], ['ag'])
    big_parts = list(early_recv[:3]) + list(late_recv)

    res = {}

    def adam(name, w2, m2, v2, parts):
        return _adamw("adamw_" + name, w2, m2, v2, parts)

    for n, parts in zip(big, big_parts):
        shp = W[n].shape
        outs = adam(n, W[n][0], M[n][0], V[n][0], parts)
        res[n] = [o_.reshape(shp) for o_ in outs]
    outs = adam('w_mod', w_mod[0], m_w_mod[0], v_w_mod[0], g_w_mod[None])
    res['w_mod'] = [o_.reshape(w_mod.shape) for o_ in outs]

    sh_shapes = [W[n].shape for n in SMALL_SHARDED]
    pk = lambda dct: _pack([dct[n] for n in SMALL_SHARDED], sh_rows)
    outs = adam('small_sharded', pk(W), pk(M), pk(V), early_recv[3])
    for n, vals in zip(SMALL_SHARDED, zip(*[_unpack(o_, sh_shapes) for o_ in outs])):
        res[n] = list(vals)

    pr = lambda dct: _pack([dct[n] for n in rep_names], rep_rows)
    outs = adam('replicated', pr(W), pr(M), pr(V), rep_all)
    for n, vals in zip(rep_names, zip(*[_unpack(o_, rep_shapes) for o_ in outs])):
        res[n] = list(vals)
    as_tile = lambda a: a.reshape(SUBLANES, D // SUBLANES)
    outs = adam('c_ctx', as_tile(c_ctx), as_tile(m_c_ctx), as_tile(v_c_ctx), c_ctx_all)
    res['c_ctx'] = [o_.reshape(c_ctx.shape) for o_ in outs]

    return (loss, grad_x, *[res[n][0] for n in WEIGHTS], *[res[n][1] for n in WEIGHTS],
            *[res[n][2] for n in WEIGHTS], *[res[n][3] for n in WEIGHTS])
```

```python
import functools
import math

import jax
import jax.numpy as jnp
from jax import lax
from jax.experimental import pallas as pl
from jax.experimental.pallas import tpu as pltpu

F32 = jnp.float32
BF16 = jnp.bfloat16
MESH = pl.DeviceIdType.MESH

N_DEV = 8
ROW_TILE = 256
SUBLANES = 8
LANES = 128
VMEM_LIMIT = 56 * 1024 * 1024

D_MODEL = 1024
LRU_W = 512
LRU_HEADS = 8
LRU_CONV_K = 4
LRU_CONV_LEFT = 2
LRU_C = 8.0
HEADS = 8
NOPE = 64
ROPE = 32
VDIM = 64
QK = NOPE + ROPE
HEAD_PAD = 128
Q_RANK = 256
KV_RANK = 128
MLA_SCALE = QK ** -0.5
ROPE_PAIRS = ROPE // 4
ROPE_BASE = 10000.0
GRID_W = 64
D_FF = 2816
FFN_CONV_K = 3
FFN_CONV_LEFT = 1
N_MOD = 6
EPS = 1e-6
IN_W = 2 * LRU_W + Q_RANK + KV_RANK + ROPE
IN_W_PAD = 2 * LRU_W + Q_RANK + KV_RANK + HEAD_PAD
OFF_GR, OFF_CQ, OFF_CKV, OFF_KR = LRU_W, 2 * LRU_W, 2 * LRU_W + Q_RANK, 2 * LRU_W + Q_RANK + KV_RANK

ADAM_LR, ADAM_B1, ADAM_B2, ADAM_EPS, ADAM_WD, ADAM_STEP = 0.001, 0.9, 0.999, 1e-08, 0.01, 10

WEIGHTS = ['c_ctx', 'w_mod', 'b_mod', 'g_pre_mix', 'g_post_mix', 'g_pre_ffn', 'g_post_ffn', 'w_in', 'lru_conv_w',
           'lru_conv_b', 'lru_w_a', 'lru_b_a', 'lru_w_x', 'lru_b_x', 'lru_lambda', 'mla_g_q', 'mla_w_uq', 'mla_g_kv',
           'mla_w_ukv', 'w_out', 'ffn_w_up', 'ffn_conv_w', 'ffn_conv_b', 'ffn_w_down']
REPLICATED = ['c_ctx', 'b_mod', 'g_pre_mix', 'g_post_mix', 'g_pre_ffn', 'g_post_ffn', 'lru_conv_b', 'lru_w_a',
              'lru_w_x', 'mla_g_q', 'mla_g_kv', 'ffn_conv_b']
SMALL_SHARDED = ['lru_conv_w', 'lru_b_a', 'lru_b_x', 'lru_lambda', 'ffn_conv_w']


def _pick(d, prefs):
    for p in prefs:
        if d % p == 0:
            return p
    return d


def _params(sem=None):
    return pltpu.CompilerParams(dimension_semantics=sem, vmem_limit_bytes=VMEM_LIMIT)


MM_TILES = (1024, 1408, 768, 512, 256, 128)


def _mm(name, a, b, mode, out_dtype=F32):
    if mode == 'nn':
        (m, k), (_, n) = a.shape, b.shape
    elif mode == 'nt':
        (m, k), (n, _) = a.shape, b.shape
    else:
        (k, m), (_, n) = a.shape, b.shape
    tm = _pick(m, MM_TILES)
    tn = _pick(n, MM_TILES)
    tk = _pick(k, MM_TILES)
    nk = k // tk
    if mode == 'nn':
        a_spec = pl.BlockSpec((tm, tk), lambda i, j, kk: (i, kk))
        b_spec = pl.BlockSpec((tk, tn), lambda i, j, kk: (kk, j))
        dn = (((1,), (0,)), ((), ()))
    elif mode == 'nt':
        a_spec = pl.BlockSpec((tm, tk), lambda i, j, kk: (i, kk))
        b_spec = pl.BlockSpec((tn, tk), lambda i, j, kk: (j, kk))
        dn = (((1,), (1,)), ((), ()))
    else:
        a_spec = pl.BlockSpec((tk, tm), lambda i, j, kk: (kk, i))
        b_spec = pl.BlockSpec((tk, tn), lambda i, j, kk: (kk, j))
        dn = (((0,), (0,)), ((), ()))

    def body(a_ref, b_ref, o_ref, acc_ref):
        kk = pl.program_id(2)

        @pl.when(kk == 0)
        def _():
            acc_ref[...] = jnp.zeros_like(acc_ref)

        acc_ref[...] += lax.dot_general(a_ref[...].astype(BF16), b_ref[...].astype(BF16), dn,
                                        preferred_element_type=F32)

        @pl.when(kk == nk - 1)
        def _():
            o_ref[...] = acc_ref[...].astype(o_ref.dtype)

    return pl.pallas_call(
        body, name=name, grid=(m // tm, n // tn, nk),
        in_specs=[a_spec, b_spec], out_specs=pl.BlockSpec((tm, tn), lambda i, j, kk: (i, j)),
        out_shape=jax.ShapeDtypeStruct((m, n), out_dtype),
        scratch_shapes=[pltpu.VMEM((tm, tn), F32)],
        compiler_params=_params(("parallel", "parallel", "arbitrary")),
    )(a, b)


def _row(a, idx=None, col=None):
    return dict(a=a, idx=idx, col=col)


def _rowwise(name, fn, n_tiles, rows, bcast, out_rows, out_acc=(), tm=ROW_TILE, ride_arrs=(), ride_modes=()):
    in_specs = []
    for r in rows:
        a, idx, col = r['a'], r['idx'] or (lambda i: i), r['col']
        if a.ndim == 2:
            w, ci = col if col else (a.shape[1], 0)
            in_specs.append(pl.BlockSpec((tm, w), lambda i, idx=idx, ci=ci: (idx(i), ci)))
        else:
            in_specs.append(pl.BlockSpec((a.shape[0], tm, a.shape[2]), lambda i, idx=idx: (0, idx(i), 0)))
    for b in bcast:
        in_specs.append(pl.BlockSpec(b.shape, lambda i, nd=b.ndim: (0,) * nd))
    out_specs, out_shape = [], []
    for shape, dtype, idx in out_rows:
        idx = idx or (lambda i: i)
        if len(shape) == 2:
            out_specs.append(pl.BlockSpec((tm, shape[1]), lambda i, idx=idx: (idx(i), 0)))
        else:
            out_specs.append(pl.BlockSpec((shape[0], tm, shape[2]), lambda i, idx=idx: (0, idx(i), 0)))
        out_shape.append(jax.ShapeDtypeStruct(shape, dtype))
    for shape in out_acc:
        out_specs.append(pl.BlockSpec(shape, lambda i, nd=len(shape): (0,) * nd))
        out_shape.append(jax.ShapeDtypeStruct(shape, F32))
    nr, nb, no, na, n = len(rows), len(bcast), len(out_rows), len(out_acc), len(ride_arrs)
    any_spec = pl.BlockSpec(memory_space=pl.ANY)

    def body(*refs):
        i = pl.program_id(0)
        finish = _riding_exchange(refs, nr + nb, no + na, n, ride_modes, i == 0, i == n_tiles - 1) if n else None
        rvals = [r[...] for r in refs[:nr]]
        bvals = list(refs[nr:nr + nb])
        o_rows, o_acc = fn(i, rvals, bvals)
        outs = refs[nr + nb + n:]
        for ref, v in zip(outs[:no], o_rows):
            ref[...] = v.astype(ref.dtype)
        acc_refs = outs[no:no + na]
        if acc_refs:
            @pl.when(i == 0)
            def _():
                for ref in acc_refs:
                    ref[...] = jnp.zeros_like(ref)
            for ref, v in zip(acc_refs, o_acc):
                ref[...] += v
        if n:
            finish()

    res = pl.pallas_call(
        body, name=name, grid=(n_tiles,), in_specs=in_specs + [any_spec] * n, out_specs=out_specs + [any_spec] * n,
        out_shape=out_shape + _exchange_shapes(ride_arrs, ride_modes),
        scratch_shapes=_exchange_sems(n) if n else [],
        compiler_params=pltpu.CompilerParams(dimension_semantics=("arbitrary",), vmem_limit_bytes=VMEM_LIMIT,
                                             has_side_effects=bool(n)),
    )(*[r['a'] for r in rows], *bcast, *ride_arrs)
    return (res[:no + na], res[no + na:]) if n else res


def _single(name, fn, ins, out_shapes):
    def body(*refs):
        outs = fn(*refs[:len(ins)])
        for ref, v in zip(refs[len(ins):], outs):
            ref[...] = v.astype(ref.dtype)

    return pl.pallas_call(
        body, name=name,
        in_specs=[pl.BlockSpec(memory_space=pltpu.VMEM)] * len(ins),
        out_specs=[pl.BlockSpec(memory_space=pltpu.VMEM)] * len(out_shapes),
        out_shape=[jax.ShapeDtypeStruct(s, d) for s, d in out_shapes],
        compiler_params=_params(),
    )(*ins)


def _bc(p, n):
    return jnp.broadcast_to(p, (n, p.shape[-1]))


def _rs(g):
    return jnp.sum(g, axis=0, keepdims=True)


def _rms(x, g):
    return x * lax.rsqrt(jnp.mean(x * x, axis=-1, keepdims=True) + EPS) * g


def _conv_specs(r, cw, tm, halo=SUBLANES):
    th = tm // halo
    last = r // halo - 1
    prev = pl.BlockSpec((halo, cw), lambda c, i: (jnp.maximum(i * th - 1, 0), c))
    cur = pl.BlockSpec((tm, cw), lambda c, i: (i, c))
    nxt = pl.BlockSpec((halo, cw), lambda c, i: (jnp.minimum((i + 1) * th, last), c))
    return [prev, cur, nxt]


def _fill_ext(ext_ref, prev_ref, cur_ref, next_ref, i, n_tiles, seg_starts, tm):
    prev_ok = functools.reduce(jnp.logical_and, [i != s for s in seg_starts])
    next_ok = functools.reduce(jnp.logical_and, [i + 1 != s for s in seg_starts] + [i + 1 < n_tiles])
    ext_ref[0:SUBLANES, :] = jnp.where(prev_ok, prev_ref[...].astype(F32), 0.0)
    ext_ref[SUBLANES:SUBLANES + tm, :] = cur_ref[...].astype(F32)
    ext_ref[SUBLANES + tm:, :] = jnp.where(next_ok, next_ref[...].astype(F32), 0.0)


def _dwconv_fwd(name, x, w, b, left, seg_starts, cw=512, tm=ROW_TILE):
    r, c = x.shape[0], w.shape[1]
    kw = w.shape[0]
    n_tiles = r // tm

    def body(prev_ref, cur_ref, next_ref, w_ref, b_ref, o_ref, ext_ref):
        i = pl.program_id(1)
        _fill_ext(ext_ref, prev_ref, cur_ref, next_ref, i, n_tiles, seg_starts, tm)
        out = jnp.broadcast_to(b_ref[...], (tm, cw))
        for k in range(kw):
            out = out + ext_ref[pl.ds(SUBLANES + k - left, tm), :] * w_ref[k:k + 1, :]
        o_ref[...] = out

    return pl.pallas_call(
        body, name=name, grid=(c // cw, n_tiles),
        in_specs=_conv_specs(r, cw, tm) + [pl.BlockSpec((kw, cw), lambda c_, i: (0, c_)),
                                           pl.BlockSpec((1, cw), lambda c_, i: (0, c_))],
        out_specs=pl.BlockSpec((tm, cw), lambda c_, i: (i, c_)),
        out_shape=jax.ShapeDtypeStruct((r, c), F32),
        scratch_shapes=[pltpu.VMEM((tm + 2 * SUBLANES, cw), F32)],
        compiler_params=_params(("parallel", "arbitrary")),
    )(x, x, x, w, b)


def _dwconv_bwd(name, x, dy, w, left, seg_starts, out_dtype=F32, cw=512, tm=ROW_TILE):
    r, c = dy.shape
    kw = w.shape[0]
    n_tiles = r // tm

    def body(xp, xc, xn, dp, dc, dn, w_ref, dx_ref, dw_ref, db_ref, xe_ref, de_ref):
        i = pl.program_id(1)
        _fill_ext(xe_ref, xp, xc, xn, i, n_tiles, seg_starts, tm)
        _fill_ext(de_ref, dp, dc, dn, i, n_tiles, seg_starts, tm)
        dyc = dc[...].astype(F32)
        dx = jnp.zeros((tm, cw), F32)
        dws = []
        for k in range(kw):
            dx = dx + de_ref[pl.ds(SUBLANES - k + left, tm), :] * w_ref[k:k + 1, :]
            dws.append(jnp.sum(dyc * xe_ref[pl.ds(SUBLANES + k - left, tm), :], axis=0, keepdims=True))
        dx_ref[...] = dx.astype(dx_ref.dtype)

        @pl.when(i == 0)
        def _():
            dw_ref[...] = jnp.zeros_like(dw_ref)
            db_ref[...] = jnp.zeros_like(db_ref)

        for k in range(kw):
            dw_ref[k:k + 1, :] += dws[k]
        db_ref[...] += jnp.sum(dyc, axis=0, keepdims=True)

    return pl.pallas_call(
        body, name=name, grid=(c // cw, n_tiles),
        in_specs=_conv_specs(r, cw, tm) + _conv_specs(r, cw, tm) + [pl.BlockSpec((kw, cw), lambda c_, i: (0, c_))],
        out_specs=[pl.BlockSpec((tm, cw), lambda c_, i: (i, c_)),
                   pl.BlockSpec((kw, cw), lambda c_, i: (0, c_)),
                   pl.BlockSpec((1, cw), lambda c_, i: (0, c_))],
        out_shape=[jax.ShapeDtypeStruct((r, c), out_dtype), jax.ShapeDtypeStruct((kw, c), F32),
                   jax.ShapeDtypeStruct((1, c), F32)],
        scratch_shapes=[pltpu.VMEM((tm + 2 * SUBLANES, cw), F32), pltpu.VMEM((tm + 2 * SUBLANES, cw), F32)],
        compiler_params=_params(("parallel", "arbitrary")),
    )(x, x, x, dy, dy, dy, w)


FF_TILE = 256
FF_HALO = 16
FF_STRIP = 32


def _ffn_fill(ext_ref, prev_ref, cur_ref, next_ref, i, n_tiles, tm):
    ext_ref[0:FF_HALO, :] = jnp.where(i > 0, prev_ref[...].astype(F32), 0.0)
    ext_ref[FF_HALO:FF_HALO + tm, :] = cur_ref[...].astype(F32)
    ext_ref[FF_HALO + tm:, :] = jnp.where(i + 1 < n_tiles, next_ref[...].astype(F32), 0.0)


def _ffn_conv(ext_ref, w_ref, b_ref, start, rows):
    out = jnp.broadcast_to(b_ref[...], (rows, 2 * FF_TILE))
    for k in range(FFN_CONV_K):
        out = out + ext_ref[pl.ds(start + k - FFN_CONV_LEFT, rows), :] * w_ref[k:k + 1, :]
    return out


def _ffn_mid_fwd(up, w, b):
    s, c2 = up.shape
    tm = _pick(s, (2 * ROW_TILE, ROW_TILE))
    n_tiles = s // tm
    cw = 2 * FF_TILE

    def body(prev_ref, cur_ref, next_ref, w_ref, b_ref, o_ref, ext_ref):
        i = pl.program_id(1)
        _ffn_fill(ext_ref, prev_ref, cur_ref, next_ref, i, n_tiles, tm)
        for r0 in range(0, tm, FF_STRIP):
            upc = _ffn_conv(ext_ref, w_ref, b_ref, FF_HALO + r0, FF_STRIP)
            uv, gv = upc[:, :FF_TILE], upc[:, FF_TILE:]
            o_ref[r0:r0 + FF_STRIP, :] = (gv * jax.nn.sigmoid(gv) * uv).astype(o_ref.dtype)

    return pl.pallas_call(
        body, name="ffn_mid", grid=(c2 // cw, n_tiles),
        in_specs=_conv_specs(s, cw, tm, FF_HALO) + [pl.BlockSpec((FFN_CONV_K, cw), lambda c_, i: (0, c_)),
                                                   pl.BlockSpec((1, cw), lambda c_, i: (0, c_))],
        out_specs=pl.BlockSpec((tm, FF_TILE), lambda c_, i: (i, c_)),
        out_shape=jax.ShapeDtypeStruct((s, c2 // 2), BF16),
        scratch_shapes=[pltpu.VMEM((tm + 2 * FF_HALO, cw), F32)],
        compiler_params=_params(("parallel", "arbitrary")),
    )(up, up, up, w, b)


def _ffn_mid_bwd(up, d_act, w, b, tm=ROW_TILE):
    s, c2 = up.shape
    n_tiles = s // tm
    cw = 2 * FF_TILE
    h8 = SUBLANES

    def gate_bwd(upc, dact):
        uv, gv = upc[:, :FF_TILE], upc[:, FF_TILE:]
        sg = jax.nn.sigmoid(gv)
        return jnp.concatenate([dact * (gv * sg), dact * uv * (sg * (1.0 + gv * (1.0 - sg)))], axis=1)

    def body(up_p, up_c, up_n, da_p, da_c, da_n, w_ref, b_ref, dup_ref, dw_ref, db_ref, ext_ref, dext_ref):
        i = pl.program_id(1)
        _ffn_fill(ext_ref, up_p, up_c, up_n, i, n_tiles, tm)
        dws = [jnp.zeros((1, cw), F32) for _ in range(FFN_CONV_K)]
        dbs = jnp.zeros((1, cw), F32)
        for r0 in range(0, tm, FF_STRIP):
            d_c = gate_bwd(_ffn_conv(ext_ref, w_ref, b_ref, FF_HALO + r0, FF_STRIP),
                           da_c[r0:r0 + FF_STRIP, :].astype(F32))
            dext_ref[FF_HALO + r0:FF_HALO + r0 + FF_STRIP, :] = d_c
            for k in range(FFN_CONV_K):
                xk = ext_ref[pl.ds(FF_HALO + r0 + k - FFN_CONV_LEFT, FF_STRIP), :]
                dws[k] = dws[k] + jnp.sum(d_c * xk, axis=0, keepdims=True)
            dbs = dbs + jnp.sum(d_c, axis=0, keepdims=True)
        da_prev = jnp.where(i > 0, da_p[...].astype(F32)[FF_HALO - h8:, :], 0.0)
        da_next = jnp.where(i + 1 < n_tiles, da_n[...].astype(F32)[:h8, :], 0.0)
        dext_ref[FF_HALO - h8:FF_HALO, :] = gate_bwd(_ffn_conv(ext_ref, w_ref, b_ref, FF_HALO - h8, h8), da_prev)
        dext_ref[FF_HALO + tm:FF_HALO + tm + h8, :] = gate_bwd(_ffn_conv(ext_ref, w_ref, b_ref, FF_HALO + tm, h8), da_next)
        for r0 in range(0, tm, FF_STRIP):
            dup = jnp.zeros((FF_STRIP, cw), F32)
            for k in range(FFN_CONV_K):
                dup = dup + dext_ref[pl.ds(FF_HALO + r0 - k + FFN_CONV_LEFT, FF_STRIP), :] * w_ref[k:k + 1, :]
            dup_ref[r0:r0 + FF_STRIP, :] = dup.astype(dup_ref.dtype)

        @pl.when(i == 0)
        def _():
            dw_ref[...] = jnp.zeros_like(dw_ref)
            db_ref[...] = jnp.zeros_like(db_ref)

        for k in range(FFN_CONV_K):
            dw_ref[k:k + 1, :] += dws[k]
        db_ref[...] += dbs

    def half_specs():
        th = tm // FF_HALO
        last = s // FF_HALO - 1
        return [pl.BlockSpec((FF_HALO, FF_TILE), lambda c_, i: (jnp.maximum(i * th - 1, 0), c_)),
                pl.BlockSpec((tm, FF_TILE), lambda c_, i: (i, c_)),
                pl.BlockSpec((FF_HALO, FF_TILE), lambda c_, i: (jnp.minimum((i + 1) * th, last), c_))]

    return pl.pallas_call(
        body, name="ffn_mid_bwd", grid=(c2 // cw, n_tiles),
        in_specs=_conv_specs(s, cw, tm, FF_HALO) + half_specs() + [
            pl.BlockSpec((FFN_CONV_K, cw), lambda c_, i: (0, c_)), pl.BlockSpec((1, cw), lambda c_, i: (0, c_))],
        out_specs=[pl.BlockSpec((tm, cw), lambda c_, i: (i, c_)),
                   pl.BlockSpec((FFN_CONV_K, cw), lambda c_, i: (0, c_)),
                   pl.BlockSpec((1, cw), lambda c_, i: (0, c_))],
        out_shape=[jax.ShapeDtypeStruct((s, c2), BF16), jax.ShapeDtypeStruct((FFN_CONV_K, c2), F32),
                   jax.ShapeDtypeStruct((1, c2), F32)],
        scratch_shapes=[pltpu.VMEM((tm + 2 * FF_HALO, cw), F32), pltpu.VMEM((tm + 2 * FF_HALO, cw), F32)],
        compiler_params=_params(("parallel", "arbitrary")),
    )(up, up, up, d_act, d_act, d_act, w, b)


def _ff_permute(name, w, to_tiles):
    r = w.shape[0]
    nb = D_FF // FF_TILE
    natural = pl.BlockSpec((r, FF_TILE), lambda j, half: (0, half * nb + j))
    tiled = pl.BlockSpec((r, FF_TILE), lambda j, half: (0, 2 * j + half))

    def body(x_ref, o_ref):
        o_ref[...] = x_ref[...]

    return pl.pallas_call(
        body, name=name, grid=(nb, 2),
        in_specs=[natural if to_tiles else tiled], out_specs=tiled if to_tiles else natural,
        out_shape=jax.ShapeDtypeStruct(w.shape, w.dtype), compiler_params=_params(("parallel", "parallel")),
    )(w)


def _ff_to_tiles(w):
    r = w.shape[0]
    return w.reshape(r, 2, D_FF // FF_TILE, FF_TILE).transpose(0, 2, 1, 3).reshape(r, 2 * D_FF)


def _ff_from_tiles(w):
    r = w.shape[0]
    return w.reshape(r, D_FF // FF_TILE, 2, FF_TILE).transpose(0, 2, 1, 3).reshape(r, 2 * D_FF)


SCAN_UNROLL = 8


def _scan(name, a, u, reverse, shifted=False, u_off=0, with_prev=False):
    t, c = a.shape
    us = u.shape[0]
    n8 = t // SUBLANES
    lo, hi = 0, SUBLANES - 1

    def body(a_ref, u_ref, h_ref, *prev_ref):
        row = lax.broadcasted_iota(jnp.int32, (SUBLANES, LANES), 0)
        last = lo if reverse else hi

        def tile(ref, base):
            return ref[pl.ds(pl.multiple_of(base, SUBLANES), SUBLANES), :]

        def local(blk):
            base = blk * SUBLANES
            av = tile(a_ref, base)
            if shifted and reverse:
                nb = tile(a_ref, jnp.minimum(base + SUBLANES, t - SUBLANES))
                edge = jnp.where(base + SUBLANES >= t, 1.0, pltpu.roll(nb, hi, 0))
                av = jnp.where(row < hi, pltpu.roll(av, hi, 0), edge)
            elif shifted:
                pb = tile(a_ref, jnp.maximum(base - SUBLANES, 0))
                edge = jnp.where(base == 0, 1.0, pltpu.roll(pb, 1, 0))
                av = jnp.where(row >= 1, pltpu.roll(av, 1, 0), edge)
            ub = base - u_off
            hv = jnp.where((ub >= 0) & (ub < us), tile(u_ref, jnp.clip(ub, 0, us - SUBLANES)), 0.0)
            for s in (1, 2, 4):
                shift = SUBLANES - s if reverse else s
                ok = (row < SUBLANES - s) if reverse else (row >= s)
                a_sh = jnp.where(ok, pltpu.roll(av, shift, 0), 1.0)
                h_sh = jnp.where(ok, pltpu.roll(hv, shift, 0), 0.0)
                hv = av * h_sh + hv
                av = av * a_sh
            a_last = jnp.sum(jnp.where(row == last, av, 0.0), axis=0, keepdims=True)
            h_last = jnp.sum(jnp.where(row == last, hv, 0.0), axis=0, keepdims=True)
            return base, av, hv, a_last, h_last

        def step(j, carry):
            parts = []
            for k in range(SCAN_UNROLL):
                idx = j * SCAN_UNROLL + k
                parts.append(local((n8 - 1 - idx) if reverse else idx))
            for base, av, hv, a_last, h_last in parts:
                rows = pl.ds(pl.multiple_of(base, SUBLANES), SUBLANES)
                h_true = av * carry + hv
                h_ref[rows, :] = h_true
                if with_prev:
                    if reverse:
                        prev_ref[0][rows, :] = jnp.where(row < hi, pltpu.roll(h_true, hi, 0), carry)
                    else:
                        prev_ref[0][rows, :] = jnp.where(row >= 1, pltpu.roll(h_true, 1, 0), carry)
                carry = a_last * carry + h_last
            return carry

        lax.fori_loop(0, n8 // SCAN_UNROLL, step, jnp.zeros((1, LANES), F32))

    spec = pl.BlockSpec((t, LANES), lambda j: (0, j))
    n_out = 2 if with_prev else 1
    res = pl.pallas_call(
        body, name=name, grid=(c // LANES,),
        in_specs=[spec, pl.BlockSpec((us, LANES), lambda j: (0, j))],
        out_specs=[spec] * n_out, out_shape=[jax.ShapeDtypeStruct((t, c), F32)] * n_out,
        compiler_params=_params(("parallel",)),
    )(a, u)
    return res if with_prev else res[0]


NT_DIMS = (((1,), (1,)), ((), ()))


LOG2E = 1.4426950408889634
SCALE2 = MLA_SCALE * LOG2E
ATTN_TILES = (512, 256, 128)
KEY_CHUNKS = (768, 512, 256, 128)
QUERY_CHUNKS = (1024, 512, 256, 128)


def _attn_fwd(q, k, v, ride_arrs, ride_modes):
    h, s, _ = q.shape
    t = k.shape[1]
    tq = _pick(s, ATTN_TILES)
    ck = _pick(t, KEY_CHUNKS)
    n = len(ride_arrs)

    def body(*refs):
        q_ref, k_ref, v_ref = refs[:3]
        o_ref, lse_ref = refs[3 + n:5 + n]
        hh, i = pl.program_id(0), pl.program_id(1)
        finish = _riding_exchange(refs, 3, 2, n, ride_modes, (hh == 0) & (i == 0),
                                  (hh == h - 1) & (i == s // tq - 1))
        qv = q_ref[0]
        def scores(j):
            return lax.dot_general(qv, k_ref[0, j * ck:(j + 1) * ck, :], NT_DIMS, preferred_element_type=F32)

        m = l = acc = None
        s_next = scores(0)
        for j in range(t // ck):
            vj = v_ref[0, j * ck:(j + 1) * ck, :]
            s2, s_next = s_next, (scores(j + 1) if j + 1 < t // ck else None)
            mj = jnp.max(s2, axis=-1, keepdims=True)
            m_new = mj if j == 0 else jnp.maximum(m, mj)
            p = jnp.exp2(s2 - m_new)
            lj = jnp.sum(p, axis=-1, keepdims=True)
            pv = jnp.dot(p.astype(BF16), vj, preferred_element_type=F32)
            if j == 0:
                l, acc = lj, pv
            else:
                alpha = jnp.exp2(m - m_new)
                l, acc = alpha * l + lj, alpha * acc + pv
            m = m_new
        o_ref[0] = acc / l
        lse_ref[0] = _rows8(jnp.broadcast_to(m + jnp.log2(l), (tq, LANES)))
        finish()

    any_spec = pl.BlockSpec(memory_space=pl.ANY)
    res = pl.pallas_call(
        body, name="attn_fwd", grid=(h, s // tq),
        in_specs=[pl.BlockSpec((1, tq, HEAD_PAD), lambda hh, i: (hh, i, 0)),
                  pl.BlockSpec((1, t, HEAD_PAD), lambda hh, i: (hh, 0, 0)),
                  pl.BlockSpec((1, t, VDIM), lambda hh, i: (hh, 0, 0))] + [any_spec] * n,
        out_specs=[pl.BlockSpec((1, tq, VDIM), lambda hh, i: (hh, i, 0)),
                   pl.BlockSpec((1, SUBLANES, tq), lambda hh, i: (hh, 0, i))] + [any_spec] * n,
        out_shape=[jax.ShapeDtypeStruct((h, s, VDIM), F32), jax.ShapeDtypeStruct((h, SUBLANES, s), F32)]
        + _exchange_shapes(ride_arrs, ride_modes),
        scratch_shapes=_exchange_sems(n),
        compiler_params=pltpu.CompilerParams(dimension_semantics=("arbitrary", "arbitrary"),
                                             vmem_limit_bytes=VMEM_LIMIT, has_side_effects=True),
    )(q, k, v, *ride_arrs)
    return res[0], res[1], res[2:]


TN_DIMS = (((0,), (0,)), ((), ()))


def _rows8(x):
    return jnp.transpose(x)[:SUBLANES, :]


def _attn_delta(do_src, o_t, tm=ROW_TILE):
    s = o_t.shape[0]
    w = HEADS * VDIM

    def body(do_ref, o_ref, d_ref):
        prod = do_ref[...] * o_ref[...]
        for c in range(w // LANES):
            x = prod[:, c * LANES:(c + 1) * LANES]
            shift = VDIM // 2
            while shift:
                x = x + pltpu.roll(x, LANES - shift, 1)
                shift //= 2
            xt = jnp.transpose(x)
            d_ref[2 * c] = xt[:SUBLANES, :]
            d_ref[2 * c + 1] = xt[VDIM:VDIM + SUBLANES, :]

    return pl.pallas_call(
        body, name="attn_delta", grid=(s // tm,),
        in_specs=[pl.BlockSpec((tm, w), lambda i: (i, 1)), pl.BlockSpec((tm, w), lambda i: (i, 0))],
        out_specs=pl.BlockSpec((HEADS, SUBLANES, tm), lambda i: (0, 0, i)),
        out_shape=jax.ShapeDtypeStruct((HEADS, SUBLANES, s), F32),
        compiler_params=_params(("parallel",)),
    )(do_src, o_t)


def _attn_bwd(q, k, v, do, lse_row, delta_row, ride_arrs, ride_modes):
    h, s, _ = q.shape
    t = k.shape[1]
    tk = _pick(t, (768,) + ATTN_TILES)
    cq = _pick(s, QUERY_CHUNKS)
    n = len(ride_arrs)

    def body(*refs):
        q_ref, k_ref, v_ref, do_ref, lse_ref, delta_ref = refs[:6]
        dq_ref, dk_ref, dv_ref = refs[6 + n:9 + n]
        hh, i = pl.program_id(0), pl.program_id(1)
        finish = _riding_exchange(refs, 6, 3, n, ride_modes, (hh == 0) & (i == 0),
                                  (hh == h - 1) & (i == t // tk - 1))

        @pl.when(i == 0)
        def _():
            dq_ref[...] = jnp.zeros_like(dq_ref)

        kt, vt = k_ref[0], v_ref[0]
        dk = dv = None
        for j in range(s // cq):
            rows = slice(j * cq, (j + 1) * cq)
            qj, doj = q_ref[0, rows, :], do_ref[0, rows, :]
            pt = jnp.exp2(lax.dot_general(kt, qj, NT_DIMS, preferred_element_type=F32) - lse_ref[0, 0:1, rows])
            dv_j = jnp.dot(pt.astype(BF16), doj, preferred_element_type=F32)
            dpt = lax.dot_general(vt, doj, NT_DIMS, preferred_element_type=F32)
            dst = (pt * (dpt - delta_ref[0, 0:1, rows])).astype(BF16)
            dk_j = jnp.dot(dst, qj, preferred_element_type=F32)
            dq_ref[0, rows, :] += lax.dot_general(dst, kt, TN_DIMS, preferred_element_type=F32)
            dk, dv = (dk_j, dv_j) if j == 0 else (dk + dk_j, dv + dv_j)
        dk_ref[0] = dk * (1.0 / LOG2E)
        dv_ref[0] = dv
        finish()

    any_spec = pl.BlockSpec(memory_space=pl.ANY)
    res = pl.pallas_call(
        body, name="attn_bwd", grid=(h, t // tk),
        in_specs=[pl.BlockSpec((1, s, HEAD_PAD), lambda hh, i: (hh, 0, 0)),
                  pl.BlockSpec((1, tk, HEAD_PAD), lambda hh, i: (hh, i, 0)),
                  pl.BlockSpec((1, tk, VDIM), lambda hh, i: (hh, i, 0)),
                  pl.BlockSpec((1, s, VDIM), lambda hh, i: (hh, 0, 0)),
                  pl.BlockSpec((1, SUBLANES, s), lambda hh, i: (hh, 0, 0)),
                  pl.BlockSpec((1, SUBLANES, s), lambda hh, i: (hh, 0, 0))] + [any_spec] * n,
        out_specs=[pl.BlockSpec((1, s, HEAD_PAD), lambda hh, i: (hh, 0, 0)),
                   pl.BlockSpec((1, tk, HEAD_PAD), lambda hh, i: (hh, i, 0)),
                   pl.BlockSpec((1, tk, VDIM), lambda hh, i: (hh, i, 0))] + [any_spec] * n,
        out_shape=[jax.ShapeDtypeStruct((h, s, HEAD_PAD), F32), jax.ShapeDtypeStruct((h, t, HEAD_PAD), F32),
                   jax.ShapeDtypeStruct((h, t, VDIM), F32)] + _exchange_shapes(ride_arrs, ride_modes),
        scratch_shapes=_exchange_sems(n),
        compiler_params=pltpu.CompilerParams(dimension_semantics=("arbitrary", "arbitrary"),
                                             vmem_limit_bytes=VMEM_LIMIT, has_side_effects=True),
    )(q, k, v, do, lse_row, delta_row, *ride_arrs)
    return res[0], res[1], res[2], res[3:]


def _exchange_shapes(arrs, modes):
    return [jax.ShapeDtypeStruct((N_DEV,) + a.shape if md == 'ag' else a.shape, a.dtype) for a, md in zip(arrs, modes)]


def _exchange_sems(n):
    return [pltpu.SemaphoreType.DMA((n, N_DEV - 1)), pltpu.SemaphoreType.DMA((n, N_DEV - 1)),
            pltpu.SemaphoreType.DMA((n,))]


def _exchange_copies(ins, outs, modes, send_sems, recv_sems, local_sems):
    x, y, c = lax.axis_index("x"), lax.axis_index("y"), lax.axis_index("c")
    me = 4 * x + 2 * y + c
    copies = []
    for a in range(len(ins)):
        ag = modes[a] == 'ag'
        copies.append(pltpu.make_async_copy(ins[a] if ag else ins[a].at[me], outs[a].at[me], local_sems.at[a]))
        for k in range(1, N_DEV):
            px = 1 - x if k & 4 else x
            py = 1 - y if k & 2 else y
            pc = 1 - c if k & 1 else c
            src = ins[a] if ag else ins[a].at[4 * px + 2 * py + pc]
            copies.append(pltpu.make_async_remote_copy(
                src_ref=src, dst_ref=outs[a].at[me], send_sem=send_sems.at[a, k - 1],
                recv_sem=recv_sems.at[a, k - 1], device_id=(px, py, pc), device_id_type=MESH))
    return copies


def _exchange(name, arrs, modes):
    n = len(arrs)

    def body(*refs):
        copies = _exchange_copies(refs[:n], refs[n:2 * n], modes, *refs[2 * n:])
        for cp in copies:
            cp.start()
        for cp in copies:
            cp.wait()

    return pl.pallas_call(
        body, name=name,
        in_specs=[pl.BlockSpec(memory_space=pl.ANY)] * n,
        out_specs=[pl.BlockSpec(memory_space=pl.ANY)] * n,
        out_shape=_exchange_shapes(arrs, modes),
        scratch_shapes=_exchange_sems(n),
        compiler_params=pltpu.CompilerParams(has_side_effects=True),
    )(*arrs)


def _riding_exchange(refs, n_in, n_out, n, modes, first, last):
    ins = refs[n_in:n_in + n]
    outs = refs[n_in + n + n_out:n_in + 2 * n + n_out]
    sems = refs[n_in + 2 * n + n_out:n_in + 2 * n + n_out + 3]

    @pl.when(first)
    def _():
        for cp in _exchange_copies(ins, outs, modes, *sems):
            cp.start()

    def finish():
        @pl.when(last)
        def _():
            for cp in _exchange_copies(ins, outs, modes, *sems):
                cp.wait()

    return finish


def _adamw(name, w, m, v, gparts):
    r, c = w.shape
    npart = gparts.shape[0]
    tr = _pick(r, (256, 128, 64, 32, 16, 8))
    spec = pl.BlockSpec((tr, c), lambda i: (i, 0))

    def body(w_ref, m_ref, v_ref, g_ref, go_ref, d_ref, mo_ref, vo_ref):
        g = g_ref[0]
        for p in range(1, npart):
            g = g + g_ref[p]
        m1 = ADAM_B1 * m_ref[...] + (1.0 - ADAM_B1) * g
        v1 = ADAM_B2 * v_ref[...] + (1.0 - ADAM_B2) * (g * g)
        m_hat = m1 / (1.0 - ADAM_B1 ** ADAM_STEP)
        v_hat = v1 / (1.0 - ADAM_B2 ** ADAM_STEP)
        go_ref[...] = g
        d_ref[...] = -ADAM_LR * (m_hat / (jnp.sqrt(v_hat) + ADAM_EPS) + ADAM_WD * w_ref[...])
        mo_ref[...] = m1
        vo_ref[...] = v1

    return pl.pallas_call(
        body, name=name, grid=(r // tr,),
        in_specs=[spec, spec, spec, pl.BlockSpec((npart, tr, c), lambda i: (0, i, 0))],
        out_specs=[spec] * 4, out_shape=[jax.ShapeDtypeStruct((r, c), F32)] * 4,
        compiler_params=_params(("parallel",)),
    )(w, m, v, gparts)


def _pack(arrs, rows):
    flat = jnp.concatenate([a.reshape(-1) for a in arrs])
    return jnp.pad(flat, (0, rows * LANES - flat.shape[0])).reshape(rows, LANES)


def _unpack(packed, shapes):
    flat, out, off = packed.reshape(-1), [], 0
    for s in shapes:
        n = math.prod(s)
        out.append(flat[off:off + n].reshape(s))
        off += n
    return out


def _pack_rows(n_elems):
    return -(-n_elems // (SUBLANES * LANES)) * SUBLANES


def _cols_from_shards(g):
    return g.transpose(1, 0, 2).reshape(g.shape[1], N_DEV * g.shape[2])


def _cols_to_shards(w):
    r, c = w.shape
    return w.reshape(r, N_DEV, c // N_DEV).transpose(1, 0, 2)


def _rope_tables(n_lat, n_ctx):
    inv = ROPE_BASE ** (-jnp.arange(ROPE_PAIRS, dtype=F32) / ROPE_PAIRS)
    seg = [(l - NOPE) // ROPE_PAIRS if NOPE <= l < QK else -1 for l in range(HEAD_PAD)]
    freq = jnp.take(inv, jnp.array([(l - NOPE) % ROPE_PAIRS if NOPE <= l < QK else 0 for l in range(HEAD_PAD)]))
    seg = jnp.array(seg)
    f_row = jnp.where((seg == 0) | (seg == 1), freq, 0.0)[None, :]
    f_col = jnp.where((seg == 2) | (seg == 3), freq, 0.0)[None, :]
    t = jnp.arange(n_ctx + n_lat) - n_ctx
    pos = jnp.maximum(t, 0)
    row = jnp.where(t >= 0, pos // GRID_W, 0).astype(F32)[:, None]
    col = jnp.where(t >= 0, pos % GRID_W, 0).astype(F32)[:, None]
    ang = row * f_row + col * f_col
    sin = jnp.sin(ang)
    sin_up = jnp.where(((seg == 0) | (seg == 2))[None, :], -sin, 0.0)
    sin_dn = jnp.where(((seg == 1) | (seg == 3))[None, :], sin, 0.0)
    return jnp.cos(ang), sin_up, sin_dn


def _rope(x, cos_t, sin_up, sin_dn):
    return x * cos_t + pltpu.roll(x, HEAD_PAD - ROPE_PAIRS, 1) * sin_up + pltpu.roll(x, ROPE_PAIRS, 1) * sin_dn


def _rope_t(dy, cos_t, sin_up, sin_dn):
    return (dy * cos_t + pltpu.roll(dy * sin_up, ROPE_PAIRS, 1)
            + pltpu.roll(dy * sin_dn, HEAD_PAD - ROPE_PAIRS, 1))


def _softplus(x):
    return jnp.maximum(x, 0.0) + jnp.log(1.0 + jnp.exp(-jnp.abs(x)))


def _gates(z, xcv, lam_sp):
    outs = []
    for d in range(2):
        r = jax.nn.sigmoid(z[:, (2 * d) * LRU_W:(2 * d + 1) * LRU_W])
        ig = jax.nn.sigmoid(z[:, (2 * d + 1) * LRU_W:(2 * d + 2) * LRU_W])
        log_a = -LRU_C * r * lam_sp[:, d * LRU_W:(d + 1) * LRU_W]
        a = jnp.exp(log_a)
        u = jnp.sqrt(-jnp.tanh(log_a) * (a * a + 1.0)) * (ig * xcv)
        outs += [a, u]
    return tuple(outs)


def kernel(x, c, ctx, c_ctx, w_mod, b_mod, g_pre_mix, g_post_mix, g_pre_ffn, g_post_ffn, w_in, lru_conv_w, lru_conv_b, lru_w_a, lru_b_a, lru_w_x, lru_b_x, lru_lambda, mla_g_q, mla_w_uq, mla_g_kv, mla_w_ukv, w_out, ffn_w_up, ffn_conv_w, ffn_conv_b, ffn_w_down, loss_target, m_c_ctx, m_w_mod, m_b_mod, m_g_pre_mix, m_g_post_mix, m_g_pre_ffn, m_g_post_ffn, m_w_in, m_lru_conv_w, m_lru_conv_b, m_lru_w_a, m_lru_b_a, m_lru_w_x, m_lru_b_x, m_lru_lambda, m_mla_g_q, m_mla_w_uq, m_mla_g_kv, m_mla_w_ukv, m_w_out, m_ffn_w_up, m_ffn_conv_w, m_ffn_conv_b, m_ffn_w_down, v_c_ctx, v_w_mod, v_b_mod, v_g_pre_mix, v_g_post_mix, v_g_pre_ffn, v_g_post_ffn, v_w_in, v_lru_conv_w, v_lru_conv_b, v_lru_w_a, v_lru_b_a, v_lru_w_x, v_lru_b_x, v_lru_lambda, v_mla_g_q, v_mla_w_uq, v_mla_g_kv, v_mla_w_ukv, v_w_out, v_ffn_w_up, v_ffn_conv_w, v_ffn_conv_b, v_ffn_w_down):
    W = dict(c_ctx=c_ctx, w_mod=w_mod, b_mod=b_mod, g_pre_mix=g_pre_mix, g_post_mix=g_post_mix, g_pre_ffn=g_pre_ffn,
             g_post_ffn=g_post_ffn, w_in=w_in, lru_conv_w=lru_conv_w, lru_conv_b=lru_conv_b, lru_w_a=lru_w_a,
             lru_b_a=lru_b_a, lru_w_x=lru_w_x, lru_b_x=lru_b_x, lru_lambda=lru_lambda, mla_g_q=mla_g_q,
             mla_w_uq=mla_w_uq, mla_g_kv=mla_g_kv, mla_w_ukv=mla_w_ukv, w_out=w_out, ffn_w_up=ffn_w_up,
             ffn_conv_w=ffn_conv_w, ffn_conv_b=ffn_conv_b, ffn_w_down=ffn_w_down)
    M = dict(c_ctx=m_c_ctx, w_mod=m_w_mod, b_mod=m_b_mod, g_pre_mix=m_g_pre_mix, g_post_mix=m_g_post_mix,
             g_pre_ffn=m_g_pre_ffn, g_post_ffn=m_g_post_ffn, w_in=m_w_in, lru_conv_w=m_lru_conv_w,
             lru_conv_b=m_lru_conv_b, lru_w_a=m_lru_w_a, lru_b_a=m_lru_b_a, lru_w_x=m_lru_w_x, lru_b_x=m_lru_b_x,
             lru_lambda=m_lru_lambda, mla_g_q=m_mla_g_q, mla_w_uq=m_mla_w_uq, mla_g_kv=m_mla_g_kv,
             mla_w_ukv=m_mla_w_ukv, w_out=m_w_out, ffn_w_up=m_ffn_w_up, ffn_conv_w=m_ffn_conv_w,
             ffn_conv_b=m_ffn_conv_b, ffn_w_down=m_ffn_w_down)
    V = dict(c_ctx=v_c_ctx, w_mod=v_w_mod, b_mod=v_b_mod, g_pre_mix=v_g_pre_mix, g_post_mix=v_g_post_mix,
             g_pre_ffn=v_g_pre_ffn, g_post_ffn=v_g_post_ffn, w_in=v_w_in, lru_conv_w=v_lru_conv_w,
             lru_conv_b=v_lru_conv_b, lru_w_a=v_lru_w_a, lru_b_a=v_lru_b_a, lru_w_x=v_lru_w_x, lru_b_x=v_lru_b_x,
             lru_lambda=v_lru_lambda, mla_g_q=v_mla_g_q, mla_w_uq=v_mla_w_uq, mla_g_kv=v_mla_g_kv,
             mla_w_ukv=v_mla_w_ukv, w_out=v_w_out, ffn_w_up=v_ffn_w_up, ffn_conv_w=v_ffn_conv_w,
             ffn_conv_b=v_ffn_conv_b, ffn_w_down=v_ffn_w_down)

    D = D_MODEL
    S, CN = x.shape[1], ctx.shape[1]
    T = S + CN
    TM = ROW_TILE
    ct, ns, nt = CN // TM, S // TM, T // TM
    me = 4 * lax.axis_index("x") + 2 * lax.axis_index("y") + lax.axis_index("c")

    lat = lambda i: i + ct
    swp = lambda i: jnp.where(i < ct, i + ns, i - ct)
    lat_or_0 = lambda i: jnp.maximum(i - ct, 0)

    small_shapes = [W[n].shape[1:] for n in SMALL_SHARDED] + [(D,)]
    n_small = sum(math.prod(s) for s in small_shapes)
    small_rows = _pack_rows(n_small)
    small_loc = _pack([W[n][0] for n in SMALL_SHARDED] + [c[0]], small_rows)
    early = ['w_in', 'mla_w_uq', 'mla_w_ukv']
    late = ['w_out', 'ffn_w_up', 'ffn_w_down']
    big = early + late
    gathered = _exchange("gather_weights", [W[n][0].astype(BF16) for n in early] + [small_loc], ['ag'] * 4)
    gw = dict(zip(early, gathered[:3]))
    small_all = [_unpack(gathered[3][d], small_shapes) for d in range(N_DEV)]
    full_small = {n: jnp.concatenate([small_all[d][j] for d in range(N_DEV)], axis=-1)
                  for j, n in enumerate(SMALL_SHARDED)}
    c_all = jnp.stack([small_all[d][-1] for d in range(N_DEV)])

    w_in_f = _cols_from_shards(gw['w_in'])
    w_in_p = jnp.concatenate([w_in_f[:, :OFF_KR], jnp.zeros((D, NOPE), BF16), w_in_f[:, OFF_KR:],
                              jnp.zeros((D, HEAD_PAD - QK), BF16)], axis=1)
    w_uq_f = _cols_from_shards(gw['mla_w_uq']).reshape(Q_RANK, HEADS, QK)
    wq_p = jnp.pad(w_uq_f, ((0, 0), (0, 0), (0, HEAD_PAD - QK))).reshape(Q_RANK, HEADS * HEAD_PAD)
    w_ukv_f = _cols_from_shards(gw['mla_w_ukv']).reshape(KV_RANK, HEADS, NOPE + VDIM)
    wk_p = jnp.pad(w_ukv_f[:, :, :NOPE], ((0, 0), (0, 0), (0, HEAD_PAD - NOPE))).reshape(KV_RANK, HEADS * HEAD_PAD)
    wv_f = w_ukv_f[:, :, NOPE:].reshape(KV_RANK, HEADS * VDIM)

    lru_cw, lru_ba, lru_bx, lru_lam, ffn_cw = [full_small[n] for n in SMALL_SHARDED]
    ffn_cw_t, ffn_cb_t = _ff_to_tiles(ffn_cw), _ff_to_tiles(ffn_conv_b)

    def block_diag(w):
        eye = jnp.eye(LRU_HEADS, dtype=w.dtype)
        return jnp.einsum('hij,hg->higj', w, eye).reshape(LRU_W, LRU_W)

    w_gate = jnp.concatenate([block_diag(lru_w_a[0, 0]), block_diag(lru_w_x[0, 0]),
                              block_diag(lru_w_a[0, 1]), block_diag(lru_w_x[0, 1])], axis=1).astype(BF16)
    b_gate = jnp.concatenate([lru_ba[0], lru_bx[0], lru_ba[1], lru_bx[1]])[None]
    lam_row = lru_lam.reshape(1, 2 * LRU_W)

    c16 = jnp.concatenate([c_all, c_ctx[None], jnp.zeros((2 * SUBLANES - N_DEV - 1, D), F32)])
    ncol = w_mod.shape[2]
    b_mod_loc = lax.dynamic_slice(b_mod, (0, me * ncol), (1, ncol))

    def mod_fwd(c16_r, w_r, b_r):
        c16_v = c16_r[...]
        sl = c16_v * jax.nn.sigmoid(c16_v)
        return (jnp.dot(sl.astype(BF16), w_r[...].astype(BF16), preferred_element_type=F32) + b_r[...],)

    (mod_part,) = _single("mod_fwd", mod_fwd, [c16, w_mod[0], b_mod_loc], [((2 * SUBLANES, ncol), F32)])
    (mod_g,) = _exchange("gather_mod", [mod_part], ['ag'])
    mod_all = _cols_from_shards(mod_g)
    mod_lat = lax.dynamic_slice(mod_all, (me, 0), (1, N_MOD * D)).reshape(N_MOD, D)
    mod_ctx = mod_all[N_DEV].reshape(N_MOD, D)

    xs, tgt = x[0], loss_target[0]
    xa_rows = [_row(ctx[0], lambda i: jnp.minimum(i, ct - 1)), _row(xs, lat_or_0)]

    def sel_mod(i, ml, mc, r0):
        sh = jnp.where(i < ct, mc[r0:r0 + 1, :], ml[r0:r0 + 1, :])
        sc = jnp.where(i < ct, mc[r0 + 1:r0 + 2, :], ml[r0 + 1:r0 + 2, :])
        return sh, sc

    def pre_fn(xv, g, sh, sc):
        return _rms(xv, g) * (1.0 + sc) + sh

    def k_pre(i, rv, bv):
        sh, sc = sel_mod(i, bv[1], bv[2], 0)
        return (pre_fn(jnp.where(i < ct, rv[0], rv[1]), bv[0][...], sh, sc),), ()

    (h_pre,) = _rowwise("pre_mix", k_pre, nt, xa_rows, [g_pre_mix, mod_lat, mod_ctx], [((T, D), BF16, None)])
    proj = _mm("in_proj", h_pre, w_in_p, 'nn')

    xcv = _dwconv_fwd("lru_conv", proj, lru_cw, lru_conv_b, LRU_CONV_LEFT, (0, ct))

    def k_gates(i, rv, bv):
        xv = rv[0]
        z = jnp.dot(xv.astype(BF16), bv[0][...], preferred_element_type=F32) + bv[1][...]
        return _gates(z, xv, _bc(_softplus(-bv[2][...]), TM)), ()

    a0, u0, a1, u1 = _rowwise("lru_gates", k_gates, nt, [_row(xcv)], [w_gate, b_gate, lam_row],
                              [((T, LRU_W), F32, None), ((T, LRU_W), F32, None),
                               ((T, LRU_W), F32, swp), ((T, LRU_W), F32, swp)])
    h0, hprev0 = _scan("lru_scan_f", a0, u0, False, with_prev=True)
    h1, hprev1 = _scan("lru_scan_r", a1, u1, True, with_prev=True)

    cos_q, sup_q, sdn_q = _rope_tables(S, 0)
    cos_k, sup_k, sdn_k = _rope_tables(S, CN)
    cq_row = _row(proj, lat, (Q_RANK, OFF_CQ // Q_RANK))
    ckv_row = _row(proj, None, (KV_RANK, OFF_CKV // KV_RANK))

    def heads_of(xl):
        return [xl[:, hh * HEAD_PAD:(hh + 1) * HEAD_PAD] for hh in range(HEADS)]

    def k_q_path(i, rv, bv):
        cqv, ctb, sub, sdb = rv
        cqn_v = _rms(cqv, bv[0][...]).astype(BF16)
        ql = jnp.dot(cqn_v, bv[1][...], preferred_element_type=F32)
        return (cqn_v, jnp.stack([_rope(qh, ctb, sub, sdb) * SCALE2 for qh in heads_of(ql)])), ()

    cqn, q = _rowwise("q_path", k_q_path, ns, [cq_row, _row(cos_q), _row(sup_q), _row(sdn_q)], [mla_g_q, wq_p],
                      [((S, Q_RANK), BF16, None), ((HEADS, S, HEAD_PAD), BF16, None)])

    def k_kv_path(i, rv, bv):
        ckv_v, krp, ctb, sub, sdb = rv
        ckvn_v = _rms(ckv_v, bv[0][...]).astype(BF16)
        kl = jnp.dot(ckvn_v, bv[1][...], preferred_element_type=F32)
        vl = jnp.dot(ckvn_v, bv[2][...], preferred_element_type=F32)
        kr = _rope(krp, ctb, sub, sdb)
        return (ckvn_v, jnp.stack([kh + kr for kh in heads_of(kl)]), vl), ()

    ckvn, k, v_lin = _rowwise(
        "kv_path", k_kv_path, nt,
        [ckv_row, _row(proj, None, (HEAD_PAD, OFF_KR // HEAD_PAD)), _row(cos_k), _row(sup_k), _row(sdn_k)],
        [mla_g_kv, wk_p, wv_f],
        [((T, KV_RANK), BF16, None), ((HEADS, T, HEAD_PAD), BF16, None), ((T, HEADS * VDIM), BF16, None)])
    v = v_lin.reshape(T, HEADS, VDIM).transpose(1, 0, 2)

    o, lse, late_g = _attn_fwd(q, k, v, [W[n][0].astype(BF16) for n in late], ['ag'] * 3)
    w_out_f = late_g[0].reshape(D, D)
    w_up_t = _ff_permute("w_up_to_tiles", _cols_from_shards(late_g[1]), True)
    w_down_f = late_g[2].reshape(D_FF, D)
    o_t = o.transpose(1, 0, 2).reshape(S, HEADS * VDIM)

    def lru_out_fn(hf, hr, gr):
        return (hf + hr) * jax.nn.gelu(gr)

    def k_mix_in(i, rv, bv):
        return (jnp.concatenate([lru_out_fn(rv[0], rv[1], rv[2]), rv[3]], axis=1),), ()

    gr_row = _row(proj, lat, (LRU_W, OFF_GR // LRU_W))
    y_in = _rowwise("mix_in", k_mix_in, ns, [_row(h0, lat), _row(h1), gr_row, _row(o_t)], [],
                    [((S, D), BF16, None)])[0]
    y = _mm("out_proj", y_in, w_out_f, 'nn')

    def post_mix_fn(xv, yv, gt, g):
        return xv + gt * _rms(yv, g)

    def k_post_mix(i, rv, bv):
        ml = bv[0]
        x1v = post_mix_fn(rv[0], rv[1], ml[2:3, :], bv[1][...])
        return (x1v, pre_fn(x1v, bv[2][...], ml[3:4, :], ml[4:5, :])), ()

    x1, h2 = _rowwise("post_mix", k_post_mix, ns, [_row(xs), _row(y)], [mod_lat, g_post_mix, g_pre_ffn],
                      [((S, D), F32, None), ((S, D), BF16, None)])
    up = _mm("ffn_up", h2, w_up_t, 'nn', out_dtype=BF16)
    act = _ffn_mid_fwd(up, ffn_cw_t, ffn_cb_t)
    f = _mm("ffn_down", act, w_down_f, 'nn')

    def loss_fn(x1v, fv, gt, g, tg):
        x2 = x1v + gt * _rms(fv, g)
        err = x2 - tg
        return 0.5 * jnp.sum(jnp.mean(err * err, axis=-1))

    def k_loss(i, rv, bv):
        gtb, gb = _bc(bv[0][5:6, :], TM), _bc(bv[1][...], TM)
        val, (dx1v, dfv, dgt, dg) = jax.value_and_grad(loss_fn, argnums=(0, 1, 2, 3))(rv[0], rv[1], gtb, gb, rv[2])
        return (dx1v, dfv), (jnp.full((1, LANES), val, F32), _rs(dgt), _rs(dg))

    dx1_a, df, loss_acc, d_gt2, d_g_post_ffn = _rowwise(
        "loss_bwd", k_loss, ns, [_row(x1), _row(f), _row(tgt)], [mod_lat, g_post_ffn],
        [((S, D), F32, None), ((S, D), BF16, None)], [(1, LANES), (1, D), (1, D)])

    d_act = _mm("ffn_down_dx", df, w_down_f, 'nt', out_dtype=BF16)
    d_w_down = _mm("ffn_down_dw", act, df, 'tn')
    d_up, d_ffn_cw_t, d_ffn_cb_t = _ffn_mid_bwd(up, d_act, ffn_cw_t, ffn_cb_t)
    d_ffn_cw, d_ffn_cb = _ff_from_tiles(d_ffn_cw_t), _ff_from_tiles(d_ffn_cb_t)
    d_h2 = _mm("ffn_up_dx", d_up, w_up_t, 'nt')
    d_w_up = _ff_permute("d_w_up_from_tiles", _mm("ffn_up_dw", h2, d_up, 'tn'), False)

    def k_pre_ffn_bwd(i, rv, bv):
        ml = bv[0]
        gb, shb, scb = _bc(bv[1][...], TM), _bc(ml[3:4, :], TM), _bc(ml[4:5, :], TM)
        _, pull = jax.vjp(pre_fn, rv[0], gb, shb, scb)
        dxv, dg, dsh, dsc = pull(rv[1])
        return (rv[2] + dxv,), (_rs(dg), _rs(dsh), _rs(dsc))

    dx1, d_g_pre_ffn, d_sh2, d_sc2 = _rowwise(
        "pre_ffn_bwd", k_pre_ffn_bwd, ns, [_row(x1), _row(d_h2), _row(dx1_a)], [mod_lat, g_pre_ffn],
        [((S, D), F32, None)], [(1, D), (1, D), (1, D)])

    def k_post_mix_bwd(i, rv, bv):
        gtb, gb = _bc(bv[0][2:3, :], TM), _bc(bv[1][...], TM)
        _, pull = jax.vjp(post_mix_fn, rv[0], rv[1], gtb, gb)
        _, dyv, dgt, dg = pull(rv[2])
        return (dyv,), (_rs(dgt), _rs(dg))

    dy, d_gt1, d_g_post_mix = _rowwise(
        "post_mix_bwd", k_post_mix_bwd, ns, [_row(xs), _row(y), _row(dx1)], [mod_lat, g_post_mix],
        [((S, D), BF16, None)], [(1, D), (1, D)])
    d_y_in = _mm("out_proj_dx", dy, w_out_f, 'nt')
    d_w_out = _mm("out_proj_dw", y_in, dy, 'tn')

    def k_lru_out_bwd(i, rv, bv):
        _, pull = jax.vjp(lru_out_fn, rv[0], rv[1], rv[2])
        dhf, _, dgr = pull(rv[3])
        return (dhf, dgr), ()

    d_hsum, d_gr = _rowwise("lru_out_bwd", k_lru_out_bwd, ns,
                            [_row(h0, lat), _row(h1), gr_row, _row(d_y_in, None, (LRU_W, 0))], [],
                            [((S, LRU_W), F32, None), ((S, LRU_W), F32, None)])

    do_b = d_y_in[:, LRU_W:].astype(BF16).reshape(S, HEADS, VDIM).transpose(1, 0, 2)
    delta = _attn_delta(d_y_in, o_t)
    late_sends = [d_w_out.reshape(N_DEV, D // N_DEV, D), _cols_to_shards(d_w_up),
                  d_w_down.reshape(N_DEV, D_FF // N_DEV, D)]
    dq, dk, dv, late_recv = _attn_bwd(q, k, v, do_b, lse, delta, late_sends, ['a2a'] * 3)

    def rms_bwd(xv, g, dy):
        _, pull = jax.vjp(_rms, xv, _bc(g, TM))
        dxv, dg = pull(dy)
        return dxv, _rs(dg)

    def k_q_path_bwd(i, rv, bv):
        dqv, ctb, sub, sdb, cqv = rv
        dql = jnp.concatenate([_rope_t(dqv[hh] * MLA_SCALE, ctb, sub, sdb) for hh in range(HEADS)], axis=1).astype(BF16)
        d_cqn_v = lax.dot_general(dql, bv[1][...], NT_DIMS, preferred_element_type=F32)
        dxv, dg = rms_bwd(cqv, bv[0][...], d_cqn_v)
        return (dql, dxv), (dg,)

    dq_lin, d_cq, d_g_q = _rowwise(
        "q_path_bwd", k_q_path_bwd, ns, [_row(dq), _row(cos_q), _row(sup_q), _row(sdn_q), cq_row], [mla_g_q, wq_p],
        [((S, HEADS * HEAD_PAD), BF16, None), ((S, Q_RANK), F32, None)], [(1, Q_RANK)])
    d_wq_p = _mm("q_proj_dw", cqn, dq_lin, 'tn')

    dv_lin = dv.transpose(1, 0, 2).reshape(T, HEADS * VDIM).astype(BF16)

    def k_kv_path_bwd(i, rv, bv):
        dkv_, ctb, sub, sdb, dvl, ckv_v = rv
        tot = dkv_[0]
        for hh in range(1, HEADS):
            tot = tot + dkv_[hh]
        lane = lax.broadcasted_iota(jnp.int32, tot.shape, 1)
        tot = jnp.where((lane >= NOPE) & (lane < QK), tot, 0.0)
        dkl = jnp.concatenate([dkv_[hh] for hh in range(HEADS)], axis=1).astype(BF16)
        d_ckvn_v = (lax.dot_general(dkl, bv[1][...], NT_DIMS, preferred_element_type=F32)
                    + lax.dot_general(dvl, bv[2][...], NT_DIMS, preferred_element_type=F32))
        dxv, dg = rms_bwd(ckv_v, bv[0][...], d_ckvn_v)
        return (dkl, _rope_t(tot, ctb, sub, sdb), dxv), (dg,)

    dk_lin, d_krp, d_ckv, d_g_kv = _rowwise(
        "kv_path_bwd", k_kv_path_bwd, nt,
        [_row(dk), _row(cos_k), _row(sup_k), _row(sdn_k), _row(dv_lin), ckv_row], [mla_g_kv, wk_p, wv_f],
        [((T, HEADS * HEAD_PAD), BF16, None), ((T, HEAD_PAD), F32, None), ((T, KV_RANK), F32, None)], [(1, KV_RANK)])
    d_wk_p = _mm("k_proj_dw", ckvn, dk_lin, 'tn')
    d_wv = _mm("v_proj_dw", ckvn, dv_lin, 'tn')

    lam0 = _scan("lru_scan_f_bwd", a0, d_hsum, True, shifted=True, u_off=CN)
    lam1 = _scan("lru_scan_r_bwd", a1, d_hsum, False, shifted=True, u_off=0)

    def k_gates_bwd(i, rv, bv):
        xv, l0, hp0, l1, hp1 = rv
        wg, bg, lamv = [b[...] for b in bv]
        xb = xv.astype(BF16)
        z = jnp.dot(xb, wg, preferred_element_type=F32) + bg
        spb = _bc(_softplus(-lamv), TM)
        _, pull = jax.vjp(_gates, z, xv, spb)
        dz, dxv, dsp = pull((l0 * hp0, l0, l1 * hp1, l1))
        dzb = dz.astype(BF16)
        dxv = dxv + lax.dot_general(dzb, wg, NT_DIMS, preferred_element_type=F32)
        dwg = lax.dot_general(xb, dzb, (((0,), (0,)), ((), ())), preferred_element_type=F32)
        dlam = -_rs(dsp) * jax.nn.sigmoid(-lamv)
        return (dxv,), (dwg, _rs(dz), dlam)

    d_xcv, d_w_gate, d_b_gate, d_lam = _rowwise(
        "lru_gates_bwd", k_gates_bwd, nt,
        [_row(xcv), _row(lam0), _row(hprev0), _row(lam1, swp), _row(hprev1, swp)], [w_gate, b_gate, lam_row],
        [((T, LRU_W), F32, None)], [(LRU_W, 4 * LRU_W), (1, 4 * LRU_W), (1, 2 * LRU_W)])
    d_xr, d_lru_cw, d_lru_cb = _dwconv_bwd("lru_conv_bwd", proj, d_xcv, lru_cw, LRU_CONV_LEFT, (0, ct))

    def k_dproj(i, rv, bv):
        is_lat = i >= ct
        return (jnp.concatenate([rv[0], jnp.where(is_lat, rv[1], 0.0), jnp.where(is_lat, rv[2], 0.0), rv[3], rv[4]],
                                axis=1),), ()

    d_proj = _rowwise("d_proj", k_dproj, nt,
                      [_row(d_xr), _row(d_gr, lat_or_0), _row(d_cq, lat_or_0), _row(d_ckv), _row(d_krp)], [],
                      [((T, IN_W_PAD), BF16, None)])[0]
    d_w_in_p = _mm("in_proj_dw", h_pre, d_proj, 'tn')

    d_b_a = jnp.stack([d_b_gate[0, 0:LRU_W], d_b_gate[0, 2 * LRU_W:3 * LRU_W]])
    d_b_x = jnp.stack([d_b_gate[0, LRU_W:2 * LRU_W], d_b_gate[0, 3 * LRU_W:]])
    d_w_in = jnp.concatenate([d_w_in_p[:, :OFF_KR], d_w_in_p[:, OFF_KR + NOPE:OFF_KR + QK]], axis=1)
    d_w_uq = d_wq_p.reshape(Q_RANK, HEADS, HEAD_PAD)[:, :, :QK].reshape(Q_RANK, HEADS * QK)
    d_w_ukv = jnp.concatenate([d_wk_p.reshape(KV_RANK, HEADS, HEAD_PAD)[:, :, :NOPE],
                               d_wv.reshape(KV_RANK, HEADS, VDIM)], axis=2).reshape(KV_RANK, HEADS * (NOPE + VDIM))
    small_full = dict(lru_conv_w=d_lru_cw, lru_b_a=d_b_a, lru_b_x=d_b_x, lru_lambda=d_lam.reshape(2, LRU_W),
                      ffn_conv_w=d_ffn_cw)
    small_sh = jnp.concatenate([_cols_to_shards(small_full[n]).reshape(N_DEV, -1) for n in SMALL_SHARDED], axis=1)
    n_sh = small_sh.shape[1]
    sh_rows = _pack_rows(n_sh)
    small_sh = jnp.pad(small_sh, ((0, 0), (0, sh_rows * LANES - n_sh))).reshape(N_DEV, sh_rows, LANES)
    early_sends = [_cols_to_shards(d_w_in), _cols_to_shards(d_w_uq), _cols_to_shards(d_w_ukv), small_sh]

    def k_pre_bwd(i, rv, bv):
        g, ml, mc = bv[0][...], bv[1], bv[2]
        sh, sc = sel_mod(i, ml, mc, 0)
        d_h = lax.dot_general(rv[2], bv[3][...], NT_DIMS, preferred_element_type=F32)
        _, pull = jax.vjp(pre_fn, jnp.where(i < ct, rv[0], rv[1]), _bc(g, TM), _bc(sh, TM), _bc(sc, TM))
        dxv, dg, dsh, dsc = pull(d_h)
        is_lat = i >= ct
        dsh, dsc = _rs(dsh), _rs(dsc)
        zero = jnp.zeros_like(dsh)
        return ((dxv + rv[3],),
                (_rs(dg), jnp.where(is_lat, dsh, zero), jnp.where(is_lat, dsc, zero),
                 jnp.where(is_lat, zero, dsh), jnp.where(is_lat, zero, dsc)))

    (dxl, d_g_pre_mix, d_sh1, d_sc1, d_csh1, d_csc1), early_recv = _rowwise(
        "pre_mix_bwd", k_pre_bwd, nt, xa_rows + [_row(d_proj), _row(dx1, lat_or_0)],
        [g_pre_mix, mod_lat, mod_ctx, w_in_p], [((S, D), F32, lat_or_0)], [(1, D)] * 5,
        ride_arrs=early_sends, ride_modes=['a2a'] * 4)
    grad_x = dxl[None]

    zrow = jnp.zeros((1, D), F32)
    d_mod_lat = jnp.concatenate([d_sh1, d_sc1, d_gt1, d_sh2, d_sc2, d_gt2], axis=1)
    d_mod_ctx = jnp.concatenate([d_csh1, d_csc1, zrow, zrow, zrow, zrow], axis=1)
    loss_row = jnp.pad(loss_acc, ((0, 0), (0, N_MOD * D - LANES)))
    d_mod_mine = jnp.concatenate([d_mod_lat, d_mod_ctx, loss_row, jnp.zeros((SUBLANES - 3, N_MOD * D), F32)])

    def diag_blocks(dw):
        return jnp.stack([dw[hh * 64:(hh + 1) * 64, hh * 64:(hh + 1) * 64] for hh in range(LRU_HEADS)])

    d_lru_w_a = jnp.stack([diag_blocks(d_w_gate[:, 0:LRU_W]), diag_blocks(d_w_gate[:, 2 * LRU_W:3 * LRU_W])])[None]
    d_lru_w_x = jnp.stack([diag_blocks(d_w_gate[:, LRU_W:2 * LRU_W]), diag_blocks(d_w_gate[:, 3 * LRU_W:])])[None]
    rep_part = dict(b_mod=d_mod_lat + d_mod_ctx, g_pre_mix=d_g_pre_mix, g_post_mix=d_g_post_mix,
                    g_pre_ffn=d_g_pre_ffn, g_post_ffn=d_g_post_ffn, lru_conv_b=d_lru_cb, lru_w_a=d_lru_w_a,
                    lru_w_x=d_lru_w_x, mla_g_q=d_g_q, mla_g_kv=d_g_kv, ffn_conv_b=d_ffn_cb)
    rep_names = [n for n in REPLICATED if n != 'c_ctx']
    rep_shapes = [W[n].shape for n in rep_names]
    rep_rows = -(-_pack_rows(sum(W[n].size for n in rep_names)) // ROW_TILE) * ROW_TILE
    rep_loc = _pack([rep_part[n] for n in rep_names], rep_rows)
    d_mod_all, rep_all = _exchange("gather_dmod", [d_mod_mine, rep_loc], ['ag'] * 2)
    loss = jnp.sum(d_mod_all[:, 2, 0])
    dm_lat_loc = lax.dynamic_slice(d_mod_all[:, 0], (0, me * ncol), (N_DEV, ncol))
    dm_ctx_loc = lax.dynamic_slice(d_mod_all[:, 1], (0, me * ncol), (N_DEV, ncol))

    def mod_bwd(c16_r, w_r, dml_r, dmc_r):
        c16_v = c16_r[...]
        sig = jax.nn.sigmoid(c16_v)
        sl = c16_v * sig
        dctx = dmc_r[0:1, :]
        for d in range(1, N_DEV):
            dctx = dctx + dmc_r[d:d + 1, :]
        row = lax.broadcasted_iota(jnp.int32, (2 * SUBLANES, ncol), 0)
        dm16 = dml_r[...] + jnp.where(row == N_DEV, _bc(dctx, 2 * SUBLANES), 0.0)
        dw = lax.dot_general(sl.astype(BF16), dm16.astype(BF16), (((0,), (0,)), ((), ())), preferred_element_type=F32)
        dsl = lax.dot_general(dm16.astype(BF16), w_r[...].astype(BF16), NT_DIMS, preferred_element_type=F32)
        dc = dsl * (sig * (1.0 + c16_v * (1.0 - sig)))
        return dw, dc

    dm_lat16 = jnp.concatenate([dm_lat_loc, jnp.zeros((2 * SUBLANES - N_DEV, ncol), F32)])
    g_w_mod, dc16 = _single("mod_bwd", mod_bwd, [c16, w_mod[0], dm_lat16, dm_ctx_loc],
                            [((D, ncol), F32), ((2 * SUBLANES, D), F32)])
    d_c_ctx_part = dc16[N_DEV]

    (c_ctx_all,) = _exchange("gather_d_c_ctx", [d_c_ctx_part.reshape(SUBLANES, D // SUBLANES)], ['ag'])
    big_parts = list(early_recv[:3]) + list(late_recv)

    res = {}

    def adam(name, w2, m2, v2, parts):
        return _adamw("adamw_" + name, w2, m2, v2, parts)

    for n, parts in zip(big, big_parts):
        shp = W[n].shape
        outs = adam(n, W[n][0], M[n][0], V[n][0], parts)
        res[n] = [o_.reshape(shp) for o_ in outs]
    outs = adam('w_mod', w_mod[0], m_w_mod[0], v_w_mod[0], g_w_mod[None])
    res['w_mod'] = [o_.reshape(w_mod.shape) for o_ in outs]

    sh_shapes = [W[n].shape for n in SMALL_SHARDED]
    pk = lambda dct: _pack([dct[n] for n in SMALL_SHARDED], sh_rows)
    outs = adam('small_sharded', pk(W), pk(M), pk(V), early_recv[3])
    for n, vals in zip(SMALL_SHARDED, zip(*[_unpack(o_, sh_shapes) for o_ in outs])):
        res[n] = list(vals)

    pr = lambda dct: _pack([dct[n] for n in rep_names], rep_rows)
    outs = adam('replicated', pr(W), pr(M), pr(V), rep_all)
    for n, vals in zip(rep_names, zip(*[_unpack(o_, rep_shapes) for o_ in outs])):
        res[n] = list(vals)
    as_tile = lambda a: a.reshape(SUBLANES, D // SUBLANES)
    outs = adam('c_ctx', as_tile(c_ctx), as_tile(m_c_ctx), as_tile(v_c_ctx), c_ctx_all)
    res['c_ctx'] = [o_.reshape(c_ctx.shape) for o_ in outs]

    return (loss, grad_x, *[res[n][0] for n in WEIGHTS], *[res[n][1] for n in WEIGHTS],
            *[res[n][2] for n in WEIGHTS], *[res[n][3] for n in WEIGHTS])
```

```python
import functools
import math

import jax
import jax.numpy as jnp
from jax import lax
from jax.experimental import pallas as pl
from jax.experimental.pallas import tpu as pltpu

F32 = jnp.float32
BF16 = jnp.bfloat16
MESH = pl.DeviceIdType.MESH

N_DEV = 8
ROW_TILE = 256
SUBLANES = 8
LANES = 128
VMEM_LIMIT = 56 * 1024 * 1024

D_MODEL = 1024
LRU_W = 512
LRU_HEADS = 8
LRU_CONV_K = 4
LRU_CONV_LEFT = 2
LRU_C = 8.0
HEADS = 8
NOPE = 64
ROPE = 32
VDIM = 64
QK = NOPE + ROPE
HEAD_PAD = 128
Q_RANK = 256
KV_RANK = 128
MLA_SCALE = QK ** -0.5
ROPE_PAIRS = ROPE // 4
ROPE_BASE = 10000.0
GRID_W = 64
D_FF = 2816
FFN_CONV_K = 3
FFN_CONV_LEFT = 1
N_MOD = 6
EPS = 1e-6
IN_W = 2 * LRU_W + Q_RANK + KV_RANK + ROPE
IN_W_PAD = 2 * LRU_W + Q_RANK + KV_RANK + HEAD_PAD
OFF_GR, OFF_CQ, OFF_CKV, OFF_KR = LRU_W, 2 * LRU_W, 2 * LRU_W + Q_RANK, 2 * LRU_W + Q_RANK + KV_RANK

ADAM_LR, ADAM_B1, ADAM_B2, ADAM_EPS, ADAM_WD, ADAM_STEP = 0.001, 0.9, 0.999, 1e-08, 0.01, 10

WEIGHTS = ['c_ctx', 'w_mod', 'b_mod', 'g_pre_mix', 'g_post_mix', 'g_pre_ffn', 'g_post_ffn', 'w_in', 'lru_conv_w',
           'lru_conv_b', 'lru_w_a', 'lru_b_a', 'lru_w_x', 'lru_b_x', 'lru_lambda', 'mla_g_q', 'mla_w_uq', 'mla_g_kv',
           'mla_w_ukv', 'w_out', 'ffn_w_up', 'ffn_conv_w', 'ffn_conv_b', 'ffn_w_down']
REPLICATED = ['c_ctx', 'b_mod', 'g_pre_mix', 'g_post_mix', 'g_pre_ffn', 'g_post_ffn', 'lru_conv_b', 'lru_w_a',
              'lru_w_x', 'mla_g_q', 'mla_g_kv', 'ffn_conv_b']
SMALL_SHARDED = ['lru_conv_w', 'lru_b_a', 'lru_b_x', 'lru_lambda', 'ffn_conv_w']


def _pick(d, prefs):
    for p in prefs:
        if d % p == 0:
            return p
    return d


def _params(sem=None):
    return pltpu.CompilerParams(dimension_semantics=sem, vmem_limit_bytes=VMEM_LIMIT)


MM_TILES = (1024, 1408, 768, 512, 256, 128)


def _mm(name, a, b, mode, out_dtype=F32):
    if mode == 'nn':
        (m, k), (_, n) = a.shape, b.shape
    elif mode == 'nt':
        (m, k), (n, _) = a.shape, b.shape
    else:
        (k, m), (_, n) = a.shape, b.shape
    tm = _pick(m, MM_TILES)
    tn = _pick(n, MM_TILES)
    tk = _pick(k, MM_TILES)
    nk = k // tk
    if mode == 'nn':
        a_spec = pl.BlockSpec((tm, tk), lambda i, j, kk: (i, kk))
        b_spec = pl.BlockSpec((tk, tn), lambda i, j, kk: (kk, j))
        dn = (((1,), (0,)), ((), ()))
    elif mode == 'nt':
        a_spec = pl.BlockSpec((tm, tk), lambda i, j, kk: (i, kk))
        b_spec = pl.BlockSpec((tn, tk), lambda i, j, kk: (j, kk))
        dn = (((1,), (1,)), ((), ()))
    else:
        a_spec = pl.BlockSpec((tk, tm), lambda i, j, kk: (kk, i))
        b_spec = pl.BlockSpec((tk, tn), lambda i, j, kk: (kk, j))
        dn = (((0,), (0,)), ((), ()))

    def body(a_ref, b_ref, o_ref, acc_ref):
        kk = pl.program_id(2)

        @pl.when(kk == 0)
        def _():
            acc_ref[...] = jnp.zeros_like(acc_ref)

        acc_ref[...] += lax.dot_general(a_ref[...].astype(BF16), b_ref[...].astype(BF16), dn,
                                        preferred_element_type=F32)

        @pl.when(kk == nk - 1)
        def _():
            o_ref[...] = acc_ref[...].astype(o_ref.dtype)

    return pl.pallas_call(
        body, name=name, grid=(m // tm, n // tn, nk),
        in_specs=[a_spec, b_spec], out_specs=pl.BlockSpec((tm, tn), lambda i, j, kk: (i, j)),
        out_shape=jax.ShapeDtypeStruct((m, n), out_dtype),
        scratch_shapes=[pltpu.VMEM((tm, tn), F32)],
        compiler_params=_params(("parallel", "parallel", "arbitrary")),
    )(a, b)


def _row(a, idx=None, col=None):
    return dict(a=a, idx=idx, col=col)


def _rowwise(name, fn, n_tiles, rows, bcast, out_rows, out_acc=(), tm=ROW_TILE, ride_arrs=(), ride_modes=()):
    in_specs = []
    for r in rows:
        a, idx, col = r['a'], r['idx'] or (lambda i: i), r['col']
        if a.ndim == 2:
            w, ci = col if col else (a.shape[1], 0)
            in_specs.append(pl.BlockSpec((tm, w), lambda i, idx=idx, ci=ci: (idx(i), ci)))
        else:
            in_specs.append(pl.BlockSpec((a.shape[0], tm, a.shape[2]), lambda i, idx=idx: (0, idx(i), 0)))
    for b in bcast:
        in_specs.append(pl.BlockSpec(b.shape, lambda i, nd=b.ndim: (0,) * nd))
    out_specs, out_shape = [], []
    for shape, dtype, idx in out_rows:
        idx = idx or (lambda i: i)
        if len(shape) == 2:
            out_specs.append(pl.BlockSpec((tm, shape[1]), lambda i, idx=idx: (idx(i), 0)))
        else:
            out_specs.append(pl.BlockSpec((shape[0], tm, shape[2]), lambda i, idx=idx: (0, idx(i), 0)))
        out_shape.append(jax.ShapeDtypeStruct(shape, dtype))
    for shape in out_acc:
        out_specs.append(pl.BlockSpec(shape, lambda i, nd=len(shape): (0,) * nd))
        out_shape.append(jax.ShapeDtypeStruct(shape, F32))
    nr, nb, no, na, n = len(rows), len(bcast), len(out_rows), len(out_acc), len(ride_arrs)
    any_spec = pl.BlockSpec(memory_space=pl.ANY)

    def body(*refs):
        i = pl.program_id(0)
        finish = _riding_exchange(refs, nr + nb, no + na, n, ride_modes, i == 0, i == n_tiles - 1) if n else None
        rvals = [r[...] for r in refs[:nr]]
        bvals = list(refs[nr:nr + nb])
        o_rows, o_acc = fn(i, rvals, bvals)
        outs = refs[nr + nb + n:]
        for ref, v in zip(outs[:no], o_rows):
            ref[...] = v.astype(ref.dtype)
        acc_refs = outs[no:no + na]
        if acc_refs:
            @pl.when(i == 0)
            def _():
                for ref in acc_refs:
                    ref[...] = jnp.zeros_like(ref)
            for ref, v in zip(acc_refs, o_acc):
                ref[...] += v
        if n:
            finish()

    res = pl.pallas_call(
        body, name=name, grid=(n_tiles,), in_specs=in_specs + [any_spec] * n, out_specs=out_specs + [any_spec] * n,
        out_shape=out_shape + _exchange_shapes(ride_arrs, ride_modes),
        scratch_shapes=_exchange_sems(n) if n else [],
        compiler_params=pltpu.CompilerParams(dimension_semantics=("arbitrary",), vmem_limit_bytes=VMEM_LIMIT,
                                             has_side_effects=bool(n)),
    )(*[r['a'] for r in rows], *bcast, *ride_arrs)
    return (res[:no + na], res[no + na:]) if n else res


def _single(name, fn, ins, out_shapes):
    def body(*refs):
        outs = fn(*refs[:len(ins)])
        for ref, v in zip(refs[len(ins):], outs):
            ref[...] = v.astype(ref.dtype)

    return pl.pallas_call(
        body, name=name,
        in_specs=[pl.BlockSpec(memory_space=pltpu.VMEM)] * len(ins),
        out_specs=[pl.BlockSpec(memory_space=pltpu.VMEM)] * len(out_shapes),
        out_shape=[jax.ShapeDtypeStruct(s, d) for s, d in out_shapes],
        compiler_params=_params(),
    )(*ins)


def _bc(p, n):
    return jnp.broadcast_to(p, (n, p.shape[-1]))


def _rs(g):
    return jnp.sum(g, axis=0, keepdims=True)


def _rms(x, g):
    return x * lax.rsqrt(jnp.mean(x * x, axis=-1, keepdims=True) + EPS) * g


def _conv_specs(r, cw, tm, halo=SUBLANES):
    th = tm // halo
    last = r // halo - 1
    prev = pl.BlockSpec((halo, cw), lambda c, i: (jnp.maximum(i * th - 1, 0), c))
    cur = pl.BlockSpec((tm, cw), lambda c, i: (i, c))
    nxt = pl.BlockSpec((halo, cw), lambda c, i: (jnp.minimum((i + 1) * th, last), c))
    return [prev, cur, nxt]


def _fill_ext(ext_ref, prev_ref, cur_ref, next_ref, i, n_tiles, seg_starts, tm):
    prev_ok = functools.reduce(jnp.logical_and, [i != s for s in seg_starts])
    next_ok = functools.reduce(jnp.logical_and, [i + 1 != s for s in seg_starts] + [i + 1 < n_tiles])
    ext_ref[0:SUBLANES, :] = jnp.where(prev_ok, prev_ref[...].astype(F32), 0.0)
    ext_ref[SUBLANES:SUBLANES + tm, :] = cur_ref[...].astype(F32)
    ext_ref[SUBLANES + tm:, :] = jnp.where(next_ok, next_ref[...].astype(F32), 0.0)


def _dwconv_fwd(name, x, w, b, left, seg_starts, cw=512, tm=ROW_TILE):
    r, c = x.shape[0], w.shape[1]
    kw = w.shape[0]
    n_tiles = r // tm

    def body(prev_ref, cur_ref, next_ref, w_ref, b_ref, o_ref, ext_ref):
        i = pl.program_id(1)
        _fill_ext(ext_ref, prev_ref, cur_ref, next_ref, i, n_tiles, seg_starts, tm)
        out = jnp.broadcast_to(b_ref[...], (tm, cw))
        for k in range(kw):
            out = out + ext_ref[pl.ds(SUBLANES + k - left, tm), :] * w_ref[k:k + 1, :]
        o_ref[...] = out

    return pl.pallas_call(
        body, name=name, grid=(c // cw, n_tiles),
        in_specs=_conv_specs(r, cw, tm) + [pl.BlockSpec((kw, cw), lambda c_, i: (0, c_)),
                                           pl.BlockSpec((1, cw), lambda c_, i: (0, c_))],
        out_specs=pl.BlockSpec((tm, cw), lambda c_, i: (i, c_)),
        out_shape=jax.ShapeDtypeStruct((r, c), F32),
        scratch_shapes=[pltpu.VMEM((tm + 2 * SUBLANES, cw), F32)],
        compiler_params=_params(("parallel", "arbitrary")),
    )(x, x, x, w, b)


def _dwconv_bwd(name, x, dy, w, left, seg_starts, out_dtype=F32, cw=512, tm=ROW_TILE):
    r, c = dy.shape
    kw = w.shape[0]
    n_tiles = r // tm

    def body(xp, xc, xn, dp, dc, dn, w_ref, dx_ref, dw_ref, db_ref, xe_ref, de_ref):
        i = pl.program_id(1)
        _fill_ext(xe_ref, xp, xc, xn, i, n_tiles, seg_starts, tm)
        _fill_ext(de_ref, dp, dc, dn, i, n_tiles, seg_starts, tm)
        dyc = dc[...].astype(F32)
        dx = jnp.zeros((tm, cw), F32)
        dws = []
        for k in range(kw):
            dx = dx + de_ref[pl.ds(SUBLANES - k + left, tm), :] * w_ref[k:k + 1, :]
            dws.append(jnp.sum(dyc * xe_ref[pl.ds(SUBLANES + k - left, tm), :], axis=0, keepdims=True))
        dx_ref[...] = dx.astype(dx_ref.dtype)

        @pl.when(i == 0)
        def _():
            dw_ref[...] = jnp.zeros_like(dw_ref)
            db_ref[...] = jnp.zeros_like(db_ref)

        for k in range(kw):
            dw_ref[k:k + 1, :] += dws[k]
        db_ref[...] += jnp.sum(dyc, axis=0, keepdims=True)

    return pl.pallas_call(
        body, name=name, grid=(c // cw, n_tiles),
        in_specs=_conv_specs(r, cw, tm) + _conv_specs(r, cw, tm) + [pl.BlockSpec((kw, cw), lambda c_, i: (0, c_))],
        out_specs=[pl.BlockSpec((tm, cw), lambda c_, i: (i, c_)),
                   pl.BlockSpec((kw, cw), lambda c_, i: (0, c_)),
                   pl.BlockSpec((1, cw), lambda c_, i: (0, c_))],
        out_shape=[jax.ShapeDtypeStruct((r, c), out_dtype), jax.ShapeDtypeStruct((kw, c), F32),
                   jax.ShapeDtypeStruct((1, c), F32)],
        scratch_shapes=[pltpu.VMEM((tm + 2 * SUBLANES, cw), F32), pltpu.VMEM((tm + 2 * SUBLANES, cw), F32)],
        compiler_params=_params(("parallel", "arbitrary")),
    )(x, x, x, dy, dy, dy, w)


FF_TILE = 256
FF_HALO = 16
FF_STRIP = 32


def _ffn_fill(ext_ref, prev_ref, cur_ref, next_ref, i, n_tiles, tm):
    ext_ref[0:FF_HALO, :] = jnp.where(i > 0, prev_ref[...].astype(F32), 0.0)
    ext_ref[FF_HALO:FF_HALO + tm, :] = cur_ref[...].astype(F32)
    ext_ref[FF_HALO + tm:, :] = jnp.where(i + 1 < n_tiles, next_ref[...].astype(F32), 0.0)


def _ffn_conv(ext_ref, w_ref, b_ref, start, rows):
    out = jnp.broadcast_to(b_ref[...], (rows, 2 * FF_TILE))
    for k in range(FFN_CONV_K):
        out = out + ext_ref[pl.ds(start + k - FFN_CONV_LEFT, rows), :] * w_ref[k:k + 1, :]
    return out


def _ffn_mid_fwd(up, w, b):
    s, c2 = up.shape
    tm = _pick(s, (2 * ROW_TILE, ROW_TILE))
    n_tiles = s // tm
    cw = 2 * FF_TILE

    def body(prev_ref, cur_ref, next_ref, w_ref, b_ref, o_ref, ext_ref):
        i = pl.program_id(1)
        _ffn_fill(ext_ref, prev_ref, cur_ref, next_ref, i, n_tiles, tm)
        for r0 in range(0, tm, FF_STRIP):
            upc = _ffn_conv(ext_ref, w_ref, b_ref, FF_HALO + r0, FF_STRIP)
            uv, gv = upc[:, :FF_TILE], upc[:, FF_TILE:]
            o_ref[r0:r0 + FF_STRIP, :] = (gv * jax.nn.sigmoid(gv) * uv).astype(o_ref.dtype)

    return pl.pallas_call(
        body, name="ffn_mid", grid=(c2 // cw, n_tiles),
        in_specs=_conv_specs(s, cw, tm, FF_HALO) + [pl.BlockSpec((FFN_CONV_K, cw), lambda c_, i: (0, c_)),
                                                   pl.BlockSpec((1, cw), lambda c_, i: (0, c_))],
        out_specs=pl.BlockSpec((tm, FF_TILE), lambda c_, i: (i, c_)),
        out_shape=jax.ShapeDtypeStruct((s, c2 // 2), BF16),
        scratch_shapes=[pltpu.VMEM((tm + 2 * FF_HALO, cw), F32)],
        compiler_params=_params(("parallel", "arbitrary")),
    )(up, up, up, w, b)


def _ffn_mid_bwd(up, d_act, w, b, tm=ROW_TILE):
    s, c2 = up.shape
    n_tiles = s // tm
    cw = 2 * FF_TILE
    h8 = SUBLANES

    def gate_bwd(upc, dact):
        uv, gv = upc[:, :FF_TILE], upc[:, FF_TILE:]
        sg = jax.nn.sigmoid(gv)
        return jnp.concatenate([dact * (gv * sg), dact * uv * (sg * (1.0 + gv * (1.0 - sg)))], axis=1)

    def body(up_p, up_c, up_n, da_p, da_c, da_n, w_ref, b_ref, dup_ref, dw_ref, db_ref, ext_ref, dext_ref):
        i = pl.program_id(1)
        _ffn_fill(ext_ref, up_p, up_c, up_n, i, n_tiles, tm)
        dws = [jnp.zeros((1, cw), F32) for _ in range(FFN_CONV_K)]
        dbs = jnp.zeros((1, cw), F32)
        for r0 in range(0, tm, FF_STRIP):
            d_c = gate_bwd(_ffn_conv(ext_ref, w_ref, b_ref, FF_HALO + r0, FF_STRIP),
                           da_c[r0:r0 + FF_STRIP, :].astype(F32))
            dext_ref[FF_HALO + r0:FF_HALO + r0 + FF_STRIP, :] = d_c
            for k in range(FFN_CONV_K):
                xk = ext_ref[pl.ds(FF_HALO + r0 + k - FFN_CONV_LEFT, FF_STRIP), :]
                dws[k] = dws[k] + jnp.sum(d_c * xk, axis=0, keepdims=True)
            dbs = dbs + jnp.sum(d_c, axis=0, keepdims=True)
        da_prev = jnp.where(i > 0, da_p[...].astype(F32)[FF_HALO - h8:, :], 0.0)
        da_next = jnp.where(i + 1 < n_tiles, da_n[...].astype(F32)[:h8, :], 0.0)
        dext_ref[FF_HALO - h8:FF_HALO, :] = gate_bwd(_ffn_conv(ext_ref, w_ref, b_ref, FF_HALO - h8, h8), da_prev)
        dext_ref[FF_HALO + tm:FF_HALO + tm + h8, :] = gate_bwd(_ffn_conv(ext_ref, w_ref, b_ref, FF_HALO + tm, h8), da_next)
        for r0 in range(0, tm, FF_STRIP):
            dup = jnp.zeros((FF_STRIP, cw), F32)
            for k in range(FFN_CONV_K):
                dup = dup + dext_ref[pl.ds(FF_HALO + r0 - k + FFN_CONV_LEFT, FF_STRIP), :] * w_ref[k:k + 1, :]
            dup_ref[r0:r0 + FF_STRIP, :] = dup.astype(dup_ref.dtype)

        @pl.when(i == 0)
        def _():
            dw_ref[...] = jnp.zeros_like(dw_ref)
            db_ref[...] = jnp.zeros_like(db_ref)

        for k in range(FFN_CONV_K):
            dw_ref[k:k + 1, :] += dws[k]
        db_ref[...] += dbs

    def half_specs():
        th = tm // FF_HALO
        last = s // FF_HALO - 1
        return [pl.BlockSpec((FF_HALO, FF_TILE), lambda c_, i: (jnp.maximum(i * th - 1, 0), c_)),
                pl.BlockSpec((tm, FF_TILE), lambda c_, i: (i, c_)),
                pl.BlockSpec((FF_HALO, FF_TILE), lambda c_, i: (jnp.minimum((i + 1) * th, last), c_))]

    return pl.pallas_call(
        body, name="ffn_mid_bwd", grid=(c2 // cw, n_tiles),
        in_specs=_conv_specs(s, cw, tm, FF_HALO) + half_specs() + [
            pl.BlockSpec((FFN_CONV_K, cw), lambda c_, i: (0, c_)), pl.BlockSpec((1, cw), lambda c_, i: (0, c_))],
        out_specs=[pl.BlockSpec((tm, cw), lambda c_, i: (i, c_)),
                   pl.BlockSpec((FFN_CONV_K, cw), lambda c_, i: (0, c_)),
                   pl.BlockSpec((1, cw), lambda c_, i: (0, c_))],
        out_shape=[jax.ShapeDtypeStruct((s, c2), BF16), jax.ShapeDtypeStruct((FFN_CONV_K, c2), F32),
                   jax.ShapeDtypeStruct((1, c2), F32)],
        scratch_shapes=[pltpu.VMEM((tm + 2 * FF_HALO, cw), F32), pltpu.VMEM((tm + 2 * FF_HALO, cw), F32)],
        compiler_params=_params(("parallel", "arbitrary")),
    )(up, up, up, d_act, d_act, d_act, w, b)


def _ff_permute(name, w, to_tiles):
    r = w.shape[0]
    nb = D_FF // FF_TILE
    natural = pl.BlockSpec((r, FF_TILE), lambda j, half: (0, half * nb + j))
    tiled = pl.BlockSpec((r, FF_TILE), lambda j, half: (0, 2 * j + half))

    def body(x_ref, o_ref):
        o_ref[...] = x_ref[...]

    return pl.pallas_call(
        body, name=name, grid=(nb, 2),
        in_specs=[natural if to_tiles else tiled], out_specs=tiled if to_tiles else natural,
        out_shape=jax.ShapeDtypeStruct(w.shape, w.dtype), compiler_params=_params(("parallel", "parallel")),
    )(w)


def _ff_to_tiles(w):
    r = w.shape[0]
    return w.reshape(r, 2, D_FF // FF_TILE, FF_TILE).transpose(0, 2, 1, 3).reshape(r, 2 * D_FF)


def _ff_from_tiles(w):
    r = w.shape[0]
    return w.reshape(r, D_FF // FF_TILE, 2, FF_TILE).transpose(0, 2, 1, 3).reshape(r, 2 * D_FF)


SCAN_UNROLL = 8


def _scan(name, a, u, reverse, shifted=False, u_off=0, with_prev=False):
    t, c = a.shape
    us = u.shape[0]
    n8 = t // SUBLANES
    lo, hi = 0, SUBLANES - 1

    def body(a_ref, u_ref, h_ref, *prev_ref):
        row = lax.broadcasted_iota(jnp.int32, (SUBLANES, LANES), 0)
        last = lo if reverse else hi

        def tile(ref, base):
            return ref[pl.ds(pl.multiple_of(base, SUBLANES), SUBLANES), :]

        def local(blk):
            base = blk * SUBLANES
            av = tile(a_ref, base)
            if shifted and reverse:
                nb = tile(a_ref, jnp.minimum(base + SUBLANES, t - SUBLANES))
                edge = jnp.where(base + SUBLANES >= t, 1.0, pltpu.roll(nb, hi, 0))
                av = jnp.where(row < hi, pltpu.roll(av, hi, 0), edge)
            elif shifted:
                pb = tile(a_ref, jnp.maximum(base - SUBLANES, 0))
                edge = jnp.where(base == 0, 1.0, pltpu.roll(pb, 1, 0))
                av = jnp.where(row >= 1, pltpu.roll(av, 1, 0), edge)
            ub = base - u_off
            hv = jnp.where((ub >= 0) & (ub < us), tile(u_ref, jnp.clip(ub, 0, us - SUBLANES)), 0.0)
            for s in (1, 2, 4):
                shift = SUBLANES - s if reverse else s
                ok = (row < SUBLANES - s) if reverse else (row >= s)
                a_sh = jnp.where(ok, pltpu.roll(av, shift, 0), 1.0)
                h_sh = jnp.where(ok, pltpu.roll(hv, shift, 0), 0.0)
                hv = av * h_sh + hv
                av = av * a_sh
            a_last = jnp.sum(jnp.where(row == last, av, 0.0), axis=0, keepdims=True)
            h_last = jnp.sum(jnp.where(row == last, hv, 0.0), axis=0, keepdims=True)
            return base, av, hv, a_last, h_last

        def step(j, carry):
            parts = []
            for k in range(SCAN_UNROLL):
                idx = j * SCAN_UNROLL + k
                parts.append(local((n8 - 1 - idx) if reverse else idx))
            for base, av, hv, a_last, h_last in parts:
                rows = pl.ds(pl.multiple_of(base, SUBLANES), SUBLANES)
                h_true = av * carry + hv
                h_ref[rows, :] = h_true
                if with_prev:
                    if reverse:
                        prev_ref[0][rows, :] = jnp.where(row < hi, pltpu.roll(h_true, hi, 0), carry)
                    else:
                        prev_ref[0][rows, :] = jnp.where(row >= 1, pltpu.roll(h_true, 1, 0), carry)
                carry = a_last * carry + h_last
            return carry

        lax.fori_loop(0, n8 // SCAN_UNROLL, step, jnp.zeros((1, LANES), F32))

    spec = pl.BlockSpec((t, LANES), lambda j: (0, j))
    n_out = 2 if with_prev else 1
    res = pl.pallas_call(
        body, name=name, grid=(c // LANES,),
        in_specs=[spec, pl.BlockSpec((us, LANES), lambda j: (0, j))],
        out_specs=[spec] * n_out, out_shape=[jax.ShapeDtypeStruct((t, c), F32)] * n_out,
        compiler_params=_params(("parallel",)),
    )(a, u)
    return res if with_prev else res[0]


NT_DIMS = (((1,), (1,)), ((), ()))


LOG2E = 1.4426950408889634
SCALE2 = MLA_SCALE * LOG2E
ATTN_TILES = (512, 256, 128)
KEY_CHUNKS = (768, 512, 256, 128)
QUERY_CHUNKS = (1024, 512, 256, 128)


def _attn_fwd(q, k, v, ride_arrs, ride_modes):
    h, s, _ = q.shape
    t = k.shape[1]
    tq = _pick(s, ATTN_TILES)
    ck = _pick(t, KEY_CHUNKS)
    n = len(ride_arrs)

    def body(*refs):
        q_ref, k_ref, v_ref = refs[:3]
        o_ref, lse_ref = refs[3 + n:5 + n]
        hh, i = pl.program_id(0), pl.program_id(1)
        finish = _riding_exchange(refs, 3, 2, n, ride_modes, (hh == 0) & (i == 0),
                                  (hh == h - 1) & (i == s // tq - 1))
        qv = q_ref[0]
        def scores(j):
            return lax.dot_general(qv, k_ref[0, j * ck:(j + 1) * ck, :], NT_DIMS, preferred_element_type=F32)

        m = l = acc = None
        s_next = scores(0)
        for j in range(t // ck):
            vj = v_ref[0, j * ck:(j + 1) * ck, :]
            s2, s_next = s_next, (scores(j + 1) if j + 1 < t // ck else None)
            mj = jnp.max(s2, axis=-1, keepdims=True)
            m_new = mj if j == 0 else jnp.maximum(m, mj)
            p = jnp.exp2(s2 - m_new)
            lj = jnp.sum(p, axis=-1, keepdims=True)
            pv = jnp.dot(p.astype(BF16), vj, preferred_element_type=F32)
            if j == 0:
                l, acc = lj, pv
            else:
                alpha = jnp.exp2(m - m_new)
                l, acc = alpha * l + lj, alpha * acc + pv
            m = m_new
        o_ref[0] = acc / l
        lse_ref[0] = _rows8(jnp.broadcast_to(m + jnp.log2(l), (tq, LANES)))
        finish()

    any_spec = pl.BlockSpec(memory_space=pl.ANY)
    res = pl.pallas_call(
        body, name="attn_fwd", grid=(h, s // tq),
        in_specs=[pl.BlockSpec((1, tq, HEAD_PAD), lambda hh, i: (hh, i, 0)),
                  pl.BlockSpec((1, t, HEAD_PAD), lambda hh, i: (hh, 0, 0)),
                  pl.BlockSpec((1, t, VDIM), lambda hh, i: (hh, 0, 0))] + [any_spec] * n,
        out_specs=[pl.BlockSpec((1, tq, VDIM), lambda hh, i: (hh, i, 0)),
                   pl.BlockSpec((1, SUBLANES, tq), lambda hh, i: (hh, 0, i))] + [any_spec] * n,
        out_shape=[jax.ShapeDtypeStruct((h, s, VDIM), F32), jax.ShapeDtypeStruct((h, SUBLANES, s), F32)]
        + _exchange_shapes(ride_arrs, ride_modes),
        scratch_shapes=_exchange_sems(n),
        compiler_params=pltpu.CompilerParams(dimension_semantics=("arbitrary", "arbitrary"),
                                             vmem_limit_bytes=VMEM_LIMIT, has_side_effects=True),
    )(q, k, v, *ride_arrs)
    return res[0], res[1], res[2:]


TN_DIMS = (((0,), (0,)), ((), ()))


def _rows8(x):
    return jnp.transpose(x)[:SUBLANES, :]


def _attn_delta(do_src, o_t, tm=ROW_TILE):
    s = o_t.shape[0]
    w = HEADS * VDIM

    def body(do_ref, o_ref, d_ref):
        prod = do_ref[...] * o_ref[...]
        for c in range(w // LANES):
            x = prod[:, c * LANES:(c + 1) * LANES]
            shift = VDIM // 2
            while shift:
                x = x + pltpu.roll(x, LANES - shift, 1)
                shift //= 2
            xt = jnp.transpose(x)
            d_ref[2 * c] = xt[:SUBLANES, :]
            d_ref[2 * c + 1] = xt[VDIM:VDIM + SUBLANES, :]

    return pl.pallas_call(
        body, name="attn_delta", grid=(s // tm,),
        in_specs=[pl.BlockSpec((tm, w), lambda i: (i, 1)), pl.BlockSpec((tm, w), lambda i: (i, 0))],
        out_specs=pl.BlockSpec((HEADS, SUBLANES, tm), lambda i: (0, 0, i)),
        out_shape=jax.ShapeDtypeStruct((HEADS, SUBLANES, s), F32),
        compiler_params=_params(("parallel",)),
    )(do_src, o_t)


def _attn_bwd(q, k, v, do, lse_row, delta_row, ride_arrs, ride_modes):
    h, s, _ = q.shape
    t = k.shape[1]
    tk = _pick(t, (768,) + ATTN_TILES)
    cq = _pick(s, QUERY_CHUNKS)
    n = len(ride_arrs)

    def body(*refs):
        q_ref, k_ref, v_ref, do_ref, lse_ref, delta_ref = refs[:6]
        dq_ref, dk_ref, dv_ref = refs[6 + n:9 + n]
        hh, i = pl.program_id(0), pl.program_id(1)
        finish = _riding_exchange(refs, 6, 3, n, ride_modes, (hh == 0) & (i == 0),
                                  (hh == h - 1) & (i == t // tk - 1))

        @pl.when(i == 0)
        def _():
            dq_ref[...] = jnp.zeros_like(dq_ref)

        kt, vt = k_ref[0], v_ref[0]
        dk = dv = None
        for j in range(s // cq):
            rows = slice(j * cq, (j + 1) * cq)
            qj, doj = q_ref[0, rows, :], do_ref[0, rows, :]
            pt = jnp.exp2(lax.dot_general(kt, qj, NT_DIMS, preferred_element_type=F32) - lse_ref[0, 0:1, rows])
            dv_j = jnp.dot(pt.astype(BF16), doj, preferred_element_type=F32)
            dpt = lax.dot_general(vt, doj, NT_DIMS, preferred_element_type=F32)
            dst = (pt * (dpt - delta_ref[0, 0:1, rows])).astype(BF16)
            dk_j = jnp.dot(dst, qj, preferred_element_type=F32)
            dq_ref[0, rows, :] += lax.dot_general(dst, kt, TN_DIMS, preferred_element_type=F32)
            dk, dv = (dk_j, dv_j) if j == 0 else (dk + dk_j, dv + dv_j)
        dk_ref[0] = dk * (1.0 / LOG2E)
        dv_ref[0] = dv
        finish()

    any_spec = pl.BlockSpec(memory_space=pl.ANY)
    res = pl.pallas_call(
        body, name="attn_bwd", grid=(h, t // tk),
        in_specs=[pl.BlockSpec((1, s, HEAD_PAD), lambda hh, i: (hh, 0, 0)),
                  pl.BlockSpec((1, tk, HEAD_PAD), lambda hh, i: (hh, i, 0)),
                  pl.BlockSpec((1, tk, VDIM), lambda hh, i: (hh, i, 0)),
                  pl.BlockSpec((1, s, VDIM), lambda hh, i: (hh, 0, 0)),
                  pl.BlockSpec((1, SUBLANES, s), lambda hh, i: (hh, 0, 0)),
                  pl.BlockSpec((1, SUBLANES, s), lambda hh, i: (hh, 0, 0))] + [any_spec] * n,
        out_specs=[pl.BlockSpec((1, s, HEAD_PAD), lambda hh, i: (hh, 0, 0)),
                   pl.BlockSpec((1, tk, HEAD_PAD), lambda hh, i: (hh, i, 0)),
                   pl.BlockSpec((1, tk, VDIM), lambda hh, i: (hh, i, 0))] + [any_spec] * n,
        out_shape=[jax.ShapeDtypeStruct((h, s, HEAD_PAD), F32), jax.ShapeDtypeStruct((h, t, HEAD_PAD), F32),
                   jax.ShapeDtypeStruct((h, t, VDIM), F32)] + _exchange_shapes(ride_arrs, ride_modes),
        scratch_shapes=_exchange_sems(n),
        compiler_params=pltpu.CompilerParams(dimension_semantics=("arbitrary", "arbitrary"),
                                             vmem_limit_bytes=VMEM_LIMIT, has_side_effects=True),
    )(q, k, v, do, lse_row, delta_row, *ride_arrs)
    return res[0], res[1], res[2], res[3:]


def _exchange_shapes(arrs, modes):
    return [jax.ShapeDtypeStruct((N_DEV,) + a.shape if md == 'ag' else a.shape, a.dtype) for a, md in zip(arrs, modes)]


def _exchange_sems(n):
    return [pltpu.SemaphoreType.DMA((n, N_DEV - 1)), pltpu.SemaphoreType.DMA((n, N_DEV - 1)),
            pltpu.SemaphoreType.DMA((n,))]


def _exchange_copies(ins, outs, modes, send_sems, recv_sems, local_sems):
    x, y, c = lax.axis_index("x"), lax.axis_index("y"), lax.axis_index("c")
    me = 4 * x + 2 * y + c
    copies = []
    for a in range(len(ins)):
        ag = modes[a] == 'ag'
        copies.append(pltpu.make_async_copy(ins[a] if ag else ins[a].at[me], outs[a].at[me], local_sems.at[a]))
        for k in range(1, N_DEV):
            px = 1 - x if k & 4 else x
            py = 1 - y if k & 2 else y
            pc = 1 - c if k & 1 else c
            src = ins[a] if ag else ins[a].at[4 * px + 2 * py + pc]
            copies.append(pltpu.make_async_remote_copy(
                src_ref=src, dst_ref=outs[a].at[me], send_sem=send_sems.at[a, k - 1],
                recv_sem=recv_sems.at[a, k - 1], device_id=(px, py, pc), device_id_type=MESH))
    return copies


def _exchange(name, arrs, modes):
    n = len(arrs)

    def body(*refs):
        copies = _exchange_copies(refs[:n], refs[n:2 * n], modes, *refs[2 * n:])
        for cp in copies:
            cp.start()
        for cp in copies:
            cp.wait()

    return pl.pallas_call(
        body, name=name,
        in_specs=[pl.BlockSpec(memory_space=pl.ANY)] * n,
        out_specs=[pl.BlockSpec(memory_space=pl.ANY)] * n,
        out_shape=_exchange_shapes(arrs, modes),
        scratch_shapes=_exchange_sems(n),
        compiler_params=pltpu.CompilerParams(has_side_effects=True),
    )(*arrs)


def _riding_exchange(refs, n_in, n_out, n, modes, first, last):
    ins = refs[n_in:n_in + n]
    outs = refs[n_in + n + n_out:n_in + 2 * n + n_out]
    sems = refs[n_in + 2 * n + n_out:n_in + 2 * n + n_out + 3]

    @pl.when(first)
    def _():
        for cp in _exchange_copies(ins, outs, modes, *sems):
            cp.start()

    def finish():
        @pl.when(last)
        def _():
            for cp in _exchange_copies(ins, outs, modes, *sems):
                cp.wait()

    return finish


def _adamw(name, w, m, v, gparts):
    r, c = w.shape
    npart = gparts.shape[0]
    tr = _pick(r, (256, 128, 64, 32, 16, 8))
    spec = pl.BlockSpec((tr, c), lambda i: (i, 0))

    def body(w_ref, m_ref, v_ref, g_ref, go_ref, d_ref, mo_ref, vo_ref):
        g = g_ref[0]
        for p in range(1, npart):
            g = g + g_ref[p]
        m1 = ADAM_B1 * m_ref[...] + (1.0 - ADAM_B1) * g
        v1 = ADAM_B2 * v_ref[...] + (1.0 - ADAM_B2) * (g * g)
        m_hat = m1 / (1.0 - ADAM_B1 ** ADAM_STEP)
        v_hat = v1 / (1.0 - ADAM_B2 ** ADAM_STEP)
        go_ref[...] = g
        d_ref[...] = -ADAM_LR * (m_hat / (jnp.sqrt(v_hat) + ADAM_EPS) + ADAM_WD * w_ref[...])
        mo_ref[...] = m1
        vo_ref[...] = v1

    return pl.pallas_call(
        body, name=name, grid=(r // tr,),
        in_specs=[spec, spec, spec, pl.BlockSpec((npart, tr, c), lambda i: (0, i, 0))],
        out_specs=[spec] * 4, out_shape=[jax.ShapeDtypeStruct((r, c), F32)] * 4,
        compiler_params=_params(("parallel",)),
    )(w, m, v, gparts)


def _pack(arrs, rows):
    flat = jnp.concatenate([a.reshape(-1) for a in arrs])
    return jnp.pad(flat, (0, rows * LANES - flat.shape[0])).reshape(rows, LANES)


def _unpack(packed, shapes):
    flat, out, off = packed.reshape(-1), [], 0
    for s in shapes:
        n = math.prod(s)
        out.append(flat[off:off + n].reshape(s))
        off += n
    return out


def _pack_rows(n_elems):
    return -(-n_elems // (SUBLANES * LANES)) * SUBLANES


def _cols_from_shards(g):
    return g.transpose(1, 0, 2).reshape(g.shape[1], N_DEV * g.shape[2])


def _cols_to_shards(w):
    r, c = w.shape
    return w.reshape(r, N_DEV, c // N_DEV).transpose(1, 0, 2)


def _rope_tables(n_lat, n_ctx):
    inv = ROPE_BASE ** (-jnp.arange(ROPE_PAIRS, dtype=F32) / ROPE_PAIRS)
    seg = [(l - NOPE) // ROPE_PAIRS if NOPE <= l < QK else -1 for l in range(HEAD_PAD)]
    freq = jnp.take(inv, jnp.array([(l - NOPE) % ROPE_PAIRS if NOPE <= l < QK else 0 for l in range(HEAD_PAD)]))
    seg = jnp.array(seg)
    f_row = jnp.where((seg == 0) | (seg == 1), freq, 0.0)[None, :]
    f_col = jnp.where((seg == 2) | (seg == 3), freq, 0.0)[None, :]
    t = jnp.arange(n_ctx + n_lat) - n_ctx
    pos = jnp.maximum(t, 0)
    row = jnp.where(t >= 0, pos // GRID_W, 0).astype(F32)[:, None]
    col = jnp.where(t >= 0, pos % GRID_W, 0).astype(F32)[:, None]
    ang = row * f_row + col * f_col
    sin = jnp.sin(ang)
    sin_up = jnp.where(((seg == 0) | (seg == 2))[None, :], -sin, 0.0)
    sin_dn = jnp.where(((seg == 1) | (seg == 3))[None, :], sin, 0.0)
    return jnp.cos(ang), sin_up, sin_dn


def _rope(x, cos_t, sin_up, sin_dn):
    return x * cos_t + pltpu.roll(x, HEAD_PAD - ROPE_PAIRS, 1) * sin_up + pltpu.roll(x, ROPE_PAIRS, 1) * sin_dn


def _rope_t(dy, cos_t, sin_up, sin_dn):
    return (dy * cos_t + pltpu.roll(dy * sin_up, ROPE_PAIRS, 1)
            + pltpu.roll(dy * sin_dn, HEAD_PAD - ROPE_PAIRS, 1))


def _softplus(x):
    return jnp.maximum(x, 0.0) + jnp.log(1.0 + jnp.exp(-jnp.abs(x)))


def _gates(z, xcv, lam_sp):
    outs = []
    for d in range(2):
        r = jax.nn.sigmoid(z[:, (2 * d) * LRU_W:(2 * d + 1) * LRU_W])
        ig = jax.nn.sigmoid(z[:, (2 * d + 1) * LRU_W:(2 * d + 2) * LRU_W])
        log_a = -LRU_C * r * lam_sp[:, d * LRU_W:(d + 1) * LRU_W]
        a = jnp.exp(log_a)
        u = jnp.sqrt(-jnp.tanh(log_a) * (a * a + 1.0)) * (ig * xcv)
        outs += [a, u]
    return tuple(outs)


def kernel(x, c, ctx, c_ctx, w_mod, b_mod, g_pre_mix, g_post_mix, g_pre_ffn, g_post_ffn, w_in, lru_conv_w, lru_conv_b, lru_w_a, lru_b_a, lru_w_x, lru_b_x, lru_lambda, mla_g_q, mla_w_uq, mla_g_kv, mla_w_ukv, w_out, ffn_w_up, ffn_conv_w, ffn_conv_b, ffn_w_down, loss_target, m_c_ctx, m_w_mod, m_b_mod, m_g_pre_mix, m_g_post_mix, m_g_pre_ffn, m_g_post_ffn, m_w_in, m_lru_conv_w, m_lru_conv_b, m_lru_w_a, m_lru_b_a, m_lru_w_x, m_lru_b_x, m_lru_lambda, m_mla_g_q, m_mla_w_uq, m_mla_g_kv, m_mla_w_ukv, m_w_out, m_ffn_w_up, m_ffn_conv_w, m_ffn_conv_b, m_ffn_w_down, v_c_ctx, v_w_mod, v_b_mod, v_g_pre_mix, v_g_post_mix, v_g_pre_ffn, v_g_post_ffn, v_w_in, v_lru_conv_w, v_lru_conv_b, v_lru_w_a, v_lru_b_a, v_lru_w_x, v_lru_b_x, v_lru_lambda, v_mla_g_q, v_mla_w_uq, v_mla_g_kv, v_mla_w_ukv, v_w_out, v_ffn_w_up, v_ffn_conv_w, v_ffn_conv_b, v_ffn_w_down):
    W = dict(c_ctx=c_ctx, w_mod=w_mod, b_mod=b_mod, g_pre_mix=g_pre_mix, g_post_mix=g_post_mix, g_pre_ffn=g_pre_ffn,
             g_post_ffn=g_post_ffn, w_in=w_in, lru_conv_w=lru_conv_w, lru_conv_b=lru_conv_b, lru_w_a=lru_w_a,
             lru_b_a=lru_b_a, lru_w_x=lru_w_x, lru_b_x=lru_b_x, lru_lambda=lru_lambda, mla_g_q=mla_g_q,
             mla_w_uq=mla_w_uq, mla_g_kv=mla_g_kv, mla_w_ukv=mla_w_ukv, w_out=w_out, ffn_w_up=ffn_w_up,
             ffn_conv_w=ffn_conv_w, ffn_conv_b=ffn_conv_b, ffn_w_down=ffn_w_down)
    M = dict(c_ctx=m_c_ctx, w_mod=m_w_mod, b_mod=m_b_mod, g_pre_mix=m_g_pre_mix, g_post_mix=m_g_post_mix,
             g_pre_ffn=m_g_pre_ffn, g_post_ffn=m_g_post_ffn, w_in=m_w_in, lru_conv_w=m_lru_conv_w,
             lru_conv_b=m_lru_conv_b, lru_w_a=m_lru_w_a, lru_b_a=m_lru_b_a, lru_w_x=m_lru_w_x, lru_b_x=m_lru_b_x,
             lru_lambda=m_lru_lambda, mla_g_q=m_mla_g_q, mla_w_uq=m_mla_w_uq, mla_g_kv=m_mla_g_kv,
             mla_w_ukv=m_mla_w_ukv, w_out=m_w_out, ffn_w_up=m_ffn_w_up, ffn_conv_w=m_ffn_conv_w,
             ffn_conv_b=m_ffn_conv_b, ffn_w_down=m_ffn_w_down)
    V = dict(c_ctx=v_c_ctx, w_mod=v_w_mod, b_mod=v_b_mod, g_pre_mix=v_g_pre_mix, g_post_mix=v_g_post_mix,
             g_pre_ffn=v_g_pre_ffn, g_post_ffn=v_g_post_ffn, w_in=v_w_in, lru_conv_w=v_lru_conv_w,
             lru_conv_b=v_lru_conv_b, lru_w_a=v_lru_w_a, lru_b_a=v_lru_b_a, lru_w_x=v_lru_w_x, lru_b_x=v_lru_b_x,
             lru_lambda=v_lru_lambda, mla_g_q=v_mla_g_q, mla_w_uq=v_mla_w_uq, mla_g_kv=v_mla_g_kv,
             mla_w_ukv=v_mla_w_ukv, w_out=v_w_out, ffn_w_up=v_ffn_w_up, ffn_conv_w=v_ffn_conv_w,
             ffn_conv_b=v_ffn_conv_b, ffn_w_down=v_ffn_w_down)

    D = D_MODEL
    S, CN = x.shape[1], ctx.shape[1]
    T = S + CN
    TM = ROW_TILE
    ct, ns, nt = CN // TM, S // TM, T // TM
    me = 4 * lax.axis_index("x") + 2 * lax.axis_index("y") + lax.axis_index("c")

    lat = lambda i: i + ct
    swp = lambda i: jnp.where(i < ct, i + ns, i - ct)
    lat_or_0 = lambda i: jnp.maximum(i - ct, 0)

    small_shapes = [W[n].shape[1:] for n in SMALL_SHARDED] + [(D,)]
    n_small = sum(math.prod(s) for s in small_shapes)
    small_rows = _pack_rows(n_small)
    small_loc = _pack([W[n][0] for n in SMALL_SHARDED] + [c[0]], small_rows)
    early = ['w_in', 'mla_w_uq', 'mla_w_ukv']
    late = ['w_out', 'ffn_w_up', 'ffn_w_down']
    big = early + late
    gathered = _exchange("gather_weights", [W[n][0].astype(BF16) for n in early] + [small_loc], ['ag'] * 4)
    gw = dict(zip(early, gathered[:3]))
    small_all = [_unpack(gathered[3][d], small_shapes) for d in range(N_DEV)]
    full_small = {n: jnp.concatenate([small_all[d][j] for d in range(N_DEV)], axis=-1)
                  for j, n in enumerate(SMALL_SHARDED)}
    c_all = jnp.stack([small_all[d][-1] for d in range(N_DEV)])

    w_in_f = _cols_from_shards(gw['w_in'])
    w_in_p = jnp.concatenate([w_in_f[:, :OFF_KR], jnp.zeros((D, NOPE), BF16), w_in_f[:, OFF_KR:],
                              jnp.zeros((D, HEAD_PAD - QK), BF16)], axis=1)
    w_uq_f = _cols_from_shards(gw['mla_w_uq']).reshape(Q_RANK, HEADS, QK)
    wq_p = jnp.pad(w_uq_f, ((0, 0), (0, 0), (0, HEAD_PAD - QK))).reshape(Q_RANK, HEADS * HEAD_PAD)
    w_ukv_f = _cols_from_shards(gw['mla_w_ukv']).reshape(KV_RANK, HEADS, NOPE + VDIM)
    wk_p = jnp.pad(w_ukv_f[:, :, :NOPE], ((0, 0), (0, 0), (0, HEAD_PAD - NOPE))).reshape(KV_RANK, HEADS * HEAD_PAD)
    wv_f = w_ukv_f[:, :, NOPE:].reshape(KV_RANK, HEADS * VDIM)

    lru_cw, lru_ba, lru_bx, lru_lam, ffn_cw = [full_small[n] for n in SMALL_SHARDED]
    ffn_cw_t, ffn_cb_t = _ff_to_tiles(ffn_cw), _ff_to_tiles(ffn_conv_b)

    def block_diag(w):
        eye = jnp.eye(LRU_HEADS, dtype=w.dtype)
        return jnp.einsum('hij,hg->higj', w, eye).reshape(LRU_W, LRU_W)

    w_gate = jnp.concatenate([block_diag(lru_w_a[0, 0]), block_diag(lru_w_x[0, 0]),
                              block_diag(lru_w_a[0, 1]), block_diag(lru_w_x[0, 1])], axis=1).astype(BF16)
    b_gate = jnp.concatenate([lru_ba[0], lru_bx[0], lru_ba[1], lru_bx[1]])[None]
    lam_row = lru_lam.reshape(1, 2 * LRU_W)

    c16 = jnp.concatenate([c_all, c_ctx[None], jnp.zeros((2 * SUBLANES - N_DEV - 1, D), F32)])
    ncol = w_mod.shape[2]
    b_mod_loc = lax.dynamic_slice(b_mod, (0, me * ncol), (1, ncol))

    def mod_fwd(c16_r, w_r, b_r):
        c16_v = c16_r[...]
        sl = c16_v * jax.nn.sigmoid(c16_v)
        return (jnp.dot(sl.astype(BF16), w_r[...].astype(BF16), preferred_element_type=F32) + b_r[...],)

    (mod_part,) = _single("mod_fwd", mod_fwd, [c16, w_mod[0], b_mod_loc], [((2 * SUBLANES, ncol), F32)])
    (mod_g,) = _exchange("gather_mod", [mod_part], ['ag'])
    mod_all = _cols_from_shards(mod_g)
    mod_lat = lax.dynamic_slice(mod_all, (me, 0), (1, N_MOD * D)).reshape(N_MOD, D)
    mod_ctx = mod_all[N_DEV].reshape(N_MOD, D)

    xs, tgt = x[0], loss_target[0]
    xa_rows = [_row(ctx[0], lambda i: jnp.minimum(i, ct - 1)), _row(xs, lat_or_0)]

    def sel_mod(i, ml, mc, r0):
        sh = jnp.where(i < ct, mc[r0:r0 + 1, :], ml[r0:r0 + 1, :])
        sc = jnp.where(i < ct, mc[r0 + 1:r0 + 2, :], ml[r0 + 1:r0 + 2, :])
        return sh, sc

    def pre_fn(xv, g, sh, sc):
        return _rms(xv, g) * (1.0 + sc) + sh

    def k_pre(i, rv, bv):
        sh, sc = sel_mod(i, bv[1], bv[2], 0)
        hv = pre_fn(jnp.where(i < ct, rv[0], rv[1]), bv[0][...], sh, sc).astype(BF16)
        return (hv, jnp.dot(hv, bv[3][...], preferred_element_type=F32)), ()

    h_pre, proj = _rowwise("pre_mix", k_pre, nt, xa_rows, [g_pre_mix, mod_lat, mod_ctx, w_in_p],
                           [((T, D), BF16, None), ((T, IN_W_PAD), F32, None)])

    xcv = _dwconv_fwd("lru_conv", proj, lru_cw, lru_conv_b, LRU_CONV_LEFT, (0, ct))

    def k_gates(i, rv, bv):
        xv = rv[0]
        z = jnp.dot(xv.astype(BF16), bv[0][...], preferred_element_type=F32) + bv[1][...]
        return _gates(z, xv, _bc(_softplus(-bv[2][...]), TM)), ()

    a0, u0, a1, u1 = _rowwise("lru_gates", k_gates, nt, [_row(xcv)], [w_gate, b_gate, lam_row],
                              [((T, LRU_W), F32, None), ((T, LRU_W), F32, None),
                               ((T, LRU_W), F32, swp), ((T, LRU_W), F32, swp)])
    h0, hprev0 = _scan("lru_scan_f", a0, u0, False, with_prev=True)
    h1, hprev1 = _scan("lru_scan_r", a1, u1, True, with_prev=True)

    cos_k, sup_k, sdn_k = _rope_tables(S, CN)
    q_tables = [_row(cos_k, lat), _row(sup_k, lat), _row(sdn_k, lat)]
    cq_row = _row(proj, lat, (Q_RANK, OFF_CQ // Q_RANK))
    ckv_row = _row(proj, None, (KV_RANK, OFF_CKV // KV_RANK))

    def heads_of(xl):
        return [xl[:, hh * HEAD_PAD:(hh + 1) * HEAD_PAD] for hh in range(HEADS)]

    def k_q_path(i, rv, bv):
        cqv, ctb, sub, sdb = rv
        cqn_v = _rms(cqv, bv[0][...]).astype(BF16)
        ql = jnp.dot(cqn_v, bv[1][...], preferred_element_type=F32)
        return (cqn_v, jnp.stack([_rope(qh, ctb, sub, sdb) * SCALE2 for qh in heads_of(ql)])), ()

    cqn, q = _rowwise("q_path", k_q_path, ns, [cq_row] + q_tables, [mla_g_q, wq_p],
                      [((S, Q_RANK), BF16, None), ((HEADS, S, HEAD_PAD), BF16, None)])

    def k_kv_path(i, rv, bv):
        ckv_v, krp, ctb, sub, sdb = rv
        ckvn_v = _rms(ckv_v, bv[0][...]).astype(BF16)
        kl = jnp.dot(ckvn_v, bv[1][...], preferred_element_type=F32)
        vl = jnp.dot(ckvn_v, bv[2][...], preferred_element_type=F32)
        kr = _rope(krp, ctb, sub, sdb)
        return (ckvn_v, jnp.stack([kh + kr for kh in heads_of(kl)]), vl), ()

    ckvn, k, v_lin = _rowwise(
        "kv_path", k_kv_path, nt,
        [ckv_row, _row(proj, None, (HEAD_PAD, OFF_KR // HEAD_PAD)), _row(cos_k), _row(sup_k), _row(sdn_k)],
        [mla_g_kv, wk_p, wv_f],
        [((T, KV_RANK), BF16, None), ((HEADS, T, HEAD_PAD), BF16, None), ((T, HEADS * VDIM), BF16, None)])
    v = v_lin.reshape(T, HEADS, VDIM).transpose(1, 0, 2)

    o, lse, late_g = _attn_fwd(q, k, v, [W[n][0].astype(BF16) for n in late], ['ag'] * 3)
    w_out_f = late_g[0].reshape(D, D)
    w_up_t = _ff_permute("w_up_to_tiles", _cols_from_shards(late_g[1]), True)
    w_down_f = late_g[2].reshape(D_FF, D)
    o_t = o.transpose(1, 0, 2).reshape(S, HEADS * VDIM)

    def lru_out_fn(hf, hr, gr):
        return (hf + hr) * jax.nn.gelu(gr)

    def post_mix_fn(xv, yv, gt, g):
        return xv + gt * _rms(yv, g)

    def k_mix_out(i, rv, bv):
        ml = bv[0]
        y_in_v = jnp.concatenate([lru_out_fn(rv[0], rv[1], rv[2]), rv[3]], axis=1).astype(BF16)
        yv = jnp.dot(y_in_v, bv[3][...], preferred_element_type=F32)
        x1v = post_mix_fn(rv[4], yv, ml[2:3, :], bv[1][...])
        return (y_in_v, yv, x1v, pre_fn(x1v, bv[2][...], ml[3:4, :], ml[4:5, :])), ()

    gr_row = _row(proj, lat, (LRU_W, OFF_GR // LRU_W))
    y_in, y, x1, h2 = _rowwise(
        "mix_out", k_mix_out, ns, [_row(h0, lat), _row(h1), gr_row, _row(o_t), _row(xs)],
        [mod_lat, g_post_mix, g_pre_ffn, w_out_f],
        [((S, D), BF16, None), ((S, D), F32, None), ((S, D), F32, None), ((S, D), BF16, None)])
    up = _mm("ffn_up", h2, w_up_t, 'nn', out_dtype=BF16)
    act = _ffn_mid_fwd(up, ffn_cw_t, ffn_cb_t)
    f = _mm("ffn_down", act, w_down_f, 'nn')

    def loss_fn(x1v, fv, gt, g, tg):
        x2 = x1v + gt * _rms(fv, g)
        err = x2 - tg
        return 0.5 * jnp.sum(jnp.mean(err * err, axis=-1))

    def k_loss(i, rv, bv):
        gtb, gb = _bc(bv[0][5:6, :], TM), _bc(bv[1][...], TM)
        val, (dx1v, dfv, dgt, dg) = jax.value_and_grad(loss_fn, argnums=(0, 1, 2, 3))(rv[0], rv[1], gtb, gb, rv[2])
        return (dx1v, dfv), (jnp.full((1, LANES), val, F32), _rs(dgt), _rs(dg))

    dx1_a, df, loss_acc, d_gt2, d_g_post_ffn = _rowwise(
        "loss_bwd", k_loss, ns, [_row(x1), _row(f), _row(tgt)], [mod_lat, g_post_ffn],
        [((S, D), F32, None), ((S, D), BF16, None)], [(1, LANES), (1, D), (1, D)])

    d_act = _mm("ffn_down_dx", df, w_down_f, 'nt', out_dtype=BF16)
    d_w_down = _mm("ffn_down_dw", act, df, 'tn')
    d_up, d_ffn_cw_t, d_ffn_cb_t = _ffn_mid_bwd(up, d_act, ffn_cw_t, ffn_cb_t)
    d_ffn_cw, d_ffn_cb = _ff_from_tiles(d_ffn_cw_t), _ff_from_tiles(d_ffn_cb_t)
    d_h2 = _mm("ffn_up_dx", d_up, w_up_t, 'nt')
    d_w_up = _ff_permute("d_w_up_from_tiles", _mm("ffn_up_dw", h2, d_up, 'tn'), False)

    def k_pre_ffn_bwd(i, rv, bv):
        ml = bv[0]
        gb, shb, scb = _bc(bv[1][...], TM), _bc(ml[3:4, :], TM), _bc(ml[4:5, :], TM)
        _, pull = jax.vjp(pre_fn, rv[0], gb, shb, scb)
        dxv, dg, dsh, dsc = pull(rv[1])
        return (rv[2] + dxv,), (_rs(dg), _rs(dsh), _rs(dsc))

    dx1, d_g_pre_ffn, d_sh2, d_sc2 = _rowwise(
        "pre_ffn_bwd", k_pre_ffn_bwd, ns, [_row(x1), _row(d_h2), _row(dx1_a)], [mod_lat, g_pre_ffn],
        [((S, D), F32, None)], [(1, D), (1, D), (1, D)])

    def k_post_mix_bwd(i, rv, bv):
        gtb, gb = _bc(bv[0][2:3, :], TM), _bc(bv[1][...], TM)
        _, pull = jax.vjp(post_mix_fn, rv[0], rv[1], gtb, gb)
        _, dyv, dgt, dg = pull(rv[2])
        return (dyv,), (_rs(dgt), _rs(dg))

    dy, d_gt1, d_g_post_mix = _rowwise(
        "post_mix_bwd", k_post_mix_bwd, ns, [_row(xs), _row(y), _row(dx1)], [mod_lat, g_post_mix],
        [((S, D), BF16, None)], [(1, D), (1, D)])
    d_y_in = _mm("out_proj_dx", dy, w_out_f, 'nt')
    d_w_out = _mm("out_proj_dw", y_in, dy, 'tn')

    def k_lru_out_bwd(i, rv, bv):
        _, pull = jax.vjp(lru_out_fn, rv[0], rv[1], rv[2])
        dhf, _, dgr = pull(rv[3])
        return (dhf, dgr), ()

    d_hsum, d_gr = _rowwise("lru_out_bwd", k_lru_out_bwd, ns,
                            [_row(h0, lat), _row(h1), gr_row, _row(d_y_in, None, (LRU_W, 0))], [],
                            [((S, LRU_W), F32, None), ((S, LRU_W), F32, None)])

    do_b = d_y_in[:, LRU_W:].astype(BF16).reshape(S, HEADS, VDIM).transpose(1, 0, 2)
    delta = _attn_delta(d_y_in, o_t)
    late_sends = [d_w_out.reshape(N_DEV, D // N_DEV, D), _cols_to_shards(d_w_up),
                  d_w_down.reshape(N_DEV, D_FF // N_DEV, D)]
    dq, dk, dv, late_recv = _attn_bwd(q, k, v, do_b, lse, delta, late_sends, ['a2a'] * 3)

    def rms_bwd(xv, g, dy):
        _, pull = jax.vjp(_rms, xv, _bc(g, TM))
        dxv, dg = pull(dy)
        return dxv, _rs(dg)

    def k_q_path_bwd(i, rv, bv):
        dqv, ctb, sub, sdb, cqv = rv
        dql = jnp.concatenate([_rope_t(dqv[hh] * MLA_SCALE, ctb, sub, sdb) for hh in range(HEADS)], axis=1).astype(BF16)
        d_cqn_v = lax.dot_general(dql, bv[1][...], NT_DIMS, preferred_element_type=F32)
        dxv, dg = rms_bwd(cqv, bv[0][...], d_cqn_v)
        return (dql, dxv), (dg,)

    dq_lin, d_cq, d_g_q = _rowwise(
        "q_path_bwd", k_q_path_bwd, ns, [_row(dq)] + q_tables + [cq_row], [mla_g_q, wq_p],
        [((S, HEADS * HEAD_PAD), BF16, None), ((S, Q_RANK), F32, None)], [(1, Q_RANK)])
    d_wq_p = _mm("q_proj_dw", cqn, dq_lin, 'tn')

    dv_lin = dv.transpose(1, 0, 2).reshape(T, HEADS * VDIM).astype(BF16)

    def k_kv_path_bwd(i, rv, bv):
        dkv_, ctb, sub, sdb, dvl, ckv_v = rv
        tot = dkv_[0]
        for hh in range(1, HEADS):
            tot = tot + dkv_[hh]
        lane = lax.broadcasted_iota(jnp.int32, tot.shape, 1)
        tot = jnp.where((lane >= NOPE) & (lane < QK), tot, 0.0)
        dkl = jnp.concatenate([dkv_[hh] for hh in range(HEADS)], axis=1).astype(BF16)
        d_ckvn_v = (lax.dot_general(dkl, bv[1][...], NT_DIMS, preferred_element_type=F32)
                    + lax.dot_general(dvl, bv[2][...], NT_DIMS, preferred_element_type=F32))
        dxv, dg = rms_bwd(ckv_v, bv[0][...], d_ckvn_v)
        return (dkl, _rope_t(tot, ctb, sub, sdb), dxv), (dg,)

    dk_lin, d_krp, d_ckv, d_g_kv = _rowwise(
        "kv_path_bwd", k_kv_path_bwd, nt,
        [_row(dk), _row(cos_k), _row(sup_k), _row(sdn_k), _row(dv_lin), ckv_row], [mla_g_kv, wk_p, wv_f],
        [((T, HEADS * HEAD_PAD), BF16, None), ((T, HEAD_PAD), F32, None), ((T, KV_RANK), F32, None)], [(1, KV_RANK)])
    d_wk_p = _mm("k_proj_dw", ckvn, dk_lin, 'tn')
    d_wv = _mm("v_proj_dw", ckvn, dv_lin, 'tn')

    lam0 = _scan("lru_scan_f_bwd", a0, d_hsum, True, shifted=True, u_off=CN)
    lam1 = _scan("lru_scan_r_bwd", a1, d_hsum, False, shifted=True, u_off=0)

    def k_gates_bwd(i, rv, bv):
        xv, l0, hp0, l1, hp1 = rv
        wg, bg, lamv = [b[...] for b in bv]
        xb = xv.astype(BF16)
        z = jnp.dot(xb, wg, preferred_element_type=F32) + bg
        spb = _bc(_softplus(-lamv), TM)
        _, pull = jax.vjp(_gates, z, xv, spb)
        dz, dxv, dsp = pull((l0 * hp0, l0, l1 * hp1, l1))
        dzb = dz.astype(BF16)
        dxv = dxv + lax.dot_general(dzb, wg, NT_DIMS, preferred_element_type=F32)
        dwg = lax.dot_general(xb, dzb, (((0,), (0,)), ((), ())), preferred_element_type=F32)
        dlam = -_rs(dsp) * jax.nn.sigmoid(-lamv)
        return (dxv,), (dwg, _rs(dz), dlam)

    d_xcv, d_w_gate, d_b_gate, d_lam = _rowwise(
        "lru_gates_bwd", k_gates_bwd, nt,
        [_row(xcv), _row(lam0), _row(hprev0), _row(lam1, swp), _row(hprev1, swp)], [w_gate, b_gate, lam_row],
        [((T, LRU_W), F32, None)], [(LRU_W, 4 * LRU_W), (1, 4 * LRU_W), (1, 2 * LRU_W)])
    d_xr, d_lru_cw, d_lru_cb = _dwconv_bwd("lru_conv_bwd", proj, d_xcv, lru_cw, LRU_CONV_LEFT, (0, ct))

    def k_dproj(i, rv, bv):
        is_lat = i >= ct
        return (jnp.concatenate([rv[0], jnp.where(is_lat, rv[1], 0.0), jnp.where(is_lat, rv[2], 0.0), rv[3], rv[4]],
                                axis=1),), ()

    d_proj = _rowwise("d_proj", k_dproj, nt,
                      [_row(d_xr), _row(d_gr, lat_or_0), _row(d_cq, lat_or_0), _row(d_ckv), _row(d_krp)], [],
                      [((T, IN_W_PAD), BF16, None)])[0]
    d_w_in_p = _mm("in_proj_dw", h_pre, d_proj, 'tn')

    d_b_a = jnp.stack([d_b_gate[0, 0:LRU_W], d_b_gate[0, 2 * LRU_W:3 * LRU_W]])
    d_b_x = jnp.stack([d_b_gate[0, LRU_W:2 * LRU_W], d_b_gate[0, 3 * LRU_W:]])
    d_w_in = jnp.concatenate([d_w_in_p[:, :OFF_KR], d_w_in_p[:, OFF_KR + NOPE:OFF_KR + QK]], axis=1)
    d_w_uq = d_wq_p.reshape(Q_RANK, HEADS, HEAD_PAD)[:, :, :QK].reshape(Q_RANK, HEADS * QK)
    d_w_ukv = jnp.concatenate([d_wk_p.reshape(KV_RANK, HEADS, HEAD_PAD)[:, :, :NOPE],
                               d_wv.reshape(KV_RANK, HEADS, VDIM)], axis=2).reshape(KV_RANK, HEADS * (NOPE + VDIM))
    small_full = dict(lru_conv_w=d_lru_cw, lru_b_a=d_b_a, lru_b_x=d_b_x, lru_lambda=d_lam.reshape(2, LRU_W),
                      ffn_conv_w=d_ffn_cw)
    small_sh = jnp.concatenate([_cols_to_shards(small_full[n]).reshape(N_DEV, -1) for n in SMALL_SHARDED], axis=1)
    n_sh = small_sh.shape[1]
    sh_rows = _pack_rows(n_sh)
    small_sh = jnp.pad(small_sh, ((0, 0), (0, sh_rows * LANES - n_sh))).reshape(N_DEV, sh_rows, LANES)
    early_sends = [_cols_to_shards(d_w_in), _cols_to_shards(d_w_uq), _cols_to_shards(d_w_ukv), small_sh]

    def k_pre_bwd(i, rv, bv):
        g, ml, mc = bv[0][...], bv[1], bv[2]
        sh, sc = sel_mod(i, ml, mc, 0)
        d_h = lax.dot_general(rv[2], bv[3][...], NT_DIMS, preferred_element_type=F32)
        _, pull = jax.vjp(pre_fn, jnp.where(i < ct, rv[0], rv[1]), _bc(g, TM), _bc(sh, TM), _bc(sc, TM))
        dxv, dg, dsh, dsc = pull(d_h)
        is_lat = i >= ct
        dsh, dsc = _rs(dsh), _rs(dsc)
        zero = jnp.zeros_like(dsh)
        return ((dxv + rv[3],),
                (_rs(dg), jnp.where(is_lat, dsh, zero), jnp.where(is_lat, dsc, zero),
                 jnp.where(is_lat, zero, dsh), jnp.where(is_lat, zero, dsc)))

    (dxl, d_g_pre_mix, d_sh1, d_sc1, d_csh1, d_csc1), early_recv = _rowwise(
        "pre_mix_bwd", k_pre_bwd, nt, xa_rows + [_row(d_proj), _row(dx1, lat_or_0)],
        [g_pre_mix, mod_lat, mod_ctx, w_in_p], [((S, D), F32, lat_or_0)], [(1, D)] * 5,
        ride_arrs=early_sends, ride_modes=['a2a'] * 4)
    grad_x = dxl[None]

    zrow = jnp.zeros((1, D), F32)
    d_mod_lat = jnp.concatenate([d_sh1, d_sc1, d_gt1, d_sh2, d_sc2, d_gt2], axis=1)
    d_mod_ctx = jnp.concatenate([d_csh1, d_csc1, zrow, zrow, zrow, zrow], axis=1)
    loss_row = jnp.pad(loss_acc, ((0, 0), (0, N_MOD * D - LANES)))
    d_mod_mine = jnp.concatenate([d_mod_lat, d_mod_ctx, loss_row, jnp.zeros((SUBLANES - 3, N_MOD * D), F32)])

    def diag_blocks(dw):
        return jnp.stack([dw[hh * 64:(hh + 1) * 64, hh * 64:(hh + 1) * 64] for hh in range(LRU_HEADS)])

    d_lru_w_a = jnp.stack([diag_blocks(d_w_gate[:, 0:LRU_W]), diag_blocks(d_w_gate[:, 2 * LRU_W:3 * LRU_W])])[None]
    d_lru_w_x = jnp.stack([diag_blocks(d_w_gate[:, LRU_W:2 * LRU_W]), diag_blocks(d_w_gate[:, 3 * LRU_W:])])[None]
    rep_part = dict(b_mod=d_mod_lat + d_mod_ctx, g_pre_mix=d_g_pre_mix, g_post_mix=d_g_post_mix,
                    g_pre_ffn=d_g_pre_ffn, g_post_ffn=d_g_post_ffn, lru_conv_b=d_lru_cb, lru_w_a=d_lru_w_a,
                    lru_w_x=d_lru_w_x, mla_g_q=d_g_q, mla_g_kv=d_g_kv, ffn_conv_b=d_ffn_cb)
    rep_names = [n for n in REPLICATED if n != 'c_ctx']
    rep_shapes = [W[n].shape for n in rep_names]
    rep_rows = -(-_pack_rows(sum(W[n].size for n in rep_names)) // ROW_TILE) * ROW_TILE
    rep_loc = _pack([rep_part[n] for n in rep_names], rep_rows)
    d_mod_all, rep_all = _exchange("gather_dmod", [d_mod_mine, rep_loc], ['ag'] * 2)
    loss = jnp.sum(d_mod_all[:, 2, 0])
    dm_lat_loc = lax.dynamic_slice(d_mod_all[:, 0], (0, me * ncol), (N_DEV, ncol))
    dm_ctx_loc = lax.dynamic_slice(d_mod_all[:, 1], (0, me * ncol), (N_DEV, ncol))

    def mod_bwd(c16_r, w_r, dml_r, dmc_r):
        c16_v = c16_r[...]
        sig = jax.nn.sigmoid(c16_v)
        sl = c16_v * sig
        dctx = dmc_r[0:1, :]
        for d in range(1, N_DEV):
            dctx = dctx + dmc_r[d:d + 1, :]
        row = lax.broadcasted_iota(jnp.int32, (2 * SUBLANES, ncol), 0)
        dm16 = dml_r[...] + jnp.where(row == N_DEV, _bc(dctx, 2 * SUBLANES), 0.0)
        dw = lax.dot_general(sl.astype(BF16), dm16.astype(BF16), (((0,), (0,)), ((), ())), preferred_element_type=F32)
        dsl = lax.dot_general(dm16.astype(BF16), w_r[...].astype(BF16), NT_DIMS, preferred_element_type=F32)
        dc = dsl * (sig * (1.0 + c16_v * (1.0 - sig)))
        return dw, dc

    dm_lat16 = jnp.concatenate([dm_lat_loc, jnp.zeros((2 * SUBLANES - N_DEV, ncol), F32)])
    g_w_mod, dc16 = _single("mod_bwd", mod_bwd, [c16, w_mod[0], dm_lat16, dm_ctx_loc],
                            [((D, ncol), F32), ((2 * SUBLANES, D), F32)])
    d_c_ctx_part = dc16[N_DEV]

    (c_ctx_all,) = _exchange("gather_d_c_ctx", [d_c_ctx_part.reshape(SUBLANES, D // SUBLANES)], ['ag'])
    big_parts = list(early_recv[:3]) + list(late_recv)

    res = {}

    def adam(name, w2, m2, v2, parts):
        return _adamw("adamw_" + name, w2, m2, v2, parts)

    for n, parts in zip(big, big_parts):
        shp = W[n].shape
        outs = adam(n, W[n][0], M[n][0], V[n][0], parts)
        res[n] = [o_.reshape(shp) for o_ in outs]
    outs = adam('w_mod', w_mod[0], m_w_mod[0], v_w_mod[0], g_w_mod[None])
    res['w_mod'] = [o_.reshape(w_mod.shape) for o_ in outs]

    sh_shapes = [W[n].shape for n in SMALL_SHARDED]
    pk = lambda dct: _pack([dct[n] for n in SMALL_SHARDED], sh_rows)
    outs = adam('small_sharded', pk(W), pk(M), pk(V), early_recv[3])
    for n, vals in zip(SMALL_SHARDED, zip(*[_unpack(o_, sh_shapes) for o_ in outs])):
        res[n] = list(vals)

    pr = lambda dct: _pack([dct[n] for n in rep_names], rep_rows)
    outs = adam('replicated', pr(W), pr(M), pr(V), rep_all)
    for n, vals in zip(rep_names, zip(*[_unpack(o_, rep_shapes) for o_ in outs])):
        res[n] = list(vals)
    as_tile = lambda a: a.reshape(SUBLANES, D // SUBLANES)
    outs = adam('c_ctx', as_tile(c_ctx), as_tile(m_c_ctx), as_tile(v_c_ctx), c_ctx_all)
    res['c_ctx'] = [o_.reshape(c_ctx.shape) for o_ in outs]

    return (loss, grad_x, *[res[n][0] for n in WEIGHTS], *[res[n][1] for n in WEIGHTS],
            *[res[n][2] for n in WEIGHTS], *[res[n][3] for n in WEIGHTS])
```

```python
import functools
import math

import jax
import jax.numpy as jnp
from jax import lax
from jax.experimental import pallas as pl
from jax.experimental.pallas import tpu as pltpu

F32 = jnp.float32
BF16 = jnp.bfloat16
MESH = pl.DeviceIdType.MESH

N_DEV = 8
ROW_TILE = 256
SUBLANES = 8
LANES = 128
VMEM_LIMIT = 56 * 1024 * 1024

D_MODEL = 1024
LRU_W = 512
LRU_HEADS = 8
LRU_CONV_K = 4
LRU_CONV_LEFT = 2
LRU_C = 8.0
HEADS = 8
NOPE = 64
ROPE = 32
VDIM = 64
QK = NOPE + ROPE
HEAD_PAD = 128
Q_RANK = 256
KV_RANK = 128
MLA_SCALE = QK ** -0.5
ROPE_PAIRS = ROPE // 4
ROPE_BASE = 10000.0
GRID_W = 64
D_FF = 2816
FFN_CONV_K = 3
FFN_CONV_LEFT = 1
N_MOD = 6
EPS = 1e-6
IN_W = 2 * LRU_W + Q_RANK + KV_RANK + ROPE
IN_W_PAD = 2 * LRU_W + Q_RANK + KV_RANK + HEAD_PAD
OFF_GR, OFF_CQ, OFF_CKV, OFF_KR = LRU_W, 2 * LRU_W, 2 * LRU_W + Q_RANK, 2 * LRU_W + Q_RANK + KV_RANK

ADAM_LR, ADAM_B1, ADAM_B2, ADAM_EPS, ADAM_WD, ADAM_STEP = 0.001, 0.9, 0.999, 1e-08, 0.01, 10

WEIGHTS = ['c_ctx', 'w_mod', 'b_mod', 'g_pre_mix', 'g_post_mix', 'g_pre_ffn', 'g_post_ffn', 'w_in', 'lru_conv_w',
           'lru_conv_b', 'lru_w_a', 'lru_b_a', 'lru_w_x', 'lru_b_x', 'lru_lambda', 'mla_g_q', 'mla_w_uq', 'mla_g_kv',
           'mla_w_ukv', 'w_out', 'ffn_w_up', 'ffn_conv_w', 'ffn_conv_b', 'ffn_w_down']
REPLICATED = ['c_ctx', 'b_mod', 'g_pre_mix', 'g_post_mix', 'g_pre_ffn', 'g_post_ffn', 'lru_conv_b', 'lru_w_a',
              'lru_w_x', 'mla_g_q', 'mla_g_kv', 'ffn_conv_b']
SMALL_SHARDED = ['lru_conv_w', 'lru_b_a', 'lru_b_x', 'lru_lambda', 'ffn_conv_w']


def _pick(d, prefs):
    for p in prefs:
        if d % p == 0:
            return p
    return d


def _params(sem=None):
    return pltpu.CompilerParams(dimension_semantics=sem, vmem_limit_bytes=VMEM_LIMIT)


MM_TILES = (1024, 1408, 768, 512, 256, 128)


def _mm(name, a, b, mode, out_dtype=F32):
    if mode == 'nn':
        (m, k), (_, n) = a.shape, b.shape
    elif mode == 'nt':
        (m, k), (n, _) = a.shape, b.shape
    else:
        (k, m), (_, n) = a.shape, b.shape
    tm = _pick(m, MM_TILES)
    tn = _pick(n, MM_TILES)
    tk = _pick(k, MM_TILES)
    nk = k // tk
    if mode == 'nn':
        a_spec = pl.BlockSpec((tm, tk), lambda i, j, kk: (i, kk))
        b_spec = pl.BlockSpec((tk, tn), lambda i, j, kk: (kk, j))
        dn = (((1,), (0,)), ((), ()))
    elif mode == 'nt':
        a_spec = pl.BlockSpec((tm, tk), lambda i, j, kk: (i, kk))
        b_spec = pl.BlockSpec((tn, tk), lambda i, j, kk: (j, kk))
        dn = (((1,), (1,)), ((), ()))
    else:
        a_spec = pl.BlockSpec((tk, tm), lambda i, j, kk: (kk, i))
        b_spec = pl.BlockSpec((tk, tn), lambda i, j, kk: (kk, j))
        dn = (((0,), (0,)), ((), ()))

    def body(a_ref, b_ref, o_ref, acc_ref):
        kk = pl.program_id(2)

        @pl.when(kk == 0)
        def _():
            acc_ref[...] = jnp.zeros_like(acc_ref)

        acc_ref[...] += lax.dot_general(a_ref[...].astype(BF16), b_ref[...].astype(BF16), dn,
                                        preferred_element_type=F32)

        @pl.when(kk == nk - 1)
        def _():
            o_ref[...] = acc_ref[...].astype(o_ref.dtype)

    return pl.pallas_call(
        body, name=name, grid=(m // tm, n // tn, nk),
        in_specs=[a_spec, b_spec], out_specs=pl.BlockSpec((tm, tn), lambda i, j, kk: (i, j)),
        out_shape=jax.ShapeDtypeStruct((m, n), out_dtype),
        scratch_shapes=[pltpu.VMEM((tm, tn), F32)],
        compiler_params=_params(("parallel", "parallel", "arbitrary")),
    )(a, b)


def _row(a, idx=None, col=None):
    return dict(a=a, idx=idx, col=col)


def _rowwise(name, fn, n_tiles, rows, bcast, out_rows, out_acc=(), tm=ROW_TILE, ride_arrs=(), ride_modes=()):
    in_specs = []
    for r in rows:
        a, idx, col = r['a'], r['idx'] or (lambda i: i), r['col']
        if a.ndim == 2:
            w, ci = col if col else (a.shape[1], 0)
            in_specs.append(pl.BlockSpec((tm, w), lambda i, idx=idx, ci=ci: (idx(i), ci)))
        else:
            in_specs.append(pl.BlockSpec((a.shape[0], tm, a.shape[2]), lambda i, idx=idx: (0, idx(i), 0)))
    for b in bcast:
        in_specs.append(pl.BlockSpec(b.shape, lambda i, nd=b.ndim: (0,) * nd))
    out_specs, out_shape = [], []
    for shape, dtype, idx in out_rows:
        idx = idx or (lambda i: i)
        if len(shape) == 2:
            out_specs.append(pl.BlockSpec((tm, shape[1]), lambda i, idx=idx: (idx(i), 0)))
        else:
            out_specs.append(pl.BlockSpec((shape[0], tm, shape[2]), lambda i, idx=idx: (0, idx(i), 0)))
        out_shape.append(jax.ShapeDtypeStruct(shape, dtype))
    for shape in out_acc:
        out_specs.append(pl.BlockSpec(shape, lambda i, nd=len(shape): (0,) * nd))
        out_shape.append(jax.ShapeDtypeStruct(shape, F32))
    nr, nb, no, na, n = len(rows), len(bcast), len(out_rows), len(out_acc), len(ride_arrs)
    any_spec = pl.BlockSpec(memory_space=pl.ANY)

    def body(*refs):
        i = pl.program_id(0)
        finish = _riding_exchange(refs, nr + nb, no + na, n, ride_modes, i == 0, i == n_tiles - 1) if n else None
        rvals = [r[...] for r in refs[:nr]]
        bvals = list(refs[nr:nr + nb])
        o_rows, o_acc = fn(i, rvals, bvals)
        outs = refs[nr + nb + n:]
        for ref, v in zip(outs[:no], o_rows):
            ref[...] = v.astype(ref.dtype)
        acc_refs = outs[no:no + na]
        if acc_refs:
            @pl.when(i == 0)
            def _():
                for ref in acc_refs:
                    ref[...] = jnp.zeros_like(ref)
            for ref, v in zip(acc_refs, o_acc):
                ref[...] += v
        if n:
            finish()

    res = pl.pallas_call(
        body, name=name, grid=(n_tiles,), in_specs=in_specs + [any_spec] * n, out_specs=out_specs + [any_spec] * n,
        out_shape=out_shape + _exchange_shapes(ride_arrs, ride_modes),
        scratch_shapes=_exchange_sems(n) if n else [],
        compiler_params=pltpu.CompilerParams(dimension_semantics=("arbitrary",), vmem_limit_bytes=VMEM_LIMIT,
                                             has_side_effects=bool(n)),
    )(*[r['a'] for r in rows], *bcast, *ride_arrs)
    return (res[:no + na], res[no + na:]) if n else res


def _single(name, fn, ins, out_shapes):
    def body(*refs):
        outs = fn(*refs[:len(ins)])
        for ref, v in zip(refs[len(ins):], outs):
            ref[...] = v.astype(ref.dtype)

    return pl.pallas_call(
        body, name=name,
        in_specs=[pl.BlockSpec(memory_space=pltpu.VMEM)] * len(ins),
        out_specs=[pl.BlockSpec(memory_space=pltpu.VMEM)] * len(out_shapes),
        out_shape=[jax.ShapeDtypeStruct(s, d) for s, d in out_shapes],
        compiler_params=_params(),
    )(*ins)


def _bc(p, n):
    return jnp.broadcast_to(p, (n, p.shape[-1]))


def _rs(g):
    return jnp.sum(g, axis=0, keepdims=True)


def _rms(x, g):
    return x * lax.rsqrt(jnp.mean(x * x, axis=-1, keepdims=True) + EPS) * g


def _conv_specs(r, cw, tm, halo=SUBLANES):
    th = tm // halo
    last = r // halo - 1
    prev = pl.BlockSpec((halo, cw), lambda c, i: (jnp.maximum(i * th - 1, 0), c))
    cur = pl.BlockSpec((tm, cw), lambda c, i: (i, c))
    nxt = pl.BlockSpec((halo, cw), lambda c, i: (jnp.minimum((i + 1) * th, last), c))
    return [prev, cur, nxt]


def _fill_ext(ext_ref, prev_ref, cur_ref, next_ref, i, n_tiles, seg_starts, tm):
    prev_ok = functools.reduce(jnp.logical_and, [i != s for s in seg_starts])
    next_ok = functools.reduce(jnp.logical_and, [i + 1 != s for s in seg_starts] + [i + 1 < n_tiles])
    ext_ref[0:SUBLANES, :] = jnp.where(prev_ok, prev_ref[...].astype(F32), 0.0)
    ext_ref[SUBLANES:SUBLANES + tm, :] = cur_ref[...].astype(F32)
    ext_ref[SUBLANES + tm:, :] = jnp.where(next_ok, next_ref[...].astype(F32), 0.0)


def _dwconv_fwd(name, x, w, b, left, seg_starts, cw=512, tm=ROW_TILE):
    r, c = x.shape[0], w.shape[1]
    kw = w.shape[0]
    n_tiles = r // tm

    def body(prev_ref, cur_ref, next_ref, w_ref, b_ref, o_ref, ext_ref):
        i = pl.program_id(1)
        _fill_ext(ext_ref, prev_ref, cur_ref, next_ref, i, n_tiles, seg_starts, tm)
        out = jnp.broadcast_to(b_ref[...], (tm, cw))
        for k in range(kw):
            out = out + ext_ref[pl.ds(SUBLANES + k - left, tm), :] * w_ref[k:k + 1, :]
        o_ref[...] = out

    return pl.pallas_call(
        body, name=name, grid=(c // cw, n_tiles),
        in_specs=_conv_specs(r, cw, tm) + [pl.BlockSpec((kw, cw), lambda c_, i: (0, c_)),
                                           pl.BlockSpec((1, cw), lambda c_, i: (0, c_))],
        out_specs=pl.BlockSpec((tm, cw), lambda c_, i: (i, c_)),
        out_shape=jax.ShapeDtypeStruct((r, c), F32),
        scratch_shapes=[pltpu.VMEM((tm + 2 * SUBLANES, cw), F32)],
        compiler_params=_params(("parallel", "arbitrary")),
    )(x, x, x, w, b)


def _dwconv_bwd(name, x, dy, w, left, seg_starts, out_dtype=F32, cw=512, tm=ROW_TILE):
    r, c = dy.shape
    kw = w.shape[0]
    n_tiles = r // tm

    def body(xp, xc, xn, dp, dc, dn, w_ref, dx_ref, dw_ref, db_ref, xe_ref, de_ref):
        i = pl.program_id(1)
        _fill_ext(xe_ref, xp, xc, xn, i, n_tiles, seg_starts, tm)
        _fill_ext(de_ref, dp, dc, dn, i, n_tiles, seg_starts, tm)
        dyc = dc[...].astype(F32)
        dx = jnp.zeros((tm, cw), F32)
        dws = []
        for k in range(kw):
            dx = dx + de_ref[pl.ds(SUBLANES - k + left, tm), :] * w_ref[k:k + 1, :]
            dws.append(jnp.sum(dyc * xe_ref[pl.ds(SUBLANES + k - left, tm), :], axis=0, keepdims=True))
        dx_ref[...] = dx.astype(dx_ref.dtype)

        @pl.when(i == 0)
        def _():
            dw_ref[...] = jnp.zeros_like(dw_ref)
            db_ref[...] = jnp.zeros_like(db_ref)

        for k in range(kw):
            dw_ref[k:k + 1, :] += dws[k]
        db_ref[...] += jnp.sum(dyc, axis=0, keepdims=True)

    return pl.pallas_call(
        body, name=name, grid=(c // cw, n_tiles),
        in_specs=_conv_specs(r, cw, tm) + _conv_specs(r, cw, tm) + [pl.BlockSpec((kw, cw), lambda c_, i: (0, c_))],
        out_specs=[pl.BlockSpec((tm, cw), lambda c_, i: (i, c_)),
                   pl.BlockSpec((kw, cw), lambda c_, i: (0, c_)),
                   pl.BlockSpec((1, cw), lambda c_, i: (0, c_))],
        out_shape=[jax.ShapeDtypeStruct((r, c), out_dtype), jax.ShapeDtypeStruct((kw, c), F32),
                   jax.ShapeDtypeStruct((1, c), F32)],
        scratch_shapes=[pltpu.VMEM((tm + 2 * SUBLANES, cw), F32), pltpu.VMEM((tm + 2 * SUBLANES, cw), F32)],
        compiler_params=_params(("parallel", "arbitrary")),
    )(x, x, x, dy, dy, dy, w)


FF_TILE = 256
FF_HALO = 16
FF_STRIP = 32


def _ffn_fill(ext_ref, prev_ref, cur_ref, next_ref, i, n_tiles, tm):
    ext_ref[0:FF_HALO, :] = jnp.where(i > 0, prev_ref[...].astype(F32), 0.0)
    ext_ref[FF_HALO:FF_HALO + tm, :] = cur_ref[...].astype(F32)
    ext_ref[FF_HALO + tm:, :] = jnp.where(i + 1 < n_tiles, next_ref[...].astype(F32), 0.0)


def _ffn_conv(ext_ref, w_ref, b_ref, start, rows):
    out = jnp.broadcast_to(b_ref[...], (rows, 2 * FF_TILE))
    for k in range(FFN_CONV_K):
        out = out + ext_ref[pl.ds(start + k - FFN_CONV_LEFT, rows), :] * w_ref[k:k + 1, :]
    return out


def _ffn_mid_fwd(up, w, b):
    s, c2 = up.shape
    tm = _pick(s, (2 * ROW_TILE, ROW_TILE))
    n_tiles = s // tm
    cw = 2 * FF_TILE

    def body(prev_ref, cur_ref, next_ref, w_ref, b_ref, o_ref, ext_ref):
        i = pl.program_id(1)
        _ffn_fill(ext_ref, prev_ref, cur_ref, next_ref, i, n_tiles, tm)
        for r0 in range(0, tm, FF_STRIP):
            upc = _ffn_conv(ext_ref, w_ref, b_ref, FF_HALO + r0, FF_STRIP)
            uv, gv = upc[:, :FF_TILE], upc[:, FF_TILE:]
            o_ref[r0:r0 + FF_STRIP, :] = (gv * jax.nn.sigmoid(gv) * uv).astype(o_ref.dtype)

    return pl.pallas_call(
        body, name="ffn_mid", grid=(c2 // cw, n_tiles),
        in_specs=_conv_specs(s, cw, tm, FF_HALO) + [pl.BlockSpec((FFN_CONV_K, cw), lambda c_, i: (0, c_)),
                                                   pl.BlockSpec((1, cw), lambda c_, i: (0, c_))],
        out_specs=pl.BlockSpec((tm, FF_TILE), lambda c_, i: (i, c_)),
        out_shape=jax.ShapeDtypeStruct((s, c2 // 2), BF16),
        scratch_shapes=[pltpu.VMEM((tm + 2 * FF_HALO, cw), F32)],
        compiler_params=_params(("parallel", "arbitrary")),
    )(up, up, up, w, b)


def _ffn_mid_bwd(up, d_act, w, b, tm=ROW_TILE):
    s, c2 = up.shape
    n_tiles = s // tm
    cw = 2 * FF_TILE
    h8 = SUBLANES

    def gate_bwd(upc, dact):
        uv, gv = upc[:, :FF_TILE], upc[:, FF_TILE:]
        sg = jax.nn.sigmoid(gv)
        return jnp.concatenate([dact * (gv * sg), dact * uv * (sg * (1.0 + gv * (1.0 - sg)))], axis=1)

    def body(up_p, up_c, up_n, da_p, da_c, da_n, w_ref, b_ref, dup_ref, dw_ref, db_ref, ext_ref, dext_ref):
        i = pl.program_id(1)
        _ffn_fill(ext_ref, up_p, up_c, up_n, i, n_tiles, tm)
        dws = [jnp.zeros((1, cw), F32) for _ in range(FFN_CONV_K)]
        dbs = jnp.zeros((1, cw), F32)
        for r0 in range(0, tm, FF_STRIP):
            d_c = gate_bwd(_ffn_conv(ext_ref, w_ref, b_ref, FF_HALO + r0, FF_STRIP),
                           da_c[r0:r0 + FF_STRIP, :].astype(F32))
            dext_ref[FF_HALO + r0:FF_HALO + r0 + FF_STRIP, :] = d_c
            for k in range(FFN_CONV_K):
                xk = ext_ref[pl.ds(FF_HALO + r0 + k - FFN_CONV_LEFT, FF_STRIP), :]
                dws[k] = dws[k] + jnp.sum(d_c * xk, axis=0, keepdims=True)
            dbs = dbs + jnp.sum(d_c, axis=0, keepdims=True)
        da_prev = jnp.where(i > 0, da_p[...].astype(F32)[FF_HALO - h8:, :], 0.0)
        da_next = jnp.where(i + 1 < n_tiles, da_n[...].astype(F32)[:h8, :], 0.0)
        dext_ref[FF_HALO - h8:FF_HALO, :] = gate_bwd(_ffn_conv(ext_ref, w_ref, b_ref, FF_HALO - h8, h8), da_prev)
        dext_ref[FF_HALO + tm:FF_HALO + tm + h8, :] = gate_bwd(_ffn_conv(ext_ref, w_ref, b_ref, FF_HALO + tm, h8), da_next)
        for r0 in range(0, tm, FF_STRIP):
            dup = jnp.zeros((FF_STRIP, cw), F32)
            for k in range(FFN_CONV_K):
                dup = dup + dext_ref[pl.ds(FF_HALO + r0 - k + FFN_CONV_LEFT, FF_STRIP), :] * w_ref[k:k + 1, :]
            dup_ref[r0:r0 + FF_STRIP, :] = dup.astype(dup_ref.dtype)

        @pl.when(i == 0)
        def _():
            dw_ref[...] = jnp.zeros_like(dw_ref)
            db_ref[...] = jnp.zeros_like(db_ref)

        for k in range(FFN_CONV_K):
            dw_ref[k:k + 1, :] += dws[k]
        db_ref[...] += dbs

    def half_specs():
        th = tm // FF_HALO
        last = s // FF_HALO - 1
        return [pl.BlockSpec((FF_HALO, FF_TILE), lambda c_, i: (jnp.maximum(i * th - 1, 0), c_)),
                pl.BlockSpec((tm, FF_TILE), lambda c_, i: (i, c_)),
                pl.BlockSpec((FF_HALO, FF_TILE), lambda c_, i: (jnp.minimum((i + 1) * th, last), c_))]

    return pl.pallas_call(
        body, name="ffn_mid_bwd", grid=(c2 // cw, n_tiles),
        in_specs=_conv_specs(s, cw, tm, FF_HALO) + half_specs() + [
            pl.BlockSpec((FFN_CONV_K, cw), lambda c_, i: (0, c_)), pl.BlockSpec((1, cw), lambda c_, i: (0, c_))],
        out_specs=[pl.BlockSpec((tm, cw), lambda c_, i: (i, c_)),
                   pl.BlockSpec((FFN_CONV_K, cw), lambda c_, i: (0, c_)),
                   pl.BlockSpec((1, cw), lambda c_, i: (0, c_))],
        out_shape=[jax.ShapeDtypeStruct((s, c2), BF16), jax.ShapeDtypeStruct((FFN_CONV_K, c2), F32),
                   jax.ShapeDtypeStruct((1, c2), F32)],
        scratch_shapes=[pltpu.VMEM((tm + 2 * FF_HALO, cw), F32), pltpu.VMEM((tm + 2 * FF_HALO, cw), F32)],
        compiler_params=_params(("parallel", "arbitrary")),
    )(up, up, up, d_act, d_act, d_act, w, b)


def _ff_permute(name, w, to_tiles):
    r = w.shape[0]
    nb = D_FF // FF_TILE
    natural = pl.BlockSpec((r, FF_TILE), lambda j, half: (0, half * nb + j))
    tiled = pl.BlockSpec((r, FF_TILE), lambda j, half: (0, 2 * j + half))

    def body(x_ref, o_ref):
        o_ref[...] = x_ref[...]

    return pl.pallas_call(
        body, name=name, grid=(nb, 2),
        in_specs=[natural if to_tiles else tiled], out_specs=tiled if to_tiles else natural,
        out_shape=jax.ShapeDtypeStruct(w.shape, w.dtype), compiler_params=_params(("parallel", "parallel")),
    )(w)


def _ff_to_tiles(w):
    r = w.shape[0]
    return w.reshape(r, 2, D_FF // FF_TILE, FF_TILE).transpose(0, 2, 1, 3).reshape(r, 2 * D_FF)


def _ff_from_tiles(w):
    r = w.shape[0]
    return w.reshape(r, D_FF // FF_TILE, 2, FF_TILE).transpose(0, 2, 1, 3).reshape(r, 2 * D_FF)


SCAN_UNROLL = 8


def _scan(name, a, u, reverse, shifted=False, u_off=0, with_prev=False):
    t, c = a.shape
    us = u.shape[0]
    n8 = t // SUBLANES
    lo, hi = 0, SUBLANES - 1

    def body(a_ref, u_ref, h_ref, *prev_ref):
        row = lax.broadcasted_iota(jnp.int32, (SUBLANES, LANES), 0)
        last = lo if reverse else hi

        def tile(ref, base):
            return ref[pl.ds(pl.multiple_of(base, SUBLANES), SUBLANES), :]

        def local(blk):
            base = blk * SUBLANES
            av = tile(a_ref, base)
            if shifted and reverse:
                nb = tile(a_ref, jnp.minimum(base + SUBLANES, t - SUBLANES))
                edge = jnp.where(base + SUBLANES >= t, 1.0, pltpu.roll(nb, hi, 0))
                av = jnp.where(row < hi, pltpu.roll(av, hi, 0), edge)
            elif shifted:
                pb = tile(a_ref, jnp.maximum(base - SUBLANES, 0))
                edge = jnp.where(base == 0, 1.0, pltpu.roll(pb, 1, 0))
                av = jnp.where(row >= 1, pltpu.roll(av, 1, 0), edge)
            ub = base - u_off
            hv = jnp.where((ub >= 0) & (ub < us), tile(u_ref, jnp.clip(ub, 0, us - SUBLANES)), 0.0)
            for s in (1, 2, 4):
                shift = SUBLANES - s if reverse else s
                ok = (row < SUBLANES - s) if reverse else (row >= s)
                a_sh = jnp.where(ok, pltpu.roll(av, shift, 0), 1.0)
                h_sh = jnp.where(ok, pltpu.roll(hv, shift, 0), 0.0)
                hv = av * h_sh + hv
                av = av * a_sh
            a_last = jnp.sum(jnp.where(row == last, av, 0.0), axis=0, keepdims=True)
            h_last = jnp.sum(jnp.where(row == last, hv, 0.0), axis=0, keepdims=True)
            return base, av, hv, a_last, h_last

        def step(j, carry):
            parts = []
            for k in range(SCAN_UNROLL):
                idx = j * SCAN_UNROLL + k
                parts.append(local((n8 - 1 - idx) if reverse else idx))
            for base, av, hv, a_last, h_last in parts:
                rows = pl.ds(pl.multiple_of(base, SUBLANES), SUBLANES)
                h_true = av * carry + hv
                h_ref[rows, :] = h_true
                if with_prev:
                    if reverse:
                        prev_ref[0][rows, :] = jnp.where(row < hi, pltpu.roll(h_true, hi, 0), carry)
                    else:
                        prev_ref[0][rows, :] = jnp.where(row >= 1, pltpu.roll(h_true, 1, 0), carry)
                carry = a_last * carry + h_last
            return carry

        lax.fori_loop(0, n8 // SCAN_UNROLL, step, jnp.zeros((1, LANES), F32))

    spec = pl.BlockSpec((t, LANES), lambda j: (0, j))
    n_out = 2 if with_prev else 1
    res = pl.pallas_call(
        body, name=name, grid=(c // LANES,),
        in_specs=[spec, pl.BlockSpec((us, LANES), lambda j: (0, j))],
        out_specs=[spec] * n_out, out_shape=[jax.ShapeDtypeStruct((t, c), F32)] * n_out,
        compiler_params=_params(("parallel",)),
    )(a, u)
    return res if with_prev else res[0]


NT_DIMS = (((1,), (1,)), ((), ()))


LOG2E = 1.4426950408889634
SCALE2 = MLA_SCALE * LOG2E
ATTN_TILES = (512, 256, 128)
KEY_CHUNKS = (768, 512, 256, 128)
QUERY_CHUNKS = (1024, 512, 256, 128)


def _attn_fwd(q, k, v, ride_arrs, ride_modes):
    h, s, _ = q.shape
    t = k.shape[1]
    tq = _pick(s, ATTN_TILES)
    ck = _pick(t, KEY_CHUNKS)
    n = len(ride_arrs)

    def body(*refs):
        q_ref, k_ref, v_ref = refs[:3]
        o_ref, lse_ref = refs[3 + n:5 + n]
        hh, i = pl.program_id(0), pl.program_id(1)
        finish = _riding_exchange(refs, 3, 2, n, ride_modes, (hh == 0) & (i == 0),
                                  (hh == h - 1) & (i == s // tq - 1))
        qv = q_ref[0]
        def scores(j):
            return lax.dot_general(qv, k_ref[0, j * ck:(j + 1) * ck, :], NT_DIMS, preferred_element_type=F32)

        m = l = acc = None
        s_next = scores(0)
        for j in range(t // ck):
            vj = v_ref[0, j * ck:(j + 1) * ck, :]
            s2, s_next = s_next, (scores(j + 1) if j + 1 < t // ck else None)
            mj = jnp.max(s2, axis=-1, keepdims=True)
            m_new = mj if j == 0 else jnp.maximum(m, mj)
            p = jnp.exp2(s2 - m_new)
            lj = jnp.sum(p, axis=-1, keepdims=True)
            pv = jnp.dot(p.astype(BF16), vj, preferred_element_type=F32)
            if j == 0:
                l, acc = lj, pv
            else:
                alpha = jnp.exp2(m - m_new)
                l, acc = alpha * l + lj, alpha * acc + pv
            m = m_new
        o_ref[0] = acc / l
        lse_ref[0] = _rows8(jnp.broadcast_to(m + jnp.log2(l), (tq, LANES)))
        finish()

    any_spec = pl.BlockSpec(memory_space=pl.ANY)
    res = pl.pallas_call(
        body, name="attn_fwd", grid=(h, s // tq),
        in_specs=[pl.BlockSpec((1, tq, HEAD_PAD), lambda hh, i: (hh, i, 0)),
                  pl.BlockSpec((1, t, HEAD_PAD), lambda hh, i: (hh, 0, 0)),
                  pl.BlockSpec((1, t, VDIM), lambda hh, i: (hh, 0, 0))] + [any_spec] * n,
        out_specs=[pl.BlockSpec((1, tq, VDIM), lambda hh, i: (hh, i, 0)),
                   pl.BlockSpec((1, SUBLANES, tq), lambda hh, i: (hh, 0, i))] + [any_spec] * n,
        out_shape=[jax.ShapeDtypeStruct((h, s, VDIM), F32), jax.ShapeDtypeStruct((h, SUBLANES, s), F32)]
        + _exchange_shapes(ride_arrs, ride_modes),
        scratch_shapes=_exchange_sems(n),
        compiler_params=pltpu.CompilerParams(dimension_semantics=("arbitrary", "arbitrary"),
                                             vmem_limit_bytes=VMEM_LIMIT, has_side_effects=True),
    )(q, k, v, *ride_arrs)
    return res[0], res[1], res[2:]


TN_DIMS = (((0,), (0,)), ((), ()))


def _rows8(x):
    return jnp.transpose(x)[:SUBLANES, :]


def _attn_delta(do_src, o_t, tm=ROW_TILE):
    s = o_t.shape[0]
    w = HEADS * VDIM

    def body(do_ref, o_ref, d_ref):
        prod = do_ref[...] * o_ref[...]
        for c in range(w // LANES):
            x = prod[:, c * LANES:(c + 1) * LANES]
            shift = VDIM // 2
            while shift:
                x = x + pltpu.roll(x, LANES - shift, 1)
                shift //= 2
            xt = jnp.transpose(x)
            d_ref[2 * c] = xt[:SUBLANES, :]
            d_ref[2 * c + 1] = xt[VDIM:VDIM + SUBLANES, :]

    return pl.pallas_call(
        body, name="attn_delta", grid=(s // tm,),
        in_specs=[pl.BlockSpec((tm, w), lambda i: (i, 1)), pl.BlockSpec((tm, w), lambda i: (i, 0))],
        out_specs=pl.BlockSpec((HEADS, SUBLANES, tm), lambda i: (0, 0, i)),
        out_shape=jax.ShapeDtypeStruct((HEADS, SUBLANES, s), F32),
        compiler_params=_params(("parallel",)),
    )(do_src, o_t)


def _attn_bwd(q, k, v, do, lse_row, delta_row, ride_arrs, ride_modes):
    h, s, _ = q.shape
    t = k.shape[1]
    tk = _pick(t, (768,) + ATTN_TILES)
    cq = _pick(s, QUERY_CHUNKS)
    n = len(ride_arrs)

    def body(*refs):
        q_ref, k_ref, v_ref, do_ref, lse_ref, delta_ref = refs[:6]
        dq_ref, dk_ref, dv_ref = refs[6 + n:9 + n]
        hh, i = pl.program_id(0), pl.program_id(1)
        finish = _riding_exchange(refs, 6, 3, n, ride_modes, (hh == 0) & (i == 0),
                                  (hh == h - 1) & (i == t // tk - 1))

        @pl.when(i == 0)
        def _():
            dq_ref[...] = jnp.zeros_like(dq_ref)

        kt, vt = k_ref[0], v_ref[0]
        dk = dv = None
        for j in range(s // cq):
            rows = slice(j * cq, (j + 1) * cq)
            qj, doj = q_ref[0, rows, :], do_ref[0, rows, :]
            pt = jnp.exp2(lax.dot_general(kt, qj, NT_DIMS, preferred_element_type=F32) - lse_ref[0, 0:1, rows])
            dv_j = jnp.dot(pt.astype(BF16), doj, preferred_element_type=F32)
            dpt = lax.dot_general(vt, doj, NT_DIMS, preferred_element_type=F32)
            dst = (pt * (dpt - delta_ref[0, 0:1, rows])).astype(BF16)
            dk_j = jnp.dot(dst, qj, preferred_element_type=F32)
            dq_ref[0, rows, :] += lax.dot_general(dst, kt, TN_DIMS, preferred_element_type=F32)
            dk, dv = (dk_j, dv_j) if j == 0 else (dk + dk_j, dv + dv_j)
        dk_ref[0] = dk * (1.0 / LOG2E)
        dv_ref[0] = dv
        finish()

    any_spec = pl.BlockSpec(memory_space=pl.ANY)
    res = pl.pallas_call(
        body, name="attn_bwd", grid=(h, t // tk),
        in_specs=[pl.BlockSpec((1, s, HEAD_PAD), lambda hh, i: (hh, 0, 0)),
                  pl.BlockSpec((1, tk, HEAD_PAD), lambda hh, i: (hh, i, 0)),
                  pl.BlockSpec((1, tk, VDIM), lambda hh, i: (hh, i, 0)),
                  pl.BlockSpec((1, s, VDIM), lambda hh, i: (hh, 0, 0)),
                  pl.BlockSpec((1, SUBLANES, s), lambda hh, i: (hh, 0, 0)),
                  pl.BlockSpec((1, SUBLANES, s), lambda hh, i: (hh, 0, 0))] + [any_spec] * n,
        out_specs=[pl.BlockSpec((1, s, HEAD_PAD), lambda hh, i: (hh, 0, 0)),
                   pl.BlockSpec((1, tk, HEAD_PAD), lambda hh, i: (hh, i, 0)),
                   pl.BlockSpec((1, tk, VDIM), lambda hh, i: (hh, i, 0))] + [any_spec] * n,
        out_shape=[jax.ShapeDtypeStruct((h, s, HEAD_PAD), F32), jax.ShapeDtypeStruct((h, t, HEAD_PAD), F32),
                   jax.ShapeDtypeStruct((h, t, VDIM), F32)] + _exchange_shapes(ride_arrs, ride_modes),
        scratch_shapes=_exchange_sems(n),
        compiler_params=pltpu.CompilerParams(dimension_semantics=("arbitrary", "arbitrary"),
                                             vmem_limit_bytes=VMEM_LIMIT, has_side_effects=True),
    )(q, k, v, do, lse_row, delta_row, *ride_arrs)
    return res[0], res[1], res[2], res[3:]


def _exchange_shapes(arrs, modes):
    return [jax.ShapeDtypeStruct((N_DEV,) + a.shape if md == 'ag' else a.shape, a.dtype) for a, md in zip(arrs, modes)]


def _exchange_sems(n):
    return [pltpu.SemaphoreType.DMA((n, N_DEV - 1)), pltpu.SemaphoreType.DMA((n, N_DEV - 1)),
            pltpu.SemaphoreType.DMA((n,))]


def _exchange_copies(ins, outs, modes, send_sems, recv_sems, local_sems):
    x, y, c = lax.axis_index("x"), lax.axis_index("y"), lax.axis_index("c")
    me = 4 * x + 2 * y + c
    copies = []
    for a in range(len(ins)):
        ag = modes[a] == 'ag'
        copies.append(pltpu.make_async_copy(ins[a] if ag else ins[a].at[me], outs[a].at[me], local_sems.at[a]))
        for k in range(1, N_DEV):
            px = 1 - x if k & 4 else x
            py = 1 - y if k & 2 else y
            pc = 1 - c if k & 1 else c
            src = ins[a] if ag else ins[a].at[4 * px + 2 * py + pc]
            copies.append(pltpu.make_async_remote_copy(
                src_ref=src, dst_ref=outs[a].at[me], send_sem=send_sems.at[a, k - 1],
                recv_sem=recv_sems.at[a, k - 1], device_id=(px, py, pc), device_id_type=MESH))
    return copies


def _exchange(name, arrs, modes):
    n = len(arrs)

    def body(*refs):
        copies = _exchange_copies(refs[:n], refs[n:2 * n], modes, *refs[2 * n:])
        for cp in copies:
            cp.start()
        for cp in copies:
            cp.wait()

    return pl.pallas_call(
        body, name=name,
        in_specs=[pl.BlockSpec(memory_space=pl.ANY)] * n,
        out_specs=[pl.BlockSpec(memory_space=pl.ANY)] * n,
        out_shape=_exchange_shapes(arrs, modes),
        scratch_shapes=_exchange_sems(n),
        compiler_params=pltpu.CompilerParams(has_side_effects=True),
    )(*arrs)


def _riding_exchange(refs, n_in, n_out, n, modes, first, last):
    ins = refs[n_in:n_in + n]
    outs = refs[n_in + n + n_out:n_in + 2 * n + n_out]
    sems = refs[n_in + 2 * n + n_out:n_in + 2 * n + n_out + 3]

    @pl.when(first)
    def _():
        for cp in _exchange_copies(ins, outs, modes, *sems):
            cp.start()

    def finish():
        @pl.when(last)
        def _():
            for cp in _exchange_copies(ins, outs, modes, *sems):
                cp.wait()

    return finish


def _adamw(name, w, m, v, gparts):
    r, c = w.shape
    npart = gparts.shape[0]
    tr = _pick(r, (256, 128, 64, 32, 16, 8))
    spec = pl.BlockSpec((tr, c), lambda i: (i, 0))

    def body(w_ref, m_ref, v_ref, g_ref, go_ref, d_ref, mo_ref, vo_ref):
        g = g_ref[0]
        for p in range(1, npart):
            g = g + g_ref[p]
        m1 = ADAM_B1 * m_ref[...] + (1.0 - ADAM_B1) * g
        v1 = ADAM_B2 * v_ref[...] + (1.0 - ADAM_B2) * (g * g)
        m_hat = m1 / (1.0 - ADAM_B1 ** ADAM_STEP)
        v_hat = v1 / (1.0 - ADAM_B2 ** ADAM_STEP)
        go_ref[...] = g
        d_ref[...] = -ADAM_LR * (m_hat / (jnp.sqrt(v_hat) + ADAM_EPS) + ADAM_WD * w_ref[...])
        mo_ref[...] = m1
        vo_ref[...] = v1

    return pl.pallas_call(
        body, name=name, grid=(r // tr,),
        in_specs=[spec, spec, spec, pl.BlockSpec((npart, tr, c), lambda i: (0, i, 0))],
        out_specs=[spec] * 4, out_shape=[jax.ShapeDtypeStruct((r, c), F32)] * 4,
        compiler_params=_params(("parallel",)),
    )(w, m, v, gparts)


def _pack(arrs, rows):
    flat = jnp.concatenate([a.reshape(-1) for a in arrs])
    return jnp.pad(flat, (0, rows * LANES - flat.shape[0])).reshape(rows, LANES)


def _unpack(packed, shapes):
    flat, out, off = packed.reshape(-1), [], 0
    for s in shapes:
        n = math.prod(s)
        out.append(flat[off:off + n].reshape(s))
        off += n
    return out


def _pack_rows(n_elems):
    return -(-n_elems // (SUBLANES * LANES)) * SUBLANES


def _cols_from_shards(g):
    return g.transpose(1, 0, 2).reshape(g.shape[1], N_DEV * g.shape[2])


def _cols_to_shards(w):
    r, c = w.shape
    return w.reshape(r, N_DEV, c // N_DEV).transpose(1, 0, 2)


def _rope_tables(n_lat, n_ctx):
    inv = ROPE_BASE ** (-jnp.arange(ROPE_PAIRS, dtype=F32) / ROPE_PAIRS)
    seg = [(l - NOPE) // ROPE_PAIRS if NOPE <= l < QK else -1 for l in range(HEAD_PAD)]
    freq = jnp.take(inv, jnp.array([(l - NOPE) % ROPE_PAIRS if NOPE <= l < QK else 0 for l in range(HEAD_PAD)]))
    seg = jnp.array(seg)
    f_row = jnp.where((seg == 0) | (seg == 1), freq, 0.0)[None, :]
    f_col = jnp.where((seg == 2) | (seg == 3), freq, 0.0)[None, :]
    t = jnp.arange(n_ctx + n_lat) - n_ctx
    pos = jnp.maximum(t, 0)
    row = jnp.where(t >= 0, pos // GRID_W, 0).astype(F32)[:, None]
    col = jnp.where(t >= 0, pos % GRID_W, 0).astype(F32)[:, None]
    ang = row * f_row + col * f_col
    sin = jnp.sin(ang)
    sin_up = jnp.where(((seg == 0) | (seg == 2))[None, :], -sin, 0.0)
    sin_dn = jnp.where(((seg == 1) | (seg == 3))[None, :], sin, 0.0)
    return jnp.cos(ang), sin_up, sin_dn


def _rope(x, cos_t, sin_up, sin_dn):
    return x * cos_t + pltpu.roll(x, HEAD_PAD - ROPE_PAIRS, 1) * sin_up + pltpu.roll(x, ROPE_PAIRS, 1) * sin_dn


def _rope_t(dy, cos_t, sin_up, sin_dn):
    return (dy * cos_t + pltpu.roll(dy * sin_up, ROPE_PAIRS, 1)
            + pltpu.roll(dy * sin_dn, HEAD_PAD - ROPE_PAIRS, 1))


def _softplus(x):
    return jnp.maximum(x, 0.0) + jnp.log(1.0 + jnp.exp(-jnp.abs(x)))


def _gates(z, xcv, lam_sp):
    outs = []
    for d in range(2):
        r = jax.nn.sigmoid(z[:, (2 * d) * LRU_W:(2 * d + 1) * LRU_W])
        ig = jax.nn.sigmoid(z[:, (2 * d + 1) * LRU_W:(2 * d + 2) * LRU_W])
        log_a = -LRU_C * r * lam_sp[:, d * LRU_W:(d + 1) * LRU_W]
        a = jnp.exp(log_a)
        u = jnp.sqrt(-jnp.tanh(log_a) * (a * a + 1.0)) * (ig * xcv)
        outs += [a, u]
    return tuple(outs)


def kernel(x, c, ctx, c_ctx, w_mod, b_mod, g_pre_mix, g_post_mix, g_pre_ffn, g_post_ffn, w_in, lru_conv_w, lru_conv_b, lru_w_a, lru_b_a, lru_w_x, lru_b_x, lru_lambda, mla_g_q, mla_w_uq, mla_g_kv, mla_w_ukv, w_out, ffn_w_up, ffn_conv_w, ffn_conv_b, ffn_w_down, loss_target, m_c_ctx, m_w_mod, m_b_mod, m_g_pre_mix, m_g_post_mix, m_g_pre_ffn, m_g_post_ffn, m_w_in, m_lru_conv_w, m_lru_conv_b, m_lru_w_a, m_lru_b_a, m_lru_w_x, m_lru_b_x, m_lru_lambda, m_mla_g_q, m_mla_w_uq, m_mla_g_kv, m_mla_w_ukv, m_w_out, m_ffn_w_up, m_ffn_conv_w, m_ffn_conv_b, m_ffn_w_down, v_c_ctx, v_w_mod, v_b_mod, v_g_pre_mix, v_g_post_mix, v_g_pre_ffn, v_g_post_ffn, v_w_in, v_lru_conv_w, v_lru_conv_b, v_lru_w_a, v_lru_b_a, v_lru_w_x, v_lru_b_x, v_lru_lambda, v_mla_g_q, v_mla_w_uq, v_mla_g_kv, v_mla_w_ukv, v_w_out, v_ffn_w_up, v_ffn_conv_w, v_ffn_conv_b, v_ffn_w_down):
    W = dict(c_ctx=c_ctx, w_mod=w_mod, b_mod=b_mod, g_pre_mix=g_pre_mix, g_post_mix=g_post_mix, g_pre_ffn=g_pre_ffn,
             g_post_ffn=g_post_ffn, w_in=w_in, lru_conv_w=lru_conv_w, lru_conv_b=lru_conv_b, lru_w_a=lru_w_a,
             lru_b_a=lru_b_a, lru_w_x=lru_w_x, lru_b_x=lru_b_x, lru_lambda=lru_lambda, mla_g_q=mla_g_q,
             mla_w_uq=mla_w_uq, mla_g_kv=mla_g_kv, mla_w_ukv=mla_w_ukv, w_out=w_out, ffn_w_up=ffn_w_up,
             ffn_conv_w=ffn_conv_w, ffn_conv_b=ffn_conv_b, ffn_w_down=ffn_w_down)
    M = dict(c_ctx=m_c_ctx, w_mod=m_w_mod, b_mod=m_b_mod, g_pre_mix=m_g_pre_mix, g_post_mix=m_g_post_mix,
             g_pre_ffn=m_g_pre_ffn, g_post_ffn=m_g_post_ffn, w_in=m_w_in, lru_conv_w=m_lru_conv_w,
             lru_conv_b=m_lru_conv_b, lru_w_a=m_lru_w_a, lru_b_a=m_lru_b_a, lru_w_x=m_lru_w_x, lru_b_x=m_lru_b_x,
             lru_lambda=m_lru_lambda, mla_g_q=m_mla_g_q, mla_w_uq=m_mla_w_uq, mla_g_kv=m_mla_g_kv,
             mla_w_ukv=m_mla_w_ukv, w_out=m_w_out, ffn_w_up=m_ffn_w_up, ffn_conv_w=m_ffn_conv_w,
             ffn_conv_b=m_ffn_conv_b, ffn_w_down=m_ffn_w_down)
    V = dict(c_ctx=v_c_ctx, w_mod=v_w_mod, b_mod=v_b_mod, g_pre_mix=v_g_pre_mix, g_post_mix=v_g_post_mix,
             g_pre_ffn=v_g_pre_ffn, g_post_ffn=v_g_post_ffn, w_in=v_w_in, lru_conv_w=v_lru_conv_w,
             lru_conv_b=v_lru_conv_b, lru_w_a=v_lru_w_a, lru_b_a=v_lru_b_a, lru_w_x=v_lru_w_x, lru_b_x=v_lru_b_x,
             lru_lambda=v_lru_lambda, mla_g_q=v_mla_g_q, mla_w_uq=v_mla_w_uq, mla_g_kv=v_mla_g_kv,
             mla_w_ukv=v_mla_w_ukv, w_out=v_w_out, ffn_w_up=v_ffn_w_up, ffn_conv_w=v_ffn_conv_w,
             ffn_conv_b=v_ffn_conv_b, ffn_w_down=v_ffn_w_down)

    D = D_MODEL
    S, CN = x.shape[1], ctx.shape[1]
    T = S + CN
    TM = ROW_TILE
    ct, ns, nt = CN // TM, S // TM, T // TM
    me = 4 * lax.axis_index("x") + 2 * lax.axis_index("y") + lax.axis_index("c")

    lat = lambda i: i + ct
    swp = lambda i: jnp.where(i < ct, i + ns, i - ct)
    lat_or_0 = lambda i: jnp.maximum(i - ct, 0)

    small_shapes = [W[n].shape[1:] for n in SMALL_SHARDED] + [(D,)]
    n_small = sum(math.prod(s) for s in small_shapes)
    small_rows = _pack_rows(n_small)
    small_loc = _pack([W[n][0] for n in SMALL_SHARDED] + [c[0]], small_rows)
    early = ['w_in', 'mla_w_uq', 'mla_w_ukv']
    late = ['w_out', 'ffn_w_up', 'ffn_w_down']
    big = early + late
    gathered = _exchange("gather_weights", [W[n][0].astype(BF16) for n in early] + [small_loc], ['ag'] * 4)
    gw = dict(zip(early, gathered[:3]))
    small_all = [_unpack(gathered[3][d], small_shapes) for d in range(N_DEV)]
    full_small = {n: jnp.concatenate([small_all[d][j] for d in range(N_DEV)], axis=-1)
                  for j, n in enumerate(SMALL_SHARDED)}
    c_all = jnp.stack([small_all[d][-1] for d in range(N_DEV)])

    w_in_f = _cols_from_shards(gw['w_in'])
    w_in_p = jnp.concatenate([w_in_f[:, :OFF_KR], jnp.zeros((D, NOPE), BF16), w_in_f[:, OFF_KR:],
                              jnp.zeros((D, HEAD_PAD - QK), BF16)], axis=1)
    w_uq_f = _cols_from_shards(gw['mla_w_uq']).reshape(Q_RANK, HEADS, QK)
    wq_p = jnp.pad(w_uq_f, ((0, 0), (0, 0), (0, HEAD_PAD - QK))).reshape(Q_RANK, HEADS * HEAD_PAD)
    w_ukv_f = _cols_from_shards(gw['mla_w_ukv']).reshape(KV_RANK, HEADS, NOPE + VDIM)
    wk_p = jnp.pad(w_ukv_f[:, :, :NOPE], ((0, 0), (0, 0), (0, HEAD_PAD - NOPE))).reshape(KV_RANK, HEADS * HEAD_PAD)
    wv_f = w_ukv_f[:, :, NOPE:].reshape(KV_RANK, HEADS * VDIM)

    lru_cw, lru_ba, lru_bx, lru_lam, ffn_cw = [full_small[n] for n in SMALL_SHARDED]
    ffn_cw_t, ffn_cb_t = _ff_to_tiles(ffn_cw), _ff_to_tiles(ffn_conv_b)

    def block_diag(w):
        eye = jnp.eye(LRU_HEADS, dtype=w.dtype)
        return jnp.einsum('hij,hg->higj', w, eye).reshape(LRU_W, LRU_W)

    w_gate = jnp.concatenate([block_diag(lru_w_a[0, 0]), block_diag(lru_w_x[0, 0]),
                              block_diag(lru_w_a[0, 1]), block_diag(lru_w_x[0, 1])], axis=1).astype(BF16)
    b_gate = jnp.concatenate([lru_ba[0], lru_bx[0], lru_ba[1], lru_bx[1]])[None]
    lam_row = lru_lam.reshape(1, 2 * LRU_W)

    c16 = jnp.concatenate([c_all, c_ctx[None], jnp.zeros((2 * SUBLANES - N_DEV - 1, D), F32)])
    ncol = w_mod.shape[2]
    b_mod_loc = lax.dynamic_slice(b_mod, (0, me * ncol), (1, ncol))

    def mod_fwd(c16_r, w_r, b_r):
        c16_v = c16_r[...]
        sl = c16_v * jax.nn.sigmoid(c16_v)
        return (jnp.dot(sl.astype(BF16), w_r[...].astype(BF16), preferred_element_type=F32) + b_r[...],)

    (mod_part,) = _single("mod_fwd", mod_fwd, [c16, w_mod[0], b_mod_loc], [((2 * SUBLANES, ncol), F32)])
    (mod_g,) = _exchange("gather_mod", [mod_part], ['ag'])
    mod_all = _cols_from_shards(mod_g)
    mod_lat = lax.dynamic_slice(mod_all, (me, 0), (1, N_MOD * D)).reshape(N_MOD, D)
    mod_ctx = mod_all[N_DEV].reshape(N_MOD, D)

    xs, tgt = x[0], loss_target[0]
    xa_rows = [_row(ctx[0], lambda i: jnp.minimum(i, ct - 1)), _row(xs, lat_or_0)]

    def sel_mod(i, ml, mc, r0):
        sh = jnp.where(i < ct, mc[r0:r0 + 1, :], ml[r0:r0 + 1, :])
        sc = jnp.where(i < ct, mc[r0 + 1:r0 + 2, :], ml[r0 + 1:r0 + 2, :])
        return sh, sc

    def pre_fn(xv, g, sh, sc):
        return _rms(xv, g) * (1.0 + sc) + sh

    def k_pre(i, rv, bv):
        sh, sc = sel_mod(i, bv[1], bv[2], 0)
        hv = pre_fn(jnp.where(i < ct, rv[0], rv[1]), bv[0][...], sh, sc).astype(BF16)
        return (hv, jnp.dot(hv, bv[3][...], preferred_element_type=F32)), ()

    h_pre, proj = _rowwise("pre_mix", k_pre, nt, xa_rows, [g_pre_mix, mod_lat, mod_ctx, w_in_p],
                           [((T, D), BF16, None), ((T, IN_W_PAD), F32, None)])

    xcv = _dwconv_fwd("lru_conv", proj, lru_cw, lru_conv_b, LRU_CONV_LEFT, (0, ct))

    def k_gates(i, rv, bv):
        xv = rv[0]
        z = jnp.dot(xv.astype(BF16), bv[0][...], preferred_element_type=F32) + bv[1][...]
        return _gates(z, xv, _bc(_softplus(-bv[2][...]), TM)), ()

    a0, u0, a1, u1 = _rowwise("lru_gates", k_gates, nt, [_row(xcv)], [w_gate, b_gate, lam_row],
                              [((T, LRU_W), F32, None), ((T, LRU_W), F32, None),
                               ((T, LRU_W), F32, swp), ((T, LRU_W), F32, swp)])
    h0, hprev0 = _scan("lru_scan_f", a0, u0, False, with_prev=True)
    h1, hprev1 = _scan("lru_scan_r", a1, u1, True, with_prev=True)

    cos_k, sup_k, sdn_k = _rope_tables(S, CN)
    q_tables = [_row(cos_k, lat), _row(sup_k, lat), _row(sdn_k, lat)]
    cq_row = _row(proj, lat, (Q_RANK, OFF_CQ // Q_RANK))
    ckv_row = _row(proj, None, (KV_RANK, OFF_CKV // KV_RANK))

    def heads_of(xl):
        return [xl[:, hh * HEAD_PAD:(hh + 1) * HEAD_PAD] for hh in range(HEADS)]

    def k_q_path(i, rv, bv):
        cqv, ctb, sub, sdb = rv
        cqn_v = _rms(cqv, bv[0][...]).astype(BF16)
        ql = jnp.dot(cqn_v, bv[1][...], preferred_element_type=F32)
        return (cqn_v, jnp.stack([_rope(qh, ctb, sub, sdb) * SCALE2 for qh in heads_of(ql)])), ()

    cqn, q = _rowwise("q_path", k_q_path, ns, [cq_row] + q_tables, [mla_g_q, wq_p],
                      [((S, Q_RANK), BF16, None), ((HEADS, S, HEAD_PAD), BF16, None)])

    def k_kv_path(i, rv, bv):
        ckv_v, krp, ctb, sub, sdb = rv
        ckvn_v = _rms(ckv_v, bv[0][...]).astype(BF16)
        kl = jnp.dot(ckvn_v, bv[1][...], preferred_element_type=F32)
        vl = jnp.dot(ckvn_v, bv[2][...], preferred_element_type=F32)
        kr = _rope(krp, ctb, sub, sdb)
        return (ckvn_v, jnp.stack([kh + kr for kh in heads_of(kl)]), vl), ()

    ckvn, k, v_lin = _rowwise(
        "kv_path", k_kv_path, nt,
        [ckv_row, _row(proj, None, (HEAD_PAD, OFF_KR // HEAD_PAD)), _row(cos_k), _row(sup_k), _row(sdn_k)],
        [mla_g_kv, wk_p, wv_f],
        [((T, KV_RANK), BF16, None), ((HEADS, T, HEAD_PAD), BF16, None), ((T, HEADS * VDIM), BF16, None)])
    v = v_lin.reshape(T, HEADS, VDIM).transpose(1, 0, 2)

    o, lse, late_g = _attn_fwd(q, k, v, [W[n][0].astype(BF16) for n in late], ['ag'] * 3)
    w_out_f = late_g[0].reshape(D, D)
    w_up_t = _ff_permute("w_up_to_tiles", _cols_from_shards(late_g[1]), True)
    w_down_f = late_g[2].reshape(D_FF, D)
    o_t = o.transpose(1, 0, 2).reshape(S, HEADS * VDIM)

    def lru_out_fn(hf, hr, gr):
        return (hf + hr) * jax.nn.gelu(gr)

    def post_mix_fn(xv, yv, gt, g):
        return xv + gt * _rms(yv, g)

    def k_mix_out(i, rv, bv):
        ml = bv[0]
        y_in_v = jnp.concatenate([lru_out_fn(rv[0], rv[1], rv[2]), rv[3]], axis=1).astype(BF16)
        yv = jnp.dot(y_in_v, bv[3][...], preferred_element_type=F32)
        x1v = post_mix_fn(rv[4], yv, ml[2:3, :], bv[1][...])
        return (y_in_v, yv, x1v, pre_fn(x1v, bv[2][...], ml[3:4, :], ml[4:5, :])), ()

    gr_row = _row(proj, lat, (LRU_W, OFF_GR // LRU_W))
    y_in, y, x1, h2 = _rowwise(
        "mix_out", k_mix_out, ns, [_row(h0, lat), _row(h1), gr_row, _row(o_t), _row(xs)],
        [mod_lat, g_post_mix, g_pre_ffn, w_out_f],
        [((S, D), BF16, None), ((S, D), F32, None), ((S, D), F32, None), ((S, D), BF16, None)])
    up = _mm("ffn_up", h2, w_up_t, 'nn', out_dtype=BF16)
    act = _ffn_mid_fwd(up, ffn_cw_t, ffn_cb_t)
    f = _mm("ffn_down", act, w_down_f, 'nn')

    def loss_fn(x1v, fv, gt, g, tg):
        x2 = x1v + gt * _rms(fv, g)
        err = x2 - tg
        return 0.5 * jnp.sum(jnp.mean(err * err, axis=-1))

    def k_loss(i, rv, bv):
        gtb, gb = _bc(bv[0][5:6, :], TM), _bc(bv[1][...], TM)
        val, (dx1v, dfv, dgt, dg) = jax.value_and_grad(loss_fn, argnums=(0, 1, 2, 3))(rv[0], rv[1], gtb, gb, rv[2])
        return (dx1v, dfv), (jnp.full((1, LANES), val, F32), _rs(dgt), _rs(dg))

    dx1_a, df, loss_acc, d_gt2, d_g_post_ffn = _rowwise(
        "loss_bwd", k_loss, ns, [_row(x1), _row(f), _row(tgt)], [mod_lat, g_post_ffn],
        [((S, D), F32, None), ((S, D), BF16, None)], [(1, LANES), (1, D), (1, D)])

    d_act = _mm("ffn_down_dx", df, w_down_f, 'nt', out_dtype=BF16)
    d_w_down = _mm("ffn_down_dw", act, df, 'tn')
    d_up, d_ffn_cw_t, d_ffn_cb_t = _ffn_mid_bwd(up, d_act, ffn_cw_t, ffn_cb_t)
    d_ffn_cw, d_ffn_cb = _ff_from_tiles(d_ffn_cw_t), _ff_from_tiles(d_ffn_cb_t)
    d_h2 = _mm("ffn_up_dx", d_up, w_up_t, 'nt')
    d_w_up = _ff_permute("d_w_up_from_tiles", _mm("ffn_up_dw", h2, d_up, 'tn'), False)

    def k_pre_ffn_bwd(i, rv, bv):
        ml = bv[0]
        gb, shb, scb = _bc(bv[1][...], TM), _bc(ml[3:4, :], TM), _bc(ml[4:5, :], TM)
        _, pull = jax.vjp(pre_fn, rv[0], gb, shb, scb)
        dxv, dg, dsh, dsc = pull(rv[1])
        return (rv[2] + dxv,), (_rs(dg), _rs(dsh), _rs(dsc))

    dx1, d_g_pre_ffn, d_sh2, d_sc2 = _rowwise(
        "pre_ffn_bwd", k_pre_ffn_bwd, ns, [_row(x1), _row(d_h2), _row(dx1_a)], [mod_lat, g_pre_ffn],
        [((S, D), F32, None)], [(1, D), (1, D), (1, D)])

    def k_post_mix_bwd(i, rv, bv):
        gtb, gb = _bc(bv[0][2:3, :], TM), _bc(bv[1][...], TM)
        _, pull = jax.vjp(post_mix_fn, rv[0], rv[1], gtb, gb)
        _, dyv, dgt, dg = pull(rv[2])
        dyb = dyv.astype(BF16)
        return (dyb, lax.dot_general(dyb, bv[2][...], NT_DIMS, preferred_element_type=F32)), (_rs(dgt), _rs(dg))

    dy, d_y_in, d_gt1, d_g_post_mix = _rowwise(
        "post_mix_bwd", k_post_mix_bwd, ns, [_row(xs), _row(y), _row(dx1)], [mod_lat, g_post_mix, w_out_f],
        [((S, D), BF16, None), ((S, D), F32, None)], [(1, D), (1, D)])
    d_w_out = _mm("out_proj_dw", y_in, dy, 'tn')

    def k_lru_out_bwd(i, rv, bv):
        _, pull = jax.vjp(lru_out_fn, rv[0], rv[1], rv[2])
        dhf, _, dgr = pull(rv[3])
        return (dhf, dgr), ()

    d_hsum, d_gr = _rowwise("lru_out_bwd", k_lru_out_bwd, ns,
                            [_row(h0, lat), _row(h1), gr_row, _row(d_y_in, None, (LRU_W, 0))], [],
                            [((S, LRU_W), F32, None), ((S, LRU_W), F32, None)])

    do_b = d_y_in[:, LRU_W:].astype(BF16).reshape(S, HEADS, VDIM).transpose(1, 0, 2)
    delta = _attn_delta(d_y_in, o_t)
    late_sends = [d_w_out.reshape(N_DEV, D // N_DEV, D), _cols_to_shards(d_w_up),
                  d_w_down.reshape(N_DEV, D_FF // N_DEV, D)]
    dq, dk, dv, late_recv = _attn_bwd(q, k, v, do_b, lse, delta, late_sends, ['a2a'] * 3)

    def rms_bwd(xv, g, dy):
        _, pull = jax.vjp(_rms, xv, _bc(g, TM))
        dxv, dg = pull(dy)
        return dxv, _rs(dg)

    def k_q_path_bwd(i, rv, bv):
        dqv, ctb, sub, sdb, cqv = rv
        dql = jnp.concatenate([_rope_t(dqv[hh] * MLA_SCALE, ctb, sub, sdb) for hh in range(HEADS)], axis=1).astype(BF16)
        d_cqn_v = lax.dot_general(dql, bv[1][...], NT_DIMS, preferred_element_type=F32)
        dxv, dg = rms_bwd(cqv, bv[0][...], d_cqn_v)
        return (dql, dxv), (dg,)

    dq_lin, d_cq, d_g_q = _rowwise(
        "q_path_bwd", k_q_path_bwd, ns, [_row(dq)] + q_tables + [cq_row], [mla_g_q, wq_p],
        [((S, HEADS * HEAD_PAD), BF16, None), ((S, Q_RANK), F32, None)], [(1, Q_RANK)])
    d_wq_p = _mm("q_proj_dw", cqn, dq_lin, 'tn')

    dv_lin = dv.transpose(1, 0, 2).reshape(T, HEADS * VDIM).astype(BF16)

    def k_kv_path_bwd(i, rv, bv):
        dkv_, ctb, sub, sdb, dvl, ckv_v = rv
        tot = dkv_[0]
        for hh in range(1, HEADS):
            tot = tot + dkv_[hh]
        lane = lax.broadcasted_iota(jnp.int32, tot.shape, 1)
        tot = jnp.where((lane >= NOPE) & (lane < QK), tot, 0.0)
        dkl = jnp.concatenate([dkv_[hh] for hh in range(HEADS)], axis=1).astype(BF16)
        d_ckvn_v = (lax.dot_general(dkl, bv[1][...], NT_DIMS, preferred_element_type=F32)
                    + lax.dot_general(dvl, bv[2][...], NT_DIMS, preferred_element_type=F32))
        dxv, dg = rms_bwd(ckv_v, bv[0][...], d_ckvn_v)
        return (dkl, _rope_t(tot, ctb, sub, sdb), dxv), (dg,)

    dk_lin, d_krp, d_ckv, d_g_kv = _rowwise(
        "kv_path_bwd", k_kv_path_bwd, nt,
        [_row(dk), _row(cos_k), _row(sup_k), _row(sdn_k), _row(dv_lin), ckv_row], [mla_g_kv, wk_p, wv_f],
        [((T, HEADS * HEAD_PAD), BF16, None), ((T, HEAD_PAD), F32, None), ((T, KV_RANK), F32, None)], [(1, KV_RANK)])
    d_wk_p = _mm("k_proj_dw", ckvn, dk_lin, 'tn')
    d_wv = _mm("v_proj_dw", ckvn, dv_lin, 'tn')

    lam0 = _scan("lru_scan_f_bwd", a0, d_hsum, True, shifted=True, u_off=CN)
    lam1 = _scan("lru_scan_r_bwd", a1, d_hsum, False, shifted=True, u_off=0)

    def k_gates_bwd(i, rv, bv):
        xv, l0, hp0, l1, hp1 = rv
        wg, bg, lamv = [b[...] for b in bv]
        xb = xv.astype(BF16)
        z = jnp.dot(xb, wg, preferred_element_type=F32) + bg
        spb = _bc(_softplus(-lamv), TM)
        _, pull = jax.vjp(_gates, z, xv, spb)
        dz, dxv, dsp = pull((l0 * hp0, l0, l1 * hp1, l1))
        dzb = dz.astype(BF16)
        dxv = dxv + lax.dot_general(dzb, wg, NT_DIMS, preferred_element_type=F32)
        dwg = lax.dot_general(xb, dzb, (((0,), (0,)), ((), ())), preferred_element_type=F32)
        dlam = -_rs(dsp) * jax.nn.sigmoid(-lamv)
        return (dxv,), (dwg, _rs(dz), dlam)

    d_xcv, d_w_gate, d_b_gate, d_lam = _rowwise(
        "lru_gates_bwd", k_gates_bwd, nt,
        [_row(xcv), _row(lam0), _row(hprev0), _row(lam1, swp), _row(hprev1, swp)], [w_gate, b_gate, lam_row],
        [((T, LRU_W), F32, None)], [(LRU_W, 4 * LRU_W), (1, 4 * LRU_W), (1, 2 * LRU_W)])
    d_xr, d_lru_cw, d_lru_cb = _dwconv_bwd("lru_conv_bwd", proj, d_xcv, lru_cw, LRU_CONV_LEFT, (0, ct))

    def k_dproj(i, rv, bv):
        is_lat = i >= ct
        return (jnp.concatenate([rv[0], jnp.where(is_lat, rv[1], 0.0), jnp.where(is_lat, rv[2], 0.0), rv[3], rv[4]],
                                axis=1),), ()

    d_proj = _rowwise("d_proj", k_dproj, nt,
                      [_row(d_xr), _row(d_gr, lat_or_0), _row(d_cq, lat_or_0), _row(d_ckv), _row(d_krp)], [],
                      [((T, IN_W_PAD), BF16, None)])[0]
    d_w_in_p = _mm("in_proj_dw", h_pre, d_proj, 'tn')

    d_b_a = jnp.stack([d_b_gate[0, 0:LRU_W], d_b_gate[0, 2 * LRU_W:3 * LRU_W]])
    d_b_x = jnp.stack([d_b_gate[0, LRU_W:2 * LRU_W], d_b_gate[0, 3 * LRU_W:]])
    d_w_in = jnp.concatenate([d_w_in_p[:, :OFF_KR], d_w_in_p[:, OFF_KR + NOPE:OFF_KR + QK]], axis=1)
    d_w_uq = d_wq_p.reshape(Q_RANK, HEADS, HEAD_PAD)[:, :, :QK].reshape(Q_RANK, HEADS * QK)
    d_w_ukv = jnp.concatenate([d_wk_p.reshape(KV_RANK, HEADS, HEAD_PAD)[:, :, :NOPE],
                               d_wv.reshape(KV_RANK, HEADS, VDIM)], axis=2).reshape(KV_RANK, HEADS * (NOPE + VDIM))
    small_full = dict(lru_conv_w=d_lru_cw, lru_b_a=d_b_a, lru_b_x=d_b_x, lru_lambda=d_lam.reshape(2, LRU_W),
                      ffn_conv_w=d_ffn_cw)
    small_sh = jnp.concatenate([_cols_to_shards(small_full[n]).reshape(N_DEV, -1) for n in SMALL_SHARDED], axis=1)
    n_sh = small_sh.shape[1]
    sh_rows = _pack_rows(n_sh)
    small_sh = jnp.pad(small_sh, ((0, 0), (0, sh_rows * LANES - n_sh))).reshape(N_DEV, sh_rows, LANES)
    early_sends = [_cols_to_shards(d_w_in), _cols_to_shards(d_w_uq), _cols_to_shards(d_w_ukv), small_sh]

    def k_pre_bwd(i, rv, bv):
        g, ml, mc = bv[0][...], bv[1], bv[2]
        sh, sc = sel_mod(i, ml, mc, 0)
        d_h = lax.dot_general(rv[2], bv[3][...], NT_DIMS, preferred_element_type=F32)
        _, pull = jax.vjp(pre_fn, jnp.where(i < ct, rv[0], rv[1]), _bc(g, TM), _bc(sh, TM), _bc(sc, TM))
        dxv, dg, dsh, dsc = pull(d_h)
        is_lat = i >= ct
        dsh, dsc = _rs(dsh), _rs(dsc)
        zero = jnp.zeros_like(dsh)
        return ((dxv + rv[3],),
                (_rs(dg), jnp.where(is_lat, dsh, zero), jnp.where(is_lat, dsc, zero),
                 jnp.where(is_lat, zero, dsh), jnp.where(is_lat, zero, dsc)))

    (dxl, d_g_pre_mix, d_sh1, d_sc1, d_csh1, d_csc1), early_recv = _rowwise(
        "pre_mix_bwd", k_pre_bwd, nt, xa_rows + [_row(d_proj), _row(dx1, lat_or_0)],
        [g_pre_mix, mod_lat, mod_ctx, w_in_p], [((S, D), F32, lat_or_0)], [(1, D)] * 5,
        ride_arrs=early_sends, ride_modes=['a2a'] * 4)
    grad_x = dxl[None]

    zrow = jnp.zeros((1, D), F32)
    d_mod_lat = jnp.concatenate([d_sh1, d_sc1, d_gt1, d_sh2, d_sc2, d_gt2], axis=1)
    d_mod_ctx = jnp.concatenate([d_csh1, d_csc1, zrow, zrow, zrow, zrow], axis=1)
    loss_row = jnp.pad(loss_acc, ((0, 0), (0, N_MOD * D - LANES)))
    d_mod_mine = jnp.concatenate([d_mod_lat, d_mod_ctx, loss_row, jnp.zeros((SUBLANES - 3, N_MOD * D), F32)])

    def diag_blocks(dw):
        return jnp.stack([dw[hh * 64:(hh + 1) * 64, hh * 64:(hh + 1) * 64] for hh in range(LRU_HEADS)])

    d_lru_w_a = jnp.stack([diag_blocks(d_w_gate[:, 0:LRU_W]), diag_blocks(d_w_gate[:, 2 * LRU_W:3 * LRU_W])])[None]
    d_lru_w_x = jnp.stack([diag_blocks(d_w_gate[:, LRU_W:2 * LRU_W]), diag_blocks(d_w_gate[:, 3 * LRU_W:])])[None]
    rep_part = dict(b_mod=d_mod_lat + d_mod_ctx, g_pre_mix=d_g_pre_mix, g_post_mix=d_g_post_mix,
                    g_pre_ffn=d_g_pre_ffn, g_post_ffn=d_g_post_ffn, lru_conv_b=d_lru_cb, lru_w_a=d_lru_w_a,
                    lru_w_x=d_lru_w_x, mla_g_q=d_g_q, mla_g_kv=d_g_kv, ffn_conv_b=d_ffn_cb)
    rep_names = [n for n in REPLICATED if n != 'c_ctx']
    rep_shapes = [W[n].shape for n in rep_names]
    rep_rows = -(-_pack_rows(sum(W[n].size for n in rep_names)) // ROW_TILE) * ROW_TILE
    rep_loc = _pack([rep_part[n] for n in rep_names], rep_rows)
    d_mod_all, rep_all = _exchange("gather_dmod", [d_mod_mine, rep_loc], ['ag'] * 2)
    loss = jnp.sum(d_mod_all[:, 2, 0])
    dm_lat_loc = lax.dynamic_slice(d_mod_all[:, 0], (0, me * ncol), (N_DEV, ncol))
    dm_ctx_loc = lax.dynamic_slice(d_mod_all[:, 1], (0, me * ncol), (N_DEV, ncol))

    def mod_bwd(c16_r, w_r, dml_r, dmc_r):
        c16_v = c16_r[...]
        sig = jax.nn.sigmoid(c16_v)
        sl = c16_v * sig
        dctx = dmc_r[0:1, :]
        for d in range(1, N_DEV):
            dctx = dctx + dmc_r[d:d + 1, :]
        row = lax.broadcasted_iota(jnp.int32, (2 * SUBLANES, ncol), 0)
        dm16 = dml_r[...] + jnp.where(row == N_DEV, _bc(dctx, 2 * SUBLANES), 0.0)
        dw = lax.dot_general(sl.astype(BF16), dm16.astype(BF16), (((0,), (0,)), ((), ())), preferred_element_type=F32)
        dsl = lax.dot_general(dm16.astype(BF16), w_r[...].astype(BF16), NT_DIMS, preferred_element_type=F32)
        dc = dsl * (sig * (1.0 + c16_v * (1.0 - sig)))
        return dw, dc

    dm_lat16 = jnp.concatenate([dm_lat_loc, jnp.zeros((2 * SUBLANES - N_DEV, ncol), F32)])
    g_w_mod, dc16 = _single("mod_bwd", mod_bwd, [c16, w_mod[0], dm_lat16, dm_ctx_loc],
                            [((D, ncol), F32), ((2 * SUBLANES, D), F32)])
    d_c_ctx_part = dc16[N_DEV]

    (c_ctx_all,) = _exchange("gather_d_c_ctx", [d_c_ctx_part.reshape(SUBLANES, D // SUBLANES)], ['ag'])
    big_parts = list(early_recv[:3]) + list(late_recv)

    res = {}

    def adam(name, w2, m2, v2, parts):
        return _adamw("adamw_" + name, w2, m2, v2, parts)

    for n, parts in zip(big, big_parts):
        shp = W[n].shape
        outs = adam(n, W[n][0], M[n][0], V[n][0], parts)
        res[n] = [o_.reshape(shp) for o_ in outs]
    outs = adam('w_mod', w_mod[0], m_w_mod[0], v_w_mod[0], g_w_mod[None])
    res['w_mod'] = [o_.reshape(w_mod.shape) for o_ in outs]

    sh_shapes = [W[n].shape for n in SMALL_SHARDED]
    pk = lambda dct: _pack([dct[n] for n in SMALL_SHARDED], sh_rows)
    outs = adam('small_sharded', pk(W), pk(M), pk(V), early_recv[3])
    for n, vals in zip(SMALL_SHARDED, zip(*[_unpack(o_, sh_shapes) for o_ in outs])):
        res[n] = list(vals)

    pr = lambda dct: _pack([dct[n] for n in rep_names], rep_rows)
    outs = adam('replicated', pr(W), pr(M), pr(V), rep_all)
    for n, vals in zip(rep_names, zip(*[_unpack(o_, rep_shapes) for o_ in outs])):
        res[n] = list(vals)
    as_tile = lambda a: a.reshape(SUBLANES, D // SUBLANES)
    outs = adam('c_ctx', as_tile(c_ctx), as_tile(m_c_ctx), as_tile(v_c_ctx), c_ctx_all)
    res['c_ctx'] = [o_.reshape(c_ctx.shape) for o_ in outs]

    return (loss, grad_x, *[res[n][0] for n in WEIGHTS], *[res[n][1] for n in WEIGHTS],
            *[res[n][2] for n in WEIGHTS], *[res[n][3] for n in WEIGHTS])
```

```python
import functools
import math

import jax
import jax.numpy as jnp
from jax import lax
from jax.experimental import pallas as pl
from jax.experimental.pallas import tpu as pltpu

F32 = jnp.float32
BF16 = jnp.bfloat16
MESH = pl.DeviceIdType.MESH

N_DEV = 8
ROW_TILE = 256
VJP_STRIPS = 4
SUBLANES = 8
LANES = 128
VMEM_LIMIT = 56 * 1024 * 1024

D_MODEL = 1024
LRU_W = 512
LRU_HEADS = 8
LRU_CONV_K = 4
LRU_CONV_LEFT = 2
LRU_C = 8.0
HEADS = 8
NOPE = 64
ROPE = 32
VDIM = 64
QK = NOPE + ROPE
HEAD_PAD = 128
Q_RANK = 256
KV_RANK = 128
MLA_SCALE = QK ** -0.5
ROPE_PAIRS = ROPE // 4
ROPE_BASE = 10000.0
GRID_W = 64
D_FF = 2816
FFN_CONV_K = 3
FFN_CONV_LEFT = 1
N_MOD = 6
EPS = 1e-6
IN_W = 2 * LRU_W + Q_RANK + KV_RANK + ROPE
IN_W_PAD = 2 * LRU_W + Q_RANK + KV_RANK + HEAD_PAD
OFF_GR, OFF_CQ, OFF_CKV, OFF_KR = LRU_W, 2 * LRU_W, 2 * LRU_W + Q_RANK, 2 * LRU_W + Q_RANK + KV_RANK

ADAM_LR, ADAM_B1, ADAM_B2, ADAM_EPS, ADAM_WD, ADAM_STEP = 0.001, 0.9, 0.999, 1e-08, 0.01, 10

WEIGHTS = ['c_ctx', 'w_mod', 'b_mod', 'g_pre_mix', 'g_post_mix', 'g_pre_ffn', 'g_post_ffn', 'w_in', 'lru_conv_w',
           'lru_conv_b', 'lru_w_a', 'lru_b_a', 'lru_w_x', 'lru_b_x', 'lru_lambda', 'mla_g_q', 'mla_w_uq', 'mla_g_kv',
           'mla_w_ukv', 'w_out', 'ffn_w_up', 'ffn_conv_w', 'ffn_conv_b', 'ffn_w_down']
REPLICATED = ['c_ctx', 'b_mod', 'g_pre_mix', 'g_post_mix', 'g_pre_ffn', 'g_post_ffn', 'lru_conv_b', 'lru_w_a',
              'lru_w_x', 'mla_g_q', 'mla_g_kv', 'ffn_conv_b']
SMALL_SHARDED = ['lru_conv_w', 'lru_b_a', 'lru_b_x', 'lru_lambda', 'ffn_conv_w']


def _pick(d, prefs):
    for p in prefs:
        if d % p == 0:
            return p
    return d


def _params(sem=None):
    return pltpu.CompilerParams(dimension_semantics=sem, vmem_limit_bytes=VMEM_LIMIT)


MM_TILES = (1024, 1408, 768, 512, 256, 128)


def _mm(name, a, b, mode, out_dtype=F32):
    if mode == 'nn':
        (m, k), (_, n) = a.shape, b.shape
    elif mode == 'nt':
        (m, k), (n, _) = a.shape, b.shape
    else:
        (k, m), (_, n) = a.shape, b.shape
    tm = _pick(m, MM_TILES)
    tn = _pick(n, MM_TILES)
    tk = _pick(k, MM_TILES)
    nk = k // tk
    if mode == 'nn':
        a_spec = pl.BlockSpec((tm, tk), lambda i, j, kk: (i, kk))
        b_spec = pl.BlockSpec((tk, tn), lambda i, j, kk: (kk, j))
        dn = (((1,), (0,)), ((), ()))
    elif mode == 'nt':
        a_spec = pl.BlockSpec((tm, tk), lambda i, j, kk: (i, kk))
        b_spec = pl.BlockSpec((tn, tk), lambda i, j, kk: (j, kk))
        dn = (((1,), (1,)), ((), ()))
    else:
        a_spec = pl.BlockSpec((tk, tm), lambda i, j, kk: (kk, i))
        b_spec = pl.BlockSpec((tk, tn), lambda i, j, kk: (kk, j))
        dn = (((0,), (0,)), ((), ()))

    def body(a_ref, b_ref, o_ref, acc_ref):
        kk = pl.program_id(2)

        @pl.when(kk == 0)
        def _():
            acc_ref[...] = jnp.zeros_like(acc_ref)

        acc_ref[...] += lax.dot_general(a_ref[...].astype(BF16), b_ref[...].astype(BF16), dn,
                                        preferred_element_type=F32)

        @pl.when(kk == nk - 1)
        def _():
            o_ref[...] = acc_ref[...].astype(o_ref.dtype)

    return pl.pallas_call(
        body, name=name, grid=(m // tm, n // tn, nk),
        in_specs=[a_spec, b_spec], out_specs=pl.BlockSpec((tm, tn), lambda i, j, kk: (i, j)),
        out_shape=jax.ShapeDtypeStruct((m, n), out_dtype),
        scratch_shapes=[pltpu.VMEM((tm, tn), F32)],
        compiler_params=_params(("parallel", "parallel", "arbitrary")),
    )(a, b)


def _row(a, idx=None, col=None):
    return dict(a=a, idx=idx, col=col)


def _rowwise(name, fn, n_tiles, rows, bcast, out_rows, out_acc=(), tm=ROW_TILE, ride_arrs=(), ride_modes=(), strips=1):
    in_specs = []
    for r in rows:
        a, idx, col = r['a'], r['idx'] or (lambda i: i), r['col']
        if a.ndim == 2:
            w, ci = col if col else (a.shape[1], 0)
            in_specs.append(pl.BlockSpec((tm, w), lambda i, idx=idx, ci=ci: (idx(i), ci)))
        else:
            in_specs.append(pl.BlockSpec((a.shape[0], tm, a.shape[2]), lambda i, idx=idx: (0, idx(i), 0)))
    for b in bcast:
        in_specs.append(pl.BlockSpec(b.shape, lambda i, nd=b.ndim: (0,) * nd))
    out_specs, out_shape = [], []
    for shape, dtype, idx in out_rows:
        idx = idx or (lambda i: i)
        if len(shape) == 2:
            out_specs.append(pl.BlockSpec((tm, shape[1]), lambda i, idx=idx: (idx(i), 0)))
        else:
            out_specs.append(pl.BlockSpec((shape[0], tm, shape[2]), lambda i, idx=idx: (0, idx(i), 0)))
        out_shape.append(jax.ShapeDtypeStruct(shape, dtype))
    for shape in out_acc:
        out_specs.append(pl.BlockSpec(shape, lambda i, nd=len(shape): (0,) * nd))
        out_shape.append(jax.ShapeDtypeStruct(shape, F32))
    nr, nb, no, na, n = len(rows), len(bcast), len(out_rows), len(out_acc), len(ride_arrs)
    any_spec = pl.BlockSpec(memory_space=pl.ANY)

    def body(*refs):
        i = pl.program_id(0)
        finish = _riding_exchange(refs, nr + nb, no + na, n, ride_modes, i == 0, i == n_tiles - 1) if n else None
        bvals = list(refs[nr:nr + nb])
        outs = refs[nr + nb + n:]
        acc_refs = outs[no:no + na]
        sub = tm // strips
        acc_sum = None
        for k in range(strips):
            def rows_of(ref):
                return ref[k * sub:(k + 1) * sub, :] if len(ref.shape) == 2 else ref[:, k * sub:(k + 1) * sub, :]
            o_rows, o_acc = fn(i, [rows_of(r) for r in refs[:nr]], bvals)
            for ref, v in zip(outs[:no], o_rows):
                if len(ref.shape) == 2:
                    ref[k * sub:(k + 1) * sub, :] = v.astype(ref.dtype)
                else:
                    ref[:, k * sub:(k + 1) * sub, :] = v.astype(ref.dtype)
            acc_sum = list(o_acc) if k == 0 else [a + b for a, b in zip(acc_sum, o_acc)]
        if acc_refs:
            @pl.when(i == 0)
            def _():
                for ref in acc_refs:
                    ref[...] = jnp.zeros_like(ref)
            for ref, v in zip(acc_refs, acc_sum):
                ref[...] += v
        if n:
            finish()

    res = pl.pallas_call(
        body, name=name, grid=(n_tiles,), in_specs=in_specs + [any_spec] * n, out_specs=out_specs + [any_spec] * n,
        out_shape=out_shape + _exchange_shapes(ride_arrs, ride_modes),
        scratch_shapes=_exchange_sems(n) if n else [],
        compiler_params=pltpu.CompilerParams(dimension_semantics=("arbitrary",), vmem_limit_bytes=VMEM_LIMIT,
                                             has_side_effects=bool(n)),
    )(*[r['a'] for r in rows], *bcast, *ride_arrs)
    return (res[:no + na], res[no + na:]) if n else res


def _single(name, fn, ins, out_shapes):
    def body(*refs):
        outs = fn(*refs[:len(ins)])
        for ref, v in zip(refs[len(ins):], outs):
            ref[...] = v.astype(ref.dtype)

    return pl.pallas_call(
        body, name=name,
        in_specs=[pl.BlockSpec(memory_space=pltpu.VMEM)] * len(ins),
        out_specs=[pl.BlockSpec(memory_space=pltpu.VMEM)] * len(out_shapes),
        out_shape=[jax.ShapeDtypeStruct(s, d) for s, d in out_shapes],
        compiler_params=_params(),
    )(*ins)


def _bc(p, n):
    return jnp.broadcast_to(p, (n, p.shape[-1]))


def _rs(g):
    return jnp.sum(g, axis=0, keepdims=True)


def _rms(x, g):
    return x * lax.rsqrt(jnp.mean(x * x, axis=-1, keepdims=True) + EPS) * g


def _conv_specs(r, cw, tm, halo=SUBLANES):
    th = tm // halo
    last = r // halo - 1
    prev = pl.BlockSpec((halo, cw), lambda c, i: (jnp.maximum(i * th - 1, 0), c))
    cur = pl.BlockSpec((tm, cw), lambda c, i: (i, c))
    nxt = pl.BlockSpec((halo, cw), lambda c, i: (jnp.minimum((i + 1) * th, last), c))
    return [prev, cur, nxt]


def _fill_ext(ext_ref, prev_ref, cur_ref, next_ref, i, n_tiles, seg_starts, tm):
    prev_ok = functools.reduce(jnp.logical_and, [i != s for s in seg_starts])
    next_ok = functools.reduce(jnp.logical_and, [i + 1 != s for s in seg_starts] + [i + 1 < n_tiles])
    ext_ref[0:SUBLANES, :] = jnp.where(prev_ok, prev_ref[...].astype(F32), 0.0)
    ext_ref[SUBLANES:SUBLANES + tm, :] = cur_ref[...].astype(F32)
    ext_ref[SUBLANES + tm:, :] = jnp.where(next_ok, next_ref[...].astype(F32), 0.0)


def _dwconv_fwd(name, x, w, b, left, seg_starts, cw=512, tm=ROW_TILE):
    r, c = x.shape[0], w.shape[1]
    kw = w.shape[0]
    n_tiles = r // tm

    def body(prev_ref, cur_ref, next_ref, w_ref, b_ref, o_ref, ext_ref):
        i = pl.program_id(1)
        _fill_ext(ext_ref, prev_ref, cur_ref, next_ref, i, n_tiles, seg_starts, tm)
        out = jnp.broadcast_to(b_ref[...], (tm, cw))
        for k in range(kw):
            out = out + ext_ref[pl.ds(SUBLANES + k - left, tm), :] * w_ref[k:k + 1, :]
        o_ref[...] = out

    return pl.pallas_call(
        body, name=name, grid=(c // cw, n_tiles),
        in_specs=_conv_specs(r, cw, tm) + [pl.BlockSpec((kw, cw), lambda c_, i: (0, c_)),
                                           pl.BlockSpec((1, cw), lambda c_, i: (0, c_))],
        out_specs=pl.BlockSpec((tm, cw), lambda c_, i: (i, c_)),
        out_shape=jax.ShapeDtypeStruct((r, c), F32),
        scratch_shapes=[pltpu.VMEM((tm + 2 * SUBLANES, cw), F32)],
        compiler_params=_params(("parallel", "arbitrary")),
    )(x, x, x, w, b)


def _dwconv_bwd(name, x, dy, w, left, seg_starts, out_dtype=F32, cw=512, tm=ROW_TILE):
    r, c = dy.shape
    kw = w.shape[0]
    n_tiles = r // tm

    def body(xp, xc, xn, dp, dc, dn, w_ref, dx_ref, dw_ref, db_ref, xe_ref, de_ref):
        i = pl.program_id(1)
        _fill_ext(xe_ref, xp, xc, xn, i, n_tiles, seg_starts, tm)
        _fill_ext(de_ref, dp, dc, dn, i, n_tiles, seg_starts, tm)
        dyc = dc[...].astype(F32)
        dx = jnp.zeros((tm, cw), F32)
        dws = []
        for k in range(kw):
            dx = dx + de_ref[pl.ds(SUBLANES - k + left, tm), :] * w_ref[k:k + 1, :]
            dws.append(jnp.sum(dyc * xe_ref[pl.ds(SUBLANES + k - left, tm), :], axis=0, keepdims=True))
        dx_ref[...] = dx.astype(dx_ref.dtype)

        @pl.when(i == 0)
        def _():
            dw_ref[...] = jnp.zeros_like(dw_ref)
            db_ref[...] = jnp.zeros_like(db_ref)

        for k in range(kw):
            dw_ref[k:k + 1, :] += dws[k]
        db_ref[...] += jnp.sum(dyc, axis=0, keepdims=True)

    return pl.pallas_call(
        body, name=name, grid=(c // cw, n_tiles),
        in_specs=_conv_specs(r, cw, tm) + _conv_specs(r, cw, tm) + [pl.BlockSpec((kw, cw), lambda c_, i: (0, c_))],
        out_specs=[pl.BlockSpec((tm, cw), lambda c_, i: (i, c_)),
                   pl.BlockSpec((kw, cw), lambda c_, i: (0, c_)),
                   pl.BlockSpec((1, cw), lambda c_, i: (0, c_))],
        out_shape=[jax.ShapeDtypeStruct((r, c), out_dtype), jax.ShapeDtypeStruct((kw, c), F32),
                   jax.ShapeDtypeStruct((1, c), F32)],
        scratch_shapes=[pltpu.VMEM((tm + 2 * SUBLANES, cw), F32), pltpu.VMEM((tm + 2 * SUBLANES, cw), F32)],
        compiler_params=_params(("parallel", "arbitrary")),
    )(x, x, x, dy, dy, dy, w)


FF_TILE = 256
FF_HALO = 16
FF_STRIP = 32


def _ffn_fill(ext_ref, prev_ref, cur_ref, next_ref, i, n_tiles, tm):
    ext_ref[0:FF_HALO, :] = jnp.where(i > 0, prev_ref[...].astype(F32), 0.0)
    ext_ref[FF_HALO:FF_HALO + tm, :] = cur_ref[...].astype(F32)
    ext_ref[FF_HALO + tm:, :] = jnp.where(i + 1 < n_tiles, next_ref[...].astype(F32), 0.0)


def _ffn_conv(ext_ref, w_ref, b_ref, start, rows):
    out = jnp.broadcast_to(b_ref[...], (rows, 2 * FF_TILE))
    for k in range(FFN_CONV_K):
        out = out + ext_ref[pl.ds(start + k - FFN_CONV_LEFT, rows), :] * w_ref[k:k + 1, :]
    return out


def _ffn_mid_fwd(up, w, b):
    s, c2 = up.shape
    tm = _pick(s, (2 * ROW_TILE, ROW_TILE))
    n_tiles = s // tm
    cw = 2 * FF_TILE

    def body(prev_ref, cur_ref, next_ref, w_ref, b_ref, o_ref, ext_ref):
        i = pl.program_id(1)
        _ffn_fill(ext_ref, prev_ref, cur_ref, next_ref, i, n_tiles, tm)
        for r0 in range(0, tm, FF_STRIP):
            upc = _ffn_conv(ext_ref, w_ref, b_ref, FF_HALO + r0, FF_STRIP)
            uv, gv = upc[:, :FF_TILE], upc[:, FF_TILE:]
            o_ref[r0:r0 + FF_STRIP, :] = (gv * jax.nn.sigmoid(gv) * uv).astype(o_ref.dtype)

    return pl.pallas_call(
        body, name="ffn_mid", grid=(c2 // cw, n_tiles),
        in_specs=_conv_specs(s, cw, tm, FF_HALO) + [pl.BlockSpec((FFN_CONV_K, cw), lambda c_, i: (0, c_)),
                                                   pl.BlockSpec((1, cw), lambda c_, i: (0, c_))],
        out_specs=pl.BlockSpec((tm, FF_TILE), lambda c_, i: (i, c_)),
        out_shape=jax.ShapeDtypeStruct((s, c2 // 2), BF16),
        scratch_shapes=[pltpu.VMEM((tm + 2 * FF_HALO, cw), F32)],
        compiler_params=_params(("parallel", "arbitrary")),
    )(up, up, up, w, b)


def _ffn_mid_bwd(up, d_act, w, b, tm=ROW_TILE):
    s, c2 = up.shape
    n_tiles = s // tm
    cw = 2 * FF_TILE
    h8 = SUBLANES

    def gate_bwd(upc, dact):
        uv, gv = upc[:, :FF_TILE], upc[:, FF_TILE:]
        sg = jax.nn.sigmoid(gv)
        return jnp.concatenate([dact * (gv * sg), dact * uv * (sg * (1.0 + gv * (1.0 - sg)))], axis=1)

    def body(up_p, up_c, up_n, da_p, da_c, da_n, w_ref, b_ref, dup_ref, dw_ref, db_ref, ext_ref, dext_ref):
        i = pl.program_id(1)
        _ffn_fill(ext_ref, up_p, up_c, up_n, i, n_tiles, tm)
        dws = [jnp.zeros((1, cw), F32) for _ in range(FFN_CONV_K)]
        dbs = jnp.zeros((1, cw), F32)
        for r0 in range(0, tm, FF_STRIP):
            d_c = gate_bwd(_ffn_conv(ext_ref, w_ref, b_ref, FF_HALO + r0, FF_STRIP),
                           da_c[r0:r0 + FF_STRIP, :].astype(F32))
            dext_ref[FF_HALO + r0:FF_HALO + r0 + FF_STRIP, :] = d_c
            for k in range(FFN_CONV_K):
                xk = ext_ref[pl.ds(FF_HALO + r0 + k - FFN_CONV_LEFT, FF_STRIP), :]
                dws[k] = dws[k] + jnp.sum(d_c * xk, axis=0, keepdims=True)
            dbs = dbs + jnp.sum(d_c, axis=0, keepdims=True)
        da_prev = jnp.where(i > 0, da_p[...].astype(F32)[FF_HALO - h8:, :], 0.0)
        da_next = jnp.where(i + 1 < n_tiles, da_n[...].astype(F32)[:h8, :], 0.0)
        dext_ref[FF_HALO - h8:FF_HALO, :] = gate_bwd(_ffn_conv(ext_ref, w_ref, b_ref, FF_HALO - h8, h8), da_prev)
        dext_ref[FF_HALO + tm:FF_HALO + tm + h8, :] = gate_bwd(_ffn_conv(ext_ref, w_ref, b_ref, FF_HALO + tm, h8), da_next)
        for r0 in range(0, tm, FF_STRIP):
            dup = jnp.zeros((FF_STRIP, cw), F32)
            for k in range(FFN_CONV_K):
                dup = dup + dext_ref[pl.ds(FF_HALO + r0 - k + FFN_CONV_LEFT, FF_STRIP), :] * w_ref[k:k + 1, :]
            dup_ref[r0:r0 + FF_STRIP, :] = dup.astype(dup_ref.dtype)

        @pl.when(i == 0)
        def _():
            dw_ref[...] = jnp.zeros_like(dw_ref)
            db_ref[...] = jnp.zeros_like(db_ref)

        for k in range(FFN_CONV_K):
            dw_ref[k:k + 1, :] += dws[k]
        db_ref[...] += dbs

    def half_specs():
        th = tm // FF_HALO
        last = s // FF_HALO - 1
        return [pl.BlockSpec((FF_HALO, FF_TILE), lambda c_, i: (jnp.maximum(i * th - 1, 0), c_)),
                pl.BlockSpec((tm, FF_TILE), lambda c_, i: (i, c_)),
                pl.BlockSpec((FF_HALO, FF_TILE), lambda c_, i: (jnp.minimum((i + 1) * th, last), c_))]

    return pl.pallas_call(
        body, name="ffn_mid_bwd", grid=(c2 // cw, n_tiles),
        in_specs=_conv_specs(s, cw, tm, FF_HALO) + half_specs() + [
            pl.BlockSpec((FFN_CONV_K, cw), lambda c_, i: (0, c_)), pl.BlockSpec((1, cw), lambda c_, i: (0, c_))],
        out_specs=[pl.BlockSpec((tm, cw), lambda c_, i: (i, c_)),
                   pl.BlockSpec((FFN_CONV_K, cw), lambda c_, i: (0, c_)),
                   pl.BlockSpec((1, cw), lambda c_, i: (0, c_))],
        out_shape=[jax.ShapeDtypeStruct((s, c2), BF16), jax.ShapeDtypeStruct((FFN_CONV_K, c2), F32),
                   jax.ShapeDtypeStruct((1, c2), F32)],
        scratch_shapes=[pltpu.VMEM((tm + 2 * FF_HALO, cw), F32), pltpu.VMEM((tm + 2 * FF_HALO, cw), F32)],
        compiler_params=_params(("parallel", "arbitrary")),
    )(up, up, up, d_act, d_act, d_act, w, b)


def _ff_permute(name, w, to_tiles):
    r = w.shape[0]
    nb = D_FF // FF_TILE
    natural = pl.BlockSpec((r, FF_TILE), lambda j, half: (0, half * nb + j))
    tiled = pl.BlockSpec((r, FF_TILE), lambda j, half: (0, 2 * j + half))

    def body(x_ref, o_ref):
        o_ref[...] = x_ref[...]

    return pl.pallas_call(
        body, name=name, grid=(nb, 2),
        in_specs=[natural if to_tiles else tiled], out_specs=tiled if to_tiles else natural,
        out_shape=jax.ShapeDtypeStruct(w.shape, w.dtype), compiler_params=_params(("parallel", "parallel")),
    )(w)


def _ff_to_tiles(w):
    r = w.shape[0]
    return w.reshape(r, 2, D_FF // FF_TILE, FF_TILE).transpose(0, 2, 1, 3).reshape(r, 2 * D_FF)


def _ff_from_tiles(w):
    r = w.shape[0]
    return w.reshape(r, D_FF // FF_TILE, 2, FF_TILE).transpose(0, 2, 1, 3).reshape(r, 2 * D_FF)


SCAN_UNROLL = 8


def _scan(name, a, u, reverse, shifted=False, u_off=0, with_prev=False):
    t, c = a.shape
    us = u.shape[0]
    n8 = t // SUBLANES
    lo, hi = 0, SUBLANES - 1

    def body(a_ref, u_ref, h_ref, *prev_ref):
        row = lax.broadcasted_iota(jnp.int32, (SUBLANES, LANES), 0)
        last = lo if reverse else hi

        def tile(ref, base):
            return ref[pl.ds(pl.multiple_of(base, SUBLANES), SUBLANES), :]

        def local(blk):
            base = blk * SUBLANES
            av = tile(a_ref, base)
            if shifted and reverse:
                nb = tile(a_ref, jnp.minimum(base + SUBLANES, t - SUBLANES))
                edge = jnp.where(base + SUBLANES >= t, 1.0, pltpu.roll(nb, hi, 0))
                av = jnp.where(row < hi, pltpu.roll(av, hi, 0), edge)
            elif shifted:
                pb = tile(a_ref, jnp.maximum(base - SUBLANES, 0))
                edge = jnp.where(base == 0, 1.0, pltpu.roll(pb, 1, 0))
                av = jnp.where(row >= 1, pltpu.roll(av, 1, 0), edge)
            ub = base - u_off
            hv = jnp.where((ub >= 0) & (ub < us), tile(u_ref, jnp.clip(ub, 0, us - SUBLANES)), 0.0)
            for s in (1, 2, 4):
                shift = SUBLANES - s if reverse else s
                ok = (row < SUBLANES - s) if reverse else (row >= s)
                a_sh = jnp.where(ok, pltpu.roll(av, shift, 0), 1.0)
                h_sh = jnp.where(ok, pltpu.roll(hv, shift, 0), 0.0)
                hv = av * h_sh + hv
                av = av * a_sh
            a_last = jnp.sum(jnp.where(row == last, av, 0.0), axis=0, keepdims=True)
            h_last = jnp.sum(jnp.where(row == last, hv, 0.0), axis=0, keepdims=True)
            return base, av, hv, a_last, h_last

        def step(j, carry):
            parts = []
            for k in range(SCAN_UNROLL):
                idx = j * SCAN_UNROLL + k
                parts.append(local((n8 - 1 - idx) if reverse else idx))
            for base, av, hv, a_last, h_last in parts:
                rows = pl.ds(pl.multiple_of(base, SUBLANES), SUBLANES)
                h_true = av * carry + hv
                h_ref[rows, :] = h_true
                if with_prev:
                    if reverse:
                        prev_ref[0][rows, :] = jnp.where(row < hi, pltpu.roll(h_true, hi, 0), carry)
                    else:
                        prev_ref[0][rows, :] = jnp.where(row >= 1, pltpu.roll(h_true, 1, 0), carry)
                carry = a_last * carry + h_last
            return carry

        lax.fori_loop(0, n8 // SCAN_UNROLL, step, jnp.zeros((1, LANES), F32))

    spec = pl.BlockSpec((t, LANES), lambda j: (0, j))
    n_out = 2 if with_prev else 1
    res = pl.pallas_call(
        body, name=name, grid=(c // LANES,),
        in_specs=[spec, pl.BlockSpec((us, LANES), lambda j: (0, j))],
        out_specs=[spec] * n_out, out_shape=[jax.ShapeDtypeStruct((t, c), F32)] * n_out,
        compiler_params=_params(("parallel",)),
    )(a, u)
    return res if with_prev else res[0]


NT_DIMS = (((1,), (1,)), ((), ()))


LOG2E = 1.4426950408889634
SCALE2 = MLA_SCALE * LOG2E
ATTN_TILES = (512, 256, 128)
KEY_CHUNKS = (768, 512, 256, 128)
QUERY_CHUNKS = (1024, 512, 256, 128)


def _attn_fwd(q, k, v, ride_arrs, ride_modes):
    h, s, _ = q.shape
    t = k.shape[1]
    tq = _pick(s, ATTN_TILES)
    ck = _pick(t, KEY_CHUNKS)
    n = len(ride_arrs)

    def body(*refs):
        q_ref, k_ref, v_ref = refs[:3]
        o_ref, lse_ref = refs[3 + n:5 + n]
        hh, i = pl.program_id(0), pl.program_id(1)
        finish = _riding_exchange(refs, 3, 2, n, ride_modes, (hh == 0) & (i == 0),
                                  (hh == h - 1) & (i == s // tq - 1))
        qv = q_ref[0]
        def scores(j):
            return lax.dot_general(qv, k_ref[0, j * ck:(j + 1) * ck, :], NT_DIMS, preferred_element_type=F32)

        m = l = acc = None
        s_next = scores(0)
        for j in range(t // ck):
            vj = v_ref[0, j * ck:(j + 1) * ck, :]
            s2, s_next = s_next, (scores(j + 1) if j + 1 < t // ck else None)
            mj = jnp.max(s2, axis=-1, keepdims=True)
            m_new = mj if j == 0 else jnp.maximum(m, mj)
            p = jnp.exp2(s2 - m_new)
            lj = jnp.sum(p, axis=-1, keepdims=True)
            pv = jnp.dot(p.astype(BF16), vj, preferred_element_type=F32)
            if j == 0:
                l, acc = lj, pv
            else:
                alpha = jnp.exp2(m - m_new)
                l, acc = alpha * l + lj, alpha * acc + pv
            m = m_new
        o_ref[0] = acc / l
        lse_ref[0] = _rows8(jnp.broadcast_to(m + jnp.log2(l), (tq, LANES)))
        finish()

    any_spec = pl.BlockSpec(memory_space=pl.ANY)
    res = pl.pallas_call(
        body, name="attn_fwd", grid=(h, s // tq),
        in_specs=[pl.BlockSpec((1, tq, HEAD_PAD), lambda hh, i: (hh, i, 0)),
                  pl.BlockSpec((1, t, HEAD_PAD), lambda hh, i: (hh, 0, 0)),
                  pl.BlockSpec((1, t, VDIM), lambda hh, i: (hh, 0, 0))] + [any_spec] * n,
        out_specs=[pl.BlockSpec((1, tq, VDIM), lambda hh, i: (hh, i, 0)),
                   pl.BlockSpec((1, SUBLANES, tq), lambda hh, i: (hh, 0, i))] + [any_spec] * n,
        out_shape=[jax.ShapeDtypeStruct((h, s, VDIM), F32), jax.ShapeDtypeStruct((h, SUBLANES, s), F32)]
        + _exchange_shapes(ride_arrs, ride_modes),
        scratch_shapes=_exchange_sems(n),
        compiler_params=pltpu.CompilerParams(dimension_semantics=("arbitrary", "arbitrary"),
                                             vmem_limit_bytes=VMEM_LIMIT, has_side_effects=True),
    )(q, k, v, *ride_arrs)
    return res[0], res[1], res[2:]


TN_DIMS = (((0,), (0,)), ((), ()))


def _rows8(x):
    return jnp.transpose(x)[:SUBLANES, :]


def _attn_delta(do_src, o_t, tm=ROW_TILE):
    s = o_t.shape[0]
    w = HEADS * VDIM

    def body(do_ref, o_ref, d_ref):
        prod = do_ref[...] * o_ref[...]
        for c in range(w // LANES):
            x = prod[:, c * LANES:(c + 1) * LANES]
            shift = VDIM // 2
            while shift:
                x = x + pltpu.roll(x, LANES - shift, 1)
                shift //= 2
            xt = jnp.transpose(x)
            d_ref[2 * c] = xt[:SUBLANES, :]
            d_ref[2 * c + 1] = xt[VDIM:VDIM + SUBLANES, :]

    return pl.pallas_call(
        body, name="attn_delta", grid=(s // tm,),
        in_specs=[pl.BlockSpec((tm, w), lambda i: (i, 1)), pl.BlockSpec((tm, w), lambda i: (i, 0))],
        out_specs=pl.BlockSpec((HEADS, SUBLANES, tm), lambda i: (0, 0, i)),
        out_shape=jax.ShapeDtypeStruct((HEADS, SUBLANES, s), F32),
        compiler_params=_params(("parallel",)),
    )(do_src, o_t)


def _attn_bwd(q, k, v, do, lse_row, delta_row, ride_arrs, ride_modes):
    h, s, _ = q.shape
    t = k.shape[1]
    tk = _pick(t, (768,) + ATTN_TILES)
    cq = _pick(s, QUERY_CHUNKS)
    n = len(ride_arrs)

    def body(*refs):
        q_ref, k_ref, v_ref, do_ref, lse_ref, delta_ref = refs[:6]
        dq_ref, dk_ref, dv_ref = refs[6 + n:9 + n]
        hh, i = pl.program_id(0), pl.program_id(1)
        finish = _riding_exchange(refs, 6, 3, n, ride_modes, (hh == 0) & (i == 0),
                                  (hh == h - 1) & (i == t // tk - 1))

        @pl.when(i == 0)
        def _():
            dq_ref[...] = jnp.zeros_like(dq_ref)

        kt, vt = k_ref[0], v_ref[0]
        dk = dv = None
        for j in range(s // cq):
            rows = slice(j * cq, (j + 1) * cq)
            qj, doj = q_ref[0, rows, :], do_ref[0, rows, :]
            pt = jnp.exp2(lax.dot_general(kt, qj, NT_DIMS, preferred_element_type=F32) - lse_ref[0, 0:1, rows])
            dv_j = jnp.dot(pt.astype(BF16), doj, preferred_element_type=F32)
            dpt = lax.dot_general(vt, doj, NT_DIMS, preferred_element_type=F32)
            dst = (pt * (dpt - delta_ref[0, 0:1, rows])).astype(BF16)
            dk_j = jnp.dot(dst, qj, preferred_element_type=F32)
            dq_ref[0, rows, :] += lax.dot_general(dst, kt, TN_DIMS, preferred_element_type=F32)
            dk, dv = (dk_j, dv_j) if j == 0 else (dk + dk_j, dv + dv_j)
        dk_ref[0] = dk * (1.0 / LOG2E)
        dv_ref[0] = dv
        finish()

    any_spec = pl.BlockSpec(memory_space=pl.ANY)
    res = pl.pallas_call(
        body, name="attn_bwd", grid=(h, t // tk),
        in_specs=[pl.BlockSpec((1, s, HEAD_PAD), lambda hh, i: (hh, 0, 0)),
                  pl.BlockSpec((1, tk, HEAD_PAD), lambda hh, i: (hh, i, 0)),
                  pl.BlockSpec((1, tk, VDIM), lambda hh, i: (hh, i, 0)),
                  pl.BlockSpec((1, s, VDIM), lambda hh, i: (hh, 0, 0)),
                  pl.BlockSpec((1, SUBLANES, s), lambda hh, i: (hh, 0, 0)),
                  pl.BlockSpec((1, SUBLANES, s), lambda hh, i: (hh, 0, 0))] + [any_spec] * n,
        out_specs=[pl.BlockSpec((1, s, HEAD_PAD), lambda hh, i: (hh, 0, 0)),
                   pl.BlockSpec((1, tk, HEAD_PAD), lambda hh, i: (hh, i, 0)),
                   pl.BlockSpec((1, tk, VDIM), lambda hh, i: (hh, i, 0))] + [any_spec] * n,
        out_shape=[jax.ShapeDtypeStruct((h, s, HEAD_PAD), F32), jax.ShapeDtypeStruct((h, t, HEAD_PAD), F32),
                   jax.ShapeDtypeStruct((h, t, VDIM), F32)] + _exchange_shapes(ride_arrs, ride_modes),
        scratch_shapes=_exchange_sems(n),
        compiler_params=pltpu.CompilerParams(dimension_semantics=("arbitrary", "arbitrary"),
                                             vmem_limit_bytes=VMEM_LIMIT, has_side_effects=True),
    )(q, k, v, do, lse_row, delta_row, *ride_arrs)
    return res[0], res[1], res[2], res[3:]


def _exchange_shapes(arrs, modes):
    return [jax.ShapeDtypeStruct((N_DEV,) + a.shape if md == 'ag' else a.shape, a.dtype) for a, md in zip(arrs, modes)]


def _exchange_sems(n):
    return [pltpu.SemaphoreType.DMA((n, N_DEV - 1)), pltpu.SemaphoreType.DMA((n, N_DEV - 1)),
            pltpu.SemaphoreType.DMA((n,))]


def _exchange_copies(ins, outs, modes, send_sems, recv_sems, local_sems):
    x, y, c = lax.axis_index("x"), lax.axis_index("y"), lax.axis_index("c")
    me = 4 * x + 2 * y + c
    copies = []
    for a in range(len(ins)):
        ag = modes[a] == 'ag'
        copies.append(pltpu.make_async_copy(ins[a] if ag else ins[a].at[me], outs[a].at[me], local_sems.at[a]))
        for k in range(1, N_DEV):
            px = 1 - x if k & 4 else x
            py = 1 - y if k & 2 else y
            pc = 1 - c if k & 1 else c
            src = ins[a] if ag else ins[a].at[4 * px + 2 * py + pc]
            copies.append(pltpu.make_async_remote_copy(
                src_ref=src, dst_ref=outs[a].at[me], send_sem=send_sems.at[a, k - 1],
                recv_sem=recv_sems.at[a, k - 1], device_id=(px, py, pc), device_id_type=MESH))
    return copies


def _exchange(name, arrs, modes):
    n = len(arrs)

    def body(*refs):
        copies = _exchange_copies(refs[:n], refs[n:2 * n], modes, *refs[2 * n:])
        for cp in copies:
            cp.start()
        for cp in copies:
            cp.wait()

    return pl.pallas_call(
        body, name=name,
        in_specs=[pl.BlockSpec(memory_space=pl.ANY)] * n,
        out_specs=[pl.BlockSpec(memory_space=pl.ANY)] * n,
        out_shape=_exchange_shapes(arrs, modes),
        scratch_shapes=_exchange_sems(n),
        compiler_params=pltpu.CompilerParams(has_side_effects=True),
    )(*arrs)


def _riding_exchange(refs, n_in, n_out, n, modes, first, last):
    ins = refs[n_in:n_in + n]
    outs = refs[n_in + n + n_out:n_in + 2 * n + n_out]
    sems = refs[n_in + 2 * n + n_out:n_in + 2 * n + n_out + 3]

    @pl.when(first)
    def _():
        for cp in _exchange_copies(ins, outs, modes, *sems):
            cp.start()

    def finish():
        @pl.when(last)
        def _():
            for cp in _exchange_copies(ins, outs, modes, *sems):
                cp.wait()

    return finish


def _adamw(name, w, m, v, gparts):
    r, c = w.shape
    npart = gparts.shape[0]
    tr = _pick(r, (256, 128, 64, 32, 16, 8))
    spec = pl.BlockSpec((tr, c), lambda i: (i, 0))

    def body(w_ref, m_ref, v_ref, g_ref, go_ref, d_ref, mo_ref, vo_ref):
        g = g_ref[0]
        for p in range(1, npart):
            g = g + g_ref[p]
        m1 = ADAM_B1 * m_ref[...] + (1.0 - ADAM_B1) * g
        v1 = ADAM_B2 * v_ref[...] + (1.0 - ADAM_B2) * (g * g)
        m_hat = m1 / (1.0 - ADAM_B1 ** ADAM_STEP)
        v_hat = v1 / (1.0 - ADAM_B2 ** ADAM_STEP)
        go_ref[...] = g
        d_ref[...] = -ADAM_LR * (m_hat / (jnp.sqrt(v_hat) + ADAM_EPS) + ADAM_WD * w_ref[...])
        mo_ref[...] = m1
        vo_ref[...] = v1

    return pl.pallas_call(
        body, name=name, grid=(r // tr,),
        in_specs=[spec, spec, spec, pl.BlockSpec((npart, tr, c), lambda i: (0, i, 0))],
        out_specs=[spec] * 4, out_shape=[jax.ShapeDtypeStruct((r, c), F32)] * 4,
        compiler_params=_params(("parallel",)),
    )(w, m, v, gparts)


def _pack(arrs, rows):
    flat = jnp.concatenate([a.reshape(-1) for a in arrs])
    return jnp.pad(flat, (0, rows * LANES - flat.shape[0])).reshape(rows, LANES)


def _unpack(packed, shapes):
    flat, out, off = packed.reshape(-1), [], 0
    for s in shapes:
        n = math.prod(s)
        out.append(flat[off:off + n].reshape(s))
        off += n
    return out


def _pack_rows(n_elems):
    return -(-n_elems // (SUBLANES * LANES)) * SUBLANES


def _cols_from_shards(g):
    return g.transpose(1, 0, 2).reshape(g.shape[1], N_DEV * g.shape[2])


def _cols_to_shards(w):
    r, c = w.shape
    return w.reshape(r, N_DEV, c // N_DEV).transpose(1, 0, 2)


def _rope_tables(n_lat, n_ctx):
    inv = ROPE_BASE ** (-jnp.arange(ROPE_PAIRS, dtype=F32) / ROPE_PAIRS)
    seg = [(l - NOPE) // ROPE_PAIRS if NOPE <= l < QK else -1 for l in range(HEAD_PAD)]
    freq = jnp.take(inv, jnp.array([(l - NOPE) % ROPE_PAIRS if NOPE <= l < QK else 0 for l in range(HEAD_PAD)]))
    seg = jnp.array(seg)
    f_row = jnp.where((seg == 0) | (seg == 1), freq, 0.0)[None, :]
    f_col = jnp.where((seg == 2) | (seg == 3), freq, 0.0)[None, :]
    t = jnp.arange(n_ctx + n_lat) - n_ctx
    pos = jnp.maximum(t, 0)
    row = jnp.where(t >= 0, pos // GRID_W, 0).astype(F32)[:, None]
    col = jnp.where(t >= 0, pos % GRID_W, 0).astype(F32)[:, None]
    ang = row * f_row + col * f_col
    sin = jnp.sin(ang)
    sin_up = jnp.where(((seg == 0) | (seg == 2))[None, :], -sin, 0.0)
    sin_dn = jnp.where(((seg == 1) | (seg == 3))[None, :], sin, 0.0)
    return jnp.cos(ang), sin_up, sin_dn


def _rope(x, cos_t, sin_up, sin_dn):
    return x * cos_t + pltpu.roll(x, HEAD_PAD - ROPE_PAIRS, 1) * sin_up + pltpu.roll(x, ROPE_PAIRS, 1) * sin_dn


def _rope_t(dy, cos_t, sin_up, sin_dn):
    return (dy * cos_t + pltpu.roll(dy * sin_up, ROPE_PAIRS, 1)
            + pltpu.roll(dy * sin_dn, HEAD_PAD - ROPE_PAIRS, 1))


def _softplus(x):
    return jnp.maximum(x, 0.0) + jnp.log(1.0 + jnp.exp(-jnp.abs(x)))


def _gates(z, xcv, lam_sp):
    outs = []
    for d in range(2):
        r = jax.nn.sigmoid(z[:, (2 * d) * LRU_W:(2 * d + 1) * LRU_W])
        ig = jax.nn.sigmoid(z[:, (2 * d + 1) * LRU_W:(2 * d + 2) * LRU_W])
        log_a = -LRU_C * r * lam_sp[:, d * LRU_W:(d + 1) * LRU_W]
        a = jnp.exp(log_a)
        u = jnp.sqrt(-jnp.tanh(log_a) * (a * a + 1.0)) * (ig * xcv)
        outs += [a, u]
    return tuple(outs)


def kernel(x, c, ctx, c_ctx, w_mod, b_mod, g_pre_mix, g_post_mix, g_pre_ffn, g_post_ffn, w_in, lru_conv_w, lru_conv_b, lru_w_a, lru_b_a, lru_w_x, lru_b_x, lru_lambda, mla_g_q, mla_w_uq, mla_g_kv, mla_w_ukv, w_out, ffn_w_up, ffn_conv_w, ffn_conv_b, ffn_w_down, loss_target, m_c_ctx, m_w_mod, m_b_mod, m_g_pre_mix, m_g_post_mix, m_g_pre_ffn, m_g_post_ffn, m_w_in, m_lru_conv_w, m_lru_conv_b, m_lru_w_a, m_lru_b_a, m_lru_w_x, m_lru_b_x, m_lru_lambda, m_mla_g_q, m_mla_w_uq, m_mla_g_kv, m_mla_w_ukv, m_w_out, m_ffn_w_up, m_ffn_conv_w, m_ffn_conv_b, m_ffn_w_down, v_c_ctx, v_w_mod, v_b_mod, v_g_pre_mix, v_g_post_mix, v_g_pre_ffn, v_g_post_ffn, v_w_in, v_lru_conv_w, v_lru_conv_b, v_lru_w_a, v_lru_b_a, v_lru_w_x, v_lru_b_x, v_lru_lambda, v_mla_g_q, v_mla_w_uq, v_mla_g_kv, v_mla_w_ukv, v_w_out, v_ffn_w_up, v_ffn_conv_w, v_ffn_conv_b, v_ffn_w_down):
    W = dict(c_ctx=c_ctx, w_mod=w_mod, b_mod=b_mod, g_pre_mix=g_pre_mix, g_post_mix=g_post_mix, g_pre_ffn=g_pre_ffn,
             g_post_ffn=g_post_ffn, w_in=w_in, lru_conv_w=lru_conv_w, lru_conv_b=lru_conv_b, lru_w_a=lru_w_a,
             lru_b_a=lru_b_a, lru_w_x=lru_w_x, lru_b_x=lru_b_x, lru_lambda=lru_lambda, mla_g_q=mla_g_q,
             mla_w_uq=mla_w_uq, mla_g_kv=mla_g_kv, mla_w_ukv=mla_w_ukv, w_out=w_out, ffn_w_up=ffn_w_up,
             ffn_conv_w=ffn_conv_w, ffn_conv_b=ffn_conv_b, ffn_w_down=ffn_w_down)
    M = dict(c_ctx=m_c_ctx, w_mod=m_w_mod, b_mod=m_b_mod, g_pre_mix=m_g_pre_mix, g_post_mix=m_g_post_mix,
             g_pre_ffn=m_g_pre_ffn, g_post_ffn=m_g_post_ffn, w_in=m_w_in, lru_conv_w=m_lru_conv_w,
             lru_conv_b=m_lru_conv_b, lru_w_a=m_lru_w_a, lru_b_a=m_lru_b_a, lru_w_x=m_lru_w_x, lru_b_x=m_lru_b_x,
             lru_lambda=m_lru_lambda, mla_g_q=m_mla_g_q, mla_w_uq=m_mla_w_uq, mla_g_kv=m_mla_g_kv,
             mla_w_ukv=m_mla_w_ukv, w_out=m_w_out, ffn_w_up=m_ffn_w_up, ffn_conv_w=m_ffn_conv_w,
             ffn_conv_b=m_ffn_conv_b, ffn_w_down=m_ffn_w_down)
    V = dict(c_ctx=v_c_ctx, w_mod=v_w_mod, b_mod=v_b_mod, g_pre_mix=v_g_pre_mix, g_post_mix=v_g_post_mix,
             g_pre_ffn=v_g_pre_ffn, g_post_ffn=v_g_post_ffn, w_in=v_w_in, lru_conv_w=v_lru_conv_w,
             lru_conv_b=v_lru_conv_b, lru_w_a=v_lru_w_a, lru_b_a=v_lru_b_a, lru_w_x=v_lru_w_x, lru_b_x=v_lru_b_x,
             lru_lambda=v_lru_lambda, mla_g_q=v_mla_g_q, mla_w_uq=v_mla_w_uq, mla_g_kv=v_mla_g_kv,
             mla_w_ukv=v_mla_w_ukv, w_out=v_w_out, ffn_w_up=v_ffn_w_up, ffn_conv_w=v_ffn_conv_w,
             ffn_conv_b=v_ffn_conv_b, ffn_w_down=v_ffn_w_down)

    D = D_MODEL
    S, CN = x.shape[1], ctx.shape[1]
    T = S + CN
    TM = ROW_TILE
    ct, ns, nt = CN // TM, S // TM, T // TM
    me = 4 * lax.axis_index("x") + 2 * lax.axis_index("y") + lax.axis_index("c")

    lat = lambda i: i + ct
    swp = lambda i: jnp.where(i < ct, i + ns, i - ct)
    lat_or_0 = lambda i: jnp.maximum(i - ct, 0)

    small_shapes = [W[n].shape[1:] for n in SMALL_SHARDED] + [(D,)]
    n_small = sum(math.prod(s) for s in small_shapes)
    small_rows = _pack_rows(n_small)
    small_loc = _pack([W[n][0] for n in SMALL_SHARDED] + [c[0]], small_rows)
    early = ['w_in', 'mla_w_uq', 'mla_w_ukv']
    late = ['w_out', 'ffn_w_up', 'ffn_w_down']
    big = early + late
    gathered = _exchange("gather_weights", [W[n][0].astype(BF16) for n in early] + [small_loc], ['ag'] * 4)
    gw = dict(zip(early, gathered[:3]))
    small_all = [_unpack(gathered[3][d], small_shapes) for d in range(N_DEV)]
    full_small = {n: jnp.concatenate([small_all[d][j] for d in range(N_DEV)], axis=-1)
                  for j, n in enumerate(SMALL_SHARDED)}
    c_all = jnp.stack([small_all[d][-1] for d in range(N_DEV)])

    w_in_f = _cols_from_shards(gw['w_in'])
    w_in_p = jnp.concatenate([w_in_f[:, :OFF_KR], jnp.zeros((D, NOPE), BF16), w_in_f[:, OFF_KR:],
                              jnp.zeros((D, HEAD_PAD - QK), BF16)], axis=1)
    w_uq_f = _cols_from_shards(gw['mla_w_uq']).reshape(Q_RANK, HEADS, QK)
    wq_p = jnp.pad(w_uq_f, ((0, 0), (0, 0), (0, HEAD_PAD - QK))).reshape(Q_RANK, HEADS * HEAD_PAD)
    w_ukv_f = _cols_from_shards(gw['mla_w_ukv']).reshape(KV_RANK, HEADS, NOPE + VDIM)
    wk_p = jnp.pad(w_ukv_f[:, :, :NOPE], ((0, 0), (0, 0), (0, HEAD_PAD - NOPE))).reshape(KV_RANK, HEADS * HEAD_PAD)
    wv_f = w_ukv_f[:, :, NOPE:].reshape(KV_RANK, HEADS * VDIM)

    lru_cw, lru_ba, lru_bx, lru_lam, ffn_cw = [full_small[n] for n in SMALL_SHARDED]
    ffn_cw_t, ffn_cb_t = _ff_to_tiles(ffn_cw), _ff_to_tiles(ffn_conv_b)

    def block_diag(w):
        eye = jnp.eye(LRU_HEADS, dtype=w.dtype)
        return jnp.einsum('hij,hg->higj', w, eye).reshape(LRU_W, LRU_W)

    w_gate = jnp.concatenate([block_diag(lru_w_a[0, 0]), block_diag(lru_w_x[0, 0]),
                              block_diag(lru_w_a[0, 1]), block_diag(lru_w_x[0, 1])], axis=1).astype(BF16)
    b_gate = jnp.concatenate([lru_ba[0], lru_bx[0], lru_ba[1], lru_bx[1]])[None]
    lam_row = lru_lam.reshape(1, 2 * LRU_W)

    c16 = jnp.concatenate([c_all, c_ctx[None], jnp.zeros((2 * SUBLANES - N_DEV - 1, D), F32)])
    ncol = w_mod.shape[2]
    b_mod_loc = lax.dynamic_slice(b_mod, (0, me * ncol), (1, ncol))

    def mod_fwd(c16_r, w_r, b_r):
        c16_v = c16_r[...]
        sl = c16_v * jax.nn.sigmoid(c16_v)
        return (jnp.dot(sl.astype(BF16), w_r[...].astype(BF16), preferred_element_type=F32) + b_r[...],)

    (mod_part,) = _single("mod_fwd", mod_fwd, [c16, w_mod[0], b_mod_loc], [((2 * SUBLANES, ncol), F32)])
    (mod_g,) = _exchange("gather_mod", [mod_part], ['ag'])
    mod_all = _cols_from_shards(mod_g)
    mod_lat = lax.dynamic_slice(mod_all, (me, 0), (1, N_MOD * D)).reshape(N_MOD, D)
    mod_ctx = mod_all[N_DEV].reshape(N_MOD, D)

    xs, tgt = x[0], loss_target[0]
    xa_rows = [_row(ctx[0], lambda i: jnp.minimum(i, ct - 1)), _row(xs, lat_or_0)]

    def sel_mod(i, ml, mc, r0):
        sh = jnp.where(i < ct, mc[r0:r0 + 1, :], ml[r0:r0 + 1, :])
        sc = jnp.where(i < ct, mc[r0 + 1:r0 + 2, :], ml[r0 + 1:r0 + 2, :])
        return sh, sc

    def pre_fn(xv, g, sh, sc):
        return _rms(xv, g) * (1.0 + sc) + sh

    def k_pre(i, rv, bv):
        sh, sc = sel_mod(i, bv[1], bv[2], 0)
        hv = pre_fn(jnp.where(i < ct, rv[0], rv[1]), bv[0][...], sh, sc).astype(BF16)
        return (hv, jnp.dot(hv, bv[3][...], preferred_element_type=F32)), ()

    h_pre, proj = _rowwise("pre_mix", k_pre, nt, xa_rows, [g_pre_mix, mod_lat, mod_ctx, w_in_p],
                           [((T, D), BF16, None), ((T, IN_W_PAD), F32, None)])

    xcv = _dwconv_fwd("lru_conv", proj, lru_cw, lru_conv_b, LRU_CONV_LEFT, (0, ct))

    def k_gates(i, rv, bv):
        xv = rv[0]
        z = jnp.dot(xv.astype(BF16), bv[0][...], preferred_element_type=F32) + bv[1][...]
        return _gates(z, xv, _bc(_softplus(-bv[2][...]), TM)), ()

    a0, u0, a1, u1 = _rowwise("lru_gates", k_gates, nt, [_row(xcv)], [w_gate, b_gate, lam_row],
                              [((T, LRU_W), F32, None), ((T, LRU_W), F32, None),
                               ((T, LRU_W), F32, swp), ((T, LRU_W), F32, swp)])
    h0, hprev0 = _scan("lru_scan_f", a0, u0, False, with_prev=True)
    h1, hprev1 = _scan("lru_scan_r", a1, u1, True, with_prev=True)

    cos_k, sup_k, sdn_k = _rope_tables(S, CN)
    q_tables = [_row(cos_k, lat), _row(sup_k, lat), _row(sdn_k, lat)]
    cq_row = _row(proj, lat, (Q_RANK, OFF_CQ // Q_RANK))
    ckv_row = _row(proj, None, (KV_RANK, OFF_CKV // KV_RANK))

    def heads_of(xl):
        return [xl[:, hh * HEAD_PAD:(hh + 1) * HEAD_PAD] for hh in range(HEADS)]

    def k_q_path(i, rv, bv):
        cqv, ctb, sub, sdb = rv
        cqn_v = _rms(cqv, bv[0][...]).astype(BF16)
        ql = jnp.dot(cqn_v, bv[1][...], preferred_element_type=F32)
        return (cqn_v, jnp.stack([_rope(qh, ctb, sub, sdb) * SCALE2 for qh in heads_of(ql)])), ()

    cqn, q = _rowwise("q_path", k_q_path, ns, [cq_row] + q_tables, [mla_g_q, wq_p],
                      [((S, Q_RANK), BF16, None), ((HEADS, S, HEAD_PAD), BF16, None)])

    def k_kv_path(i, rv, bv):
        ckv_v, krp, ctb, sub, sdb = rv
        ckvn_v = _rms(ckv_v, bv[0][...]).astype(BF16)
        kl = jnp.dot(ckvn_v, bv[1][...], preferred_element_type=F32)
        vl = jnp.dot(ckvn_v, bv[2][...], preferred_element_type=F32)
        kr = _rope(krp, ctb, sub, sdb)
        return (ckvn_v, jnp.stack([kh + kr for kh in heads_of(kl)]), vl), ()

    ckvn, k, v_lin = _rowwise(
        "kv_path", k_kv_path, nt,
        [ckv_row, _row(proj, None, (HEAD_PAD, OFF_KR // HEAD_PAD)), _row(cos_k), _row(sup_k), _row(sdn_k)],
        [mla_g_kv, wk_p, wv_f],
        [((T, KV_RANK), BF16, None), ((HEADS, T, HEAD_PAD), BF16, None), ((T, HEADS * VDIM), BF16, None)])
    v = v_lin.reshape(T, HEADS, VDIM).transpose(1, 0, 2)

    o, lse, late_g = _attn_fwd(q, k, v, [W[n][0].astype(BF16) for n in late], ['ag'] * 3)
    w_out_f = late_g[0].reshape(D, D)
    w_up_t = _ff_permute("w_up_to_tiles", _cols_from_shards(late_g[1]), True)
    w_down_f = late_g[2].reshape(D_FF, D)
    o_t = o.transpose(1, 0, 2).reshape(S, HEADS * VDIM)

    def lru_out_fn(hf, hr, gr):
        return (hf + hr) * jax.nn.gelu(gr)

    def post_mix_fn(xv, yv, gt, g):
        return xv + gt * _rms(yv, g)

    def k_mix_out(i, rv, bv):
        ml = bv[0]
        y_in_v = jnp.concatenate([lru_out_fn(rv[0], rv[1], rv[2]), rv[3]], axis=1).astype(BF16)
        yv = jnp.dot(y_in_v, bv[3][...], preferred_element_type=F32)
        x1v = post_mix_fn(rv[4], yv, ml[2:3, :], bv[1][...])
        return (y_in_v, yv, x1v, pre_fn(x1v, bv[2][...], ml[3:4, :], ml[4:5, :])), ()

    gr_row = _row(proj, lat, (LRU_W, OFF_GR // LRU_W))
    y_in, y, x1, h2 = _rowwise(
        "mix_out", k_mix_out, ns, [_row(h0, lat), _row(h1), gr_row, _row(o_t), _row(xs)],
        [mod_lat, g_post_mix, g_pre_ffn, w_out_f],
        [((S, D), BF16, None), ((S, D), F32, None), ((S, D), F32, None), ((S, D), BF16, None)])
    up = _mm("ffn_up", h2, w_up_t, 'nn', out_dtype=BF16)
    act = _ffn_mid_fwd(up, ffn_cw_t, ffn_cb_t)
    f = _mm("ffn_down", act, w_down_f, 'nn')

    def loss_fn(x1v, fv, gt, g, tg):
        x2 = x1v + gt * _rms(fv, g)
        err = x2 - tg
        return 0.5 * jnp.sum(jnp.mean(err * err, axis=-1))

    def k_loss(i, rv, bv):
        nrow = rv[0].shape[0]
        gtb, gb = _bc(bv[0][5:6, :], nrow), _bc(bv[1][...], nrow)
        val, (dx1v, dfv, dgt, dg) = jax.value_and_grad(loss_fn, argnums=(0, 1, 2, 3))(rv[0], rv[1], gtb, gb, rv[2])
        return (dx1v, dfv), (jnp.full((1, LANES), val, F32), _rs(dgt), _rs(dg))

    dx1_a, df, loss_acc, d_gt2, d_g_post_ffn = _rowwise(
        "loss_bwd", k_loss, ns, [_row(x1), _row(f), _row(tgt)], [mod_lat, g_post_ffn],
        [((S, D), F32, None), ((S, D), BF16, None)], [(1, LANES), (1, D), (1, D)], strips=VJP_STRIPS)

    d_act = _mm("ffn_down_dx", df, w_down_f, 'nt', out_dtype=BF16)
    d_w_down = _mm("ffn_down_dw", act, df, 'tn')
    d_up, d_ffn_cw_t, d_ffn_cb_t = _ffn_mid_bwd(up, d_act, ffn_cw_t, ffn_cb_t)
    d_ffn_cw, d_ffn_cb = _ff_from_tiles(d_ffn_cw_t), _ff_from_tiles(d_ffn_cb_t)
    d_h2 = _mm("ffn_up_dx", d_up, w_up_t, 'nt')
    d_w_up = _ff_permute("d_w_up_from_tiles", _mm("ffn_up_dw", h2, d_up, 'tn'), False)

    def k_pre_ffn_bwd(i, rv, bv):
        ml = bv[0]
        nrow = rv[0].shape[0]
        gb, shb, scb = _bc(bv[1][...], nrow), _bc(ml[3:4, :], nrow), _bc(ml[4:5, :], nrow)
        _, pull = jax.vjp(pre_fn, rv[0], gb, shb, scb)
        dxv, dg, dsh, dsc = pull(rv[1])
        return (rv[2] + dxv,), (_rs(dg), _rs(dsh), _rs(dsc))

    dx1, d_g_pre_ffn, d_sh2, d_sc2 = _rowwise(
        "pre_ffn_bwd", k_pre_ffn_bwd, ns, [_row(x1), _row(d_h2), _row(dx1_a)], [mod_lat, g_pre_ffn],
        [((S, D), F32, None)], [(1, D), (1, D), (1, D)], strips=VJP_STRIPS)

    def k_post_mix_bwd(i, rv, bv):
        gtb, gb = _bc(bv[0][2:3, :], TM), _bc(bv[1][...], TM)
        _, pull = jax.vjp(post_mix_fn, rv[0], rv[1], gtb, gb)
        _, dyv, dgt, dg = pull(rv[2])
        dyb = dyv.astype(BF16)
        return (dyb, lax.dot_general(dyb, bv[2][...], NT_DIMS, preferred_element_type=F32)), (_rs(dgt), _rs(dg))

    dy, d_y_in, d_gt1, d_g_post_mix = _rowwise(
        "post_mix_bwd", k_post_mix_bwd, ns, [_row(xs), _row(y), _row(dx1)], [mod_lat, g_post_mix, w_out_f],
        [((S, D), BF16, None), ((S, D), F32, None)], [(1, D), (1, D)])
    d_w_out = _mm("out_proj_dw", y_in, dy, 'tn')

    def k_lru_out_bwd(i, rv, bv):
        _, pull = jax.vjp(lru_out_fn, rv[0], rv[1], rv[2])
        dhf, _, dgr = pull(rv[3])
        return (dhf, dgr), ()

    d_hsum, d_gr = _rowwise("lru_out_bwd", k_lru_out_bwd, ns,
                            [_row(h0, lat), _row(h1), gr_row, _row(d_y_in, None, (LRU_W, 0))], [],
                            [((S, LRU_W), F32, None), ((S, LRU_W), F32, None)], strips=VJP_STRIPS)

    do_b = d_y_in[:, LRU_W:].astype(BF16).reshape(S, HEADS, VDIM).transpose(1, 0, 2)
    delta = _attn_delta(d_y_in, o_t)
    late_sends = [d_w_out.reshape(N_DEV, D // N_DEV, D), _cols_to_shards(d_w_up),
                  d_w_down.reshape(N_DEV, D_FF // N_DEV, D)]
    dq, dk, dv, late_recv = _attn_bwd(q, k, v, do_b, lse, delta, late_sends, ['a2a'] * 3)

    def rms_bwd(xv, g, dy):
        _, pull = jax.vjp(_rms, xv, _bc(g, TM))
        dxv, dg = pull(dy)
        return dxv, _rs(dg)

    def k_q_path_bwd(i, rv, bv):
        dqv, ctb, sub, sdb, cqv = rv
        dql = jnp.concatenate([_rope_t(dqv[hh] * MLA_SCALE, ctb, sub, sdb) for hh in range(HEADS)], axis=1).astype(BF16)
        d_cqn_v = lax.dot_general(dql, bv[1][...], NT_DIMS, preferred_element_type=F32)
        dxv, dg = rms_bwd(cqv, bv[0][...], d_cqn_v)
        return (dql, dxv), (dg,)

    dq_lin, d_cq, d_g_q = _rowwise(
        "q_path_bwd", k_q_path_bwd, ns, [_row(dq)] + q_tables + [cq_row], [mla_g_q, wq_p],
        [((S, HEADS * HEAD_PAD), BF16, None), ((S, Q_RANK), F32, None)], [(1, Q_RANK)])
    d_wq_p = _mm("q_proj_dw", cqn, dq_lin, 'tn')

    dv_lin = dv.transpose(1, 0, 2).reshape(T, HEADS * VDIM).astype(BF16)

    def k_kv_path_bwd(i, rv, bv):
        dkv_, ctb, sub, sdb, dvl, ckv_v = rv
        tot = dkv_[0]
        for hh in range(1, HEADS):
            tot = tot + dkv_[hh]
        lane = lax.broadcasted_iota(jnp.int32, tot.shape, 1)
        tot = jnp.where((lane >= NOPE) & (lane < QK), tot, 0.0)
        dkl = jnp.concatenate([dkv_[hh] for hh in range(HEADS)], axis=1).astype(BF16)
        d_ckvn_v = (lax.dot_general(dkl, bv[1][...], NT_DIMS, preferred_element_type=F32)
                    + lax.dot_general(dvl, bv[2][...], NT_DIMS, preferred_element_type=F32))
        dxv, dg = rms_bwd(ckv_v, bv[0][...], d_ckvn_v)
        return (dkl, _rope_t(tot, ctb, sub, sdb), dxv), (dg,)

    dk_lin, d_krp, d_ckv, d_g_kv = _rowwise(
        "kv_path_bwd", k_kv_path_bwd, nt,
        [_row(dk), _row(cos_k), _row(sup_k), _row(sdn_k), _row(dv_lin), ckv_row], [mla_g_kv, wk_p, wv_f],
        [((T, HEADS * HEAD_PAD), BF16, None), ((T, HEAD_PAD), F32, None), ((T, KV_RANK), F32, None)], [(1, KV_RANK)])
    d_wk_p = _mm("k_proj_dw", ckvn, dk_lin, 'tn')
    d_wv = _mm("v_proj_dw", ckvn, dv_lin, 'tn')

    lam0 = _scan("lru_scan_f_bwd", a0, d_hsum, True, shifted=True, u_off=CN)
    lam1 = _scan("lru_scan_r_bwd", a1, d_hsum, False, shifted=True, u_off=0)

    def k_gates_bwd(i, rv, bv):
        xv, l0, hp0, l1, hp1 = rv
        wg, bg, lamv = [b[...] for b in bv]
        xb = xv.astype(BF16)
        z = jnp.dot(xb, wg, preferred_element_type=F32) + bg
        spb = _bc(_softplus(-lamv), TM)
        _, pull = jax.vjp(_gates, z, xv, spb)
        dz, dxv, dsp = pull((l0 * hp0, l0, l1 * hp1, l1))
        dzb = dz.astype(BF16)
        dxv = dxv + lax.dot_general(dzb, wg, NT_DIMS, preferred_element_type=F32)
        dwg = lax.dot_general(xb, dzb, (((0,), (0,)), ((), ())), preferred_element_type=F32)
        dlam = -_rs(dsp) * jax.nn.sigmoid(-lamv)
        return (dxv,), (dwg, _rs(dz), dlam)

    d_xcv, d_w_gate, d_b_gate, d_lam = _rowwise(
        "lru_gates_bwd", k_gates_bwd, nt,
        [_row(xcv), _row(lam0), _row(hprev0), _row(lam1, swp), _row(hprev1, swp)], [w_gate, b_gate, lam_row],
        [((T, LRU_W), F32, None)], [(LRU_W, 4 * LRU_W), (1, 4 * LRU_W), (1, 2 * LRU_W)])
    d_xr, d_lru_cw, d_lru_cb = _dwconv_bwd("lru_conv_bwd", proj, d_xcv, lru_cw, LRU_CONV_LEFT, (0, ct))

    def k_dproj(i, rv, bv):
        is_lat = i >= ct
        return (jnp.concatenate([rv[0], jnp.where(is_lat, rv[1], 0.0), jnp.where(is_lat, rv[2], 0.0), rv[3], rv[4]],
                                axis=1),), ()

    d_proj = _rowwise("d_proj", k_dproj, nt,
                      [_row(d_xr), _row(d_gr, lat_or_0), _row(d_cq, lat_or_0), _row(d_ckv), _row(d_krp)], [],
                      [((T, IN_W_PAD), BF16, None)])[0]
    d_w_in_p = _mm("in_proj_dw", h_pre, d_proj, 'tn')

    d_b_a = jnp.stack([d_b_gate[0, 0:LRU_W], d_b_gate[0, 2 * LRU_W:3 * LRU_W]])
    d_b_x = jnp.stack([d_b_gate[0, LRU_W:2 * LRU_W], d_b_gate[0, 3 * LRU_W:]])
    d_w_in = jnp.concatenate([d_w_in_p[:, :OFF_KR], d_w_in_p[:, OFF_KR + NOPE:OFF_KR + QK]], axis=1)
    d_w_uq = d_wq_p.reshape(Q_RANK, HEADS, HEAD_PAD)[:, :, :QK].reshape(Q_RANK, HEADS * QK)
    d_w_ukv = jnp.concatenate([d_wk_p.reshape(KV_RANK, HEADS, HEAD_PAD)[:, :, :NOPE],
                               d_wv.reshape(KV_RANK, HEADS, VDIM)], axis=2).reshape(KV_RANK, HEADS * (NOPE + VDIM))
    small_full = dict(lru_conv_w=d_lru_cw, lru_b_a=d_b_a, lru_b_x=d_b_x, lru_lambda=d_lam.reshape(2, LRU_W),
                      ffn_conv_w=d_ffn_cw)
    small_sh = jnp.concatenate([_cols_to_shards(small_full[n]).reshape(N_DEV, -1) for n in SMALL_SHARDED], axis=1)
    n_sh = small_sh.shape[1]
    sh_rows = _pack_rows(n_sh)
    small_sh = jnp.pad(small_sh, ((0, 0), (0, sh_rows * LANES - n_sh))).reshape(N_DEV, sh_rows, LANES)
    early_sends = [_cols_to_shards(d_w_in), _cols_to_shards(d_w_uq), _cols_to_shards(d_w_ukv), small_sh]

    def k_pre_bwd(i, rv, bv):
        g, ml, mc = bv[0][...], bv[1], bv[2]
        sh, sc = sel_mod(i, ml, mc, 0)
        d_h = lax.dot_general(rv[2], bv[3][...], NT_DIMS, preferred_element_type=F32)
        _, pull = jax.vjp(pre_fn, jnp.where(i < ct, rv[0], rv[1]), _bc(g, TM), _bc(sh, TM), _bc(sc, TM))
        dxv, dg, dsh, dsc = pull(d_h)
        is_lat = i >= ct
        dsh, dsc = _rs(dsh), _rs(dsc)
        zero = jnp.zeros_like(dsh)
        return ((dxv + rv[3],),
                (_rs(dg), jnp.where(is_lat, dsh, zero), jnp.where(is_lat, dsc, zero),
                 jnp.where(is_lat, zero, dsh), jnp.where(is_lat, zero, dsc)))

    (dxl, d_g_pre_mix, d_sh1, d_sc1, d_csh1, d_csc1), early_recv = _rowwise(
        "pre_mix_bwd", k_pre_bwd, nt, xa_rows + [_row(d_proj), _row(dx1, lat_or_0)],
        [g_pre_mix, mod_lat, mod_ctx, w_in_p], [((S, D), F32, lat_or_0)], [(1, D)] * 5,
        ride_arrs=early_sends, ride_modes=['a2a'] * 4)
    grad_x = dxl[None]

    zrow = jnp.zeros((1, D), F32)
    d_mod_lat = jnp.concatenate([d_sh1, d_sc1, d_gt1, d_sh2, d_sc2, d_gt2], axis=1)
    d_mod_ctx = jnp.concatenate([d_csh1, d_csc1, zrow, zrow, zrow, zrow], axis=1)
    loss_row = jnp.pad(loss_acc, ((0, 0), (0, N_MOD * D - LANES)))
    d_mod_mine = jnp.concatenate([d_mod_lat, d_mod_ctx, loss_row, jnp.zeros((SUBLANES - 3, N_MOD * D), F32)])

    def diag_blocks(dw):
        return jnp.stack([dw[hh * 64:(hh + 1) * 64, hh * 64:(hh + 1) * 64] for hh in range(LRU_HEADS)])

    d_lru_w_a = jnp.stack([diag_blocks(d_w_gate[:, 0:LRU_W]), diag_blocks(d_w_gate[:, 2 * LRU_W:3 * LRU_W])])[None]
    d_lru_w_x = jnp.stack([diag_blocks(d_w_gate[:, LRU_W:2 * LRU_W]), diag_blocks(d_w_gate[:, 3 * LRU_W:])])[None]
    rep_part = dict(b_mod=d_mod_lat + d_mod_ctx, g_pre_mix=d_g_pre_mix, g_post_mix=d_g_post_mix,
                    g_pre_ffn=d_g_pre_ffn, g_post_ffn=d_g_post_ffn, lru_conv_b=d_lru_cb, lru_w_a=d_lru_w_a,
                    lru_w_x=d_lru_w_x, mla_g_q=d_g_q, mla_g_kv=d_g_kv, ffn_conv_b=d_ffn_cb)
    rep_names = [n for n in REPLICATED if n != 'c_ctx']
    rep_shapes = [W[n].shape for n in rep_names]
    rep_rows = -(-_pack_rows(sum(W[n].size for n in rep_names)) // ROW_TILE) * ROW_TILE
    rep_loc = _pack([rep_part[n] for n in rep_names], rep_rows)
    d_mod_all, rep_all = _exchange("gather_dmod", [d_mod_mine, rep_loc], ['ag'] * 2)
    loss = jnp.sum(d_mod_all[:, 2, 0])
    dm_lat_loc = lax.dynamic_slice(d_mod_all[:, 0], (0, me * ncol), (N_DEV, ncol))
    dm_ctx_loc = lax.dynamic_slice(d_mod_all[:, 1], (0, me * ncol), (N_DEV, ncol))

    def mod_bwd(c16_r, w_r, dml_r, dmc_r):
        c16_v = c16_r[...]
        sig = jax.nn.sigmoid(c16_v)
        sl = c16_v * sig
        dctx = dmc_r[0:1, :]
        for d in range(1, N_DEV):
            dctx = dctx + dmc_r[d:d + 1, :]
        row = lax.broadcasted_iota(jnp.int32, (2 * SUBLANES, ncol), 0)
        dm16 = dml_r[...] + jnp.where(row == N_DEV, _bc(dctx, 2 * SUBLANES), 0.0)
        dw = lax.dot_general(sl.astype(BF16), dm16.astype(BF16), (((0,), (0,)), ((), ())), preferred_element_type=F32)
        dsl = lax.dot_general(dm16.astype(BF16), w_r[...].astype(BF16), NT_DIMS, preferred_element_type=F32)
        dc = dsl * (sig * (1.0 + c16_v * (1.0 - sig)))
        return dw, dc

    dm_lat16 = jnp.concatenate([dm_lat_loc, jnp.zeros((2 * SUBLANES - N_DEV, ncol), F32)])
    g_w_mod, dc16 = _single("mod_bwd", mod_bwd, [c16, w_mod[0], dm_lat16, dm_ctx_loc],
                            [((D, ncol), F32), ((2 * SUBLANES, D), F32)])
    d_c_ctx_part = dc16[N_DEV]

    (c_ctx_all,) = _exchange("gather_d_c_ctx", [d_c_ctx_part.reshape(SUBLANES, D // SUBLANES)], ['ag'])
    big_parts = list(early_recv[:3]) + list(late_recv)

    res = {}

    def adam(name, w2, m2, v2, parts):
        return _adamw("adamw_" + name, w2, m2, v2, parts)

    for n, parts in zip(big, big_parts):
        shp = W[n].shape
        outs = adam(n, W[n][0], M[n][0], V[n][0], parts)
        res[n] = [o_.reshape(shp) for o_ in outs]
    outs = adam('w_mod', w_mod[0], m_w_mod[0], v_w_mod[0], g_w_mod[None])
    res['w_mod'] = [o_.reshape(w_mod.shape) for o_ in outs]

    sh_shapes = [W[n].shape for n in SMALL_SHARDED]
    pk = lambda dct: _pack([dct[n] for n in SMALL_SHARDED], sh_rows)
    outs = adam('small_sharded', pk(W), pk(M), pk(V), early_recv[3])
    for n, vals in zip(SMALL_SHARDED, zip(*[_unpack(o_, sh_shapes) for o_ in outs])):
        res[n] = list(vals)

    pr = lambda dct: _pack([dct[n] for n in rep_names], rep_rows)
    outs = adam('replicated', pr(W), pr(M), pr(V), rep_all)
    for n, vals in zip(rep_names, zip(*[_unpack(o_, rep_shapes) for o_ in outs])):
        res[n] = list(vals)
    as_tile = lambda a: a.reshape(SUBLANES, D // SUBLANES)
    outs = adam('c_ctx', as_tile(c_ctx), as_tile(m_c_ctx), as_tile(v_c_ctx), c_ctx_all)
    res['c_ctx'] = [o_.reshape(c_ctx.shape) for o_ in outs]

    return (loss, grad_x, *[res[n][0] for n in WEIGHTS], *[res[n][1] for n in WEIGHTS],
            *[res[n][2] for n in WEIGHTS], *[res[n][3] for n in WEIGHTS])
```
